```python
import math
import jax, jax.numpy as jnp
from jax import lax
import numpy as np

D_MODEL = 1024
BATCH = 8
SEQ = 16384
DEPTH = 4

N_MIXERS = 3
EPS = 1e-6
N_A = (DEPTH + 2) // 3
N_B = (DEPTH + 1) // 3
N_C = DEPTH // 3

HG_HEADS = 8
HG_DK = D_MODEL // HG_HEADS
HG_DV = D_MODEL // HG_HEADS
HG_CHUNK = 64
HG_IN = 4 * D_MODEL

SW_HEADS = 16
SW_KV_HEADS = 4
SW_GROUP = SW_HEADS // SW_KV_HEADS
SW_DH = 64
SW_WINDOW = 128
SW_BLOCK = 128
SW_QW = SW_HEADS * SW_DH
SW_KVW = SW_KV_HEADS * SW_DH
SW_IN = 2 * SW_QW + 2 * SW_KVW
ROPE_THETA = 10000.0
POS_OFFSET_MAX = 4096

GD_QK_HEADS = 8
GD_V_HEADS = 16
GD_DK = 128
GD_DV = 128
GD_CONV = 4
GD_CHUNK = 64
GD_QKW = GD_QK_HEADS * GD_DK
GD_VW = GD_V_HEADS * GD_DV
GD_QKV = 2 * GD_QKW + GD_VW
GD_IN = GD_QKV + GD_VW + 2 * GD_V_HEADS

kernel_name = 'hybrid_hgrn2_swa_sink_gdn_interleaved'


def rmsnorm(x, g):
    xf = x.astype(jnp.float32)
    y = xf * lax.rsqrt(jnp.mean(xf * xf, axis=-1, keepdims=True) + EPS)
    return (y * g.astype(jnp.float32)).astype(x.dtype)


def l2norm(x):
    xf = x.astype(jnp.float32)
    return xf * lax.rsqrt(jnp.sum(xf * xf, axis=-1, keepdims=True) + EPS)


def rope(x, ang):
    ang = ang.reshape(ang.shape[:2] + (1,) * (x.ndim - 3) + ang.shape[-1:])
    cos, sin = jnp.cos(ang), jnp.sin(ang)
    x1, x2 = jnp.split(x.astype(jnp.float32), 2, axis=-1)
    return jnp.concatenate([x1 * cos - x2 * sin, x2 * cos + x1 * sin], axis=-1).astype(x.dtype)


def to_chunks(a, chunk):
    B, T = a.shape[:2]
    a = a.reshape((B, T // chunk, chunk) + a.shape[2:])
    return jnp.moveaxis(jnp.moveaxis(a, 1, 0), 3, 2)


def from_chunks(a):
    a = jnp.moveaxis(jnp.moveaxis(a, 2, 3), 0, 1)
    return a.reshape((a.shape[0], a.shape[1] * a.shape[2]) + a.shape[3:])


def causal_conv(x, w):
    K, C = w.shape
    return lax.conv_general_dilated(x, w[:, None, :].astype(x.dtype), window_strides=(1,),
                                    padding=[(K - 1, 0)], dimension_numbers=('NWC', 'WIO', 'NWC'),
                                    feature_group_count=C)


def hgrn2_scan(q, k, v, log_f):
    B, T, H, dk = q.shape
    dv = v.shape[-1]
    causal = jnp.tril(jnp.ones((HG_CHUNK, HG_CHUNK), bool))

    def step(S, inp):
        q_, k_, v_, lf = inp
        b = jnp.cumsum(lf, axis=-2)
        diff = b[..., :, None, :] - b[..., None, :, :]
        dec = jnp.exp(jnp.where(causal[:, :, None], diff, -jnp.inf))
        A = jnp.einsum('bhtd,bhsd,bhtsd->bhts', q_, k_, dec)
        o = jnp.einsum('bhts,bhse->bhte', A, v_) + jnp.einsum('bhtd,bhde->bhte', q_ * jnp.exp(b), S)
        b_last = b[..., -1:, :]
        S = jnp.exp(b_last)[..., 0, :, None] * S + jnp.einsum('bhsd,bhse->bhde', k_ * jnp.exp(b_last - b), v_)
        return S, o

    xs = tuple(to_chunks(a.astype(jnp.float32), HG_CHUNK) for a in (q, k, v, log_f))
    S0 = jnp.zeros((B, H, dk, dv), jnp.float32)
    _, o = lax.scan(step, S0, xs)
    return from_chunks(o)


def hgrn2_mixer(h, w_in, w_out, onorm_g, lb):
    B, T, _ = h.shape
    q, f_pre, i_, z = jnp.split(h @ w_in, [D_MODEL, 2 * D_MODEL, 3 * D_MODEL], axis=-1)
    f_pre = f_pre.astype(jnp.float32)
    lb = lb.astype(jnp.float32)
    log_f = jnp.log(lb + (1.0 - lb) * jax.nn.sigmoid(f_pre))
    k = (1.0 - lb) * jax.nn.sigmoid(-f_pre)
    q = jax.nn.silu(q)
    shp = (B, T, HG_HEADS, HG_DK)
    o = hgrn2_scan(q.reshape(shp), k.reshape(shp), i_.reshape(B, T, HG_HEADS, HG_DV), log_f.reshape(shp))
    o = rmsnorm(o, onorm_g).reshape(B, T, HG_HEADS * HG_DV).astype(h.dtype)
    return (o * jax.nn.silu(z)) @ w_out


def swa_mixer(h, w_in, w_out, qn_g, kn_g, sinks, ang):
    B, T, _ = h.shape
    nb = T // SW_BLOCK
    q, k, v, z = jnp.split(h @ w_in, [SW_QW, SW_QW + SW_KVW, SW_QW + 2 * SW_KVW], axis=-1)
    q = rope(rmsnorm(q.reshape(B, T, SW_KV_HEADS, SW_GROUP, SW_DH), qn_g), ang)
    k = rope(rmsnorm(k.reshape(B, T, SW_KV_HEADS, SW_DH), kn_g), ang)
    v = v.reshape(B, T, SW_KV_HEADS, SW_DH)

    def band(a):
        cur = a.reshape(B, nb, SW_BLOCK, SW_KV_HEADS, SW_DH)
        prev = jnp.concatenate([jnp.zeros_like(cur[:, :1]), cur[:, :-1]], axis=1)
        return jnp.concatenate([prev, cur], axis=2)

    qb = q.reshape(B, nb, SW_BLOCK, SW_KV_HEADS, SW_GROUP, SW_DH)
    kb, vb = band(k), band(v)
    s = jnp.einsum('bnqhgd,bnkhd->bnhgqk', qb, kb).astype(jnp.float32) * (SW_DH ** -0.5)
    qi = jnp.arange(SW_BLOCK)[:, None]
    kj = jnp.arange(2 * SW_BLOCK)[None, :]
    rel = qi + SW_BLOCK - kj
    in_band = (rel >= 0) & (rel < SW_WINDOW)
    key_pos = jnp.arange(nb)[:, None] * SW_BLOCK + jnp.arange(2 * SW_BLOCK)[None, :] - SW_BLOCK
    mask = in_band[None] & (key_pos >= 0)[:, None, :]
    s = jnp.where(mask[None, :, None, None], s, -jnp.inf)
    sink = sinks.astype(jnp.float32).reshape(SW_KV_HEADS, SW_GROUP)[None, None, :, :, None, None]
    m = jnp.maximum(jnp.max(s, axis=-1, keepdims=True), sink)
    p = jnp.exp(s - m)
    p = p / (jnp.sum(p, axis=-1, keepdims=True) + jnp.exp(sink - m))
    o = jnp.einsum('bnhgqk,bnkhd->bnqhgd', p.astype(h.dtype), vb).reshape(B, T, SW_QW)
    return (o * jax.nn.silu(z)) @ w_out


def gated_delta_scan(q, k, v, beta, g):
    B, T, H, dk = q.shape
    dv = v.shape[-1]
    incl = jnp.tril(jnp.ones((GD_CHUNK, GD_CHUNK), bool))
    strict = jnp.tril(jnp.ones((GD_CHUNK, GD_CHUNK), jnp.float32), -1)
    eye = jnp.eye(GD_CHUNK, dtype=jnp.float32)

    def step(S, inp):
        q_, k_, v_, beta_, g_ = inp
        d = jnp.cumsum(g_, axis=-1)
        dec = jnp.exp(jnp.where(incl, d[..., :, None] - d[..., None, :], -jnp.inf))
        kb = k_ * beta_[..., None]
        A = jnp.einsum('bhid,bhjd->bhij', kb, k_) * dec * strict
        rhs = jnp.concatenate([v_ * beta_[..., None], kb * jnp.exp(d)[..., None]], axis=-1)
        X = lax.linalg.triangular_solve(A + eye, rhs, left_side=True, lower=True, unit_diagonal=True)
        u, w = X[..., :dv], X[..., dv:]
        v_new = u - jnp.einsum('bhid,bhde->bhie', w, S)
        qk = jnp.einsum('bhid,bhjd->bhij', q_, k_) * dec
        o = jnp.einsum('bhid,bhde->bhie', q_ * jnp.exp(d)[..., None], S) + jnp.einsum('bhij,bhje->bhie', qk, v_new)
        d_last = d[..., -1:]
        S = S * jnp.exp(d_last)[..., None] + jnp.einsum('bhid,bhie->bhde', k_ * jnp.exp(d_last - d)[..., None], v_new)
        return S, o

    xs = tuple(to_chunks(a.astype(jnp.float32), GD_CHUNK) for a in (q, k, v, beta, g))
    S0 = jnp.zeros((B, H, dk, dv), jnp.float32)
    _, o = lax.scan(step, S0, xs)
    return from_chunks(o)


def gdn_mixer(h, w_in, w_out, conv_w, a_log, dt_bias, onorm_g):
    B, T, _ = h.shape
    qkv, z, a, b = jnp.split(h @ w_in, [GD_QKV, GD_QKV + GD_VW, GD_QKV + GD_VW + GD_V_HEADS], axis=-1)
    qkv = jax.nn.silu(causal_conv(qkv, conv_w))
    q, k, v = jnp.split(qkv, [GD_QKW, 2 * GD_QKW], axis=-1)
    rep = GD_V_HEADS // GD_QK_HEADS
    q = jnp.repeat(l2norm(q.reshape(B, T, GD_QK_HEADS, GD_DK)) * (GD_DK ** -0.5), rep, axis=2)
    k = jnp.repeat(l2norm(k.reshape(B, T, GD_QK_HEADS, GD_DK)), rep, axis=2)
    v = v.reshape(B, T, GD_V_HEADS, GD_DV)
    beta = jax.nn.sigmoid(b.astype(jnp.float32))
    g = -jnp.exp(a_log.astype(jnp.float32)) * jax.nn.softplus(a.astype(jnp.float32) + dt_bias.astype(jnp.float32))
    o = gated_delta_scan(q, k, v, beta, g)
    o = rmsnorm(o, onorm_g).reshape(B, T, GD_VW).astype(h.dtype)
    return (o * jax.nn.silu(z)) @ w_out


def _fwd_setup_inputs(seed: int = 0) -> dict:
    key = jax.random.key(seed)
    ks = jax.random.split(key, 24)
    nrm = lambda k, shape, s: jax.random.normal(k, shape, jnp.float32) * s
    x = nrm(ks[0], (BATCH, SEQ, D_MODEL), 1.0)
    c = nrm(ks[1], (BATCH, D_MODEL), 1.0)
    positions = (jnp.arange(SEQ, dtype=jnp.int32)[None, :]
                 + jax.random.randint(ks[2], (BATCH, 1), 0, POS_OFFSET_MAX, dtype=jnp.int32))
    hgrn_lb = nrm(ks[3], (DEPTH, D_MODEL), 0.1)
    ada_w = nrm(ks[4], (DEPTH, D_MODEL, 3 * D_MODEL), 0.5 * D_MODEL ** -0.5)
    ada_b = nrm(ks[5], (DEPTH, 3 * D_MODEL), 0.01)
    norm_g = 1.0 + nrm(ks[6], (DEPTH, D_MODEL), 0.02)
    hg_in_w = nrm(ks[7], (N_A, D_MODEL, HG_IN), D_MODEL ** -0.5)
    hg_out_w = nrm(ks[8], (N_A, HG_HEADS * HG_DV, D_MODEL), (HG_HEADS * HG_DV) ** -0.5)
    hg_onorm = 1.0 + nrm(ks[9], (N_A, HG_DV), 0.02)
    sw_in_w = nrm(ks[10], (N_B, D_MODEL, SW_IN), D_MODEL ** -0.5)
    sw_out_w = nrm(ks[11], (N_B, SW_QW, D_MODEL), SW_QW ** -0.5)
    sw_qnorm = 1.0 + nrm(ks[12], (N_B, SW_DH), 0.02)
    sw_knorm = 1.0 + nrm(ks[13], (N_B, SW_DH), 0.02)
    sw_sinks = nrm(ks[14], (N_B, SW_HEADS), 0.5)
    gd_in_w = nrm(ks[15], (N_C, D_MODEL, GD_IN), D_MODEL ** -0.5)
    gd_out_w = nrm(ks[16], (N_C, GD_VW, D_MODEL), GD_VW ** -0.5)
    gd_conv_w = nrm(ks[17], (N_C, GD_CONV, GD_QKV), GD_CONV ** -0.5)
    gd_a_log = jnp.log(jax.random.uniform(ks[18], (N_C, GD_V_HEADS), jnp.float32, 1.0, 16.0))
    dt = jnp.exp(jax.random.uniform(ks[19], (N_C, GD_V_HEADS), jnp.float32, math.log(1e-3), math.log(1e-1)))
    gd_dt_bias = dt + jnp.log(-jnp.expm1(-dt))
    gd_onorm = 1.0 + nrm(ks[20], (N_C, GD_DV), 0.02)
    return {'x': x, 'c': c, 'positions': positions, 'hgrn_lb': hgrn_lb,
            'ada_w': ada_w, 'ada_b': ada_b, 'norm_g': norm_g,
            'hg_in_w': hg_in_w, 'hg_out_w': hg_out_w, 'hg_onorm': hg_onorm,
            'sw_in_w': sw_in_w, 'sw_out_w': sw_out_w, 'sw_qnorm': sw_qnorm, 'sw_knorm': sw_knorm,
            'sw_sinks': sw_sinks,
            'gd_in_w': gd_in_w, 'gd_out_w': gd_out_w, 'gd_conv_w': gd_conv_w,
            'gd_a_log': gd_a_log, 'gd_dt_bias': gd_dt_bias, 'gd_onorm': gd_onorm}


def _fwd_reference(x, c, positions, hgrn_lb, ada_w, ada_b, norm_g,
              hg_in_w, hg_out_w, hg_onorm,
              sw_in_w, sw_out_w, sw_qnorm, sw_knorm, sw_sinks,
              gd_in_w, gd_out_w, gd_conv_w, gd_a_log, gd_dt_bias, gd_onorm):
    lb_all = jnp.cumsum(jax.nn.softmax(hgrn_lb.astype(jnp.float32), axis=0), axis=0)
    lb_all = lb_all - lb_all[0:1]
    inv_freq = ROPE_THETA ** (-jnp.arange(0, SW_DH, 2, dtype=jnp.float32) / SW_DH)
    ang = positions.astype(jnp.float32)[..., None] * inv_freq
    for i in range(DEPTH):
        j = i // N_MIXERS
        mod = (c @ ada_w[i] + ada_b[i])[:, None, :]
        shift, scale, gate = jnp.split(mod, 3, axis=-1)
        h = rmsnorm(x, norm_g[i]) * (1.0 + scale) + shift
        kind = i % N_MIXERS
        if kind == 0:
            y = hgrn2_mixer(h, hg_in_w[j], hg_out_w[j], hg_onorm[j], lb_all[i])
        elif kind == 1:
            y = swa_mixer(h, sw_in_w[j], sw_out_w[j], sw_qnorm[j], sw_knorm[j], sw_sinks[j], ang)
        else:
            y = gdn_mixer(h, gd_in_w[j], gd_out_w[j], gd_conv_w[j], gd_a_log[j], gd_dt_bias[j], gd_onorm[j])
        x = x + gate * y
    return x


import jax as _jax
import jax.numpy as _jnp

TWIN_FORMAT = 'train_step'
FWD_PARAMS = ['x', 'c', 'positions', 'hgrn_lb', 'ada_w', 'ada_b', 'norm_g', 'hg_in_w', 'hg_out_w', 'hg_onorm', 'sw_in_w', 'sw_out_w', 'sw_qnorm', 'sw_knorm', 'sw_sinks', 'gd_in_w', 'gd_out_w', 'gd_conv_w', 'gd_a_log', 'gd_dt_bias', 'gd_onorm']
TWIN_WEIGHTS = ['hgrn_lb', 'ada_w', 'ada_b', 'norm_g', 'hg_in_w', 'hg_out_w', 'hg_onorm', 'sw_in_w', 'sw_out_w', 'sw_qnorm', 'sw_knorm', 'sw_sinks', 'gd_in_w', 'gd_out_w', 'gd_conv_w', 'gd_a_log', 'gd_dt_bias', 'gd_onorm']
TWIN_DIFF_INPUT = 'x'
TWIN_INPUTS = ['x', 'c', 'positions', 'hgrn_lb', 'ada_w', 'ada_b', 'norm_g', 'hg_in_w', 'hg_out_w', 'hg_onorm', 'sw_in_w', 'sw_out_w', 'sw_qnorm', 'sw_knorm', 'sw_sinks', 'gd_in_w', 'gd_out_w', 'gd_conv_w', 'gd_a_log', 'gd_dt_bias', 'gd_onorm', 'loss_target', 'm_hgrn_lb', 'm_ada_w', 'm_ada_b', 'm_norm_g', 'm_hg_in_w', 'm_hg_out_w', 'm_hg_onorm', 'm_sw_in_w', 'm_sw_out_w', 'm_sw_qnorm', 'm_sw_knorm', 'm_sw_sinks', 'm_gd_in_w', 'm_gd_out_w', 'm_gd_conv_w', 'm_gd_a_log', 'm_gd_dt_bias', 'm_gd_onorm', 'v_hgrn_lb', 'v_ada_w', 'v_ada_b', 'v_norm_g', 'v_hg_in_w', 'v_hg_out_w', 'v_hg_onorm', 'v_sw_in_w', 'v_sw_out_w', 'v_sw_qnorm', 'v_sw_knorm', 'v_sw_sinks', 'v_gd_in_w', 'v_gd_out_w', 'v_gd_conv_w', 'v_gd_a_log', 'v_gd_dt_bias', 'v_gd_onorm']
TWIN_OUTPUTS = ['loss', 'grad_x', 'grad_hgrn_lb', 'grad_ada_w', 'grad_ada_b', 'grad_norm_g', 'grad_hg_in_w', 'grad_hg_out_w', 'grad_hg_onorm', 'grad_sw_in_w', 'grad_sw_out_w', 'grad_sw_qnorm', 'grad_sw_knorm', 'grad_sw_sinks', 'grad_gd_in_w', 'grad_gd_out_w', 'grad_gd_conv_w', 'grad_gd_a_log', 'grad_gd_dt_bias', 'grad_gd_onorm', 'delta_hgrn_lb', 'delta_ada_w', 'delta_ada_b', 'delta_norm_g', 'delta_hg_in_w', 'delta_hg_out_w', 'delta_hg_onorm', 'delta_sw_in_w', 'delta_sw_out_w', 'delta_sw_qnorm', 'delta_sw_knorm', 'delta_sw_sinks', 'delta_gd_in_w', 'delta_gd_out_w', 'delta_gd_conv_w', 'delta_gd_a_log', 'delta_gd_dt_bias', 'delta_gd_onorm', 'new_m_hgrn_lb', 'new_m_ada_w', 'new_m_ada_b', 'new_m_norm_g', 'new_m_hg_in_w', 'new_m_hg_out_w', 'new_m_hg_onorm', 'new_m_sw_in_w', 'new_m_sw_out_w', 'new_m_sw_qnorm', 'new_m_sw_knorm', 'new_m_sw_sinks', 'new_m_gd_in_w', 'new_m_gd_out_w', 'new_m_gd_conv_w', 'new_m_gd_a_log', 'new_m_gd_dt_bias', 'new_m_gd_onorm', 'new_v_hgrn_lb', 'new_v_ada_w', 'new_v_ada_b', 'new_v_norm_g', 'new_v_hg_in_w', 'new_v_hg_out_w', 'new_v_hg_onorm', 'new_v_sw_in_w', 'new_v_sw_out_w', 'new_v_sw_qnorm', 'new_v_sw_knorm', 'new_v_sw_sinks', 'new_v_gd_in_w', 'new_v_gd_out_w', 'new_v_gd_conv_w', 'new_v_gd_a_log', 'new_v_gd_dt_bias', 'new_v_gd_onorm']
TWIN_LEAF_KINDS = {'loss': 'loss', 'grad_x': 'grad_x', 'grad_hgrn_lb': 'grad_w', 'grad_ada_w': 'grad_w', 'grad_ada_b': 'grad_w', 'grad_norm_g': 'grad_w', 'grad_hg_in_w': 'grad_w', 'grad_hg_out_w': 'grad_w', 'grad_hg_onorm': 'grad_w', 'grad_sw_in_w': 'grad_w', 'grad_sw_out_w': 'grad_w', 'grad_sw_qnorm': 'grad_w', 'grad_sw_knorm': 'grad_w', 'grad_sw_sinks': 'grad_w', 'grad_gd_in_w': 'grad_w', 'grad_gd_out_w': 'grad_w', 'grad_gd_conv_w': 'grad_w', 'grad_gd_a_log': 'grad_w', 'grad_gd_dt_bias': 'grad_w', 'grad_gd_onorm': 'grad_w', 'delta_hgrn_lb': 'delta_w', 'delta_ada_w': 'delta_w', 'delta_ada_b': 'delta_w', 'delta_norm_g': 'delta_w', 'delta_hg_in_w': 'delta_w', 'delta_hg_out_w': 'delta_w', 'delta_hg_onorm': 'delta_w', 'delta_sw_in_w': 'delta_w', 'delta_sw_out_w': 'delta_w', 'delta_sw_qnorm': 'delta_w', 'delta_sw_knorm': 'delta_w', 'delta_sw_sinks': 'delta_w', 'delta_gd_in_w': 'delta_w', 'delta_gd_out_w': 'delta_w', 'delta_gd_conv_w': 'delta_w', 'delta_gd_a_log': 'delta_w', 'delta_gd_dt_bias': 'delta_w', 'delta_gd_onorm': 'delta_w', 'new_m_hgrn_lb': 'new_m', 'new_m_ada_w': 'new_m', 'new_m_ada_b': 'new_m', 'new_m_norm_g': 'new_m', 'new_m_hg_in_w': 'new_m', 'new_m_hg_out_w': 'new_m', 'new_m_hg_onorm': 'new_m', 'new_m_sw_in_w': 'new_m', 'new_m_sw_out_w': 'new_m', 'new_m_sw_qnorm': 'new_m', 'new_m_sw_knorm': 'new_m', 'new_m_sw_sinks': 'new_m', 'new_m_gd_in_w': 'new_m', 'new_m_gd_out_w': 'new_m', 'new_m_gd_conv_w': 'new_m', 'new_m_gd_a_log': 'new_m', 'new_m_gd_dt_bias': 'new_m', 'new_m_gd_onorm': 'new_m', 'new_v_hgrn_lb': 'new_v', 'new_v_ada_w': 'new_v', 'new_v_ada_b': 'new_v', 'new_v_norm_g': 'new_v', 'new_v_hg_in_w': 'new_v', 'new_v_hg_out_w': 'new_v', 'new_v_hg_onorm': 'new_v', 'new_v_sw_in_w': 'new_v', 'new_v_sw_out_w': 'new_v', 'new_v_sw_qnorm': 'new_v', 'new_v_sw_knorm': 'new_v', 'new_v_sw_sinks': 'new_v', 'new_v_gd_in_w': 'new_v', 'new_v_gd_out_w': 'new_v', 'new_v_gd_conv_w': 'new_v', 'new_v_gd_a_log': 'new_v', 'new_v_gd_dt_bias': 'new_v', 'new_v_gd_onorm': 'new_v'}


def _forward(args):
    return _fwd_reference(*[args[k] for k in FWD_PARAMS])


def _output_shape():
    def fwd():
        inp = _fwd_setup_inputs(0)
        return _fwd_reference(*[inp[k] for k in FWD_PARAMS])
    out = _jax.eval_shape(fwd)
    return out.shape, out.dtype

N_MICROBATCH = 1
ADAM_LR = 0.001
ADAM_B1 = 0.9
ADAM_B2 = 0.999
ADAM_EPS = 1e-08
ADAM_WD = 0.01
ADAM_STEP = 10
PER_EXAMPLE_BATCH_AXIS = {'x': 0, 'c': 0, 'positions': 0, 'loss_target': 0}
SHARED_INPUTS = []
_WEIGHT_DTYPES = {'hgrn_lb': _jnp.float32, 'ada_w': _jnp.float32, 'ada_b': _jnp.float32, 'norm_g': _jnp.float32, 'hg_in_w': _jnp.float32, 'hg_out_w': _jnp.float32, 'hg_onorm': _jnp.float32, 'sw_in_w': _jnp.float32, 'sw_out_w': _jnp.float32, 'sw_qnorm': _jnp.float32, 'sw_knorm': _jnp.float32, 'sw_sinks': _jnp.float32, 'gd_in_w': _jnp.float32, 'gd_out_w': _jnp.float32, 'gd_conv_w': _jnp.float32, 'gd_a_log': _jnp.float32, 'gd_dt_bias': _jnp.float32, 'gd_onorm': _jnp.float32}
MOMENT_SCALE = {'hgrn_lb': 4.822948e-02, 'ada_w': 1.011527e+01, 'ada_b': 1.350435e+01, 'norm_g': 1.664124e+01, 'hg_in_w': 2.831567e+00, 'hg_out_w': 2.751137e+00, 'hg_onorm': 1.525288e+02, 'sw_in_w': 2.734823e+00, 'sw_out_w': 1.989640e+00, 'sw_qnorm': 3.254193e+00, 'sw_knorm': 3.237395e+00, 'sw_sinks': 9.260573e-01, 'gd_in_w': 2.098931e+00, 'gd_out_w': 2.703339e+00, 'gd_conv_w': 2.097440e+00, 'gd_a_log': 2.189058e+01, 'gd_dt_bias': 2.098240e+01, 'gd_onorm': 1.497732e+02}


def _to_microbatches(a, axis):
    t = _jnp.moveaxis(a, axis, 0)
    t = t.reshape((N_MICROBATCH, t.shape[0] // N_MICROBATCH) + t.shape[1:])
    return _jnp.moveaxis(t, 1, axis + 1)


def setup_inputs(seed: int = 0) -> dict:
    inp = _fwd_setup_inputs(seed)
    key = _jax.random.fold_in(_jax.random.key(seed), 7919)
    shape, _ = _output_shape()
    out = dict(inp)
    out["loss_target"] = _jax.random.normal(_jax.random.fold_in(key, 0), shape, _jnp.float32)
    for i, name in enumerate(TWIN_WEIGHTS):
        w = inp[name].astype(_jnp.float32)
        if MOMENT_SCALE is None:
            s = _jnp.sqrt(_jnp.mean(_jnp.square(w)) + 1e-30)
        else:
            s = MOMENT_SCALE[name]
        km, kv = _jax.random.split(_jax.random.fold_in(key, i + 1))
        out[name] = w
        out["m_" + name] = s * _jax.random.normal(km, w.shape, _jnp.float32)
        out["v_" + name] = (s * s) * _jax.random.uniform(kv, w.shape, _jnp.float32, 0.5, 1.5)
    if N_MICROBATCH > 1:
        for name, axis in PER_EXAMPLE_BATCH_AXIS.items():
            out[name] = _to_microbatches(out[name], axis)
    return {'x': out['x'], 'c': out['c'], 'positions': out['positions'], 'hgrn_lb': out['hgrn_lb'], 'ada_w': out['ada_w'], 'ada_b': out['ada_b'], 'norm_g': out['norm_g'], 'hg_in_w': out['hg_in_w'], 'hg_out_w': out['hg_out_w'], 'hg_onorm': out['hg_onorm'], 'sw_in_w': out['sw_in_w'], 'sw_out_w': out['sw_out_w'], 'sw_qnorm': out['sw_qnorm'], 'sw_knorm': out['sw_knorm'], 'sw_sinks': out['sw_sinks'], 'gd_in_w': out['gd_in_w'], 'gd_out_w': out['gd_out_w'], 'gd_conv_w': out['gd_conv_w'], 'gd_a_log': out['gd_a_log'], 'gd_dt_bias': out['gd_dt_bias'], 'gd_onorm': out['gd_onorm'], 'loss_target': out['loss_target'], 'm_hgrn_lb': out['m_hgrn_lb'], 'm_ada_w': out['m_ada_w'], 'm_ada_b': out['m_ada_b'], 'm_norm_g': out['m_norm_g'], 'm_hg_in_w': out['m_hg_in_w'], 'm_hg_out_w': out['m_hg_out_w'], 'm_hg_onorm': out['m_hg_onorm'], 'm_sw_in_w': out['m_sw_in_w'], 'm_sw_out_w': out['m_sw_out_w'], 'm_sw_qnorm': out['m_sw_qnorm'], 'm_sw_knorm': out['m_sw_knorm'], 'm_sw_sinks': out['m_sw_sinks'], 'm_gd_in_w': out['m_gd_in_w'], 'm_gd_out_w': out['m_gd_out_w'], 'm_gd_conv_w': out['m_gd_conv_w'], 'm_gd_a_log': out['m_gd_a_log'], 'm_gd_dt_bias': out['m_gd_dt_bias'], 'm_gd_onorm': out['m_gd_onorm'], 'v_hgrn_lb': out['v_hgrn_lb'], 'v_ada_w': out['v_ada_w'], 'v_ada_b': out['v_ada_b'], 'v_norm_g': out['v_norm_g'], 'v_hg_in_w': out['v_hg_in_w'], 'v_hg_out_w': out['v_hg_out_w'], 'v_hg_onorm': out['v_hg_onorm'], 'v_sw_in_w': out['v_sw_in_w'], 'v_sw_out_w': out['v_sw_out_w'], 'v_sw_qnorm': out['v_sw_qnorm'], 'v_sw_knorm': out['v_sw_knorm'], 'v_sw_sinks': out['v_sw_sinks'], 'v_gd_in_w': out['v_gd_in_w'], 'v_gd_out_w': out['v_gd_out_w'], 'v_gd_conv_w': out['v_gd_conv_w'], 'v_gd_a_log': out['v_gd_a_log'], 'v_gd_dt_bias': out['v_gd_dt_bias'], 'v_gd_onorm': out['v_gd_onorm']}


def _loss(weights, diff, rest, loss_target):
    with _jax.named_scope("forward"):
        args = {**rest, TWIN_DIFF_INPUT: diff, **{k: w.astype(_WEIGHT_DTYPES[k]) for k, w in weights.items()}}
        y = _forward(args)
    with _jax.named_scope("loss_head"):
        err = _jnp.square(y.astype(_jnp.float32) - loss_target)
        return 0.5 * _jnp.sum(_jnp.mean(err, axis=-1)) if err.ndim else 0.5 * err


def _adamw(w, g, m, v):
    m = ADAM_B1 * m + (1.0 - ADAM_B1) * g
    v = ADAM_B2 * v + (1.0 - ADAM_B2) * _jnp.square(g)
    m_hat = m / (1.0 - ADAM_B1 ** ADAM_STEP)
    v_hat = v / (1.0 - ADAM_B2 ** ADAM_STEP)
    delta = -ADAM_LR * (m_hat / (_jnp.sqrt(v_hat) + ADAM_EPS) + ADAM_WD * w)
    return delta, m, v


def reference(x, c, positions, hgrn_lb, ada_w, ada_b, norm_g, hg_in_w, hg_out_w, hg_onorm, sw_in_w, sw_out_w, sw_qnorm, sw_knorm, sw_sinks, gd_in_w, gd_out_w, gd_conv_w, gd_a_log, gd_dt_bias, gd_onorm, loss_target, m_hgrn_lb, m_ada_w, m_ada_b, m_norm_g, m_hg_in_w, m_hg_out_w, m_hg_onorm, m_sw_in_w, m_sw_out_w, m_sw_qnorm, m_sw_knorm, m_sw_sinks, m_gd_in_w, m_gd_out_w, m_gd_conv_w, m_gd_a_log, m_gd_dt_bias, m_gd_onorm, v_hgrn_lb, v_ada_w, v_ada_b, v_norm_g, v_hg_in_w, v_hg_out_w, v_hg_onorm, v_sw_in_w, v_sw_out_w, v_sw_qnorm, v_sw_knorm, v_sw_sinks, v_gd_in_w, v_gd_out_w, v_gd_conv_w, v_gd_a_log, v_gd_dt_bias, v_gd_onorm):
    given = dict(x=x, c=c, positions=positions, hgrn_lb=hgrn_lb, ada_w=ada_w, ada_b=ada_b, norm_g=norm_g, hg_in_w=hg_in_w, hg_out_w=hg_out_w, hg_onorm=hg_onorm, sw_in_w=sw_in_w, sw_out_w=sw_out_w, sw_qnorm=sw_qnorm, sw_knorm=sw_knorm, sw_sinks=sw_sinks, gd_in_w=gd_in_w, gd_out_w=gd_out_w, gd_conv_w=gd_conv_w, gd_a_log=gd_a_log, gd_dt_bias=gd_dt_bias, gd_onorm=gd_onorm, loss_target=loss_target, m_hgrn_lb=m_hgrn_lb, m_ada_w=m_ada_w, m_ada_b=m_ada_b, m_norm_g=m_norm_g, m_hg_in_w=m_hg_in_w, m_hg_out_w=m_hg_out_w, m_hg_onorm=m_hg_onorm, m_sw_in_w=m_sw_in_w, m_sw_out_w=m_sw_out_w, m_sw_qnorm=m_sw_qnorm, m_sw_knorm=m_sw_knorm, m_sw_sinks=m_sw_sinks, m_gd_in_w=m_gd_in_w, m_gd_out_w=m_gd_out_w, m_gd_conv_w=m_gd_conv_w, m_gd_a_log=m_gd_a_log, m_gd_dt_bias=m_gd_dt_bias, m_gd_onorm=m_gd_onorm, v_hgrn_lb=v_hgrn_lb, v_ada_w=v_ada_w, v_ada_b=v_ada_b, v_norm_g=v_norm_g, v_hg_in_w=v_hg_in_w, v_hg_out_w=v_hg_out_w, v_hg_onorm=v_hg_onorm, v_sw_in_w=v_sw_in_w, v_sw_out_w=v_sw_out_w, v_sw_qnorm=v_sw_qnorm, v_sw_knorm=v_sw_knorm, v_sw_sinks=v_sw_sinks, v_gd_in_w=v_gd_in_w, v_gd_out_w=v_gd_out_w, v_gd_conv_w=v_gd_conv_w, v_gd_a_log=v_gd_a_log, v_gd_dt_bias=v_gd_dt_bias, v_gd_onorm=v_gd_onorm)
    weights = {n: given[n] for n in TWIN_WEIGHTS}
    shared = {n: given[n] for n in SHARED_INPUTS}
    per_example = {n: given[n] for n in ['x', 'c', 'positions']}
    grad_fn = _jax.value_and_grad(_loss, argnums=(0, 1))

    def one_microbatch(ex, loss_target):
        ex = dict(ex)
        diff = ex.pop(TWIN_DIFF_INPUT)
        return grad_fn(weights, diff, {**shared, **ex}, loss_target)

    if N_MICROBATCH == 1:
        loss, (grad_w, grad_x) = one_microbatch(per_example, given["loss_target"])
    else:
        def body(carry, xs):
            loss_sum, grad_sum = carry
            l_k, (gw_k, gx_k) = one_microbatch(xs[0], xs[1])
            with _jax.named_scope("update"):
                return (loss_sum + l_k, _jax.tree.map(_jnp.add, grad_sum, gw_k)), gx_k

        init = (_jnp.zeros((), _jnp.float32), _jax.tree.map(_jnp.zeros_like, weights))
        (loss, grad_w), grad_x = _jax.lax.scan(body, init, (per_example, given["loss_target"]))
    with _jax.named_scope("update"):
        delta_w, new_m, new_v = {}, {}, {}
        for n in TWIN_WEIGHTS:
            delta_w[n], new_m[n], new_v[n] = _adamw(weights[n], grad_w[n], given["m_" + n], given["v_" + n])
    return (loss, grad_x, *[grad_w[n] for n in TWIN_WEIGHTS], *[delta_w[n] for n in TWIN_WEIGHTS],
            *[new_m[n] for n in TWIN_WEIGHTS], *[new_v[n] for n in TWIN_WEIGHTS])
```

```python
import functools

import jax
import jax.numpy as jnp
from jax import lax
from jax.experimental import pallas as pl
from jax.experimental.pallas import tpu as pltpu

F32, BF16 = jnp.float32, jnp.bfloat16
HI = lax.Precision.HIGHEST
MESH = pl.DeviceIdType.MESH
N_DEV = 8
D = 1024
EPS = 1e-6
NEG_INF = float("-inf")

HG_HEADS, HG_DH, HG_BLK, HG_SUB = 8, 128, 128, 16
SW_KV, SW_G, SW_DH, SW_BLK = 4, 4, 64, 128
SW_QW, SW_KVW = 1024, 256
GD_QK_HEADS, GD_DK, GD_CHUNK = 8, 128, 64
GD_QKW, GD_VW, GD_QKV = 1024, 2048, 4096
CONV_K = 4
ROPE_THETA = 10000.0

ADAM_LR, ADAM_B1, ADAM_B2, ADAM_EPS, ADAM_WD, ADAM_STEP = 0.001, 0.9, 0.999, 1e-08, 0.01, 10

VMEM_LIMIT = 56 * 1024 * 1024


def _params(**kw):
    return pltpu.CompilerParams(vmem_limit_bytes=VMEM_LIMIT, **kw)


def _dot(a, b, ca, cb):
    return lax.dot_general(a.astype(BF16), b.astype(BF16), (((ca,), (cb,)), ((), ())), preferred_element_type=F32)


def mm(a, b):
    return _dot(a, b, 1, 0)


def mm_nt(a, b):
    return _dot(a, b, 1, 1)


def mm_tn(a, b):
    return _dot(a, b, 0, 0)


def mmf(a, b):
    return lax.dot_general(a, b, (((1,), (0,)), ((), ())), precision=HI, preferred_element_type=F32)


def _split3(x):
    h1 = x.astype(BF16)
    r1 = x - h1.astype(F32)
    h2 = r1.astype(BF16)
    h3 = (r1 - h2.astype(F32)).astype(BF16)
    return h1, h2, h3


def _lin01_l(m, x):
    mb = m.astype(BF16)
    return sum(lax.dot_general(mb, p, (((1,), (0,)), ((), ())), preferred_element_type=F32) for p in _split3(x))


def _lin01_r(x, m):
    mb = m.astype(BF16)
    return sum(lax.dot_general(p, mb, (((1,), (0,)), ((), ())), preferred_element_type=F32) for p in _split3(x))


@jax.custom_vjp
def lin01_l(m, mt, x):
    return _lin01_l(m, x)


def _lin01_l_fwd(m, mt, x):
    return _lin01_l(m, x), (m, mt)


def _lin01_l_bwd(res, g):
    m, mt = res
    return jnp.zeros_like(m), jnp.zeros_like(mt), _lin01_l(mt, g)


lin01_l.defvjp(_lin01_l_fwd, _lin01_l_bwd)


@jax.custom_vjp
def lin01_r(x, m, mt):
    return _lin01_r(x, m)


def _lin01_r_fwd(x, m, mt):
    return _lin01_r(x, m), (m, mt)


def _lin01_r_bwd(res, g):
    m, mt = res
    return _lin01_r(g, mt), jnp.zeros_like(m), jnp.zeros_like(mt)


lin01_r.defvjp(_lin01_r_fwd, _lin01_r_bwd)


@functools.partial(jax.custom_vjp, nondiff_argnums=(1, 2))
def rows(x, start, size):
    return lax.slice_in_dim(x, start, start + size, axis=0)


def _rows_fwd(x, start, size):
    return lax.slice_in_dim(x, start, start + size, axis=0), jnp.zeros((x.shape[0], 1), F32)


def _rows_bwd(start, size, res, g):
    total = res.shape[0]
    parts = []
    if start > 0:
        parts.append(jnp.zeros((start, g.shape[1]), g.dtype))
    parts.append(g)
    if total - start - size > 0:
        parts.append(jnp.zeros((total - start - size, g.shape[1]), g.dtype))
    return (jnp.concatenate(parts, axis=0) if len(parts) > 1 else g,)


rows.defvjp(_rows_fwd, _rows_bwd)


def sigmoid(x):
    return jax.nn.sigmoid(x)


def silu(x):
    return x * jax.nn.sigmoid(x)


def softplus(x):
    return jnp.maximum(x, 0.0) + jnp.log1p(jnp.exp(-jnp.abs(x)))


def _iota(shape, dim):
    return lax.broadcasted_iota(jnp.int32, shape, dim)


def norm_mod_fn(g, sc, sh, x):
    r = lax.rsqrt(jnp.mean(x * x, axis=-1, keepdims=True) + EPS)
    return (x * r * g) * (1.0 + sc) + sh


def _row_spec(tm, f):
    return pl.BlockSpec((tm, f), lambda i: (i, 0))


def _const_spec(shape):
    nd = len(shape)
    return pl.BlockSpec(shape, lambda i: (0,) * nd)


def _tiled(name, fn, consts, rows_in, row_outs, acc_outs=(), tm=512):
    T = rows_in[0].shape[0]
    tm = min(tm, T)
    n_c, n_r, n_ro = len(consts), len(rows_in), len(row_outs)

    def body(*refs):
        c_refs, r_refs = refs[:n_c], refs[n_c:n_c + n_r]
        ro_refs, ao_refs = refs[n_c + n_r:n_c + n_r + n_ro], refs[n_c + n_r + n_ro:]
        outs = fn(*[r[...] for r in c_refs], *[r[...] for r in r_refs])
        for r, v in zip(ro_refs, outs[:n_ro]):
            r[...] = v.astype(r.dtype)
        if ao_refs:
            @pl.when(pl.program_id(0) == 0)
            def _():
                for r in ao_refs:
                    r[...] = jnp.zeros(r.shape, r.dtype)
            for r, v in zip(ao_refs, outs[n_ro:]):
                r[...] += v

    out_shape = [jax.ShapeDtypeStruct((T, f), dt) for f, dt in row_outs] + [jax.ShapeDtypeStruct(s, F32) for s in acc_outs]
    out_specs = [_row_spec(tm, f) for f, _ in row_outs] + [_const_spec(s) for s in acc_outs]
    in_specs = [_const_spec(c.shape) for c in consts] + [_row_spec(tm, r.shape[1]) for r in rows_in]
    return pl.pallas_call(
        body, name=name, grid=(T // tm,), in_specs=in_specs, out_specs=out_specs, out_shape=out_shape,
        compiler_params=_params(dimension_semantics=("arbitrary",)),
    )(*consts, *rows_in)


def norm_mod(name, x, g, sc, sh):
    (h,) = _tiled(name, lambda g, sc, sh, x: (norm_mod_fn(g, sc, sh, x),), [g, sc, sh], [x], [(D, BF16)])
    return h


def resid_norm(name, x, y, gate, g, sc, sh):
    def fn(gate, g, sc, sh, x, y):
        xn = x + gate * y
        return xn, norm_mod_fn(g, sc, sh, xn)
    return _tiled(name, fn, [gate, g, sc, sh], [x, y], [(D, F32), (D, BF16)])


def loss_head(name, x, y, tgt, gate):
    def fn(gate, x, y, tgt):
        err = x + gate * y - tgt
        dx = err * (1.0 / D)
        per_tok = jnp.sum(err * err, axis=-1, keepdims=True) * (0.5 / D)
        loss = jnp.sum(per_tok, axis=0, keepdims=True)
        return dx, gate * dx, jnp.broadcast_to(loss, (1, 128)), jnp.sum(dx * y, axis=0, keepdims=True)
    return _tiled(name, fn, [gate], [x, y, tgt], [(D, F32), (D, BF16)], [(1, 128), (1, D)])


def bwd_rowwise(name, x, dh, dxn, g, sc, sh, y_prev=None, gate_prev=None):
    with_prev = y_prev is not None

    def fn(*a):
        if with_prev:
            gate_p, g, sc, sh, x, dh, dxn, yp = a
        else:
            g, sc, sh, x, dh, dxn = a
        _, f = jax.vjp(norm_mod_fn, g, sc, sh, x)
        dg, dsc, dsh, dx = f(dh)
        dx = dx + dxn
        if with_prev:
            return dx, gate_p * dx, dg, dsc, dsh, jnp.sum(dx * yp, axis=0, keepdims=True)
        return dx, dg, dsc, dsh

    if with_prev:
        return _tiled(name, fn, [gate_prev, g, sc, sh], [x, dh, dxn, y_prev], [(D, F32), (D, BF16)], [(1, D)] * 4)
    return _tiled(name, fn, [g, sc, sh], [x, dh, dxn], [(D, F32)], [(1, D)] * 3)


def matmul(name, a, w, out_dtype=F32, tm=256):
    T, K = a.shape
    N = w.shape[1]
    tm = min(tm, T)

    def body(a_ref, w_ref, o_ref):
        o_ref[...] = mm(a_ref[...], w_ref[...]).astype(o_ref.dtype)

    return pl.pallas_call(
        body, name=name, grid=(T // tm,),
        in_specs=[_row_spec(tm, K), _const_spec((K, N))], out_specs=_row_spec(tm, N),
        out_shape=jax.ShapeDtypeStruct((T, N), out_dtype),
        compiler_params=_params(dimension_semantics=("arbitrary",)),
    )(a, w)


def matmul_nt(name, pairs, out_dtype=F32, tm=256):
    T = pairs[0][0].shape[0]
    K = pairs[0][1].shape[0]
    tm = min(tm, T)
    n = len(pairs)

    def body(*refs):
        o_ref = refs[-1]
        acc = None
        for i in range(n):
            p = mm_nt(refs[2 * i][...], refs[2 * i + 1][...])
            acc = p if acc is None else acc + p
        o_ref[...] = acc.astype(o_ref.dtype)

    in_specs, args = [], []
    for a, w in pairs:
        in_specs += [_row_spec(tm, a.shape[1]), _const_spec(w.shape)]
        args += [a, w]
    return pl.pallas_call(
        body, name=name, grid=(T // tm,), in_specs=in_specs, out_specs=_row_spec(tm, K),
        out_shape=jax.ShapeDtypeStruct((T, K), out_dtype),
        compiler_params=_params(dimension_semantics=("arbitrary",)),
    )(*args)


def matmul_tn(name, a, b, tm=512, tn=1024):
    T, K = a.shape
    N = b.shape[1]
    tm = min(tm, T)
    tn = max(t for t in range(128, min(tn, N) + 1, 128) if N % t == 0)

    def body(a_ref, b_ref, o_ref):
        @pl.when(pl.program_id(1) == 0)
        def _():
            o_ref[...] = jnp.zeros(o_ref.shape, F32)
        o_ref[...] += mm_tn(a_ref[...], b_ref[...])

    return pl.pallas_call(
        body, name=name, grid=(N // tn, T // tm),
        in_specs=[pl.BlockSpec((tm, K), lambda j, i: (i, 0)), pl.BlockSpec((tm, tn), lambda j, i: (i, j))],
        out_specs=pl.BlockSpec((K, tn), lambda j, i: (0, j)),
        out_shape=jax.ShapeDtypeStruct((K, N), F32),
        compiler_params=_params(dimension_semantics=("arbitrary", "arbitrary")),
    )(a, b)


def hg_head(tri, trit, lb, og, st, qp, fp, iv, z):
    blk = qp.shape[0]
    q = silu(qp)
    lf = jnp.log(lb + (1.0 - lb) * sigmoid(fp))
    k = (1.0 - lb) * sigmoid(-fp)
    r = _iota((HG_SUB, 1), 0)
    outs = []
    for j in range(blk // HG_SUB):
        qj, kj, vj, lfj = (rows(a, j * HG_SUB, HG_SUB) for a in (q, k, iv, lf))
        b = lin01_l(tri, trit, lfj)
        o = mm_nt(qj * jnp.exp(b), st)
        for s in range(HG_SUB):
            sel = r == s
            ks = jnp.sum(jnp.where(sel, kj, 0.0), axis=0, keepdims=True)
            bs = jnp.sum(jnp.where(sel, b, 0.0), axis=0, keepdims=True)
            vs = jnp.sum(jnp.where(sel, vj, 0.0), axis=0, keepdims=True)
            e = jnp.exp(jnp.where(r >= s, b - bs, NEG_INF))
            o = o + jnp.sum(qj * ks * e, axis=-1, keepdims=True) * vs
        bl = jnp.sum(lfj, axis=0, keepdims=True)
        st = st * jnp.exp(bl) + mm_tn(vj, kj * jnp.exp(bl - b))
        outs.append(o)
    o = jnp.concatenate(outs, axis=0)
    on = o * lax.rsqrt(jnp.mean(o * o, axis=-1, keepdims=True) + EPS) * og
    return st, on * silu(z)


def _tri(n):
    t = (_iota((n, n), 0) >= _iota((n, n), 1)).astype(F32)
    tt = (_iota((n, n), 0) <= _iota((n, n), 1)).astype(F32)
    return t, tt


def _lane(h, base=0):
    return pl.ds(pl.multiple_of(base + h * 128, 128), 128)


def hg_fwd(name, p, lb, og):
    T = p.shape[0]
    nb = T // HG_BLK

    def body(p_ref, lb_ref, og_ref, u_ref, st_ref, s_scr):
        @pl.when(pl.program_id(0) == 0)
        def _():
            s_scr[...] = jnp.zeros(s_scr.shape, F32)
        st_ref[0] = s_scr[...]
        tri, trit = _tri(HG_SUB)
        og_v = og_ref[...]

        def head(h, carry):
            st, u = hg_head(tri, trit, lb_ref[:, _lane(h)], og_v, s_scr[h],
                            p_ref[:, _lane(h)], p_ref[:, _lane(h, D)], p_ref[:, _lane(h, 2 * D)], p_ref[:, _lane(h, 3 * D)])
            s_scr[h] = st
            u_ref[:, _lane(h)] = u.astype(BF16)
            return carry

        lax.fori_loop(0, HG_HEADS, head, 0)

    return pl.pallas_call(
        body, name=name, grid=(nb,),
        in_specs=[_row_spec(HG_BLK, 4 * D), _const_spec((1, D)), _const_spec((1, HG_DH))],
        out_specs=[_row_spec(HG_BLK, D), pl.BlockSpec((1, HG_HEADS, HG_DH, HG_DH), lambda i: (i, 0, 0, 0))],
        out_shape=[jax.ShapeDtypeStruct((T, D), BF16), jax.ShapeDtypeStruct((nb, HG_HEADS, HG_DH, HG_DH), F32)],
        scratch_shapes=[pltpu.VMEM((HG_HEADS, HG_DH, HG_DH), F32)],
        compiler_params=_params(dimension_semantics=("arbitrary",)),
    )(p, lb, og)


def hg_bwd(name, p, lb, og, states, du):
    T = p.shape[0]
    nb = T // HG_BLK

    def body(p_ref, lb_ref, og_ref, st_ref, du_ref, dp_ref, dlb_ref, dog_ref, ds_scr):
        @pl.when(pl.program_id(0) == 0)
        def _():
            ds_scr[...] = jnp.zeros(ds_scr.shape, F32)
            dlb_ref[...] = jnp.zeros(dlb_ref.shape, F32)
            dog_ref[...] = jnp.zeros(dog_ref.shape, F32)
        tri, trit = _tri(HG_SUB)
        og_v = og_ref[...]

        def head(h, carry):
            fn = functools.partial(hg_head, tri, trit)
            _, f = jax.vjp(fn, lb_ref[:, _lane(h)], og_v, st_ref[0, h],
                           p_ref[:, _lane(h)], p_ref[:, _lane(h, D)], p_ref[:, _lane(h, 2 * D)], p_ref[:, _lane(h, 3 * D)])
            dlb, dog, dst, dq, df, di, dz = f((ds_scr[h], du_ref[:, _lane(h)].astype(F32)))
            ds_scr[h] = dst
            dp_ref[:, _lane(h)] = dq.astype(BF16)
            dp_ref[:, _lane(h, D)] = df.astype(BF16)
            dp_ref[:, _lane(h, 2 * D)] = di.astype(BF16)
            dp_ref[:, _lane(h, 3 * D)] = dz.astype(BF16)
            dlb_ref[:, _lane(h)] += dlb
            dog_ref[...] += dog
            return carry

        lax.fori_loop(0, HG_HEADS, head, 0)

    rev = lambda i: (nb - 1 - i, 0)
    return pl.pallas_call(
        body, name=name, grid=(nb,),
        in_specs=[pl.BlockSpec((HG_BLK, 4 * D), rev), _const_spec((1, D)), _const_spec((1, HG_DH)),
                  pl.BlockSpec((1, HG_HEADS, HG_DH, HG_DH), lambda i: (nb - 1 - i, 0, 0, 0)),
                  pl.BlockSpec((HG_BLK, D), rev)],
        out_specs=[pl.BlockSpec((HG_BLK, 4 * D), rev), _const_spec((1, D)), _const_spec((1, HG_DH))],
        out_shape=[jax.ShapeDtypeStruct((T, 4 * D), BF16), jax.ShapeDtypeStruct((1, D), F32),
                   jax.ShapeDtypeStruct((1, HG_DH), F32)],
        scratch_shapes=[pltpu.VMEM((HG_HEADS, HG_DH, HG_DH), F32)],
        compiler_params=_params(dimension_semantics=("arbitrary",)),
    )(p, lb, og, states, du)


def swa_head(qn, kn, sinks, kprev, vprev, qp4, kp, v, z4, cos4, sin4, cosk, sink_, hot4, mask, p64, p64t):
    def norm_rope(xp, g, cs, sn):
        y = xp * lax.rsqrt(jnp.mean(xp * xp, axis=-1, keepdims=True) + EPS) * g
        return y * cs + lin01_r(y, p64, p64t) * sn

    q = norm_rope(qp4, qn, cos4, sin4)
    k = norm_rope(kp, kn, cosk, sink_)
    k2 = jnp.concatenate([kprev, k], axis=0)
    v2 = jnp.concatenate([vprev, v], axis=0)
    s = jnp.where(mask, mm_nt(q, k2) * (SW_DH ** -0.5), NEG_INF)
    sink_col = jnp.concatenate(
        [jnp.broadcast_to(jnp.sum(sinks * hot, axis=-1, keepdims=True), (SW_BLK, 1)) for hot in hot4], axis=0)
    m = lax.stop_gradient(jnp.maximum(jnp.max(s, axis=-1, keepdims=True), sink_col))
    p = jnp.exp(s - m)
    p = p / (jnp.sum(p, axis=-1, keepdims=True) + jnp.exp(sink_col - m))
    o = mm(p, v2)
    return k, v, o * silu(z4)


def _swa_consts(first):
    qi = _iota((SW_G * SW_BLK, 2 * SW_BLK), 0) % SW_BLK
    kj = _iota((SW_G * SW_BLK, 2 * SW_BLK), 1)
    rel = qi + SW_BLK - kj
    mask = (rel >= 0) & (rel < SW_BLK) & ((kj >= SW_BLK) | jnp.logical_not(first))
    i, j = _iota((SW_DH, SW_DH), 0), _iota((SW_DH, SW_DH), 1)
    p64 = (i == (j + SW_DH // 2) % SW_DH).astype(F32)
    p64t = (j == (i + SW_DH // 2) % SW_DH).astype(F32)
    return mask, p64, p64t


def _swa_hot(h):
    lane = _iota((1, SW_KV * SW_G), 1)
    return [(lane == h * SW_G + g).astype(F32) for g in range(SW_G)]


def _stack_q(ref, h, base):
    return jnp.concatenate([ref[:, pl.ds(base + h * SW_G * SW_DH + g * SW_DH, SW_DH)] for g in range(SW_G)], axis=0)


Q0, K0, V0, Z0 = 0, SW_QW, SW_QW + SW_KVW, SW_QW + 2 * SW_KVW
SW_IN = 2 * SW_QW + 2 * SW_KVW


def swa_fwd(name, p, cos, sin, qn, kn, sinks):
    T = p.shape[0]
    nb = T // SW_BLK

    def body(p_ref, cos_ref, sin_ref, qn_ref, kn_ref, sk_ref, u_ref, ko_ref, kprev, vprev):
        first = pl.program_id(0) == 0

        @pl.when(first)
        def _():
            kprev[...] = jnp.zeros(kprev.shape, F32)
            vprev[...] = jnp.zeros(vprev.shape, F32)
        mask, p64, p64t = _swa_consts(first)
        cs, sn = cos_ref[...], sin_ref[...]
        cs4, sn4 = jnp.concatenate([cs] * SW_G, axis=0), jnp.concatenate([sn] * SW_G, axis=0)
        for h in range(SW_KV):
            k, v, u4 = swa_head(qn_ref[...], kn_ref[...], sk_ref[...], kprev[h], vprev[h],
                                _stack_q(p_ref, h, Q0), p_ref[:, pl.ds(K0 + h * SW_DH, SW_DH)],
                                p_ref[:, pl.ds(V0 + h * SW_DH, SW_DH)], _stack_q(p_ref, h, Z0),
                                cs4, sn4, cs, sn, _swa_hot(h), mask, p64, p64t)
            kprev[h] = k
            vprev[h] = v
            ko_ref[:, pl.ds(h * SW_DH, SW_DH)] = k
            for g in range(SW_G):
                u_ref[:, pl.ds(h * SW_G * SW_DH + g * SW_DH, SW_DH)] = u4[g * SW_BLK:(g + 1) * SW_BLK].astype(BF16)

    return pl.pallas_call(
        body, name=name, grid=(nb,),
        in_specs=[_row_spec(SW_BLK, SW_IN), _row_spec(SW_BLK, SW_DH), _row_spec(SW_BLK, SW_DH),
                  _const_spec((1, SW_DH)), _const_spec((1, SW_DH)), _const_spec((1, SW_KV * SW_G))],
        out_specs=[_row_spec(SW_BLK, SW_QW), _row_spec(SW_BLK, SW_KVW)],
        out_shape=[jax.ShapeDtypeStruct((T, SW_QW), BF16), jax.ShapeDtypeStruct((T, SW_KVW), F32)],
        scratch_shapes=[pltpu.VMEM((SW_KV, SW_BLK, SW_DH), F32), pltpu.VMEM((SW_KV, SW_BLK, SW_DH), F32)],
        compiler_params=_params(dimension_semantics=("arbitrary",)),
    )(p, cos, sin, qn, kn, sinks)


def swa_bwd(name, p, kpost, cos, sin, qn, kn, sinks, du):
    T = p.shape[0]
    nb = T // SW_BLK

    def body(p_ref, pprev_ref, kprev_ref, cos_ref, sin_ref, qn_ref, kn_ref, sk_ref, du_ref,
             dp_ref, dqn_ref, dkn_ref, dsk_ref, dk_scr, dv_scr):
        i = pl.program_id(0)
        first = i == nb - 1

        @pl.when(i == 0)
        def _():
            dk_scr[...] = jnp.zeros(dk_scr.shape, F32)
            dv_scr[...] = jnp.zeros(dv_scr.shape, F32)
            dqn_ref[...] = jnp.zeros(dqn_ref.shape, F32)
            dkn_ref[...] = jnp.zeros(dkn_ref.shape, F32)
            dsk_ref[...] = jnp.zeros(dsk_ref.shape, F32)
        mask, p64, p64t = _swa_consts(first)
        cs, sn = cos_ref[...], sin_ref[...]
        cs4, sn4 = jnp.concatenate([cs] * SW_G, axis=0), jnp.concatenate([sn] * SW_G, axis=0)
        for h in range(SW_KV):
            hot4 = _swa_hot(h)
            fn = lambda qn, kn, sk, kpv, vpv, qp4, kp, v, z4: swa_head(
                qn, kn, sk, kpv, vpv, qp4, kp, v, z4, cs4, sn4, cs, sn, hot4, mask, p64, p64t)
            _, f = jax.vjp(fn, qn_ref[...], kn_ref[...], sk_ref[...],
                           kprev_ref[:, pl.ds(h * SW_DH, SW_DH)], pprev_ref[:, pl.ds(V0 + h * SW_DH, SW_DH)],
                           _stack_q(p_ref, h, Q0), p_ref[:, pl.ds(K0 + h * SW_DH, SW_DH)],
                           p_ref[:, pl.ds(V0 + h * SW_DH, SW_DH)], _stack_q(p_ref, h, Z0))
            du4 = _stack_q(du_ref, h, 0).astype(F32)
            dqn, dkn, dsk, dkpv, dvpv, dq4, dkp, dv, dz4 = f((dk_scr[h], dv_scr[h], du4))
            dk_scr[h] = dkpv
            dv_scr[h] = dvpv
            dp_ref[:, pl.ds(K0 + h * SW_DH, SW_DH)] = dkp.astype(BF16)
            dp_ref[:, pl.ds(V0 + h * SW_DH, SW_DH)] = dv.astype(BF16)
            for g in range(SW_G):
                c0 = h * SW_G * SW_DH + g * SW_DH
                dp_ref[:, pl.ds(Q0 + c0, SW_DH)] = dq4[g * SW_BLK:(g + 1) * SW_BLK].astype(BF16)
                dp_ref[:, pl.ds(Z0 + c0, SW_DH)] = dz4[g * SW_BLK:(g + 1) * SW_BLK].astype(BF16)
            dqn_ref[...] += dqn
            dkn_ref[...] += dkn
            dsk_ref[...] += dsk

    rev = lambda i: (nb - 1 - i, 0)
    prev = lambda i: (jnp.maximum(nb - 2 - i, 0), 0)
    return pl.pallas_call(
        body, name=name, grid=(nb,),
        in_specs=[pl.BlockSpec((SW_BLK, SW_IN), rev), pl.BlockSpec((SW_BLK, SW_IN), prev),
                  pl.BlockSpec((SW_BLK, SW_KVW), prev), pl.BlockSpec((SW_BLK, SW_DH), rev), pl.BlockSpec((SW_BLK, SW_DH), rev),
                  _const_spec((1, SW_DH)), _const_spec((1, SW_DH)), _const_spec((1, SW_KV * SW_G)),
                  pl.BlockSpec((SW_BLK, SW_QW), rev)],
        out_specs=[pl.BlockSpec((SW_BLK, SW_IN), rev), _const_spec((1, SW_DH)), _const_spec((1, SW_DH)),
                   _const_spec((1, SW_KV * SW_G))],
        out_shape=[jax.ShapeDtypeStruct((T, SW_IN), BF16), jax.ShapeDtypeStruct((1, SW_DH), F32),
                   jax.ShapeDtypeStruct((1, SW_DH), F32), jax.ShapeDtypeStruct((1, SW_KV * SW_G), F32)],
        scratch_shapes=[pltpu.VMEM((SW_KV, SW_BLK, SW_DH), F32), pltpu.VMEM((SW_KV, SW_BLK, SW_DH), F32)],
        compiler_params=_params(dimension_semantics=("arbitrary",)),
    )(p, p, kpost, cos, sin, qn, kn, sinks, du)


CONV_TC, CONV_TL, HALO = 256, 1024, 8


def _conv_taps(xe, w_ref):
    acc = w_ref[pl.ds(CONV_K - 1, 1), :] * xe
    for k in range(CONV_K - 1):
        acc = acc + w_ref[pl.ds(k, 1), :] * pltpu.roll(xe, CONV_K - 1 - k, 0)
    return acc


def conv_fwd(name, x, w8):
    T, C = x.shape
    tc = min(CONV_TC, T)
    nt = T // tc

    def body(xp_ref, x_ref, w_ref, o_ref):
        prev = jnp.where(pl.program_id(0) == 0, 0.0, xp_ref[...])
        xe = jnp.concatenate([prev, x_ref[...]], axis=0)
        o_ref[...] = silu(_conv_taps(xe, w_ref)[HALO:])

    return pl.pallas_call(
        body, name=name, grid=(nt, C // CONV_TL),
        in_specs=[pl.BlockSpec((HALO, CONV_TL), lambda i, j: (jnp.maximum(i * (tc // HALO) - 1, 0), j)),
                  pl.BlockSpec((tc, CONV_TL), lambda i, j: (i, j)), pl.BlockSpec((HALO, CONV_TL), lambda i, j: (0, j))],
        out_specs=pl.BlockSpec((tc, CONV_TL), lambda i, j: (i, j)),
        out_shape=jax.ShapeDtypeStruct((T, C), F32),
        compiler_params=_params(dimension_semantics=("arbitrary", "arbitrary")),
    )(x, x, w8)


def conv_bwd(name, x, dy, w8):
    T, C = x.shape
    tc = min(CONV_TC, T)
    nt = T // tc
    n = tc + 2 * HALO

    def body(xp_ref, x_ref, xn_ref, dy_ref, dyn_ref, w_ref, dx_ref, dw_ref):
        i = pl.program_id(1)

        @pl.when(i == 0)
        def _():
            dw_ref[...] = jnp.zeros(dw_ref.shape, F32)
        prev = jnp.where(i == 0, 0.0, xp_ref[...])
        xe = jnp.concatenate([prev, x_ref[...], xn_ref[...]], axis=0)
        dye = jnp.concatenate([jnp.zeros((HALO, CONV_TL), F32), dy_ref[...], jnp.where(i == nt - 1, 0.0, dyn_ref[...])], axis=0)
        ce = _conv_taps(xe, w_ref)
        sg = sigmoid(ce)
        dce = dye * (sg * (1.0 + ce * (1.0 - sg)))
        dx = w_ref[pl.ds(CONV_K - 1, 1), :] * dce
        for k in range(CONV_K - 1):
            dx = dx + w_ref[pl.ds(k, 1), :] * pltpu.roll(dce, n - (CONV_K - 1 - k), 0)
        dx_ref[...] = dx[HALO:HALO + tc].astype(BF16)
        dcur = dce[HALO:HALO + tc]
        for k in range(CONV_K):
            xs = xe if k == CONV_K - 1 else pltpu.roll(xe, CONV_K - 1 - k, 0)
            dw_ref[pl.ds(k, 1), :] += jnp.sum(dcur * xs[HALO:HALO + tc], axis=0, keepdims=True)

    nh = T // HALO
    prev_map = lambda j, i: (jnp.maximum(i * (tc // HALO) - 1, 0), j)
    next_map = lambda j, i: (jnp.minimum((i + 1) * (tc // HALO), nh - 1), j)
    cur_map = lambda j, i: (i, j)
    return pl.pallas_call(
        body, name=name, grid=(C // CONV_TL, nt),
        in_specs=[pl.BlockSpec((HALO, CONV_TL), prev_map), pl.BlockSpec((tc, CONV_TL), cur_map),
                  pl.BlockSpec((HALO, CONV_TL), next_map), pl.BlockSpec((tc, CONV_TL), cur_map),
                  pl.BlockSpec((HALO, CONV_TL), next_map), pl.BlockSpec((HALO, CONV_TL), lambda j, i: (0, j))],
        out_specs=[pl.BlockSpec((tc, CONV_TL), cur_map), pl.BlockSpec((HALO, CONV_TL), lambda j, i: (0, j))],
        out_shape=[jax.ShapeDtypeStruct((T, C), BF16), jax.ShapeDtypeStruct((HALO, C), F32)],
        compiler_params=_params(dimension_semantics=("arbitrary", "arbitrary")),
    )(x, x, x, dy, dy, w8)


def _unit_lower_inverse(a, eye_f):
    n = -a
    t = eye_f + n
    p = n
    for _ in range(5):
        p = mmf(p, p)
        t = t + mmf(t, p)
    return t


def gd_group(al_row, dt_row, og, s0, s1, qc, kc, v0, v1, z0, z1, ab, hots, cst):
    lt_t, eye, incl, strict = cst
    eye_f = eye.astype(F32)
    q = qc * lax.rsqrt(jnp.sum(qc * qc, axis=-1, keepdims=True) + EPS) * (GD_DK ** -0.5)
    k = kc * lax.rsqrt(jnp.sum(kc * kc, axis=-1, keepdims=True) + EPS)
    qk = mm_nt(q, k)
    res = []
    for s, v, z, ha, hb in ((s0, v0, z0, hots[0], hots[1]), (s1, v1, z1, hots[2], hots[3])):
        a_col = jnp.sum(ab * ha, axis=-1, keepdims=True)
        b_col = jnp.sum(ab * hb, axis=-1, keepdims=True)
        al = jnp.sum(al_row * ha, axis=-1, keepdims=True)
        dt = jnp.sum(dt_row * ha, axis=-1, keepdims=True)
        beta = sigmoid(b_col)
        g = -jnp.exp(al) * softplus(a_col + dt)
        d_mat = jnp.broadcast_to(jnp.sum(g * lt_t, axis=0, keepdims=True), (GD_CHUNK, GD_CHUNK))
        d = jnp.sum(jnp.where(eye, d_mat, 0.0), axis=-1, keepdims=True)
        dec = jnp.exp(jnp.where(incl, d - d_mat, NEG_INF))
        kb = k * beta
        t_inv = _unit_lower_inverse(mm_nt(kb, k) * dec * strict, eye_f)
        u = mmf(t_inv, v * beta)
        w = mmf(t_inv, kb * jnp.exp(d))
        v_new = u - mm(w, s)
        o = mm(q * jnp.exp(d), s) + mm(qk * dec, v_new)
        dl = jnp.sum(g, axis=0, keepdims=True)
        s_new = s * jnp.exp(dl) + mm_tn(k * jnp.exp(dl - d), v_new)
        on = o * lax.rsqrt(jnp.mean(o * o, axis=-1, keepdims=True) + EPS) * og
        res += [s_new, on * silu(z)]
    return res[0], res[2], res[1], res[3]


def _gd_consts():
    r, c = _iota((GD_CHUNK, GD_CHUNK), 0), _iota((GD_CHUNK, GD_CHUNK), 1)
    return (r <= c).astype(F32), r == c, r >= c, (r > c).astype(F32)


def _gd_hots(gi):
    lane = _iota((1, 128), 1)
    h0 = 2 * gi
    return [(lane == h0).astype(F32), (lane == 16 + h0).astype(F32), (lane == h0 + 1).astype(F32), (lane == 17 + h0).astype(F32)]


GD_VH = 16


def gd_fwd(name, qkv, z, ab, al, dt, og):
    T = qkv.shape[0]
    nc = T // GD_CHUNK

    def body(qkv_ref, z_ref, ab_ref, al_ref, dt_ref, og_ref, u_ref, st_ref, s_scr):
        @pl.when(pl.program_id(0) == 0)
        def _():
            s_scr[...] = jnp.zeros(s_scr.shape, F32)
        st_ref[0] = s_scr[...]
        cst = _gd_consts()
        ab_v, al_v, dt_v, og_v = ab_ref[...], al_ref[...], dt_ref[...], og_ref[...]

        def grp(gi, carry):
            h0 = 2 * gi
            s0, s1, u0, u1 = gd_group(
                al_v, dt_v, og_v, s_scr[h0], s_scr[h0 + 1], qkv_ref[:, _lane(gi)], qkv_ref[:, _lane(gi, GD_QKW)],
                qkv_ref[:, _lane(h0, 2 * GD_QKW)], qkv_ref[:, _lane(h0 + 1, 2 * GD_QKW)],
                z_ref[:, _lane(h0)], z_ref[:, _lane(h0 + 1)], ab_v, _gd_hots(gi), cst)
            s_scr[h0] = s0
            s_scr[h0 + 1] = s1
            u_ref[:, _lane(h0)] = u0.astype(BF16)
            u_ref[:, _lane(h0 + 1)] = u1.astype(BF16)
            return carry

        lax.fori_loop(0, GD_QK_HEADS, grp, 0)

    return pl.pallas_call(
        body, name=name, grid=(nc,),
        in_specs=[_row_spec(GD_CHUNK, GD_QKV), _row_spec(GD_CHUNK, GD_VW), _row_spec(GD_CHUNK, 128),
                  _const_spec((1, 128)), _const_spec((1, 128)), _const_spec((1, 128))],
        out_specs=[_row_spec(GD_CHUNK, GD_VW), pl.BlockSpec((1, GD_VH, GD_DK, GD_DK), lambda i: (i, 0, 0, 0))],
        out_shape=[jax.ShapeDtypeStruct((T, GD_VW), BF16), jax.ShapeDtypeStruct((nc, GD_VH, GD_DK, GD_DK), F32)],
        scratch_shapes=[pltpu.VMEM((GD_VH, GD_DK, GD_DK), F32)],
        compiler_params=_params(dimension_semantics=("arbitrary",)),
    )(qkv, z, ab, al, dt, og)


def gd_bwd(name, qkv, z, ab, al, dt, og, states, du):
    T = qkv.shape[0]
    nc = T // GD_CHUNK

    def body(qkv_ref, z_ref, ab_ref, al_ref, dt_ref, og_ref, st_ref, du_ref,
             dqkv_ref, dz_ref, dab_ref, dal_ref, ddt_ref, dog_ref, ds_scr):
        @pl.when(pl.program_id(0) == 0)
        def _():
            ds_scr[...] = jnp.zeros(ds_scr.shape, F32)
            dal_ref[...] = jnp.zeros(dal_ref.shape, F32)
            ddt_ref[...] = jnp.zeros(ddt_ref.shape, F32)
            dog_ref[...] = jnp.zeros(dog_ref.shape, F32)
        dab_ref[...] = jnp.zeros(dab_ref.shape, F32)
        cst = _gd_consts()
        ab_v, al_v, dt_v, og_v = ab_ref[...], al_ref[...], dt_ref[...], og_ref[...]

        def grp(gi, carry):
            h0 = 2 * gi
            hots = _gd_hots(gi)
            fn = lambda *a: gd_group(*a, hots, cst)
            _, f = jax.vjp(fn, al_v, dt_v, og_v, st_ref[0, h0], st_ref[0, h0 + 1],
                           qkv_ref[:, _lane(gi)], qkv_ref[:, _lane(gi, GD_QKW)],
                           qkv_ref[:, _lane(h0, 2 * GD_QKW)], qkv_ref[:, _lane(h0 + 1, 2 * GD_QKW)],
                           z_ref[:, _lane(h0)], z_ref[:, _lane(h0 + 1)], ab_v)
            dal, ddt, dog, ds0, ds1, dq, dk, dv0, dv1, dz0, dz1, dab = f(
                (ds_scr[h0], ds_scr[h0 + 1], du_ref[:, _lane(h0)].astype(F32), du_ref[:, _lane(h0 + 1)].astype(F32)))
            ds_scr[h0] = ds0
            ds_scr[h0 + 1] = ds1
            dqkv_ref[:, _lane(gi)] = dq
            dqkv_ref[:, _lane(gi, GD_QKW)] = dk
            dqkv_ref[:, _lane(h0, 2 * GD_QKW)] = dv0
            dqkv_ref[:, _lane(h0 + 1, 2 * GD_QKW)] = dv1
            dz_ref[:, _lane(h0)] = dz0.astype(BF16)
            dz_ref[:, _lane(h0 + 1)] = dz1.astype(BF16)
            dab_ref[...] += dab
            dal_ref[...] += dal
            ddt_ref[...] += ddt
            dog_ref[...] += dog
            return carry

        lax.fori_loop(0, GD_QK_HEADS, grp, 0)

    rev = lambda i: (nc - 1 - i, 0)
    return pl.pallas_call(
        body, name=name, grid=(nc,),
        in_specs=[pl.BlockSpec((GD_CHUNK, GD_QKV), rev), pl.BlockSpec((GD_CHUNK, GD_VW), rev), pl.BlockSpec((GD_CHUNK, 128), rev),
                  _const_spec((1, 128)), _const_spec((1, 128)), _const_spec((1, 128)),
                  pl.BlockSpec((1, GD_VH, GD_DK, GD_DK), lambda i: (nc - 1 - i, 0, 0, 0)),
                  pl.BlockSpec((GD_CHUNK, GD_VW), rev)],
        out_specs=[pl.BlockSpec((GD_CHUNK, GD_QKV), rev), pl.BlockSpec((GD_CHUNK, GD_VW), rev), pl.BlockSpec((GD_CHUNK, 128), rev),
                   _const_spec((1, 128)), _const_spec((1, 128)), _const_spec((1, 128))],
        out_shape=[jax.ShapeDtypeStruct((T, GD_QKV), F32), jax.ShapeDtypeStruct((T, GD_VW), BF16),
                   jax.ShapeDtypeStruct((T, 128), F32)] + [jax.ShapeDtypeStruct((1, 128), F32)] * 3,
        scratch_shapes=[pltpu.VMEM((GD_VH, GD_DK, GD_DK), F32)],
        compiler_params=_params(dimension_semantics=("arbitrary",)),
    )(qkv, z, ab, al, dt, og, states, du)


def _my_pos():
    return lax.axis_index("x"), lax.axis_index("y"), lax.axis_index("c")


def _peers(pos):
    out = []
    for k in range(1, N_DEV):
        dev = tuple(1 - p if (k >> s) & 1 else p for p, s in zip(pos, (2, 1, 0)))
        out.append((dev, 4 * dev[0] + 2 * dev[1] + dev[2]))
    return out


_ANY = pl.BlockSpec(memory_space=pl.ANY)
_COMM_SCRATCH = [pltpu.SemaphoreType.DMA((N_DEV - 1,)), pltpu.SemaphoreType.DMA((N_DEV - 1,)), pltpu.SemaphoreType.DMA(())]


def all_gather(name, x):
    def body(x_ref, o_ref, send_sems, recv_sems, local_sem):
        pos = _my_pos()
        me = 4 * pos[0] + 2 * pos[1] + pos[2]
        peers = _peers(pos)
        mine = pltpu.make_async_copy(x_ref, o_ref.at[me], local_sem)
        mine.start()
        sends = [pltpu.make_async_remote_copy(x_ref, o_ref.at[me], send_sems.at[k], recv_sems.at[k],
                                              device_id=dev, device_id_type=MESH) for k, (dev, _) in enumerate(peers)]
        for cp in sends:
            cp.start()
        for k, (dev, pid) in enumerate(peers):
            pltpu.make_async_remote_copy(x_ref, o_ref.at[pid], send_sems.at[k], recv_sems.at[k],
                                         device_id=dev, device_id_type=MESH).wait_recv()
        for cp in sends:
            cp.wait_send()
        mine.wait()

    return pl.pallas_call(
        body, name=name, out_shape=jax.ShapeDtypeStruct((N_DEV,) + x.shape, x.dtype),
        in_specs=[_ANY], out_specs=_ANY, scratch_shapes=_COMM_SCRATCH,
    )(x)


def all_to_all(name, parts):
    def body(x_ref, o_ref, send_sems, recv_sems, local_sem):
        pos = _my_pos()
        me = 4 * pos[0] + 2 * pos[1] + pos[2]
        peers = _peers(pos)
        mine = pltpu.make_async_copy(x_ref.at[me], o_ref.at[me], local_sem)
        mine.start()
        sends = [pltpu.make_async_remote_copy(x_ref.at[pid], o_ref.at[me], send_sems.at[k], recv_sems.at[k],
                                              device_id=dev, device_id_type=MESH) for k, (dev, pid) in enumerate(peers)]
        for cp in sends:
            cp.start()
        for k, (dev, pid) in enumerate(peers):
            pltpu.make_async_remote_copy(x_ref.at[pid], o_ref.at[pid], send_sems.at[k], recv_sems.at[k],
                                         device_id=dev, device_id_type=MESH).wait_recv()
        for cp in sends:
            cp.wait_send()
        mine.wait()

    return pl.pallas_call(
        body, name=name, out_shape=jax.ShapeDtypeStruct(parts.shape, parts.dtype),
        in_specs=[_ANY], out_specs=_ANY, scratch_shapes=_COMM_SCRATCH,
    )(parts)


def lb_rows(r0, r1, r2, r3):
    mx = jnp.maximum(jnp.maximum(r0, r1), jnp.maximum(r2, r3))
    e = [jnp.exp(r - mx) for r in (r0, r1, r2, r3)]
    inv = 1.0 / (e[0] + e[1] + e[2] + e[3])
    c0 = e[0] * inv
    c1 = c0 + e[1] * inv
    c2 = c1 + e[2] * inv
    c3 = c2 + e[3] * inv
    return c0 - c0, c1 - c0, c2 - c0, c3 - c0


def mod_partial(name, c_all, ada_w):
    n, _, w = ada_w.shape

    def body(c_ref, w_ref, o_ref):
        o_ref[0] = mm(c_ref[...], w_ref[0])

    return pl.pallas_call(
        body, name=name, grid=(n,),
        in_specs=[_const_spec((N_DEV, D)), pl.BlockSpec((1, D, w), lambda i: (i, 0, 0))],
        out_specs=pl.BlockSpec((1, N_DEV, w), lambda i: (i, 0, 0)),
        out_shape=jax.ShapeDtypeStruct((n, N_DEV, w), F32),
    )(c_all, ada_w)


def prep(name, modp, ada_b, hgrn_lb):
    def body(mp_ref, b_ref, lb_ref, mod_ref, lbo_ref):
        mod_ref[...] = mp_ref[...] + b_ref[...]
        out = lb_rows(*[lb_ref[pl.ds(i, 1), :] for i in range(4)])
        for i in range(4):
            lbo_ref[pl.ds(i, 1), :] = out[i]

    return pl.pallas_call(
        body, name=name,
        out_shape=[jax.ShapeDtypeStruct(modp.shape, F32), jax.ShapeDtypeStruct(hgrn_lb.shape, F32)],
    )(modp, ada_b, hgrn_lb)


def lb_grad(name, hgrn_lb, dlb_parts):
    def body(lb_ref, d_ref, o_ref):
        cts = []
        for i in range(4):
            acc = d_ref[0, pl.ds(i, 1), :]
            for p in range(1, N_DEV):
                acc = acc + d_ref[p, pl.ds(i, 1), :]
            cts.append(acc)
        _, f = jax.vjp(lb_rows, *[lb_ref[pl.ds(i, 1), :] for i in range(4)])
        for i, g in enumerate(f(tuple(cts))):
            o_ref[pl.ds(i, 1), :] = g

    return pl.pallas_call(body, name=name, out_shape=jax.ShapeDtypeStruct(hgrn_lb.shape, F32))(hgrn_lb, dlb_parts)


def ada_grad(name, c_t, dm):
    n, _, w = dm.shape

    def body(c_ref, d_ref, o_ref):
        acc = c_ref[:, pl.ds(0, 1)] * d_ref[0, pl.ds(0, 1), :]
        for b in range(1, N_DEV):
            acc = acc + c_ref[:, pl.ds(b, 1)] * d_ref[0, pl.ds(b, 1), :]
        o_ref[0] = acc

    return pl.pallas_call(
        body, name=name, grid=(n,),
        in_specs=[_const_spec((D, N_DEV)), pl.BlockSpec((1, N_DEV, w), lambda i: (i, 0, 0))],
        out_specs=pl.BlockSpec((1, D, w), lambda i: (i, 0, 0)),
        out_shape=jax.ShapeDtypeStruct((n, D, w), F32),
    )(c_t, dm)


ADAMW_ROWS = 256


def adamw(name, w, m, v, gparts):
    shape = w.shape
    L = shape[-1]
    R = 1
    for s in shape[:-1]:
        R *= s
    P = gparts.shape[0]
    tr = min(R, ADAMW_ROWS)

    def body(w_ref, m_ref, v_ref, g_ref, go_ref, d_ref, mo_ref, vo_ref):
        g = g_ref[0]
        for p in range(1, P):
            g = g + g_ref[p]
        mn = ADAM_B1 * m_ref[...] + (1.0 - ADAM_B1) * g
        vn = ADAM_B2 * v_ref[...] + (1.0 - ADAM_B2) * (g * g)
        m_hat = mn / (1.0 - ADAM_B1 ** ADAM_STEP)
        v_hat = vn / (1.0 - ADAM_B2 ** ADAM_STEP)
        go_ref[...] = g
        d_ref[...] = -ADAM_LR * (m_hat / (jnp.sqrt(v_hat) + ADAM_EPS) + ADAM_WD * w_ref[...])
        mo_ref[...] = mn
        vo_ref[...] = vn

    spec = pl.BlockSpec((tr, L), lambda i: (i, 0))
    outs = pl.pallas_call(
        body, name=name, grid=(R // tr,),
        in_specs=[spec, spec, spec, pl.BlockSpec((P, tr, L), lambda i: (0, i, 0))],
        out_specs=[spec] * 4, out_shape=[jax.ShapeDtypeStruct((R, L), F32)] * 4,
        compiler_params=_params(dimension_semantics=("arbitrary",)),
    )(w.reshape(R, L), m.reshape(R, L), v.reshape(R, L), gparts.reshape(P, R, L))
    return [o.reshape(shape) for o in outs]


def _gather_cols(name, w):
    g = all_gather(name, w.astype(BF16))
    n, k, w8 = w.shape
    return jnp.transpose(g, (1, 2, 0, 3)).reshape(n, k, N_DEV * w8)


def _gather_rows(name, w):
    g = all_gather(name, w.astype(BF16))
    n, k8, nn = w.shape
    return jnp.transpose(g, (1, 0, 2, 3)).reshape(n, N_DEV * k8, nn)


def _scatter_cols(dw):
    n, k, nn = dw.shape
    return jnp.transpose(dw.reshape(n, k, N_DEV, nn // N_DEV), (2, 0, 1, 3))


def _scatter_rows(dw):
    n, k, nn = dw.shape
    return jnp.transpose(dw.reshape(n, N_DEV, k // N_DEV, nn), (1, 0, 2, 3))


def _pad_lanes(v, width=128):
    return jnp.pad(v, ((0, 0), (0, width - v.shape[1])))


def _stack_rows(parts):
    n, L = len(parts), parts[0].shape[1]
    r = lax.broadcasted_iota(jnp.int32, (n, L), 0)
    out = jnp.zeros((n, L), parts[0].dtype)
    for i, p in enumerate(parts):
        out = jnp.where(r == i, jnp.broadcast_to(p, (n, L)), out)
    return out


def kernel(x, c, positions, hgrn_lb, ada_w, ada_b, norm_g, hg_in_w, hg_out_w, hg_onorm, sw_in_w, sw_out_w, sw_qnorm, sw_knorm, sw_sinks, gd_in_w, gd_out_w, gd_conv_w, gd_a_log, gd_dt_bias, gd_onorm, loss_target, m_hgrn_lb, m_ada_w, m_ada_b, m_norm_g, m_hg_in_w, m_hg_out_w, m_hg_onorm, m_sw_in_w, m_sw_out_w, m_sw_qnorm, m_sw_knorm, m_sw_sinks, m_gd_in_w, m_gd_out_w, m_gd_conv_w, m_gd_a_log, m_gd_dt_bias, m_gd_onorm, v_hgrn_lb, v_ada_w, v_ada_b, v_norm_g, v_hg_in_w, v_hg_out_w, v_hg_onorm, v_sw_in_w, v_sw_out_w, v_sw_qnorm, v_sw_knorm, v_sw_sinks, v_gd_in_w, v_gd_out_w, v_gd_conv_w, v_gd_a_log, v_gd_dt_bias, v_gd_onorm):
    pos = _my_pos()
    me = 4 * pos[0] + 2 * pos[1] + pos[2]
    x0 = x[0]
    tgt = loss_target[0]
    n_layers = norm_g.shape[0]
    aw = ada_w.shape[2]

    c_all = all_gather("ag_c", c)[:, 0, :]
    modp = all_gather("ag_mod", mod_partial("mod_partial", c_all, ada_w))
    modp_mine = lax.dynamic_index_in_dim(modp, me, axis=2, keepdims=False)
    modp_mine = jnp.transpose(modp_mine, (1, 0, 2)).reshape(n_layers, N_DEV * aw)
    mod, lb_all = prep("prep", modp_mine, ada_b, hgrn_lb)
    shift, scale, gate = mod[:, :D], mod[:, D:2 * D], mod[:, 2 * D:]
    row = lambda a, i: a[i:i + 1]

    w_hg_in = _gather_cols("ag_hg_in", hg_in_w)
    w_hg_out = _gather_rows("ag_hg_out", hg_out_w)
    w_sw_in = _gather_cols("ag_sw_in", sw_in_w)[0]
    w_sw_out = _gather_rows("ag_sw_out", sw_out_w)[0]
    w_gd_in = _gather_cols("ag_gd_in", gd_in_w)[0]
    w_gd_out = _gather_rows("ag_gd_out", gd_out_w)[0]
    w_gd_qkv, w_gd_z = w_gd_in[:, :GD_QKV], w_gd_in[:, GD_QKV:GD_QKV + GD_VW]
    w_gd_ab = _pad_lanes(w_gd_in[:, GD_QKV + GD_VW:])
    conv_all = all_gather("ag_conv", gd_conv_w)
    conv_w = jnp.transpose(conv_all[:, 0], (1, 0, 2)).reshape(CONV_K, GD_QKV)
    conv_w8 = jnp.pad(conv_w, ((0, HALO - CONV_K), (0, 0)))
    gd_al, gd_dt = _pad_lanes(gd_a_log), _pad_lanes(gd_dt_bias)

    inv_freq = ROPE_THETA ** (-jnp.arange(0, SW_DH, 2, dtype=F32) / SW_DH)
    ang = positions[0].astype(F32)[:, None] * inv_freq
    cos, sin = jnp.cos(ang), jnp.sin(ang)
    cos64, sin64 = jnp.concatenate([cos, cos], axis=-1), jnp.concatenate([-sin, sin], axis=-1)

    xs, hs, ys, saved = [x0], [], [], []
    h = norm_mod("norm0", x0, row(norm_g, 0), row(scale, 0), row(shift, 0))
    for i in range(n_layers):
        kind, j = i % 3, i // 3
        hs.append(h)
        if kind == 0:
            p = matmul(f"hg_in{i}", h, w_hg_in[j])
            u, st = hg_fwd(f"hg_fwd{i}", p, row(lb_all, i), row(hg_onorm, j))
            y = matmul(f"hg_out{i}", u, w_hg_out[j])
            saved.append((p, u, st))
        elif kind == 1:
            p = matmul(f"sw_in{i}", h, w_sw_in)
            u, kpost = swa_fwd(f"sw_fwd{i}", p, cos64, sin64, row(sw_qnorm, j), row(sw_knorm, j), row(sw_sinks, j))
            y = matmul(f"sw_out{i}", u, w_sw_out)
            saved.append((p, u, kpost))
        else:
            xq = matmul(f"gd_qkv{i}", h, w_gd_qkv)
            zz = matmul(f"gd_z{i}", h, w_gd_z)
            ab = matmul(f"gd_ab{i}", h, w_gd_ab)
            cv = conv_fwd(f"gd_conv{i}", xq, conv_w8)
            u, st = gd_fwd(f"gd_fwd{i}", cv, zz, ab, gd_al, gd_dt, row(gd_onorm, j))
            y = matmul(f"gd_out{i}", u, w_gd_out)
            saved.append((xq, zz, ab, cv, u, st))
        ys.append(y)
        if i + 1 < n_layers:
            xn, h = resid_norm(f"resid{i}", xs[i], y, row(gate, i), row(norm_g, i + 1), row(scale, i + 1), row(shift, i + 1))
            xs.append(xn)

    last = n_layers - 1
    dx, dy, loss_acc, dgate_last = loss_head("loss", xs[last], ys[last], tgt, row(gate, last))
    loss = lax.psum(loss_acc[0, 0], ("x", "y", "c"))

    dgate = [None] * n_layers
    dgate[last] = dgate_last
    dg_norm, dscale, dshift = [None] * n_layers, [None] * n_layers, [None] * n_layers
    dlb = [jnp.zeros((1, D), F32)] * n_layers
    d_hg_in, d_hg_out, d_hg_on = [None] * 2, [None] * 2, [None] * 2
    for i in range(last, -1, -1):
        kind, j = i % 3, i // 3
        h = hs[i]
        if kind == 0:
            p, u, st = saved[i]
            du = matmul_nt(f"hg_du{i}", [(dy, w_hg_out[j])])
            d_hg_out[j] = matmul_tn(f"hg_dwo{i}", u, dy)
            dp, dlb_i, dog = hg_bwd(f"hg_bwd{i}", p, row(lb_all, i), row(hg_onorm, j), st, du)
            dlb[i] = dlb_i
            d_hg_on[j] = dog
            dh = matmul_nt(f"hg_dh{i}", [(dp, w_hg_in[j])])
            d_hg_in[j] = matmul_tn(f"hg_dwi{i}", h, dp)
        elif kind == 1:
            p, u, kpost = saved[i]
            du = matmul_nt(f"sw_du{i}", [(dy, w_sw_out)])
            d_sw_out = matmul_tn(f"sw_dwo{i}", u, dy)
            dp, d_qn, d_kn, d_sk = swa_bwd(f"sw_bwd{i}", p, kpost, cos64, sin64, row(sw_qnorm, j), row(sw_knorm, j),
                                           row(sw_sinks, j), du)
            dh = matmul_nt(f"sw_dh{i}", [(dp, w_sw_in)])
            d_sw_in = matmul_tn(f"sw_dwi{i}", h, dp)
        else:
            xq, zz, ab, cv, u, st = saved[i]
            du = matmul_nt(f"gd_du{i}", [(dy, w_gd_out)])
            d_gd_out = matmul_tn(f"gd_dwo{i}", u, dy)
            dcv, dz, dab, d_al, d_dt, d_gd_on = gd_bwd(f"gd_bwd{i}", cv, zz, ab, gd_al, gd_dt, row(gd_onorm, j), st, du)
            dxq, d_conv8 = conv_bwd(f"gd_dconv{i}", xq, dcv, conv_w8)
            dh = matmul_nt(f"gd_dh{i}", [(dxq, w_gd_qkv), (dz, w_gd_z), (dab, w_gd_ab)])
            d_gd_in = jnp.concatenate([matmul_tn(f"gd_dwq{i}", h, dxq), matmul_tn(f"gd_dwz{i}", h, dz),
                                       matmul_tn(f"gd_dwab{i}", h, dab)[:, :2 * GD_VH]], axis=1)
        if i > 0:
            dx, dy, dg_norm[i], dscale[i], dshift[i], dgate[i - 1] = bwd_rowwise(
                f"bwd_row{i}", xs[i], dh, dx, row(norm_g, i), row(scale, i), row(shift, i), ys[i - 1], row(gate, i - 1))
        else:
            dx, dg_norm[i], dscale[i], dshift[i] = bwd_rowwise(
                f"bwd_row{i}", xs[i], dh, dx, row(norm_g, i), row(scale, i), row(shift, i))
    grad_x = dx[None]

    dmod = jnp.concatenate([_stack_rows(dshift), _stack_rows(dscale), _stack_rows(dgate)], axis=1)
    misc = _stack_rows(d_hg_on + [d_gd_on, _pad_lanes(d_qn), _pad_lanes(d_kn), _pad_lanes(d_sk), d_al, d_dt])
    small = jnp.concatenate([_stack_rows(dlb).reshape(-1, 128), _stack_rows(dg_norm).reshape(-1, 128),
                             dmod.reshape(-1, 128), misc], axis=0)
    small_all = all_gather("ag_small", small)
    n_lb = n_layers * D // 128
    n_mod = n_layers * 3 * D // 128
    o = 0
    dlb_parts = small_all[:, o:o + n_lb].reshape(N_DEV, n_layers, D); o += n_lb
    dgn_parts = small_all[:, o:o + n_lb].reshape(N_DEV, n_layers, D); o += n_lb
    dmod_parts = small_all[:, o:o + n_mod].reshape(N_DEV, n_layers, 3 * D); o += n_mod
    dhgon_parts = small_all[:, o:o + 2]; o += 2
    dgdon_parts = small_all[:, o:o + 1]; o += 1
    dqn_parts = small_all[:, o:o + 1, :SW_DH]; o += 1
    dkn_parts = small_all[:, o:o + 1, :SW_DH]; o += 1
    dsk_parts = small_all[:, o:o + 1, :SW_KV * SW_G]; o += 1
    dal_parts = small_all[:, o:o + 1, :GD_VH]; o += 1
    ddt_parts = small_all[:, o:o + 1, :GD_VH]; o += 1

    g_lb = lb_grad("lb_grad", hgrn_lb, dlb_parts)
    dm_mine = lax.dynamic_slice_in_dim(dmod_parts, me * aw, aw, axis=2)
    g_ada_w = ada_grad("ada_grad", jnp.transpose(c_all), jnp.transpose(dm_mine, (1, 0, 2)))

    res = {}
    res["hgrn_lb"] = adamw("aw_hgrn_lb", hgrn_lb, m_hgrn_lb, v_hgrn_lb, g_lb[None])
    res["ada_w"] = adamw("aw_ada_w", ada_w, m_ada_w, v_ada_w, g_ada_w[None])
    res["ada_b"] = adamw("aw_ada_b", ada_b, m_ada_b, v_ada_b, dmod_parts)
    res["norm_g"] = adamw("aw_norm_g", norm_g, m_norm_g, v_norm_g, dgn_parts)
    res["hg_in_w"] = adamw("aw_hg_in", hg_in_w, m_hg_in_w, v_hg_in_w,
                           all_to_all("rs_hg_in", _scatter_cols(jnp.stack(d_hg_in))))
    res["hg_out_w"] = adamw("aw_hg_out", hg_out_w, m_hg_out_w, v_hg_out_w,
                            all_to_all("rs_hg_out", _scatter_rows(jnp.stack(d_hg_out))))
    res["hg_onorm"] = adamw("aw_hg_onorm", hg_onorm, m_hg_onorm, v_hg_onorm, dhgon_parts)
    res["sw_in_w"] = adamw("aw_sw_in", sw_in_w, m_sw_in_w, v_sw_in_w, all_to_all("rs_sw_in", _scatter_cols(d_sw_in[None])))
    res["sw_out_w"] = adamw("aw_sw_out", sw_out_w, m_sw_out_w, v_sw_out_w, all_to_all("rs_sw_out", _scatter_rows(d_sw_out[None])))
    res["sw_qnorm"] = adamw("aw_sw_qn", sw_qnorm, m_sw_qnorm, v_sw_qnorm, dqn_parts)
    res["sw_knorm"] = adamw("aw_sw_kn", sw_knorm, m_sw_knorm, v_sw_knorm, dkn_parts)
    res["sw_sinks"] = adamw("aw_sw_sinks", sw_sinks, m_sw_sinks, v_sw_sinks, dsk_parts)
    res["gd_in_w"] = adamw("aw_gd_in", gd_in_w, m_gd_in_w, v_gd_in_w, all_to_all("rs_gd_in", _scatter_cols(d_gd_in[None])))
    res["gd_out_w"] = adamw("aw_gd_out", gd_out_w, m_gd_out_w, v_gd_out_w, all_to_all("rs_gd_out", _scatter_rows(d_gd_out[None])))
    res["gd_conv_w"] = adamw("aw_gd_conv", gd_conv_w, m_gd_conv_w, v_gd_conv_w,
                             all_to_all("rs_gd_conv", _scatter_cols(d_conv8[None, :CONV_K])))
    res["gd_a_log"] = adamw("aw_gd_alog", gd_a_log, m_gd_a_log, v_gd_a_log, dal_parts)
    res["gd_dt_bias"] = adamw("aw_gd_dt", gd_dt_bias, m_gd_dt_bias, v_gd_dt_bias, ddt_parts)
    res["gd_onorm"] = adamw("aw_gd_onorm", gd_onorm, m_gd_onorm, v_gd_onorm, dgdon_parts)

    order = ["hgrn_lb", "ada_w", "ada_b", "norm_g", "hg_in_w", "hg_out_w", "hg_onorm", "sw_in_w", "sw_out_w", "sw_qnorm",
             "sw_knorm", "sw_sinks", "gd_in_w", "gd_out_w", "gd_conv_w", "gd_a_log", "gd_dt_bias", "gd_onorm"]
    outs = [loss, grad_x]
    for part in range(4):
        outs += [res[n][part] for n in order]
    return tuple(outs)
```

```python
import functools

import jax
import jax.numpy as jnp
from jax import lax
from jax.experimental import pallas as pl
from jax.experimental.pallas import tpu as pltpu

F32, BF16 = jnp.float32, jnp.bfloat16
HI = lax.Precision.HIGHEST
MESH = pl.DeviceIdType.MESH
N_DEV = 8
D = 1024
EPS = 1e-6
NEG_INF = float("-inf")

HG_HEADS, HG_DH, HG_BLK, HG_SUB, HG_HB = 8, 128, 128, 8, 4
SW_KV, SW_G, SW_DH, SW_BLK = 4, 4, 64, 128
SW_QW, SW_KVW = 1024, 256
GD_QK_HEADS, GD_DK, GD_CHUNK = 8, 128, 64
GD_QKW, GD_VW, GD_QKV = 1024, 2048, 4096
CONV_K = 4
ROPE_THETA = 10000.0

ADAM_LR, ADAM_B1, ADAM_B2, ADAM_EPS, ADAM_WD, ADAM_STEP = 0.001, 0.9, 0.999, 1e-08, 0.01, 10

VMEM_LIMIT = 56 * 1024 * 1024


def _params(**kw):
    return pltpu.CompilerParams(vmem_limit_bytes=VMEM_LIMIT, **kw)


def _dot(a, b, ca, cb):
    return lax.dot_general(a.astype(BF16), b.astype(BF16), (((ca,), (cb,)), ((), ())), preferred_element_type=F32)


def mm(a, b):
    return _dot(a, b, 1, 0)


def mm_nt(a, b):
    return _dot(a, b, 1, 1)


def mm_tn(a, b):
    return _dot(a, b, 0, 0)


def mmf(a, b):
    return lax.dot_general(a, b, (((1,), (0,)), ((), ())), precision=HI, preferred_element_type=F32)


def _split3(x):
    h1 = x.astype(BF16)
    r1 = x - h1.astype(F32)
    h2 = r1.astype(BF16)
    h3 = (r1 - h2.astype(F32)).astype(BF16)
    return h1, h2, h3


def _lin01_l(m, x):
    mb = m.astype(BF16)
    return sum(lax.dot_general(mb, p, (((1,), (0,)), ((), ())), preferred_element_type=F32) for p in _split3(x))


def _lin01_r(x, m):
    mb = m.astype(BF16)
    return sum(lax.dot_general(p, mb, (((1,), (0,)), ((), ())), preferred_element_type=F32) for p in _split3(x))


def _bdot(a, b, ca, cb, prec=None):
    return lax.dot_general(a, b, (((ca,), (cb,)), ((0,), (0,))), precision=prec, preferred_element_type=F32)


def bmm(a, b):
    return _bdot(a.astype(BF16), b.astype(BF16), 2, 1)


def bmm_nt(a, b):
    return _bdot(a.astype(BF16), b.astype(BF16), 2, 2)


def bmm_tn(a, b):
    return _bdot(a.astype(BF16), b.astype(BF16), 1, 1)


def bmmf(a, b):
    return _bdot(a, b, 2, 1, lax.Precision.HIGH)


def _blin01(m, x):
    mb = m.astype(BF16)
    return sum(_bdot(mb, p, 2, 1) for p in _split3(x))


@jax.custom_vjp
def blin01(m, mt, x):
    return _blin01(m, x)


def _blin01_fwd(m, mt, x):
    return _blin01(m, x), (m, mt)


def _blin01_bwd(res, g):
    m, mt = res
    return jnp.zeros_like(m), jnp.zeros_like(mt), _blin01(mt, g)


blin01.defvjp(_blin01_fwd, _blin01_bwd)


@jax.custom_vjp
def lin01_l(m, mt, x):
    return _lin01_l(m, x)


def _lin01_l_fwd(m, mt, x):
    return _lin01_l(m, x), (m, mt)


def _lin01_l_bwd(res, g):
    m, mt = res
    return jnp.zeros_like(m), jnp.zeros_like(mt), _lin01_l(mt, g)


lin01_l.defvjp(_lin01_l_fwd, _lin01_l_bwd)


@jax.custom_vjp
def lin01_r(x, m, mt):
    return _lin01_r(x, m)


def _lin01_r_fwd(x, m, mt):
    return _lin01_r(x, m), (m, mt)


def _lin01_r_bwd(res, g):
    m, mt = res
    return _lin01_r(g, mt), jnp.zeros_like(m), jnp.zeros_like(mt)


lin01_r.defvjp(_lin01_r_fwd, _lin01_r_bwd)


@functools.partial(jax.custom_vjp, nondiff_argnums=(1, 2))
def rows(x, start, size):
    return lax.slice_in_dim(x, start, start + size, axis=x.ndim - 2)


def _rows_fwd(x, start, size):
    return lax.slice_in_dim(x, start, start + size, axis=x.ndim - 2), jnp.zeros(x.shape[:-1] + (1,), F32)


def _rows_bwd(start, size, res, g):
    ax = g.ndim - 2
    total = res.shape[ax]
    zeros = lambda n: jnp.zeros(g.shape[:ax] + (n,) + g.shape[ax + 1:], g.dtype)
    parts = []
    if start > 0:
        parts.append(zeros(start))
    parts.append(g)
    if total - start - size > 0:
        parts.append(zeros(total - start - size))
    return (jnp.concatenate(parts, axis=ax) if len(parts) > 1 else g,)


rows.defvjp(_rows_fwd, _rows_bwd)


def sigmoid(x):
    return jax.nn.sigmoid(x)


def silu(x):
    return x * jax.nn.sigmoid(x)


def softplus(x):
    return jnp.maximum(x, 0.0) + jnp.log1p(jnp.exp(-jnp.abs(x)))


def _iota(shape, dim):
    return lax.broadcasted_iota(jnp.int32, shape, dim)


def norm_mod_fn(g, sc, sh, x):
    r = lax.rsqrt(jnp.mean(x * x, axis=-1, keepdims=True) + EPS)
    return (x * r * g) * (1.0 + sc) + sh


def _row_spec(tm, f):
    return pl.BlockSpec((tm, f), lambda i: (i, 0))


def _const_spec(shape):
    nd = len(shape)
    return pl.BlockSpec(shape, lambda i: (0,) * nd)


def _tiled(name, fn, consts, rows_in, row_outs, acc_outs=(), tm=512):
    T = rows_in[0].shape[0]
    tm = min(tm, T)
    n_c, n_r, n_ro = len(consts), len(rows_in), len(row_outs)

    def body(*refs):
        c_refs, r_refs = refs[:n_c], refs[n_c:n_c + n_r]
        ro_refs, ao_refs = refs[n_c + n_r:n_c + n_r + n_ro], refs[n_c + n_r + n_ro:]
        outs = fn(*[r[...] for r in c_refs], *[r[...] for r in r_refs])
        for r, v in zip(ro_refs, outs[:n_ro]):
            r[...] = v.astype(r.dtype)
        if ao_refs:
            @pl.when(pl.program_id(0) == 0)
            def _():
                for r in ao_refs:
                    r[...] = jnp.zeros(r.shape, r.dtype)
            for r, v in zip(ao_refs, outs[n_ro:]):
                r[...] += v

    out_shape = [jax.ShapeDtypeStruct((T, f), dt) for f, dt in row_outs] + [jax.ShapeDtypeStruct(s, F32) for s in acc_outs]
    out_specs = [_row_spec(tm, f) for f, _ in row_outs] + [_const_spec(s) for s in acc_outs]
    in_specs = [_const_spec(c.shape) for c in consts] + [_row_spec(tm, r.shape[1]) for r in rows_in]
    return pl.pallas_call(
        body, name=name, grid=(T // tm,), in_specs=in_specs, out_specs=out_specs, out_shape=out_shape,
        compiler_params=_params(dimension_semantics=("arbitrary",)),
    )(*consts, *rows_in)


def norm_mod(name, x, g, sc, sh):
    (h,) = _tiled(name, lambda g, sc, sh, x: (norm_mod_fn(g, sc, sh, x),), [g, sc, sh], [x], [(D, BF16)])
    return h


def resid_norm(name, x, y, gate, g, sc, sh):
    def fn(gate, g, sc, sh, x, y):
        xn = x + gate * y
        return xn, norm_mod_fn(g, sc, sh, xn)
    return _tiled(name, fn, [gate, g, sc, sh], [x, y], [(D, F32), (D, BF16)])


def loss_head(name, x, y, tgt, gate):
    def fn(gate, x, y, tgt):
        err = x + gate * y - tgt
        dx = err * (1.0 / D)
        per_tok = jnp.sum(err * err, axis=-1, keepdims=True) * (0.5 / D)
        loss = jnp.sum(per_tok, axis=0, keepdims=True)
        return dx, gate * dx, jnp.broadcast_to(loss, (1, 128)), jnp.sum(dx * y, axis=0, keepdims=True)
    return _tiled(name, fn, [gate], [x, y, tgt], [(D, F32), (D, BF16)], [(1, 128), (1, D)])


def bwd_rowwise(name, x, dh, dxn, g, sc, sh, y_prev=None, gate_prev=None):
    with_prev = y_prev is not None

    def fn(*a):
        if with_prev:
            gate_p, g, sc, sh, x, dh, dxn, yp = a
        else:
            g, sc, sh, x, dh, dxn = a
        _, f = jax.vjp(norm_mod_fn, g, sc, sh, x)
        dg, dsc, dsh, dx = f(dh)
        dx = dx + dxn
        if with_prev:
            return dx, gate_p * dx, dg, dsc, dsh, jnp.sum(dx * yp, axis=0, keepdims=True)
        return dx, dg, dsc, dsh

    if with_prev:
        return _tiled(name, fn, [gate_prev, g, sc, sh], [x, dh, dxn, y_prev], [(D, F32), (D, BF16)], [(1, D)] * 4)
    return _tiled(name, fn, [g, sc, sh], [x, dh, dxn], [(D, F32)], [(1, D)] * 3)


def matmul(name, a, w, out_dtype=F32, tm=256):
    T, K = a.shape
    N = w.shape[1]
    tm = min(tm, T)

    def body(a_ref, w_ref, o_ref):
        o_ref[...] = mm(a_ref[...], w_ref[...]).astype(o_ref.dtype)

    return pl.pallas_call(
        body, name=name, grid=(T // tm,),
        in_specs=[_row_spec(tm, K), _const_spec((K, N))], out_specs=_row_spec(tm, N),
        out_shape=jax.ShapeDtypeStruct((T, N), out_dtype),
        compiler_params=_params(dimension_semantics=("arbitrary",)),
    )(a, w)


def matmul_nt(name, pairs, out_dtype=F32, tm=256):
    T = pairs[0][0].shape[0]
    K = pairs[0][1].shape[0]
    tm = min(tm, T)
    n = len(pairs)

    def body(*refs):
        o_ref = refs[-1]
        acc = None
        for i in range(n):
            p = mm_nt(refs[2 * i][...], refs[2 * i + 1][...])
            acc = p if acc is None else acc + p
        o_ref[...] = acc.astype(o_ref.dtype)

    in_specs, args = [], []
    for a, w in pairs:
        in_specs += [_row_spec(tm, a.shape[1]), _const_spec(w.shape)]
        args += [a, w]
    return pl.pallas_call(
        body, name=name, grid=(T // tm,), in_specs=in_specs, out_specs=_row_spec(tm, K),
        out_shape=jax.ShapeDtypeStruct((T, K), out_dtype),
        compiler_params=_params(dimension_semantics=("arbitrary",)),
    )(*args)


def matmul_tn(name, a, b, tm=512, tn=1024):
    T, K = a.shape
    N = b.shape[1]
    tm = min(tm, T)
    tn = max(t for t in range(128, min(tn, N) + 1, 128) if N % t == 0)

    def body(a_ref, b_ref, o_ref):
        @pl.when(pl.program_id(1) == 0)
        def _():
            o_ref[...] = jnp.zeros(o_ref.shape, F32)
        o_ref[...] += mm_tn(a_ref[...], b_ref[...])

    return pl.pallas_call(
        body, name=name, grid=(N // tn, T // tm),
        in_specs=[pl.BlockSpec((tm, K), lambda j, i: (i, 0)), pl.BlockSpec((tm, tn), lambda j, i: (i, j))],
        out_specs=pl.BlockSpec((K, tn), lambda j, i: (0, j)),
        out_shape=jax.ShapeDtypeStruct((K, N), F32),
        compiler_params=_params(dimension_semantics=("arbitrary", "arbitrary")),
    )(a, b)


def hg_heads(tri, trit, lb, og, st, qp, fp, iv, z):
    blk = qp.shape[1]
    q = silu(qp)
    lf = jnp.log(lb + (1.0 - lb) * sigmoid(fp))
    k = (1.0 - lb) * sigmoid(-fp)
    r = _iota((1, HG_SUB, 1), 1)
    outs = []
    for j in range(blk // HG_SUB):
        qj, kj, vj, lfj = (rows(a, j * HG_SUB, HG_SUB) for a in (q, k, iv, lf))
        b = blin01(tri, trit, lfj)
        o = bmm_nt(qj * jnp.exp(b), st)
        for s in range(HG_SUB):
            sel = r == s
            ks = jnp.sum(jnp.where(sel, kj, 0.0), axis=1, keepdims=True)
            bs = jnp.sum(jnp.where(sel, b, 0.0), axis=1, keepdims=True)
            vs = jnp.sum(jnp.where(sel, vj, 0.0), axis=1, keepdims=True)
            e = jnp.exp(jnp.where(r >= s, b - bs, NEG_INF))
            o = o + jnp.sum(qj * ks * e, axis=-1, keepdims=True) * vs
        bl = jnp.sum(lfj, axis=1, keepdims=True)
        st = st * jnp.exp(bl) + bmm_tn(vj, kj * jnp.exp(bl - b))
        outs.append(o)
    o = jnp.concatenate(outs, axis=1)
    on = o * lax.rsqrt(jnp.mean(o * o, axis=-1, keepdims=True) + EPS) * og
    return st, on * silu(z)


def _tri(n, nh):
    t = (_iota((nh, n, n), 1) >= _iota((nh, n, n), 2)).astype(F32)
    tt = (_iota((nh, n, n), 1) <= _iota((nh, n, n), 2)).astype(F32)
    return t, tt


def _hg_load(p_ref, lb_ref, heads):
    col = lambda ref, base: jnp.stack([ref[:, pl.ds(base + h * HG_DH, HG_DH)] for h in heads])
    return col(lb_ref, 0), col(p_ref, 0), col(p_ref, D), col(p_ref, 2 * D), col(p_ref, 3 * D)


def hg_fwd(name, p, lb, og):
    T = p.shape[0]
    nb = T // HG_BLK

    def body(p_ref, lb_ref, og_ref, u_ref, st_ref, s_scr):
        @pl.when(pl.program_id(0) == 0)
        def _():
            s_scr[...] = jnp.zeros(s_scr.shape, F32)
        st_ref[0] = s_scr[...]
        tri, trit = _tri(HG_SUB, HG_HB)
        og_v = og_ref[...]
        for b0 in range(0, HG_HEADS, HG_HB):
            heads = list(range(b0, b0 + HG_HB))
            lbh, qp, fp, iv, z = _hg_load(p_ref, lb_ref, heads)
            st, u = hg_heads(tri, trit, lbh, og_v, s_scr[pl.ds(b0, HG_HB)], qp, fp, iv, z)
            s_scr[pl.ds(b0, HG_HB)] = st
            for i, h in enumerate(heads):
                u_ref[:, pl.ds(h * HG_DH, HG_DH)] = u[i].astype(BF16)

    return pl.pallas_call(
        body, name=name, grid=(nb,),
        in_specs=[_row_spec(HG_BLK, 4 * D), _const_spec((1, D)), _const_spec((1, HG_DH))],
        out_specs=[_row_spec(HG_BLK, D), pl.BlockSpec((1, HG_HEADS, HG_DH, HG_DH), lambda i: (i, 0, 0, 0))],
        out_shape=[jax.ShapeDtypeStruct((T, D), BF16), jax.ShapeDtypeStruct((nb, HG_HEADS, HG_DH, HG_DH), F32)],
        scratch_shapes=[pltpu.VMEM((HG_HEADS, HG_DH, HG_DH), F32)],
        compiler_params=_params(dimension_semantics=("arbitrary",)),
    )(p, lb, og)


def hg_bwd(name, p, lb, og, states, du):
    T = p.shape[0]
    nb = T // HG_BLK

    def body(p_ref, lb_ref, og_ref, st_ref, du_ref, dp_ref, dlb_ref, dog_ref, ds_scr):
        @pl.when(pl.program_id(0) == 0)
        def _():
            ds_scr[...] = jnp.zeros(ds_scr.shape, F32)
            dlb_ref[...] = jnp.zeros(dlb_ref.shape, F32)
            dog_ref[...] = jnp.zeros(dog_ref.shape, F32)
        tri, trit = _tri(HG_SUB, HG_HB)
        og_v = og_ref[...]
        fn = functools.partial(hg_heads, tri, trit)
        for b0 in range(0, HG_HEADS, HG_HB):
            heads = list(range(b0, b0 + HG_HB))
            lbh, qp, fp, iv, z = _hg_load(p_ref, lb_ref, heads)
            _, f = jax.vjp(fn, lbh, og_v, st_ref[0, pl.ds(b0, HG_HB)], qp, fp, iv, z)
            du_h = jnp.stack([du_ref[:, pl.ds(h * HG_DH, HG_DH)].astype(F32) for h in heads])
            dlb, dog, dst, dq, df, di, dz = f((ds_scr[pl.ds(b0, HG_HB)], du_h))
            ds_scr[pl.ds(b0, HG_HB)] = dst
            dog_ref[...] += dog
            for i, h in enumerate(heads):
                for base, g in ((0, dq), (D, df), (2 * D, di), (3 * D, dz)):
                    dp_ref[:, pl.ds(base + h * HG_DH, HG_DH)] = g[i].astype(BF16)
                dlb_ref[:, pl.ds(h * HG_DH, HG_DH)] += dlb[i]

    rev = lambda i: (nb - 1 - i, 0)
    return pl.pallas_call(
        body, name=name, grid=(nb,),
        in_specs=[pl.BlockSpec((HG_BLK, 4 * D), rev), _const_spec((1, D)), _const_spec((1, HG_DH)),
                  pl.BlockSpec((1, HG_HEADS, HG_DH, HG_DH), lambda i: (nb - 1 - i, 0, 0, 0)),
                  pl.BlockSpec((HG_BLK, D), rev)],
        out_specs=[pl.BlockSpec((HG_BLK, 4 * D), rev), _const_spec((1, D)), _const_spec((1, HG_DH))],
        out_shape=[jax.ShapeDtypeStruct((T, 4 * D), BF16), jax.ShapeDtypeStruct((1, D), F32),
                   jax.ShapeDtypeStruct((1, HG_DH), F32)],
        scratch_shapes=[pltpu.VMEM((HG_HEADS, HG_DH, HG_DH), F32)],
        compiler_params=_params(dimension_semantics=("arbitrary",)),
    )(p, lb, og, states, du)


def swa_head(qn, kn, sinks, kprev, vprev, qp4, kp, v, z4, cos4, sin4, cosk, sink_, hot4, mask, p64, p64t):
    def norm_rope(xp, g, cs, sn):
        y = xp * lax.rsqrt(jnp.mean(xp * xp, axis=-1, keepdims=True) + EPS) * g
        return y * cs + lin01_r(y, p64, p64t) * sn

    q = norm_rope(qp4, qn, cos4, sin4)
    k = norm_rope(kp, kn, cosk, sink_)
    k2 = jnp.concatenate([kprev, k], axis=0)
    v2 = jnp.concatenate([vprev, v], axis=0)
    s = jnp.where(mask, mm_nt(q, k2) * (SW_DH ** -0.5), NEG_INF)
    sink_col = jnp.concatenate(
        [jnp.broadcast_to(jnp.sum(sinks * hot, axis=-1, keepdims=True), (SW_BLK, 1)) for hot in hot4], axis=0)
    m = lax.stop_gradient(jnp.maximum(jnp.max(s, axis=-1, keepdims=True), sink_col))
    p = jnp.exp(s - m)
    p = p / (jnp.sum(p, axis=-1, keepdims=True) + jnp.exp(sink_col - m))
    o = mm(p, v2)
    return k, v, o * silu(z4)


def _swa_consts(first):
    qi = _iota((SW_G * SW_BLK, 2 * SW_BLK), 0) % SW_BLK
    kj = _iota((SW_G * SW_BLK, 2 * SW_BLK), 1)
    rel = qi + SW_BLK - kj
    mask = (rel >= 0) & (rel < SW_BLK) & ((kj >= SW_BLK) | jnp.logical_not(first))
    i, j = _iota((SW_DH, SW_DH), 0), _iota((SW_DH, SW_DH), 1)
    p64 = (i == (j + SW_DH // 2) % SW_DH).astype(F32)
    p64t = (j == (i + SW_DH // 2) % SW_DH).astype(F32)
    return mask, p64, p64t


def _swa_hot(h):
    lane = _iota((1, SW_KV * SW_G), 1)
    return [(lane == h * SW_G + g).astype(F32) for g in range(SW_G)]


def _stack_q(ref, h, base):
    return jnp.concatenate([ref[:, pl.ds(base + h * SW_G * SW_DH + g * SW_DH, SW_DH)] for g in range(SW_G)], axis=0)


Q0, K0, V0, Z0 = 0, SW_QW, SW_QW + SW_KVW, SW_QW + 2 * SW_KVW
SW_IN = 2 * SW_QW + 2 * SW_KVW


def swa_fwd(name, p, cos, sin, qn, kn, sinks):
    T = p.shape[0]
    nb = T // SW_BLK

    def body(p_ref, cos_ref, sin_ref, qn_ref, kn_ref, sk_ref, u_ref, ko_ref, kprev, vprev):
        first = pl.program_id(0) == 0

        @pl.when(first)
        def _():
            kprev[...] = jnp.zeros(kprev.shape, F32)
            vprev[...] = jnp.zeros(vprev.shape, F32)
        mask, p64, p64t = _swa_consts(first)
        cs, sn = cos_ref[...], sin_ref[...]
        cs4, sn4 = jnp.concatenate([cs] * SW_G, axis=0), jnp.concatenate([sn] * SW_G, axis=0)
        for h in range(SW_KV):
            k, v, u4 = swa_head(qn_ref[...], kn_ref[...], sk_ref[...], kprev[h], vprev[h],
                                _stack_q(p_ref, h, Q0), p_ref[:, pl.ds(K0 + h * SW_DH, SW_DH)],
                                p_ref[:, pl.ds(V0 + h * SW_DH, SW_DH)], _stack_q(p_ref, h, Z0),
                                cs4, sn4, cs, sn, _swa_hot(h), mask, p64, p64t)
            kprev[h] = k
            vprev[h] = v
            ko_ref[:, pl.ds(h * SW_DH, SW_DH)] = k
            for g in range(SW_G):
                u_ref[:, pl.ds(h * SW_G * SW_DH + g * SW_DH, SW_DH)] = u4[g * SW_BLK:(g + 1) * SW_BLK].astype(BF16)

    return pl.pallas_call(
        body, name=name, grid=(nb,),
        in_specs=[_row_spec(SW_BLK, SW_IN), _row_spec(SW_BLK, SW_DH), _row_spec(SW_BLK, SW_DH),
                  _const_spec((1, SW_DH)), _const_spec((1, SW_DH)), _const_spec((1, SW_KV * SW_G))],
        out_specs=[_row_spec(SW_BLK, SW_QW), _row_spec(SW_BLK, SW_KVW)],
        out_shape=[jax.ShapeDtypeStruct((T, SW_QW), BF16), jax.ShapeDtypeStruct((T, SW_KVW), F32)],
        scratch_shapes=[pltpu.VMEM((SW_KV, SW_BLK, SW_DH), F32), pltpu.VMEM((SW_KV, SW_BLK, SW_DH), F32)],
        compiler_params=_params(dimension_semantics=("arbitrary",)),
    )(p, cos, sin, qn, kn, sinks)


def swa_bwd(name, p, kpost, cos, sin, qn, kn, sinks, du):
    T = p.shape[0]
    nb = T // SW_BLK

    def body(p_ref, pprev_ref, kprev_ref, cos_ref, sin_ref, qn_ref, kn_ref, sk_ref, du_ref,
             dp_ref, dqn_ref, dkn_ref, dsk_ref, dk_scr, dv_scr):
        i = pl.program_id(0)
        first = i == nb - 1

        @pl.when(i == 0)
        def _():
            dk_scr[...] = jnp.zeros(dk_scr.shape, F32)
            dv_scr[...] = jnp.zeros(dv_scr.shape, F32)
            dqn_ref[...] = jnp.zeros(dqn_ref.shape, F32)
            dkn_ref[...] = jnp.zeros(dkn_ref.shape, F32)
            dsk_ref[...] = jnp.zeros(dsk_ref.shape, F32)
        mask, p64, p64t = _swa_consts(first)
        cs, sn = cos_ref[...], sin_ref[...]
        cs4, sn4 = jnp.concatenate([cs] * SW_G, axis=0), jnp.concatenate([sn] * SW_G, axis=0)
        for h in range(SW_KV):
            hot4 = _swa_hot(h)
            fn = lambda qn, kn, sk, kpv, vpv, qp4, kp, v, z4: swa_head(
                qn, kn, sk, kpv, vpv, qp4, kp, v, z4, cs4, sn4, cs, sn, hot4, mask, p64, p64t)
            _, f = jax.vjp(fn, qn_ref[...], kn_ref[...], sk_ref[...],
                           kprev_ref[:, pl.ds(h * SW_DH, SW_DH)], pprev_ref[:, pl.ds(V0 + h * SW_DH, SW_DH)],
                           _stack_q(p_ref, h, Q0), p_ref[:, pl.ds(K0 + h * SW_DH, SW_DH)],
                           p_ref[:, pl.ds(V0 + h * SW_DH, SW_DH)], _stack_q(p_ref, h, Z0))
            du4 = _stack_q(du_ref, h, 0).astype(F32)
            dqn, dkn, dsk, dkpv, dvpv, dq4, dkp, dv, dz4 = f((dk_scr[h], dv_scr[h], du4))
            dk_scr[h] = dkpv
            dv_scr[h] = dvpv
            dp_ref[:, pl.ds(K0 + h * SW_DH, SW_DH)] = dkp.astype(BF16)
            dp_ref[:, pl.ds(V0 + h * SW_DH, SW_DH)] = dv.astype(BF16)
            for g in range(SW_G):
                c0 = h * SW_G * SW_DH + g * SW_DH
                dp_ref[:, pl.ds(Q0 + c0, SW_DH)] = dq4[g * SW_BLK:(g + 1) * SW_BLK].astype(BF16)
                dp_ref[:, pl.ds(Z0 + c0, SW_DH)] = dz4[g * SW_BLK:(g + 1) * SW_BLK].astype(BF16)
            dqn_ref[...] += dqn
            dkn_ref[...] += dkn
            dsk_ref[...] += dsk

    rev = lambda i: (nb - 1 - i, 0)
    prev = lambda i: (jnp.maximum(nb - 2 - i, 0), 0)
    return pl.pallas_call(
        body, name=name, grid=(nb,),
        in_specs=[pl.BlockSpec((SW_BLK, SW_IN), rev), pl.BlockSpec((SW_BLK, SW_IN), prev),
                  pl.BlockSpec((SW_BLK, SW_KVW), prev), pl.BlockSpec((SW_BLK, SW_DH), rev), pl.BlockSpec((SW_BLK, SW_DH), rev),
                  _const_spec((1, SW_DH)), _const_spec((1, SW_DH)), _const_spec((1, SW_KV * SW_G)),
                  pl.BlockSpec((SW_BLK, SW_QW), rev)],
        out_specs=[pl.BlockSpec((SW_BLK, SW_IN), rev), _const_spec((1, SW_DH)), _const_spec((1, SW_DH)),
                   _const_spec((1, SW_KV * SW_G))],
        out_shape=[jax.ShapeDtypeStruct((T, SW_IN), BF16), jax.ShapeDtypeStruct((1, SW_DH), F32),
                   jax.ShapeDtypeStruct((1, SW_DH), F32), jax.ShapeDtypeStruct((1, SW_KV * SW_G), F32)],
        scratch_shapes=[pltpu.VMEM((SW_KV, SW_BLK, SW_DH), F32), pltpu.VMEM((SW_KV, SW_BLK, SW_DH), F32)],
        compiler_params=_params(dimension_semantics=("arbitrary",)),
    )(p, p, kpost, cos, sin, qn, kn, sinks, du)


CONV_TC, CONV_TL, HALO = 256, 1024, 8


def _conv_taps(xe, w_ref):
    acc = w_ref[pl.ds(CONV_K - 1, 1), :] * xe
    for k in range(CONV_K - 1):
        acc = acc + w_ref[pl.ds(k, 1), :] * pltpu.roll(xe, CONV_K - 1 - k, 0)
    return acc


def conv_fwd(name, x, w8):
    T, C = x.shape
    tc = min(CONV_TC, T)
    nt = T // tc

    def body(xp_ref, x_ref, w_ref, o_ref):
        prev = jnp.where(pl.program_id(0) == 0, 0.0, xp_ref[...])
        xe = jnp.concatenate([prev, x_ref[...]], axis=0)
        o_ref[...] = silu(_conv_taps(xe, w_ref)[HALO:])

    return pl.pallas_call(
        body, name=name, grid=(nt, C // CONV_TL),
        in_specs=[pl.BlockSpec((HALO, CONV_TL), lambda i, j: (jnp.maximum(i * (tc // HALO) - 1, 0), j)),
                  pl.BlockSpec((tc, CONV_TL), lambda i, j: (i, j)), pl.BlockSpec((HALO, CONV_TL), lambda i, j: (0, j))],
        out_specs=pl.BlockSpec((tc, CONV_TL), lambda i, j: (i, j)),
        out_shape=jax.ShapeDtypeStruct((T, C), F32),
        compiler_params=_params(dimension_semantics=("arbitrary", "arbitrary")),
    )(x, x, w8)


def conv_bwd(name, x, dy, w8):
    T, C = x.shape
    tc = min(CONV_TC, T)
    nt = T // tc
    n = tc + 2 * HALO

    def body(xp_ref, x_ref, xn_ref, dy_ref, dyn_ref, w_ref, dx_ref, dw_ref):
        i = pl.program_id(1)

        @pl.when(i == 0)
        def _():
            dw_ref[...] = jnp.zeros(dw_ref.shape, F32)
        prev = jnp.where(i == 0, 0.0, xp_ref[...])
        xe = jnp.concatenate([prev, x_ref[...], xn_ref[...]], axis=0)
        dye = jnp.concatenate([jnp.zeros((HALO, CONV_TL), F32), dy_ref[...], jnp.where(i == nt - 1, 0.0, dyn_ref[...])], axis=0)
        ce = _conv_taps(xe, w_ref)
        sg = sigmoid(ce)
        dce = dye * (sg * (1.0 + ce * (1.0 - sg)))
        dx = w_ref[pl.ds(CONV_K - 1, 1), :] * dce
        for k in range(CONV_K - 1):
            dx = dx + w_ref[pl.ds(k, 1), :] * pltpu.roll(dce, n - (CONV_K - 1 - k), 0)
        dx_ref[...] = dx[HALO:HALO + tc].astype(BF16)
        dcur = dce[HALO:HALO + tc]
        for k in range(CONV_K):
            xs = xe if k == CONV_K - 1 else pltpu.roll(xe, CONV_K - 1 - k, 0)
            dw_ref[pl.ds(k, 1), :] += jnp.sum(dcur * xs[HALO:HALO + tc], axis=0, keepdims=True)

    nh = T // HALO
    prev_map = lambda j, i: (jnp.maximum(i * (tc // HALO) - 1, 0), j)
    next_map = lambda j, i: (jnp.minimum((i + 1) * (tc // HALO), nh - 1), j)
    cur_map = lambda j, i: (i, j)
    return pl.pallas_call(
        body, name=name, grid=(C // CONV_TL, nt),
        in_specs=[pl.BlockSpec((HALO, CONV_TL), prev_map), pl.BlockSpec((tc, CONV_TL), cur_map),
                  pl.BlockSpec((HALO, CONV_TL), next_map), pl.BlockSpec((tc, CONV_TL), cur_map),
                  pl.BlockSpec((HALO, CONV_TL), next_map), pl.BlockSpec((HALO, CONV_TL), lambda j, i: (0, j))],
        out_specs=[pl.BlockSpec((tc, CONV_TL), cur_map), pl.BlockSpec((HALO, CONV_TL), lambda j, i: (0, j))],
        out_shape=[jax.ShapeDtypeStruct((T, C), BF16), jax.ShapeDtypeStruct((HALO, C), F32)],
        compiler_params=_params(dimension_semantics=("arbitrary", "arbitrary")),
    )(x, x, x, dy, dy, w8)


GD_VH = 16
GD_HB = 8


def _unit_lower_inverse(a, eye_f):
    n = -a
    t = eye_f + n
    p = n
    for _ in range(5):
        p = bmmf(p, p)
        t = t + bmmf(t, p)
    return t


def gd_heads(al, dt, og, s, qc, kc, v, z, a_col, b_col, cst):
    lt_t, eye, incl, strict = cst
    eye_f = eye.astype(F32)
    nh = s.shape[0]
    q = qc * lax.rsqrt(jnp.sum(qc * qc, axis=-1, keepdims=True) + EPS) * (GD_DK ** -0.5)
    k = kc * lax.rsqrt(jnp.sum(kc * kc, axis=-1, keepdims=True) + EPS)
    beta = sigmoid(b_col)
    g = -jnp.exp(al) * softplus(a_col + dt)
    d_mat = jnp.broadcast_to(jnp.sum(g * lt_t, axis=1, keepdims=True), (nh, GD_CHUNK, GD_CHUNK))
    d = jnp.sum(jnp.where(eye, d_mat, 0.0), axis=-1, keepdims=True)
    dec = jnp.exp(jnp.where(incl, d - d_mat, NEG_INF))
    kb = k * beta
    t_inv = _unit_lower_inverse(bmm_nt(kb, k) * dec * strict, eye_f)
    u = bmmf(t_inv, v * beta)
    w = bmmf(t_inv, kb * jnp.exp(d))
    v_new = u - bmm(w, s)
    o = bmm(q * jnp.exp(d), s) + bmm(bmm_nt(q, k) * dec, v_new)
    dl = jnp.sum(g, axis=1, keepdims=True)
    s_new = s * jnp.exp(dl) + bmm_tn(k * jnp.exp(dl - d), v_new)
    on = o * lax.rsqrt(jnp.mean(o * o, axis=-1, keepdims=True) + EPS) * og
    return s_new, on * silu(z)


def _gd_consts():
    r, c = _iota((GD_CHUNK, GD_CHUNK), 0), _iota((GD_CHUNK, GD_CHUNK), 1)
    return (r <= c).astype(F32), r == c, r >= c, (r > c).astype(F32)


def _gd_load(qkv_ref, z_ref, ab_v, al_v, dt_v, heads):
    lane = _iota((1, 128), 1)
    hot_a = [(lane == h).astype(F32) for h in heads]
    hot_b = [(lane == GD_VH + h).astype(F32) for h in heads]
    col = lambda src, hot: jnp.stack([jnp.sum(src * m, axis=-1, keepdims=True) for m in hot])
    ops = (col(al_v, hot_a), col(dt_v, hot_a),
           jnp.stack([qkv_ref[:, pl.ds((h // 2) * GD_DK, GD_DK)] for h in heads]),
           jnp.stack([qkv_ref[:, pl.ds(GD_QKW + (h // 2) * GD_DK, GD_DK)] for h in heads]),
           jnp.stack([qkv_ref[:, pl.ds(2 * GD_QKW + h * GD_DK, GD_DK)] for h in heads]),
           jnp.stack([z_ref[:, pl.ds(h * GD_DK, GD_DK)] for h in heads]),
           col(ab_v, hot_a), col(ab_v, hot_b))
    return ops, hot_a, hot_b


def gd_fwd(name, qkv, z, ab, al, dt, og):
    T = qkv.shape[0]
    nc = T // GD_CHUNK

    def body(qkv_ref, z_ref, ab_ref, al_ref, dt_ref, og_ref, u_ref, st_ref, s_scr):
        @pl.when(pl.program_id(0) == 0)
        def _():
            s_scr[...] = jnp.zeros(s_scr.shape, F32)
        st_ref[0] = s_scr[...]
        cst = _gd_consts()
        ab_v, al_v, dt_v, og_v = ab_ref[...], al_ref[...], dt_ref[...], og_ref[...]
        for b0 in range(0, GD_VH, GD_HB):
            heads = list(range(b0, b0 + GD_HB))
            (alh, dth, q, k, v, zz, a_col, b_col), _, _ = _gd_load(qkv_ref, z_ref, ab_v, al_v, dt_v, heads)
            s_new, u = gd_heads(alh, dth, og_v, s_scr[pl.ds(b0, GD_HB)], q, k, v, zz, a_col, b_col, cst)
            s_scr[pl.ds(b0, GD_HB)] = s_new
            for i, h in enumerate(heads):
                u_ref[:, pl.ds(h * GD_DK, GD_DK)] = u[i].astype(BF16)

    return pl.pallas_call(
        body, name=name, grid=(nc,),
        in_specs=[_row_spec(GD_CHUNK, GD_QKV), _row_spec(GD_CHUNK, GD_VW), _row_spec(GD_CHUNK, 128),
                  _const_spec((1, 128)), _const_spec((1, 128)), _const_spec((1, 128))],
        out_specs=[_row_spec(GD_CHUNK, GD_VW), pl.BlockSpec((1, GD_VH, GD_DK, GD_DK), lambda i: (i, 0, 0, 0))],
        out_shape=[jax.ShapeDtypeStruct((T, GD_VW), BF16), jax.ShapeDtypeStruct((nc, GD_VH, GD_DK, GD_DK), F32)],
        scratch_shapes=[pltpu.VMEM((GD_VH, GD_DK, GD_DK), F32)],
        compiler_params=_params(dimension_semantics=("arbitrary",)),
    )(qkv, z, ab, al, dt, og)


def gd_bwd(name, qkv, z, ab, al, dt, og, states, du):
    T = qkv.shape[0]
    nc = T // GD_CHUNK

    def body(qkv_ref, z_ref, ab_ref, al_ref, dt_ref, og_ref, st_ref, du_ref,
             dqkv_ref, dz_ref, dab_ref, dal_ref, ddt_ref, dog_ref, ds_scr):
        @pl.when(pl.program_id(0) == 0)
        def _():
            ds_scr[...] = jnp.zeros(ds_scr.shape, F32)
            dal_ref[...] = jnp.zeros(dal_ref.shape, F32)
            ddt_ref[...] = jnp.zeros(ddt_ref.shape, F32)
            dog_ref[...] = jnp.zeros(dog_ref.shape, F32)
        cst = _gd_consts()
        ab_v, al_v, dt_v, og_v = ab_ref[...], al_ref[...], dt_ref[...], og_ref[...]
        dab = jnp.zeros((GD_CHUNK, 128), F32)
        dal_row = jnp.zeros((1, 128), F32)
        ddt_row = jnp.zeros((1, 128), F32)
        dog_row = jnp.zeros((1, 128), F32)
        for b0 in range(0, GD_VH, GD_HB):
            heads = list(range(b0, b0 + GD_HB))
            (alh, dth, q, k, v, zz, a_col, b_col), hot_a, hot_b = _gd_load(qkv_ref, z_ref, ab_v, al_v, dt_v, heads)
            fn = lambda *a: gd_heads(*a, cst)
            _, f = jax.vjp(fn, alh, dth, og_v, st_ref[0, pl.ds(b0, GD_HB)], q, k, v, zz, a_col, b_col)
            du_h = jnp.stack([du_ref[:, pl.ds(h * GD_DK, GD_DK)].astype(F32) for h in heads])
            dalh, ddth, dog, ds, dq, dk, dv, dz, da_col, db_col = f((ds_scr[pl.ds(b0, GD_HB)], du_h))
            ds_scr[pl.ds(b0, GD_HB)] = ds
            dog_row = dog_row + dog
            for i, h in enumerate(heads):
                dqkv_ref[:, pl.ds(2 * GD_QKW + h * GD_DK, GD_DK)] = dv[i]
                dz_ref[:, pl.ds(h * GD_DK, GD_DK)] = dz[i].astype(BF16)
                dab = dab + da_col[i] * hot_a[i] + db_col[i] * hot_b[i]
                dal_row = dal_row + dalh[i] * hot_a[i]
                ddt_row = ddt_row + ddth[i] * hot_a[i]
                if h % 2 == 0:
                    dqkv_ref[:, pl.ds((h // 2) * GD_DK, GD_DK)] = dq[i] + dq[i + 1]
                    dqkv_ref[:, pl.ds(GD_QKW + (h // 2) * GD_DK, GD_DK)] = dk[i] + dk[i + 1]
        dab_ref[...] = dab
        dal_ref[...] += dal_row
        ddt_ref[...] += ddt_row
        dog_ref[...] += dog_row

    rev = lambda i: (nc - 1 - i, 0)
    return pl.pallas_call(
        body, name=name, grid=(nc,),
        in_specs=[pl.BlockSpec((GD_CHUNK, GD_QKV), rev), pl.BlockSpec((GD_CHUNK, GD_VW), rev), pl.BlockSpec((GD_CHUNK, 128), rev),
                  _const_spec((1, 128)), _const_spec((1, 128)), _const_spec((1, 128)),
                  pl.BlockSpec((1, GD_VH, GD_DK, GD_DK), lambda i: (nc - 1 - i, 0, 0, 0)),
                  pl.BlockSpec((GD_CHUNK, GD_VW), rev)],
        out_specs=[pl.BlockSpec((GD_CHUNK, GD_QKV), rev), pl.BlockSpec((GD_CHUNK, GD_VW), rev), pl.BlockSpec((GD_CHUNK, 128), rev),
                   _const_spec((1, 128)), _const_spec((1, 128)), _const_spec((1, 128))],
        out_shape=[jax.ShapeDtypeStruct((T, GD_QKV), F32), jax.ShapeDtypeStruct((T, GD_VW), BF16),
                   jax.ShapeDtypeStruct((T, 128), F32)] + [jax.ShapeDtypeStruct((1, 128), F32)] * 3,
        scratch_shapes=[pltpu.VMEM((GD_VH, GD_DK, GD_DK), F32)],
        compiler_params=_params(dimension_semantics=("arbitrary",)),
    )(qkv, z, ab, al, dt, og, states, du)


def _my_pos():
    return lax.axis_index("x"), lax.axis_index("y"), lax.axis_index("c")


def _peers(pos):
    out = []
    for k in range(1, N_DEV):
        dev = tuple(1 - p if (k >> s) & 1 else p for p, s in zip(pos, (2, 1, 0)))
        out.append((dev, 4 * dev[0] + 2 * dev[1] + dev[2]))
    return out


_ANY = pl.BlockSpec(memory_space=pl.ANY)
_COMM_SCRATCH = [pltpu.SemaphoreType.DMA((N_DEV - 1,)), pltpu.SemaphoreType.DMA((N_DEV - 1,)), pltpu.SemaphoreType.DMA(())]


def all_gather(name, x):
    def body(x_ref, o_ref, send_sems, recv_sems, local_sem):
        pos = _my_pos()
        me = 4 * pos[0] + 2 * pos[1] + pos[2]
        peers = _peers(pos)
        mine = pltpu.make_async_copy(x_ref, o_ref.at[me], local_sem)
        mine.start()
        sends = [pltpu.make_async_remote_copy(x_ref, o_ref.at[me], send_sems.at[k], recv_sems.at[k],
                                              device_id=dev, device_id_type=MESH) for k, (dev, _) in enumerate(peers)]
        for cp in sends:
            cp.start()
        for k, (dev, pid) in enumerate(peers):
            pltpu.make_async_remote_copy(x_ref, o_ref.at[pid], send_sems.at[k], recv_sems.at[k],
                                         device_id=dev, device_id_type=MESH).wait_recv()
        for cp in sends:
            cp.wait_send()
        mine.wait()

    return pl.pallas_call(
        body, name=name, out_shape=jax.ShapeDtypeStruct((N_DEV,) + x.shape, x.dtype),
        in_specs=[_ANY], out_specs=_ANY, scratch_shapes=_COMM_SCRATCH,
    )(x)


def all_to_all(name, parts):
    def body(x_ref, o_ref, send_sems, recv_sems, local_sem):
        pos = _my_pos()
        me = 4 * pos[0] + 2 * pos[1] + pos[2]
        peers = _peers(pos)
        mine = pltpu.make_async_copy(x_ref.at[me], o_ref.at[me], local_sem)
        mine.start()
        sends = [pltpu.make_async_remote_copy(x_ref.at[pid], o_ref.at[me], send_sems.at[k], recv_sems.at[k],
                                              device_id=dev, device_id_type=MESH) for k, (dev, pid) in enumerate(peers)]
        for cp in sends:
            cp.start()
        for k, (dev, pid) in enumerate(peers):
            pltpu.make_async_remote_copy(x_ref.at[pid], o_ref.at[pid], send_sems.at[k], recv_sems.at[k],
                                         device_id=dev, device_id_type=MESH).wait_recv()
        for cp in sends:
            cp.wait_send()
        mine.wait()

    return pl.pallas_call(
        body, name=name, out_shape=jax.ShapeDtypeStruct(parts.shape, parts.dtype),
        in_specs=[_ANY], out_specs=_ANY, scratch_shapes=_COMM_SCRATCH,
    )(parts)


def lb_rows(r0, r1, r2, r3):
    mx = jnp.maximum(jnp.maximum(r0, r1), jnp.maximum(r2, r3))
    e = [jnp.exp(r - mx) for r in (r0, r1, r2, r3)]
    inv = 1.0 / (e[0] + e[1] + e[2] + e[3])
    c0 = e[0] * inv
    c1 = c0 + e[1] * inv
    c2 = c1 + e[2] * inv
    c3 = c2 + e[3] * inv
    return c0 - c0, c1 - c0, c2 - c0, c3 - c0


def mod_partial(name, c_all, ada_w):
    n, _, w = ada_w.shape

    def body(c_ref, w_ref, o_ref):
        o_ref[0] = mm(c_ref[...], w_ref[0])

    return pl.pallas_call(
        body, name=name, grid=(n,),
        in_specs=[_const_spec((N_DEV, D)), pl.BlockSpec((1, D, w), lambda i: (i, 0, 0))],
        out_specs=pl.BlockSpec((1, N_DEV, w), lambda i: (i, 0, 0)),
        out_shape=jax.ShapeDtypeStruct((n, N_DEV, w), F32),
    )(c_all, ada_w)


def prep(name, modp, ada_b, hgrn_lb):
    def body(mp_ref, b_ref, lb_ref, mod_ref, lbo_ref):
        mod_ref[...] = mp_ref[...] + b_ref[...]
        out = lb_rows(*[lb_ref[pl.ds(i, 1), :] for i in range(4)])
        for i in range(4):
            lbo_ref[pl.ds(i, 1), :] = out[i]

    return pl.pallas_call(
        body, name=name,
        out_shape=[jax.ShapeDtypeStruct(modp.shape, F32), jax.ShapeDtypeStruct(hgrn_lb.shape, F32)],
    )(modp, ada_b, hgrn_lb)


def lb_grad(name, hgrn_lb, dlb_parts):
    def body(lb_ref, d_ref, o_ref):
        cts = []
        for i in range(4):
            acc = d_ref[0, pl.ds(i, 1), :]
            for p in range(1, N_DEV):
                acc = acc + d_ref[p, pl.ds(i, 1), :]
            cts.append(acc)
        _, f = jax.vjp(lb_rows, *[lb_ref[pl.ds(i, 1), :] for i in range(4)])
        for i, g in enumerate(f(tuple(cts))):
            o_ref[pl.ds(i, 1), :] = g

    return pl.pallas_call(body, name=name, out_shape=jax.ShapeDtypeStruct(hgrn_lb.shape, F32))(hgrn_lb, dlb_parts)


def ada_grad(name, c_t, dm):
    n, _, w = dm.shape

    def body(c_ref, d_ref, o_ref):
        acc = c_ref[:, pl.ds(0, 1)] * d_ref[0, pl.ds(0, 1), :]
        for b in range(1, N_DEV):
            acc = acc + c_ref[:, pl.ds(b, 1)] * d_ref[0, pl.ds(b, 1), :]
        o_ref[0] = acc

    return pl.pallas_call(
        body, name=name, grid=(n,),
        in_specs=[_const_spec((D, N_DEV)), pl.BlockSpec((1, N_DEV, w), lambda i: (i, 0, 0))],
        out_specs=pl.BlockSpec((1, D, w), lambda i: (i, 0, 0)),
        out_shape=jax.ShapeDtypeStruct((n, D, w), F32),
    )(c_t, dm)


ADAMW_ROWS = 256


def adamw(name, w, m, v, gparts):
    shape = w.shape
    L = shape[-1]
    R = 1
    for s in shape[:-1]:
        R *= s
    P = gparts.shape[0]
    tr = min(R, ADAMW_ROWS)

    def body(w_ref, m_ref, v_ref, g_ref, go_ref, d_ref, mo_ref, vo_ref):
        g = g_ref[0]
        for p in range(1, P):
            g = g + g_ref[p]
        mn = ADAM_B1 * m_ref[...] + (1.0 - ADAM_B1) * g
        vn = ADAM_B2 * v_ref[...] + (1.0 - ADAM_B2) * (g * g)
        m_hat = mn / (1.0 - ADAM_B1 ** ADAM_STEP)
        v_hat = vn / (1.0 - ADAM_B2 ** ADAM_STEP)
        go_ref[...] = g
        d_ref[...] = -ADAM_LR * (m_hat / (jnp.sqrt(v_hat) + ADAM_EPS) + ADAM_WD * w_ref[...])
        mo_ref[...] = mn
        vo_ref[...] = vn

    spec = pl.BlockSpec((tr, L), lambda i: (i, 0))
    outs = pl.pallas_call(
        body, name=name, grid=(R // tr,),
        in_specs=[spec, spec, spec, pl.BlockSpec((P, tr, L), lambda i: (0, i, 0))],
        out_specs=[spec] * 4, out_shape=[jax.ShapeDtypeStruct((R, L), F32)] * 4,
        compiler_params=_params(dimension_semantics=("arbitrary",)),
    )(w.reshape(R, L), m.reshape(R, L), v.reshape(R, L), gparts.reshape(P, R, L))
    return [o.reshape(shape) for o in outs]


def _gather_cols(name, w):
    g = all_gather(name, w.astype(BF16))
    n, k, w8 = w.shape
    return jnp.transpose(g, (1, 2, 0, 3)).reshape(n, k, N_DEV * w8)


def _gather_rows(name, w):
    g = all_gather(name, w.astype(BF16))
    n, k8, nn = w.shape
    return jnp.transpose(g, (1, 0, 2, 3)).reshape(n, N_DEV * k8, nn)


def _scatter_cols(dw):
    n, k, nn = dw.shape
    return jnp.transpose(dw.reshape(n, k, N_DEV, nn // N_DEV), (2, 0, 1, 3))


def _scatter_rows(dw):
    n, k, nn = dw.shape
    return jnp.transpose(dw.reshape(n, N_DEV, k // N_DEV, nn), (1, 0, 2, 3))


def _pad_lanes(v, width=128):
    return jnp.pad(v, ((0, 0), (0, width - v.shape[1])))


def _stack_rows(parts):
    n, L = len(parts), parts[0].shape[1]
    r = lax.broadcasted_iota(jnp.int32, (n, L), 0)
    out = jnp.zeros((n, L), parts[0].dtype)
    for i, p in enumerate(parts):
        out = jnp.where(r == i, jnp.broadcast_to(p, (n, L)), out)
    return out


def kernel(x, c, positions, hgrn_lb, ada_w, ada_b, norm_g, hg_in_w, hg_out_w, hg_onorm, sw_in_w, sw_out_w, sw_qnorm, sw_knorm, sw_sinks, gd_in_w, gd_out_w, gd_conv_w, gd_a_log, gd_dt_bias, gd_onorm, loss_target, m_hgrn_lb, m_ada_w, m_ada_b, m_norm_g, m_hg_in_w, m_hg_out_w, m_hg_onorm, m_sw_in_w, m_sw_out_w, m_sw_qnorm, m_sw_knorm, m_sw_sinks, m_gd_in_w, m_gd_out_w, m_gd_conv_w, m_gd_a_log, m_gd_dt_bias, m_gd_onorm, v_hgrn_lb, v_ada_w, v_ada_b, v_norm_g, v_hg_in_w, v_hg_out_w, v_hg_onorm, v_sw_in_w, v_sw_out_w, v_sw_qnorm, v_sw_knorm, v_sw_sinks, v_gd_in_w, v_gd_out_w, v_gd_conv_w, v_gd_a_log, v_gd_dt_bias, v_gd_onorm):
    pos = _my_pos()
    me = 4 * pos[0] + 2 * pos[1] + pos[2]
    x0 = x[0]
    tgt = loss_target[0]
    n_layers = norm_g.shape[0]
    aw = ada_w.shape[2]

    c_all = all_gather("ag_c", c)[:, 0, :]
    modp = all_gather("ag_mod", mod_partial("mod_partial", c_all, ada_w))
    modp_mine = lax.dynamic_index_in_dim(modp, me, axis=2, keepdims=False)
    modp_mine = jnp.transpose(modp_mine, (1, 0, 2)).reshape(n_layers, N_DEV * aw)
    mod, lb_all = prep("prep", modp_mine, ada_b, hgrn_lb)
    shift, scale, gate = mod[:, :D], mod[:, D:2 * D], mod[:, 2 * D:]
    row = lambda a, i: a[i:i + 1]

    w_hg_in = _gather_cols("ag_hg_in", hg_in_w)
    w_hg_out = _gather_rows("ag_hg_out", hg_out_w)
    w_sw_in = _gather_cols("ag_sw_in", sw_in_w)[0]
    w_sw_out = _gather_rows("ag_sw_out", sw_out_w)[0]
    w_gd_in = _gather_cols("ag_gd_in", gd_in_w)[0]
    w_gd_out = _gather_rows("ag_gd_out", gd_out_w)[0]
    w_gd_qkv, w_gd_z = w_gd_in[:, :GD_QKV], w_gd_in[:, GD_QKV:GD_QKV + GD_VW]
    w_gd_ab = _pad_lanes(w_gd_in[:, GD_QKV + GD_VW:])
    conv_all = all_gather("ag_conv", gd_conv_w)
    conv_w = jnp.transpose(conv_all[:, 0], (1, 0, 2)).reshape(CONV_K, GD_QKV)
    conv_w8 = jnp.pad(conv_w, ((0, HALO - CONV_K), (0, 0)))
    gd_al, gd_dt = _pad_lanes(gd_a_log), _pad_lanes(gd_dt_bias)

    inv_freq = ROPE_THETA ** (-jnp.arange(0, SW_DH, 2, dtype=F32) / SW_DH)
    ang = positions[0].astype(F32)[:, None] * inv_freq
    cos, sin = jnp.cos(ang), jnp.sin(ang)
    cos64, sin64 = jnp.concatenate([cos, cos], axis=-1), jnp.concatenate([-sin, sin], axis=-1)

    xs, hs, ys, saved = [x0], [], [], []
    h = norm_mod("norm0", x0, row(norm_g, 0), row(scale, 0), row(shift, 0))
    for i in range(n_layers):
        kind, j = i % 3, i // 3
        hs.append(h)
        if kind == 0:
            p = matmul(f"hg_in{i}", h, w_hg_in[j])
            u, st = hg_fwd(f"hg_fwd{i}", p, row(lb_all, i), row(hg_onorm, j))
            y = matmul(f"hg_out{i}", u, w_hg_out[j])
            saved.append((p, u, st))
        elif kind == 1:
            p = matmul(f"sw_in{i}", h, w_sw_in)
            u, kpost = swa_fwd(f"sw_fwd{i}", p, cos64, sin64, row(sw_qnorm, j), row(sw_knorm, j), row(sw_sinks, j))
            y = matmul(f"sw_out{i}", u, w_sw_out)
            saved.append((p, u, kpost))
        else:
            xq = matmul(f"gd_qkv{i}", h, w_gd_qkv)
            zz = matmul(f"gd_z{i}", h, w_gd_z)
            ab = matmul(f"gd_ab{i}", h, w_gd_ab)
            cv = conv_fwd(f"gd_conv{i}", xq, conv_w8)
            u, st = gd_fwd(f"gd_fwd{i}", cv, zz, ab, gd_al, gd_dt, row(gd_onorm, j))
            y = matmul(f"gd_out{i}", u, w_gd_out)
            saved.append((xq, zz, ab, cv, u, st))
        ys.append(y)
        if i + 1 < n_layers:
            xn, h = resid_norm(f"resid{i}", xs[i], y, row(gate, i), row(norm_g, i + 1), row(scale, i + 1), row(shift, i + 1))
            xs.append(xn)

    last = n_layers - 1
    dx, dy, loss_acc, dgate_last = loss_head("loss", xs[last], ys[last], tgt, row(gate, last))
    loss = lax.psum(loss_acc[0, 0], ("x", "y", "c"))

    dgate = [None] * n_layers
    dgate[last] = dgate_last
    dg_norm, dscale, dshift = [None] * n_layers, [None] * n_layers, [None] * n_layers
    dlb = [jnp.zeros((1, D), F32)] * n_layers
    d_hg_in, d_hg_out, d_hg_on = [None] * 2, [None] * 2, [None] * 2
    for i in range(last, -1, -1):
        kind, j = i % 3, i // 3
        h = hs[i]
        if kind == 0:
            p, u, st = saved[i]
            du = matmul_nt(f"hg_du{i}", [(dy, w_hg_out[j])])
            d_hg_out[j] = matmul_tn(f"hg_dwo{i}", u, dy)
            dp, dlb_i, dog = hg_bwd(f"hg_bwd{i}", p, row(lb_all, i), row(hg_onorm, j), st, du)
            dlb[i] = dlb_i
            d_hg_on[j] = dog
            dh = matmul_nt(f"hg_dh{i}", [(dp, w_hg_in[j])])
            d_hg_in[j] = matmul_tn(f"hg_dwi{i}", h, dp)
        elif kind == 1:
            p, u, kpost = saved[i]
            du = matmul_nt(f"sw_du{i}", [(dy, w_sw_out)])
            d_sw_out = matmul_tn(f"sw_dwo{i}", u, dy)
            dp, d_qn, d_kn, d_sk = swa_bwd(f"sw_bwd{i}", p, kpost, cos64, sin64, row(sw_qnorm, j), row(sw_knorm, j),
                                           row(sw_sinks, j), du)
            dh = matmul_nt(f"sw_dh{i}", [(dp, w_sw_in)])
            d_sw_in = matmul_tn(f"sw_dwi{i}", h, dp)
        else:
            xq, zz, ab, cv, u, st = saved[i]
            du = matmul_nt(f"gd_du{i}", [(dy, w_gd_out)])
            d_gd_out = matmul_tn(f"gd_dwo{i}", u, dy)
            dcv, dz, dab, d_al, d_dt, d_gd_on = gd_bwd(f"gd_bwd{i}", cv, zz, ab, gd_al, gd_dt, row(gd_onorm, j), st, du)
            dxq, d_conv8 = conv_bwd(f"gd_dconv{i}", xq, dcv, conv_w8)
            dh = matmul_nt(f"gd_dh{i}", [(dxq, w_gd_qkv), (dz, w_gd_z), (dab, w_gd_ab)])
            d_gd_in = jnp.concatenate([matmul_tn(f"gd_dwq{i}", h, dxq), matmul_tn(f"gd_dwz{i}", h, dz),
                                       matmul_tn(f"gd_dwab{i}", h, dab)[:, :2 * GD_VH]], axis=1)
        if i > 0:
            dx, dy, dg_norm[i], dscale[i], dshift[i], dgate[i - 1] = bwd_rowwise(
                f"bwd_row{i}", xs[i], dh, dx, row(norm_g, i), row(scale, i), row(shift, i), ys[i - 1], row(gate, i - 1))
        else:
            dx, dg_norm[i], dscale[i], dshift[i] = bwd_rowwise(
                f"bwd_row{i}", xs[i], dh, dx, row(norm_g, i), row(scale, i), row(shift, i))
    grad_x = dx[None]

    dmod = jnp.concatenate([_stack_rows(dshift), _stack_rows(dscale), _stack_rows(dgate)], axis=1)
    misc = _stack_rows(d_hg_on + [d_gd_on, _pad_lanes(d_qn), _pad_lanes(d_kn), _pad_lanes(d_sk), d_al, d_dt])
    small = jnp.concatenate([_stack_rows(dlb).reshape(-1, 128), _stack_rows(dg_norm).reshape(-1, 128),
                             dmod.reshape(-1, 128), misc], axis=0)
    small_all = all_gather("ag_small", small)
    n_lb = n_layers * D // 128
    n_mod = n_layers * 3 * D // 128
    o = 0
    dlb_parts = small_all[:, o:o + n_lb].reshape(N_DEV, n_layers, D); o += n_lb
    dgn_parts = small_all[:, o:o + n_lb].reshape(N_DEV, n_layers, D); o += n_lb
    dmod_parts = small_all[:, o:o + n_mod].reshape(N_DEV, n_layers, 3 * D); o += n_mod
    dhgon_parts = small_all[:, o:o + 2]; o += 2
    dgdon_parts = small_all[:, o:o + 1]; o += 1
    dqn_parts = small_all[:, o:o + 1, :SW_DH]; o += 1
    dkn_parts = small_all[:, o:o + 1, :SW_DH]; o += 1
    dsk_parts = small_all[:, o:o + 1, :SW_KV * SW_G]; o += 1
    dal_parts = small_all[:, o:o + 1, :GD_VH]; o += 1
    ddt_parts = small_all[:, o:o + 1, :GD_VH]; o += 1

    g_lb = lb_grad("lb_grad", hgrn_lb, dlb_parts)
    dm_mine = lax.dynamic_slice_in_dim(dmod_parts, me * aw, aw, axis=2)
    g_ada_w = ada_grad("ada_grad", jnp.transpose(c_all), jnp.transpose(dm_mine, (1, 0, 2)))

    res = {}
    res["hgrn_lb"] = adamw("aw_hgrn_lb", hgrn_lb, m_hgrn_lb, v_hgrn_lb, g_lb[None])
    res["ada_w"] = adamw("aw_ada_w", ada_w, m_ada_w, v_ada_w, g_ada_w[None])
    res["ada_b"] = adamw("aw_ada_b", ada_b, m_ada_b, v_ada_b, dmod_parts)
    res["norm_g"] = adamw("aw_norm_g", norm_g, m_norm_g, v_norm_g, dgn_parts)
    res["hg_in_w"] = adamw("aw_hg_in", hg_in_w, m_hg_in_w, v_hg_in_w,
                           all_to_all("rs_hg_in", _scatter_cols(jnp.stack(d_hg_in))))
    res["hg_out_w"] = adamw("aw_hg_out", hg_out_w, m_hg_out_w, v_hg_out_w,
                            all_to_all("rs_hg_out", _scatter_rows(jnp.stack(d_hg_out))))
    res["hg_onorm"] = adamw("aw_hg_onorm", hg_onorm, m_hg_onorm, v_hg_onorm, dhgon_parts)
    res["sw_in_w"] = adamw("aw_sw_in", sw_in_w, m_sw_in_w, v_sw_in_w, all_to_all("rs_sw_in", _scatter_cols(d_sw_in[None])))
    res["sw_out_w"] = adamw("aw_sw_out", sw_out_w, m_sw_out_w, v_sw_out_w, all_to_all("rs_sw_out", _scatter_rows(d_sw_out[None])))
    res["sw_qnorm"] = adamw("aw_sw_qn", sw_qnorm, m_sw_qnorm, v_sw_qnorm, dqn_parts)
    res["sw_knorm"] = adamw("aw_sw_kn", sw_knorm, m_sw_knorm, v_sw_knorm, dkn_parts)
    res["sw_sinks"] = adamw("aw_sw_sinks", sw_sinks, m_sw_sinks, v_sw_sinks, dsk_parts)
    res["gd_in_w"] = adamw("aw_gd_in", gd_in_w, m_gd_in_w, v_gd_in_w, all_to_all("rs_gd_in", _scatter_cols(d_gd_in[None])))
    res["gd_out_w"] = adamw("aw_gd_out", gd_out_w, m_gd_out_w, v_gd_out_w, all_to_all("rs_gd_out", _scatter_rows(d_gd_out[None])))
    res["gd_conv_w"] = adamw("aw_gd_conv", gd_conv_w, m_gd_conv_w, v_gd_conv_w,
                             all_to_all("rs_gd_conv", _scatter_cols(d_conv8[None, :CONV_K])))
    res["gd_a_log"] = adamw("aw_gd_alog", gd_a_log, m_gd_a_log, v_gd_a_log, dal_parts)
    res["gd_dt_bias"] = adamw("aw_gd_dt", gd_dt_bias, m_gd_dt_bias, v_gd_dt_bias, ddt_parts)
    res["gd_onorm"] = adamw("aw_gd_onorm", gd_onorm, m_gd_onorm, v_gd_onorm, dgdon_parts)

    order = ["hgrn_lb", "ada_w", "ada_b", "norm_g", "hg_in_w", "hg_out_w", "hg_onorm", "sw_in_w", "sw_out_w", "sw_qnorm",
             "sw_knorm", "sw_sinks", "gd_in_w", "gd_out_w", "gd_conv_w", "gd_a_log", "gd_dt_bias", "gd_onorm"]
    outs = [loss, grad_x]
    for part in range(4):
        outs += [res[n][part] for n in order]
    return tuple(outs)
```

```python
import functools

import jax
import jax.numpy as jnp
from jax import lax
from jax.experimental import pallas as pl
from jax.experimental.pallas import tpu as pltpu

F32, BF16 = jnp.float32, jnp.bfloat16
HI = lax.Precision.HIGHEST
MESH = pl.DeviceIdType.MESH
N_DEV = 8
D = 1024
EPS = 1e-6
NEG_INF = float("-inf")

HG_HEADS, HG_DH, HG_BLK, HG_SUB, HG_HB = 8, 128, 128, 8, 4
SW_KV, SW_G, SW_DH, SW_BLK = 4, 4, 64, 128
SW_QW, SW_KVW = 1024, 256
GD_QK_HEADS, GD_DK, GD_CHUNK = 8, 128, 64
GD_QKW, GD_VW, GD_QKV = 1024, 2048, 4096
CONV_K = 4
ROPE_THETA = 10000.0

ADAM_LR, ADAM_B1, ADAM_B2, ADAM_EPS, ADAM_WD, ADAM_STEP = 0.001, 0.9, 0.999, 1e-08, 0.01, 10

VMEM_LIMIT = 56 * 1024 * 1024


def _params(**kw):
    return pltpu.CompilerParams(vmem_limit_bytes=VMEM_LIMIT, **kw)


def _dot(a, b, ca, cb):
    return lax.dot_general(a.astype(BF16), b.astype(BF16), (((ca,), (cb,)), ((), ())), preferred_element_type=F32)


def mm(a, b):
    return _dot(a, b, 1, 0)


def mm_nt(a, b):
    return _dot(a, b, 1, 1)


def mm_tn(a, b):
    return _dot(a, b, 0, 0)


def mmf(a, b):
    return lax.dot_general(a, b, (((1,), (0,)), ((), ())), precision=HI, preferred_element_type=F32)


def _split3(x):
    h1 = x.astype(BF16)
    r1 = x - h1.astype(F32)
    h2 = r1.astype(BF16)
    h3 = (r1 - h2.astype(F32)).astype(BF16)
    return h1, h2, h3


def _lin01_l(m, x):
    mb = m.astype(BF16)
    return sum(lax.dot_general(mb, p, (((1,), (0,)), ((), ())), preferred_element_type=F32) for p in _split3(x))


def _lin01_r(x, m):
    mb = m.astype(BF16)
    return sum(lax.dot_general(p, mb, (((1,), (0,)), ((), ())), preferred_element_type=F32) for p in _split3(x))


def _bdot(a, b, ca, cb, prec=None):
    return lax.dot_general(a, b, (((ca,), (cb,)), ((0,), (0,))), precision=prec, preferred_element_type=F32)


def bmm(a, b):
    return _bdot(a.astype(BF16), b.astype(BF16), 2, 1)


def bmm_nt(a, b):
    return _bdot(a.astype(BF16), b.astype(BF16), 2, 2)


def bmm_tn(a, b):
    return _bdot(a.astype(BF16), b.astype(BF16), 1, 1)


def bmmf(a, b):
    return _bdot(a, b, 2, 1, lax.Precision.HIGH)


def _blin01(m, x):
    mb = m.astype(BF16)
    return sum(_bdot(mb, p, 2, 1) for p in _split3(x))


@jax.custom_vjp
def blin01(m, mt, x):
    return _blin01(m, x)


def _blin01_fwd(m, mt, x):
    return _blin01(m, x), (m, mt)


def _blin01_bwd(res, g):
    m, mt = res
    return jnp.zeros_like(m), jnp.zeros_like(mt), _blin01(mt, g)


blin01.defvjp(_blin01_fwd, _blin01_bwd)


def _blin01_r(x, m):
    mb = m.astype(BF16)
    return sum(_bdot(p, mb, 2, 1) for p in _split3(x))


@jax.custom_vjp
def blin01_r(x, m, mt):
    return _blin01_r(x, m)


def _blin01_r_fwd(x, m, mt):
    return _blin01_r(x, m), (m, mt)


def _blin01_r_bwd(res, g):
    m, mt = res
    return _blin01_r(g, mt), jnp.zeros_like(m), jnp.zeros_like(mt)


blin01_r.defvjp(_blin01_r_fwd, _blin01_r_bwd)


@jax.custom_vjp
def lin01_l(m, mt, x):
    return _lin01_l(m, x)


def _lin01_l_fwd(m, mt, x):
    return _lin01_l(m, x), (m, mt)


def _lin01_l_bwd(res, g):
    m, mt = res
    return jnp.zeros_like(m), jnp.zeros_like(mt), _lin01_l(mt, g)


lin01_l.defvjp(_lin01_l_fwd, _lin01_l_bwd)


@jax.custom_vjp
def lin01_r(x, m, mt):
    return _lin01_r(x, m)


def _lin01_r_fwd(x, m, mt):
    return _lin01_r(x, m), (m, mt)


def _lin01_r_bwd(res, g):
    m, mt = res
    return _lin01_r(g, mt), jnp.zeros_like(m), jnp.zeros_like(mt)


lin01_r.defvjp(_lin01_r_fwd, _lin01_r_bwd)


@functools.partial(jax.custom_vjp, nondiff_argnums=(1, 2))
def rows(x, start, size):
    return lax.slice_in_dim(x, start, start + size, axis=x.ndim - 2)


def _rows_fwd(x, start, size):
    return lax.slice_in_dim(x, start, start + size, axis=x.ndim - 2), jnp.zeros(x.shape[:-1] + (1,), F32)


def _rows_bwd(start, size, res, g):
    ax = g.ndim - 2
    total = res.shape[ax]
    zeros = lambda n: jnp.zeros(g.shape[:ax] + (n,) + g.shape[ax + 1:], g.dtype)
    parts = []
    if start > 0:
        parts.append(zeros(start))
    parts.append(g)
    if total - start - size > 0:
        parts.append(zeros(total - start - size))
    return (jnp.concatenate(parts, axis=ax) if len(parts) > 1 else g,)


rows.defvjp(_rows_fwd, _rows_bwd)


@functools.partial(jax.custom_vjp, nondiff_argnums=(1,))
def rowsel(x, s):
    return lax.slice_in_dim(x, s, s + 1, axis=x.ndim - 2)


def _rowsel_fwd(x, s):
    return lax.slice_in_dim(x, s, s + 1, axis=x.ndim - 2), jnp.zeros(x.shape[:-1] + (1,), F32)


def _rowsel_bwd(s, res, g):
    r = lax.broadcasted_iota(jnp.int32, res.shape, res.ndim - 2)
    return (jnp.where(r == s, g, 0.0),)


rowsel.defvjp(_rowsel_fwd, _rowsel_bwd)


def sigmoid(x):
    return jax.nn.sigmoid(x)


def silu(x):
    return x * jax.nn.sigmoid(x)


def softplus(x):
    return jnp.maximum(x, 0.0) + jnp.log1p(jnp.exp(-jnp.abs(x)))


def _iota(shape, dim):
    return lax.broadcasted_iota(jnp.int32, shape, dim)


def norm_mod_fn(g, sc, sh, x):
    r = lax.rsqrt(jnp.mean(x * x, axis=-1, keepdims=True) + EPS)
    return (x * r * g) * (1.0 + sc) + sh


def _row_spec(tm, f):
    return pl.BlockSpec((tm, f), lambda i: (i, 0))


def _const_spec(shape):
    nd = len(shape)
    return pl.BlockSpec(shape, lambda i: (0,) * nd)


def _tiled(name, fn, consts, rows_in, row_outs, acc_outs=(), tm=512):
    T = rows_in[0].shape[0]
    tm = min(tm, T)
    n_c, n_r, n_ro = len(consts), len(rows_in), len(row_outs)

    def body(*refs):
        c_refs, r_refs = refs[:n_c], refs[n_c:n_c + n_r]
        ro_refs, ao_refs = refs[n_c + n_r:n_c + n_r + n_ro], refs[n_c + n_r + n_ro:]
        outs = fn(*[r[...] for r in c_refs], *[r[...] for r in r_refs])
        for r, v in zip(ro_refs, outs[:n_ro]):
            r[...] = v.astype(r.dtype)
        if ao_refs:
            @pl.when(pl.program_id(0) == 0)
            def _():
                for r in ao_refs:
                    r[...] = jnp.zeros(r.shape, r.dtype)
            for r, v in zip(ao_refs, outs[n_ro:]):
                r[...] += v

    out_shape = [jax.ShapeDtypeStruct((T, f), dt) for f, dt in row_outs] + [jax.ShapeDtypeStruct(s, F32) for s in acc_outs]
    out_specs = [_row_spec(tm, f) for f, _ in row_outs] + [_const_spec(s) for s in acc_outs]
    in_specs = [_const_spec(c.shape) for c in consts] + [_row_spec(tm, r.shape[1]) for r in rows_in]
    return pl.pallas_call(
        body, name=name, grid=(T // tm,), in_specs=in_specs, out_specs=out_specs, out_shape=out_shape,
        compiler_params=_params(dimension_semantics=("arbitrary",)),
    )(*consts, *rows_in)


def norm_mod(name, x, g, sc, sh):
    (h,) = _tiled(name, lambda g, sc, sh, x: (norm_mod_fn(g, sc, sh, x),), [g, sc, sh], [x], [(D, BF16)])
    return h


def resid_norm(name, x, y, gate, g, sc, sh):
    def fn(gate, g, sc, sh, x, y):
        xn = x + gate * y
        return xn, norm_mod_fn(g, sc, sh, xn)
    return _tiled(name, fn, [gate, g, sc, sh], [x, y], [(D, F32), (D, BF16)])


def loss_head(name, x, y, tgt, gate):
    def fn(gate, x, y, tgt):
        err = x + gate * y - tgt
        dx = err * (1.0 / D)
        per_tok = jnp.sum(err * err, axis=-1, keepdims=True) * (0.5 / D)
        loss = jnp.sum(per_tok, axis=0, keepdims=True)
        return dx, gate * dx, jnp.broadcast_to(loss, (1, 128)), jnp.sum(dx * y, axis=0, keepdims=True)
    return _tiled(name, fn, [gate], [x, y, tgt], [(D, F32), (D, BF16)], [(1, 128), (1, D)])


def bwd_rowwise(name, x, dh, dxn, g, sc, sh, y_prev=None, gate_prev=None):
    with_prev = y_prev is not None

    def fn(*a):
        if with_prev:
            gate_p, g, sc, sh, x, dh, dxn, yp = a
        else:
            g, sc, sh, x, dh, dxn = a
        _, f = jax.vjp(norm_mod_fn, g, sc, sh, x)
        dg, dsc, dsh, dx = f(dh)
        dx = dx + dxn
        if with_prev:
            return dx, gate_p * dx, dg, dsc, dsh, jnp.sum(dx * yp, axis=0, keepdims=True)
        return dx, dg, dsc, dsh

    if with_prev:
        return _tiled(name, fn, [gate_prev, g, sc, sh], [x, dh, dxn, y_prev], [(D, F32), (D, BF16)], [(1, D)] * 4)
    return _tiled(name, fn, [g, sc, sh], [x, dh, dxn], [(D, F32)], [(1, D)] * 3)


def matmul(name, a, w, out_dtype=F32, tm=256):
    T, K = a.shape
    N = w.shape[1]
    tm = min(tm, T)

    def body(a_ref, w_ref, o_ref):
        o_ref[...] = mm(a_ref[...], w_ref[...]).astype(o_ref.dtype)

    return pl.pallas_call(
        body, name=name, grid=(T // tm,),
        in_specs=[_row_spec(tm, K), _const_spec((K, N))], out_specs=_row_spec(tm, N),
        out_shape=jax.ShapeDtypeStruct((T, N), out_dtype),
        compiler_params=_params(dimension_semantics=("arbitrary",)),
    )(a, w)


def matmul_nt(name, pairs, out_dtype=F32, tm=256):
    T = pairs[0][0].shape[0]
    K = pairs[0][1].shape[0]
    tm = min(tm, T)
    n = len(pairs)

    def body(*refs):
        o_ref = refs[-1]
        acc = None
        for i in range(n):
            p = mm_nt(refs[2 * i][...], refs[2 * i + 1][...])
            acc = p if acc is None else acc + p
        o_ref[...] = acc.astype(o_ref.dtype)

    in_specs, args = [], []
    for a, w in pairs:
        in_specs += [_row_spec(tm, a.shape[1]), _const_spec(w.shape)]
        args += [a, w]
    return pl.pallas_call(
        body, name=name, grid=(T // tm,), in_specs=in_specs, out_specs=_row_spec(tm, K),
        out_shape=jax.ShapeDtypeStruct((T, K), out_dtype),
        compiler_params=_params(dimension_semantics=("arbitrary",)),
    )(*args)


def matmul_tn(name, a, b, tm=512, tn=1024):
    T, K = a.shape
    N = b.shape[1]
    tm = min(tm, T)
    tn = max(t for t in range(128, min(tn, N) + 1, 128) if N % t == 0)

    def body(a_ref, b_ref, o_ref):
        @pl.when(pl.program_id(1) == 0)
        def _():
            o_ref[...] = jnp.zeros(o_ref.shape, F32)
        o_ref[...] += mm_tn(a_ref[...], b_ref[...])

    return pl.pallas_call(
        body, name=name, grid=(N // tn, T // tm),
        in_specs=[pl.BlockSpec((tm, K), lambda j, i: (i, 0)), pl.BlockSpec((tm, tn), lambda j, i: (i, j))],
        out_specs=pl.BlockSpec((K, tn), lambda j, i: (0, j)),
        out_shape=jax.ShapeDtypeStruct((K, N), F32),
        compiler_params=_params(dimension_semantics=("arbitrary", "arbitrary")),
    )(a, b)


def hg_heads(tri, trit, lb, og, st, qp, fp, iv, z):
    blk = qp.shape[1]
    q = silu(qp)
    lf = jnp.log(lb + (1.0 - lb) * sigmoid(fp))
    k = (1.0 - lb) * sigmoid(-fp)
    r = _iota((1, HG_SUB, 1), 1)
    outs = []
    for j in range(blk // HG_SUB):
        qj, kj, vj, lfj = (rows(a, j * HG_SUB, HG_SUB) for a in (q, k, iv, lf))
        b = blin01(tri, trit, lfj)
        o = bmm_nt(qj * jnp.exp(b), st)
        for s in range(HG_SUB):
            ks, bs, vs = rowsel(kj, s), rowsel(b, s), rowsel(vj, s)
            e = jnp.exp(jnp.where(r >= s, b - bs, NEG_INF))
            o = o + jnp.sum(qj * ks * e, axis=-1, keepdims=True) * vs
        bl = jnp.sum(lfj, axis=1, keepdims=True)
        st = st * jnp.exp(bl) + bmm_tn(vj, kj * jnp.exp(bl - b))
        outs.append(o)
    o = jnp.concatenate(outs, axis=1)
    on = o * lax.rsqrt(jnp.mean(o * o, axis=-1, keepdims=True) + EPS) * og
    return st, on * silu(z)


def _tri(n, nh):
    t = (_iota((nh, n, n), 1) >= _iota((nh, n, n), 2)).astype(F32)
    tt = (_iota((nh, n, n), 1) <= _iota((nh, n, n), 2)).astype(F32)
    return t, tt


def _hg_load(p_ref, lb_ref, heads):
    col = lambda ref, base: jnp.stack([ref[:, pl.ds(base + h * HG_DH, HG_DH)] for h in heads])
    return col(lb_ref, 0), col(p_ref, 0), col(p_ref, D), col(p_ref, 2 * D), col(p_ref, 3 * D)


def hg_fwd(name, p, lb, og):
    T = p.shape[0]
    nb = T // HG_BLK

    def body(p_ref, lb_ref, og_ref, u_ref, st_ref, s_scr):
        @pl.when(pl.program_id(0) == 0)
        def _():
            s_scr[...] = jnp.zeros(s_scr.shape, F32)
        st_ref[0] = s_scr[...]
        tri, trit = _tri(HG_SUB, HG_HB)
        og_v = og_ref[...]
        for b0 in range(0, HG_HEADS, HG_HB):
            heads = list(range(b0, b0 + HG_HB))
            lbh, qp, fp, iv, z = _hg_load(p_ref, lb_ref, heads)
            st, u = hg_heads(tri, trit, lbh, og_v, s_scr[pl.ds(b0, HG_HB)], qp, fp, iv, z)
            s_scr[pl.ds(b0, HG_HB)] = st
            for i, h in enumerate(heads):
                u_ref[:, pl.ds(h * HG_DH, HG_DH)] = u[i].astype(BF16)

    return pl.pallas_call(
        body, name=name, grid=(nb,),
        in_specs=[_row_spec(HG_BLK, 4 * D), _const_spec((1, D)), _const_spec((1, HG_DH))],
        out_specs=[_row_spec(HG_BLK, D), pl.BlockSpec((1, HG_HEADS, HG_DH, HG_DH), lambda i: (i, 0, 0, 0))],
        out_shape=[jax.ShapeDtypeStruct((T, D), BF16), jax.ShapeDtypeStruct((nb, HG_HEADS, HG_DH, HG_DH), F32)],
        scratch_shapes=[pltpu.VMEM((HG_HEADS, HG_DH, HG_DH), F32)],
        compiler_params=_params(dimension_semantics=("arbitrary",)),
    )(p, lb, og)


def hg_bwd(name, p, lb, og, states, du):
    T = p.shape[0]
    nb = T // HG_BLK

    def body(p_ref, lb_ref, og_ref, st_ref, du_ref, dp_ref, dlb_ref, dog_ref, ds_scr):
        @pl.when(pl.program_id(0) == 0)
        def _():
            ds_scr[...] = jnp.zeros(ds_scr.shape, F32)
            dlb_ref[...] = jnp.zeros(dlb_ref.shape, F32)
            dog_ref[...] = jnp.zeros(dog_ref.shape, F32)
        tri, trit = _tri(HG_SUB, HG_HB)
        og_v = og_ref[...]
        fn = functools.partial(hg_heads, tri, trit)
        for b0 in range(0, HG_HEADS, HG_HB):
            heads = list(range(b0, b0 + HG_HB))
            lbh, qp, fp, iv, z = _hg_load(p_ref, lb_ref, heads)
            _, f = jax.vjp(fn, lbh, og_v, st_ref[0, pl.ds(b0, HG_HB)], qp, fp, iv, z)
            du_h = jnp.stack([du_ref[:, pl.ds(h * HG_DH, HG_DH)].astype(F32) for h in heads])
            dlb, dog, dst, dq, df, di, dz = f((ds_scr[pl.ds(b0, HG_HB)], du_h))
            ds_scr[pl.ds(b0, HG_HB)] = dst
            dog_ref[...] += dog
            for i, h in enumerate(heads):
                for base, g in ((0, dq), (D, df), (2 * D, di), (3 * D, dz)):
                    dp_ref[:, pl.ds(base + h * HG_DH, HG_DH)] = g[i].astype(BF16)
                dlb_ref[:, pl.ds(h * HG_DH, HG_DH)] += dlb[i]

    rev = lambda i: (nb - 1 - i, 0)
    return pl.pallas_call(
        body, name=name, grid=(nb,),
        in_specs=[pl.BlockSpec((HG_BLK, 4 * D), rev), _const_spec((1, D)), _const_spec((1, HG_DH)),
                  pl.BlockSpec((1, HG_HEADS, HG_DH, HG_DH), lambda i: (nb - 1 - i, 0, 0, 0)),
                  pl.BlockSpec((HG_BLK, D), rev)],
        out_specs=[pl.BlockSpec((HG_BLK, 4 * D), rev), _const_spec((1, D)), _const_spec((1, HG_DH))],
        out_shape=[jax.ShapeDtypeStruct((T, 4 * D), BF16), jax.ShapeDtypeStruct((1, D), F32),
                   jax.ShapeDtypeStruct((1, HG_DH), F32)],
        scratch_shapes=[pltpu.VMEM((HG_HEADS, HG_DH, HG_DH), F32)],
        compiler_params=_params(dimension_semantics=("arbitrary",)),
    )(p, lb, og, states, du)


def swa_heads(qn, kn, sinks, kprev, vprev, qp4, kp, v, z4, cos4, sin4, cosk, sink_, hots, mask, p64, p64t):
    def norm_rope(xp, g, cs, sn):
        y = xp * lax.rsqrt(jnp.mean(xp * xp, axis=-1, keepdims=True) + EPS) * g
        return y * cs + blin01_r(y, p64, p64t) * sn

    q = norm_rope(qp4, qn, cos4, sin4)
    k = norm_rope(kp, kn, cosk, sink_)
    k2 = jnp.concatenate([kprev, k], axis=1)
    v2 = jnp.concatenate([vprev, v], axis=1)
    s = jnp.where(mask, bmm_nt(q, k2) * (SW_DH ** -0.5), NEG_INF)
    sink_col = jnp.stack([jnp.concatenate(
        [jnp.broadcast_to(jnp.sum(sinks * hot, axis=-1, keepdims=True), (SW_BLK, 1)) for hot in hot4], axis=0)
        for hot4 in hots])
    m = lax.stop_gradient(jnp.maximum(jnp.max(s, axis=-1, keepdims=True), sink_col))
    p = jnp.exp(s - m)
    p = p / (jnp.sum(p, axis=-1, keepdims=True) + jnp.exp(sink_col - m))
    o = bmm(p, v2)
    return k, v, o * silu(z4)


def _swa_consts(first):
    qi = _iota((SW_G * SW_BLK, 2 * SW_BLK), 0) % SW_BLK
    kj = _iota((SW_G * SW_BLK, 2 * SW_BLK), 1)
    rel = qi + SW_BLK - kj
    mask = (rel >= 0) & (rel < SW_BLK) & ((kj >= SW_BLK) | jnp.logical_not(first))
    i, j = _iota((SW_KV, SW_DH, SW_DH), 1), _iota((SW_KV, SW_DH, SW_DH), 2)
    p64 = (i == (j + SW_DH // 2) % SW_DH).astype(F32)
    p64t = (j == (i + SW_DH // 2) % SW_DH).astype(F32)
    lane = _iota((1, SW_KV * SW_G), 1)
    hots = [[(lane == h * SW_G + g).astype(F32) for g in range(SW_G)] for h in range(SW_KV)]
    return mask, p64, p64t, hots


def _stack_q(ref, h, base):
    return jnp.concatenate([ref[:, pl.ds(base + h * SW_G * SW_DH + g * SW_DH, SW_DH)] for g in range(SW_G)], axis=0)


def _swa_load(p_ref):
    heads = range(SW_KV)
    return (jnp.stack([_stack_q(p_ref, h, Q0) for h in heads]),
            jnp.stack([p_ref[:, pl.ds(K0 + h * SW_DH, SW_DH)] for h in heads]),
            jnp.stack([p_ref[:, pl.ds(V0 + h * SW_DH, SW_DH)] for h in heads]),
            jnp.stack([_stack_q(p_ref, h, Z0) for h in heads]))


Q0, K0, V0, Z0 = 0, SW_QW, SW_QW + SW_KVW, SW_QW + 2 * SW_KVW
SW_IN = 2 * SW_QW + 2 * SW_KVW


def swa_fwd(name, p, cos, sin, qn, kn, sinks):
    T = p.shape[0]
    nb = T // SW_BLK

    def body(p_ref, cos_ref, sin_ref, qn_ref, kn_ref, sk_ref, u_ref, ko_ref, kprev, vprev):
        first = pl.program_id(0) == 0

        @pl.when(first)
        def _():
            kprev[...] = jnp.zeros(kprev.shape, F32)
            vprev[...] = jnp.zeros(vprev.shape, F32)
        mask, p64, p64t, hots = _swa_consts(first)
        cs, sn = cos_ref[...], sin_ref[...]
        cs4, sn4 = jnp.concatenate([cs] * SW_G, axis=0), jnp.concatenate([sn] * SW_G, axis=0)
        qp4, kp, vv, z4 = _swa_load(p_ref)
        k, v, u4 = swa_heads(qn_ref[...], kn_ref[...], sk_ref[...], kprev[...], vprev[...], qp4, kp, vv, z4,
                             cs4, sn4, cs, sn, hots, mask, p64, p64t)
        kprev[...] = k
        vprev[...] = v
        for h in range(SW_KV):
            ko_ref[:, pl.ds(h * SW_DH, SW_DH)] = k[h]
            uh = u4[h]
            for g in range(SW_G):
                u_ref[:, pl.ds(h * SW_G * SW_DH + g * SW_DH, SW_DH)] = uh[g * SW_BLK:(g + 1) * SW_BLK].astype(BF16)

    return pl.pallas_call(
        body, name=name, grid=(nb,),
        in_specs=[_row_spec(SW_BLK, SW_IN), _row_spec(SW_BLK, SW_DH), _row_spec(SW_BLK, SW_DH),
                  _const_spec((1, SW_DH)), _const_spec((1, SW_DH)), _const_spec((1, SW_KV * SW_G))],
        out_specs=[_row_spec(SW_BLK, SW_QW), _row_spec(SW_BLK, SW_KVW)],
        out_shape=[jax.ShapeDtypeStruct((T, SW_QW), BF16), jax.ShapeDtypeStruct((T, SW_KVW), F32)],
        scratch_shapes=[pltpu.VMEM((SW_KV, SW_BLK, SW_DH), F32), pltpu.VMEM((SW_KV, SW_BLK, SW_DH), F32)],
        compiler_params=_params(dimension_semantics=("arbitrary",)),
    )(p, cos, sin, qn, kn, sinks)


def swa_bwd(name, p, kpost, cos, sin, qn, kn, sinks, du):
    T = p.shape[0]
    nb = T // SW_BLK

    def body(p_ref, pprev_ref, kprev_ref, cos_ref, sin_ref, qn_ref, kn_ref, sk_ref, du_ref,
             dp_ref, dqn_ref, dkn_ref, dsk_ref, dk_scr, dv_scr):
        i = pl.program_id(0)
        first = i == nb - 1

        @pl.when(i == 0)
        def _():
            dk_scr[...] = jnp.zeros(dk_scr.shape, F32)
            dv_scr[...] = jnp.zeros(dv_scr.shape, F32)
            dqn_ref[...] = jnp.zeros(dqn_ref.shape, F32)
            dkn_ref[...] = jnp.zeros(dkn_ref.shape, F32)
            dsk_ref[...] = jnp.zeros(dsk_ref.shape, F32)
        mask, p64, p64t, hots = _swa_consts(first)
        cs, sn = cos_ref[...], sin_ref[...]
        cs4, sn4 = jnp.concatenate([cs] * SW_G, axis=0), jnp.concatenate([sn] * SW_G, axis=0)
        heads = range(SW_KV)
        fn = lambda qn, kn, sk, kpv, vpv, qp4, kp, v, z4: swa_heads(
            qn, kn, sk, kpv, vpv, qp4, kp, v, z4, cs4, sn4, cs, sn, hots, mask, p64, p64t)
        kpv = jnp.stack([kprev_ref[:, pl.ds(h * SW_DH, SW_DH)] for h in heads])
        vpv = jnp.stack([pprev_ref[:, pl.ds(V0 + h * SW_DH, SW_DH)] for h in heads])
        _, f = jax.vjp(fn, qn_ref[...], kn_ref[...], sk_ref[...], kpv, vpv, *_swa_load(p_ref))
        du4 = jnp.stack([_stack_q(du_ref, h, 0).astype(F32) for h in heads])
        dqn, dkn, dsk, dkpv, dvpv, dq4, dkp, dv, dz4 = f((dk_scr[...], dv_scr[...], du4))
        dk_scr[...] = dkpv
        dv_scr[...] = dvpv
        for h in heads:
            dp_ref[:, pl.ds(K0 + h * SW_DH, SW_DH)] = dkp[h].astype(BF16)
            dp_ref[:, pl.ds(V0 + h * SW_DH, SW_DH)] = dv[h].astype(BF16)
            dqh, dzh = dq4[h], dz4[h]
            for g in range(SW_G):
                c0 = h * SW_G * SW_DH + g * SW_DH
                dp_ref[:, pl.ds(Q0 + c0, SW_DH)] = dqh[g * SW_BLK:(g + 1) * SW_BLK].astype(BF16)
                dp_ref[:, pl.ds(Z0 + c0, SW_DH)] = dzh[g * SW_BLK:(g + 1) * SW_BLK].astype(BF16)
        dqn_ref[...] += dqn
        dkn_ref[...] += dkn
        dsk_ref[...] += dsk

    rev = lambda i: (nb - 1 - i, 0)
    prev = lambda i: (jnp.maximum(nb - 2 - i, 0), 0)
    return pl.pallas_call(
        body, name=name, grid=(nb,),
        in_specs=[pl.BlockSpec((SW_BLK, SW_IN), rev), pl.BlockSpec((SW_BLK, SW_IN), prev),
                  pl.BlockSpec((SW_BLK, SW_KVW), prev), pl.BlockSpec((SW_BLK, SW_DH), rev), pl.BlockSpec((SW_BLK, SW_DH), rev),
                  _const_spec((1, SW_DH)), _const_spec((1, SW_DH)), _const_spec((1, SW_KV * SW_G)),
                  pl.BlockSpec((SW_BLK, SW_QW), rev)],
        out_specs=[pl.BlockSpec((SW_BLK, SW_IN), rev), _const_spec((1, SW_DH)), _const_spec((1, SW_DH)),
                   _const_spec((1, SW_KV * SW_G))],
        out_shape=[jax.ShapeDtypeStruct((T, SW_IN), BF16), jax.ShapeDtypeStruct((1, SW_DH), F32),
                   jax.ShapeDtypeStruct((1, SW_DH), F32), jax.ShapeDtypeStruct((1, SW_KV * SW_G), F32)],
        scratch_shapes=[pltpu.VMEM((SW_KV, SW_BLK, SW_DH), F32), pltpu.VMEM((SW_KV, SW_BLK, SW_DH), F32)],
        compiler_params=_params(dimension_semantics=("arbitrary",)),
    )(p, p, kpost, cos, sin, qn, kn, sinks, du)


CONV_TC, CONV_TL, HALO = 256, 1024, 8


def _conv_taps(xe, w_ref):
    acc = w_ref[pl.ds(CONV_K - 1, 1), :] * xe
    for k in range(CONV_K - 1):
        acc = acc + w_ref[pl.ds(k, 1), :] * pltpu.roll(xe, CONV_K - 1 - k, 0)
    return acc


def conv_fwd(name, x, w8):
    T, C = x.shape
    tc = min(CONV_TC, T)
    nt = T // tc

    def body(xp_ref, x_ref, w_ref, o_ref):
        prev = jnp.where(pl.program_id(0) == 0, 0.0, xp_ref[...])
        xe = jnp.concatenate([prev, x_ref[...]], axis=0)
        o_ref[...] = silu(_conv_taps(xe, w_ref)[HALO:])

    return pl.pallas_call(
        body, name=name, grid=(nt, C // CONV_TL),
        in_specs=[pl.BlockSpec((HALO, CONV_TL), lambda i, j: (jnp.maximum(i * (tc // HALO) - 1, 0), j)),
                  pl.BlockSpec((tc, CONV_TL), lambda i, j: (i, j)), pl.BlockSpec((HALO, CONV_TL), lambda i, j: (0, j))],
        out_specs=pl.BlockSpec((tc, CONV_TL), lambda i, j: (i, j)),
        out_shape=jax.ShapeDtypeStruct((T, C), F32),
        compiler_params=_params(dimension_semantics=("arbitrary", "arbitrary")),
    )(x, x, w8)


def conv_bwd(name, x, dy, w8):
    T, C = x.shape
    tc = min(CONV_TC, T)
    nt = T // tc
    n = tc + 2 * HALO

    def body(xp_ref, x_ref, xn_ref, dy_ref, dyn_ref, w_ref, dx_ref, dw_ref):
        i = pl.program_id(1)

        @pl.when(i == 0)
        def _():
            dw_ref[...] = jnp.zeros(dw_ref.shape, F32)
        prev = jnp.where(i == 0, 0.0, xp_ref[...])
        xe = jnp.concatenate([prev, x_ref[...], xn_ref[...]], axis=0)
        dye = jnp.concatenate([jnp.zeros((HALO, CONV_TL), F32), dy_ref[...], jnp.where(i == nt - 1, 0.0, dyn_ref[...])], axis=0)
        ce = _conv_taps(xe, w_ref)
        sg = sigmoid(ce)
        dce = dye * (sg * (1.0 + ce * (1.0 - sg)))
        dx = w_ref[pl.ds(CONV_K - 1, 1), :] * dce
        for k in range(CONV_K - 1):
            dx = dx + w_ref[pl.ds(k, 1), :] * pltpu.roll(dce, n - (CONV_K - 1 - k), 0)
        dx_ref[...] = dx[HALO:HALO + tc].astype(BF16)
        dcur = dce[HALO:HALO + tc]
        for k in range(CONV_K):
            xs = xe if k == CONV_K - 1 else pltpu.roll(xe, CONV_K - 1 - k, 0)
            dw_ref[pl.ds(k, 1), :] += jnp.sum(dcur * xs[HALO:HALO + tc], axis=0, keepdims=True)

    nh = T // HALO
    prev_map = lambda j, i: (jnp.maximum(i * (tc // HALO) - 1, 0), j)
    next_map = lambda j, i: (jnp.minimum((i + 1) * (tc // HALO), nh - 1), j)
    cur_map = lambda j, i: (i, j)
    return pl.pallas_call(
        body, name=name, grid=(C // CONV_TL, nt),
        in_specs=[pl.BlockSpec((HALO, CONV_TL), prev_map), pl.BlockSpec((tc, CONV_TL), cur_map),
                  pl.BlockSpec((HALO, CONV_TL), next_map), pl.BlockSpec((tc, CONV_TL), cur_map),
                  pl.BlockSpec((HALO, CONV_TL), next_map), pl.BlockSpec((HALO, CONV_TL), lambda j, i: (0, j))],
        out_specs=[pl.BlockSpec((tc, CONV_TL), cur_map), pl.BlockSpec((HALO, CONV_TL), lambda j, i: (0, j))],
        out_shape=[jax.ShapeDtypeStruct((T, C), BF16), jax.ShapeDtypeStruct((HALO, C), F32)],
        compiler_params=_params(dimension_semantics=("arbitrary", "arbitrary")),
    )(x, x, x, dy, dy, w8)


GD_VH = 16
GD_HB = 8


def _unit_lower_inverse(a, eye_f):
    n = -a
    t = eye_f + n
    p = n
    for _ in range(5):
        p = bmmf(p, p)
        t = t + bmmf(t, p)
    return t


@jax.custom_vjp
def unit_lower_solve2(a, r1, r2, eye_f):
    t = _unit_lower_inverse(a, eye_f)
    return bmmf(t, r1), bmmf(t, r2)


def _uls2_fwd(a, r1, r2, eye_f):
    t = _unit_lower_inverse(a, eye_f)
    x1, x2 = bmmf(t, r1), bmmf(t, r2)
    return (x1, x2), (t, x1, x2, eye_f)


def _uls2_bwd(res, g):
    t, x1, x2, eye_f = res
    high = lax.Precision.HIGH
    d1 = _bdot(t, g[0], 1, 1, high)
    d2 = _bdot(t, g[1], 1, 1, high)
    da = -(_bdot(d1, x1, 2, 2, high) + _bdot(d2, x2, 2, 2, high))
    return da, d1, d2, jnp.zeros_like(eye_f)


unit_lower_solve2.defvjp(_uls2_fwd, _uls2_bwd)


def gd_heads(al, dt, og, s, qc, kc, v, z, a_col, b_col, cst):
    lt_t, eye, incl, strict = cst
    eye_f = eye.astype(F32)
    nh = s.shape[0]
    q = qc * lax.rsqrt(jnp.sum(qc * qc, axis=-1, keepdims=True) + EPS) * (GD_DK ** -0.5)
    k = kc * lax.rsqrt(jnp.sum(kc * kc, axis=-1, keepdims=True) + EPS)
    beta = sigmoid(b_col)
    g = -jnp.exp(al) * softplus(a_col + dt)
    d_mat = jnp.broadcast_to(jnp.sum(g * lt_t, axis=1, keepdims=True), (nh, GD_CHUNK, GD_CHUNK))
    d = jnp.sum(jnp.where(eye, d_mat, 0.0), axis=-1, keepdims=True)
    dec = jnp.exp(jnp.where(incl, d - d_mat, NEG_INF))
    kb = k * beta
    u, w = unit_lower_solve2(bmm_nt(kb, k) * dec * strict, v * beta, kb * jnp.exp(d), eye_f)
    v_new = u - bmm(w, s)
    o = bmm(q * jnp.exp(d), s) + bmm(bmm_nt(q, k) * dec, v_new)
    dl = jnp.sum(g, axis=1, keepdims=True)
    s_new = s * jnp.exp(dl) + bmm_tn(k * jnp.exp(dl - d), v_new)
    on = o * lax.rsqrt(jnp.mean(o * o, axis=-1, keepdims=True) + EPS) * og
    return s_new, on * silu(z)


def _gd_consts():
    r, c = _iota((GD_CHUNK, GD_CHUNK), 0), _iota((GD_CHUNK, GD_CHUNK), 1)
    return (r <= c).astype(F32), r == c, r >= c, (r > c).astype(F32)


def _gd_load(qkv_ref, z_ref, ab_v, al_v, dt_v, heads):
    lane = _iota((1, 128), 1)
    hot_a = [(lane == h).astype(F32) for h in heads]
    hot_b = [(lane == GD_VH + h).astype(F32) for h in heads]
    col = lambda src, hot: jnp.stack([jnp.sum(src * m, axis=-1, keepdims=True) for m in hot])
    ops = (col(al_v, hot_a), col(dt_v, hot_a),
           jnp.stack([qkv_ref[:, pl.ds((h // 2) * GD_DK, GD_DK)] for h in heads]),
           jnp.stack([qkv_ref[:, pl.ds(GD_QKW + (h // 2) * GD_DK, GD_DK)] for h in heads]),
           jnp.stack([qkv_ref[:, pl.ds(2 * GD_QKW + h * GD_DK, GD_DK)] for h in heads]),
           jnp.stack([z_ref[:, pl.ds(h * GD_DK, GD_DK)] for h in heads]),
           col(ab_v, hot_a), col(ab_v, hot_b))
    return ops, hot_a, hot_b


def gd_fwd(name, qkv, z, ab, al, dt, og):
    T = qkv.shape[0]
    nc = T // GD_CHUNK

    def body(qkv_ref, z_ref, ab_ref, al_ref, dt_ref, og_ref, u_ref, st_ref, s_scr):
        @pl.when(pl.program_id(0) == 0)
        def _():
            s_scr[...] = jnp.zeros(s_scr.shape, F32)
        st_ref[0] = s_scr[...]
        cst = _gd_consts()
        ab_v, al_v, dt_v, og_v = ab_ref[...], al_ref[...], dt_ref[...], og_ref[...]
        for b0 in range(0, GD_VH, GD_HB):
            heads = list(range(b0, b0 + GD_HB))
            (alh, dth, q, k, v, zz, a_col, b_col), _, _ = _gd_load(qkv_ref, z_ref, ab_v, al_v, dt_v, heads)
            s_new, u = gd_heads(alh, dth, og_v, s_scr[pl.ds(b0, GD_HB)], q, k, v, zz, a_col, b_col, cst)
            s_scr[pl.ds(b0, GD_HB)] = s_new
            for i, h in enumerate(heads):
                u_ref[:, pl.ds(h * GD_DK, GD_DK)] = u[i].astype(BF16)

    return pl.pallas_call(
        body, name=name, grid=(nc,),
        in_specs=[_row_spec(GD_CHUNK, GD_QKV), _row_spec(GD_CHUNK, GD_VW), _row_spec(GD_CHUNK, 128),
                  _const_spec((1, 128)), _const_spec((1, 128)), _const_spec((1, 128))],
        out_specs=[_row_spec(GD_CHUNK, GD_VW), pl.BlockSpec((1, GD_VH, GD_DK, GD_DK), lambda i: (i, 0, 0, 0))],
        out_shape=[jax.ShapeDtypeStruct((T, GD_VW), BF16), jax.ShapeDtypeStruct((nc, GD_VH, GD_DK, GD_DK), F32)],
        scratch_shapes=[pltpu.VMEM((GD_VH, GD_DK, GD_DK), F32)],
        compiler_params=_params(dimension_semantics=("arbitrary",)),
    )(qkv, z, ab, al, dt, og)


def gd_bwd(name, qkv, z, ab, al, dt, og, states, du):
    T = qkv.shape[0]
    nc = T // GD_CHUNK

    def body(qkv_ref, z_ref, ab_ref, al_ref, dt_ref, og_ref, st_ref, du_ref,
             dqkv_ref, dz_ref, dab_ref, dal_ref, ddt_ref, dog_ref, ds_scr):
        @pl.when(pl.program_id(0) == 0)
        def _():
            ds_scr[...] = jnp.zeros(ds_scr.shape, F32)
            dal_ref[...] = jnp.zeros(dal_ref.shape, F32)
            ddt_ref[...] = jnp.zeros(ddt_ref.shape, F32)
            dog_ref[...] = jnp.zeros(dog_ref.shape, F32)
        cst = _gd_consts()
        ab_v, al_v, dt_v, og_v = ab_ref[...], al_ref[...], dt_ref[...], og_ref[...]
        dab = jnp.zeros((GD_CHUNK, 128), F32)
        dal_row = jnp.zeros((1, 128), F32)
        ddt_row = jnp.zeros((1, 128), F32)
        dog_row = jnp.zeros((1, 128), F32)
        for b0 in range(0, GD_VH, GD_HB):
            heads = list(range(b0, b0 + GD_HB))
            (alh, dth, q, k, v, zz, a_col, b_col), hot_a, hot_b = _gd_load(qkv_ref, z_ref, ab_v, al_v, dt_v, heads)
            fn = lambda *a: gd_heads(*a, cst)
            _, f = jax.vjp(fn, alh, dth, og_v, st_ref[0, pl.ds(b0, GD_HB)], q, k, v, zz, a_col, b_col)
            du_h = jnp.stack([du_ref[:, pl.ds(h * GD_DK, GD_DK)].astype(F32) for h in heads])
            dalh, ddth, dog, ds, dq, dk, dv, dz, da_col, db_col = f((ds_scr[pl.ds(b0, GD_HB)], du_h))
            ds_scr[pl.ds(b0, GD_HB)] = ds
            dog_row = dog_row + dog
            for i, h in enumerate(heads):
                dqkv_ref[:, pl.ds(2 * GD_QKW + h * GD_DK, GD_DK)] = dv[i]
                dz_ref[:, pl.ds(h * GD_DK, GD_DK)] = dz[i].astype(BF16)
                dab = dab + da_col[i] * hot_a[i] + db_col[i] * hot_b[i]
                dal_row = dal_row + dalh[i] * hot_a[i]
                ddt_row = ddt_row + ddth[i] * hot_a[i]
                if h % 2 == 0:
                    dqkv_ref[:, pl.ds((h // 2) * GD_DK, GD_DK)] = dq[i] + dq[i + 1]
                    dqkv_ref[:, pl.ds(GD_QKW + (h // 2) * GD_DK, GD_DK)] = dk[i] + dk[i + 1]
        dab_ref[...] = dab
        dal_ref[...] += dal_row
        ddt_ref[...] += ddt_row
        dog_ref[...] += dog_row

    rev = lambda i: (nc - 1 - i, 0)
    return pl.pallas_call(
        body, name=name, grid=(nc,),
        in_specs=[pl.BlockSpec((GD_CHUNK, GD_QKV), rev), pl.BlockSpec((GD_CHUNK, GD_VW), rev), pl.BlockSpec((GD_CHUNK, 128), rev),
                  _const_spec((1, 128)), _const_spec((1, 128)), _const_spec((1, 128)),
                  pl.BlockSpec((1, GD_VH, GD_DK, GD_DK), lambda i: (nc - 1 - i, 0, 0, 0)),
                  pl.BlockSpec((GD_CHUNK, GD_VW), rev)],
        out_specs=[pl.BlockSpec((GD_CHUNK, GD_QKV), rev), pl.BlockSpec((GD_CHUNK, GD_VW), rev), pl.BlockSpec((GD_CHUNK, 128), rev),
                   _const_spec((1, 128)), _const_spec((1, 128)), _const_spec((1, 128))],
        out_shape=[jax.ShapeDtypeStruct((T, GD_QKV), F32), jax.ShapeDtypeStruct((T, GD_VW), BF16),
                   jax.ShapeDtypeStruct((T, 128), F32)] + [jax.ShapeDtypeStruct((1, 128), F32)] * 3,
        scratch_shapes=[pltpu.VMEM((GD_VH, GD_DK, GD_DK), F32)],
        compiler_params=_params(dimension_semantics=("arbitrary",)),
    )(qkv, z, ab, al, dt, og, states, du)


def _my_pos():
    return lax.axis_index("x"), lax.axis_index("y"), lax.axis_index("c")


def _peers(pos):
    out = []
    for k in range(1, N_DEV):
        dev = tuple(1 - p if (k >> s) & 1 else p for p, s in zip(pos, (2, 1, 0)))
        out.append((dev, 4 * dev[0] + 2 * dev[1] + dev[2]))
    return out


_ANY = pl.BlockSpec(memory_space=pl.ANY)
_COMM_SCRATCH = [pltpu.SemaphoreType.DMA((N_DEV - 1,)), pltpu.SemaphoreType.DMA((N_DEV - 1,)), pltpu.SemaphoreType.DMA(())]


def all_gather(name, x):
    def body(x_ref, o_ref, send_sems, recv_sems, local_sem):
        pos = _my_pos()
        me = 4 * pos[0] + 2 * pos[1] + pos[2]
        peers = _peers(pos)
        mine = pltpu.make_async_copy(x_ref, o_ref.at[me], local_sem)
        mine.start()
        sends = [pltpu.make_async_remote_copy(x_ref, o_ref.at[me], send_sems.at[k], recv_sems.at[k],
                                              device_id=dev, device_id_type=MESH) for k, (dev, _) in enumerate(peers)]
        for cp in sends:
            cp.start()
        for k, (dev, pid) in enumerate(peers):
            pltpu.make_async_remote_copy(x_ref, o_ref.at[pid], send_sems.at[k], recv_sems.at[k],
                                         device_id=dev, device_id_type=MESH).wait_recv()
        for cp in sends:
            cp.wait_send()
        mine.wait()

    return pl.pallas_call(
        body, name=name, out_shape=jax.ShapeDtypeStruct((N_DEV,) + x.shape, x.dtype),
        in_specs=[_ANY], out_specs=_ANY, scratch_shapes=_COMM_SCRATCH,
    )(x)


def all_to_all(name, parts):
    def body(x_ref, o_ref, send_sems, recv_sems, local_sem):
        pos = _my_pos()
        me = 4 * pos[0] + 2 * pos[1] + pos[2]
        peers = _peers(pos)
        mine = pltpu.make_async_copy(x_ref.at[me], o_ref.at[me], local_sem)
        mine.start()
        sends = [pltpu.make_async_remote_copy(x_ref.at[pid], o_ref.at[me], send_sems.at[k], recv_sems.at[k],
                                              device_id=dev, device_id_type=MESH) for k, (dev, pid) in enumerate(peers)]
        for cp in sends:
            cp.start()
        for k, (dev, pid) in enumerate(peers):
            pltpu.make_async_remote_copy(x_ref.at[pid], o_ref.at[pid], send_sems.at[k], recv_sems.at[k],
                                         device_id=dev, device_id_type=MESH).wait_recv()
        for cp in sends:
            cp.wait_send()
        mine.wait()

    return pl.pallas_call(
        body, name=name, out_shape=jax.ShapeDtypeStruct(parts.shape, parts.dtype),
        in_specs=[_ANY], out_specs=_ANY, scratch_shapes=_COMM_SCRATCH,
    )(parts)


def lb_rows(r0, r1, r2, r3):
    mx = jnp.maximum(jnp.maximum(r0, r1), jnp.maximum(r2, r3))
    e = [jnp.exp(r - mx) for r in (r0, r1, r2, r3)]
    inv = 1.0 / (e[0] + e[1] + e[2] + e[3])
    c0 = e[0] * inv
    c1 = c0 + e[1] * inv
    c2 = c1 + e[2] * inv
    c3 = c2 + e[3] * inv
    return c0 - c0, c1 - c0, c2 - c0, c3 - c0


def mod_partial(name, c_all, ada_w):
    n, _, w = ada_w.shape

    def body(c_ref, w_ref, o_ref):
        o_ref[0] = mm(c_ref[...], w_ref[0])

    return pl.pallas_call(
        body, name=name, grid=(n,),
        in_specs=[_const_spec((N_DEV, D)), pl.BlockSpec((1, D, w), lambda i: (i, 0, 0))],
        out_specs=pl.BlockSpec((1, N_DEV, w), lambda i: (i, 0, 0)),
        out_shape=jax.ShapeDtypeStruct((n, N_DEV, w), F32),
    )(c_all, ada_w)


def prep(name, modp, ada_b, hgrn_lb):
    def body(mp_ref, b_ref, lb_ref, mod_ref, lbo_ref):
        mod_ref[...] = mp_ref[...] + b_ref[...]
        out = lb_rows(*[lb_ref[pl.ds(i, 1), :] for i in range(4)])
        for i in range(4):
            lbo_ref[pl.ds(i, 1), :] = out[i]

    return pl.pallas_call(
        body, name=name,
        out_shape=[jax.ShapeDtypeStruct(modp.shape, F32), jax.ShapeDtypeStruct(hgrn_lb.shape, F32)],
    )(modp, ada_b, hgrn_lb)


def lb_grad(name, hgrn_lb, dlb_parts):
    def body(lb_ref, d_ref, o_ref):
        cts = []
        for i in range(4):
            acc = d_ref[0, pl.ds(i, 1), :]
            for p in range(1, N_DEV):
                acc = acc + d_ref[p, pl.ds(i, 1), :]
            cts.append(acc)
        _, f = jax.vjp(lb_rows, *[lb_ref[pl.ds(i, 1), :] for i in range(4)])
        for i, g in enumerate(f(tuple(cts))):
            o_ref[pl.ds(i, 1), :] = g

    return pl.pallas_call(body, name=name, out_shape=jax.ShapeDtypeStruct(hgrn_lb.shape, F32))(hgrn_lb, dlb_parts)


def ada_grad(name, c_t, dm):
    n, _, w = dm.shape

    def body(c_ref, d_ref, o_ref):
        acc = c_ref[:, pl.ds(0, 1)] * d_ref[0, pl.ds(0, 1), :]
        for b in range(1, N_DEV):
            acc = acc + c_ref[:, pl.ds(b, 1)] * d_ref[0, pl.ds(b, 1), :]
        o_ref[0] = acc

    return pl.pallas_call(
        body, name=name, grid=(n,),
        in_specs=[_const_spec((D, N_DEV)), pl.BlockSpec((1, N_DEV, w), lambda i: (i, 0, 0))],
        out_specs=pl.BlockSpec((1, D, w), lambda i: (i, 0, 0)),
        out_shape=jax.ShapeDtypeStruct((n, D, w), F32),
    )(c_t, dm)


ADAMW_ROWS = 256


def adamw(name, w, m, v, gparts):
    shape = w.shape
    L = shape[-1]
    R = 1
    for s in shape[:-1]:
        R *= s
    P = gparts.shape[0]
    tr = min(R, ADAMW_ROWS)

    def body(w_ref, m_ref, v_ref, g_ref, go_ref, d_ref, mo_ref, vo_ref):
        g = g_ref[0]
        for p in range(1, P):
            g = g + g_ref[p]
        mn = ADAM_B1 * m_ref[...] + (1.0 - ADAM_B1) * g
        vn = ADAM_B2 * v_ref[...] + (1.0 - ADAM_B2) * (g * g)
        m_hat = mn / (1.0 - ADAM_B1 ** ADAM_STEP)
        v_hat = vn / (1.0 - ADAM_B2 ** ADAM_STEP)
        go_ref[...] = g
        d_ref[...] = -ADAM_LR * (m_hat / (jnp.sqrt(v_hat) + ADAM_EPS) + ADAM_WD * w_ref[...])
        mo_ref[...] = mn
        vo_ref[...] = vn

    spec = pl.BlockSpec((tr, L), lambda i: (i, 0))
    outs = pl.pallas_call(
        body, name=name, grid=(R // tr,),
        in_specs=[spec, spec, spec, pl.BlockSpec((P, tr, L), lambda i: (0, i, 0))],
        out_specs=[spec] * 4, out_shape=[jax.ShapeDtypeStruct((R, L), F32)] * 4,
        compiler_params=_params(dimension_semantics=("arbitrary",)),
    )(w.reshape(R, L), m.reshape(R, L), v.reshape(R, L), gparts.reshape(P, R, L))
    return [o.reshape(shape) for o in outs]


def _gather_cols(name, w):
    g = all_gather(name, w.astype(BF16))
    n, k, w8 = w.shape
    return jnp.transpose(g, (1, 2, 0, 3)).reshape(n, k, N_DEV * w8)


def _gather_rows(name, w):
    g = all_gather(name, w.astype(BF16))
    n, k8, nn = w.shape
    return jnp.transpose(g, (1, 0, 2, 3)).reshape(n, N_DEV * k8, nn)


def _scatter_cols(dw):
    n, k, nn = dw.shape
    return jnp.transpose(dw.reshape(n, k, N_DEV, nn // N_DEV), (2, 0, 1, 3))


def _scatter_rows(dw):
    n, k, nn = dw.shape
    return jnp.transpose(dw.reshape(n, N_DEV, k // N_DEV, nn), (1, 0, 2, 3))


def _pad_lanes(v, width=128):
    return jnp.pad(v, ((0, 0), (0, width - v.shape[1])))


def _stack_rows(parts):
    n, L = len(parts), parts[0].shape[1]
    r = lax.broadcasted_iota(jnp.int32, (n, L), 0)
    out = jnp.zeros((n, L), parts[0].dtype)
    for i, p in enumerate(parts):
        out = jnp.where(r == i, jnp.broadcast_to(p, (n, L)), out)
    return out


def kernel(x, c, positions, hgrn_lb, ada_w, ada_b, norm_g, hg_in_w, hg_out_w, hg_onorm, sw_in_w, sw_out_w, sw_qnorm, sw_knorm, sw_sinks, gd_in_w, gd_out_w, gd_conv_w, gd_a_log, gd_dt_bias, gd_onorm, loss_target, m_hgrn_lb, m_ada_w, m_ada_b, m_norm_g, m_hg_in_w, m_hg_out_w, m_hg_onorm, m_sw_in_w, m_sw_out_w, m_sw_qnorm, m_sw_knorm, m_sw_sinks, m_gd_in_w, m_gd_out_w, m_gd_conv_w, m_gd_a_log, m_gd_dt_bias, m_gd_onorm, v_hgrn_lb, v_ada_w, v_ada_b, v_norm_g, v_hg_in_w, v_hg_out_w, v_hg_onorm, v_sw_in_w, v_sw_out_w, v_sw_qnorm, v_sw_knorm, v_sw_sinks, v_gd_in_w, v_gd_out_w, v_gd_conv_w, v_gd_a_log, v_gd_dt_bias, v_gd_onorm):
    pos = _my_pos()
    me = 4 * pos[0] + 2 * pos[1] + pos[2]
    x0 = x[0]
    tgt = loss_target[0]
    n_layers = norm_g.shape[0]
    aw = ada_w.shape[2]

    c_all = all_gather("ag_c", c)[:, 0, :]
    modp = all_gather("ag_mod", mod_partial("mod_partial", c_all, ada_w))
    modp_mine = lax.dynamic_index_in_dim(modp, me, axis=2, keepdims=False)
    modp_mine = jnp.transpose(modp_mine, (1, 0, 2)).reshape(n_layers, N_DEV * aw)
    mod, lb_all = prep("prep", modp_mine, ada_b, hgrn_lb)
    shift, scale, gate = mod[:, :D], mod[:, D:2 * D], mod[:, 2 * D:]
    row = lambda a, i: a[i:i + 1]

    w_hg_in = _gather_cols("ag_hg_in", hg_in_w)
    w_hg_out = _gather_rows("ag_hg_out", hg_out_w)
    w_sw_in = _gather_cols("ag_sw_in", sw_in_w)[0]
    w_sw_out = _gather_rows("ag_sw_out", sw_out_w)[0]
    w_gd_in = _gather_cols("ag_gd_in", gd_in_w)[0]
    w_gd_out = _gather_rows("ag_gd_out", gd_out_w)[0]
    w_gd_qkv, w_gd_z = w_gd_in[:, :GD_QKV], w_gd_in[:, GD_QKV:GD_QKV + GD_VW]
    w_gd_ab = _pad_lanes(w_gd_in[:, GD_QKV + GD_VW:])
    conv_all = all_gather("ag_conv", gd_conv_w)
    conv_w = jnp.transpose(conv_all[:, 0], (1, 0, 2)).reshape(CONV_K, GD_QKV)
    conv_w8 = jnp.pad(conv_w, ((0, HALO - CONV_K), (0, 0)))
    gd_al, gd_dt = _pad_lanes(gd_a_log), _pad_lanes(gd_dt_bias)

    inv_freq = ROPE_THETA ** (-jnp.arange(0, SW_DH, 2, dtype=F32) / SW_DH)
    ang = positions[0].astype(F32)[:, None] * inv_freq
    cos, sin = jnp.cos(ang), jnp.sin(ang)
    cos64, sin64 = jnp.concatenate([cos, cos], axis=-1), jnp.concatenate([-sin, sin], axis=-1)

    xs, hs, ys, saved = [x0], [], [], []
    h = norm_mod("norm0", x0, row(norm_g, 0), row(scale, 0), row(shift, 0))
    for i in range(n_layers):
        kind, j = i % 3, i // 3
        hs.append(h)
        if kind == 0:
            p = matmul(f"hg_in{i}", h, w_hg_in[j])
            u, st = hg_fwd(f"hg_fwd{i}", p, row(lb_all, i), row(hg_onorm, j))
            y = matmul(f"hg_out{i}", u, w_hg_out[j])
            saved.append((p, u, st))
        elif kind == 1:
            p = matmul(f"sw_in{i}", h, w_sw_in)
            u, kpost = swa_fwd(f"sw_fwd{i}", p, cos64, sin64, row(sw_qnorm, j), row(sw_knorm, j), row(sw_sinks, j))
            y = matmul(f"sw_out{i}", u, w_sw_out)
            saved.append((p, u, kpost))
        else:
            xq = matmul(f"gd_qkv{i}", h, w_gd_qkv)
            zz = matmul(f"gd_z{i}", h, w_gd_z)
            ab = matmul(f"gd_ab{i}", h, w_gd_ab)
            cv = conv_fwd(f"gd_conv{i}", xq, conv_w8)
            u, st = gd_fwd(f"gd_fwd{i}", cv, zz, ab, gd_al, gd_dt, row(gd_onorm, j))
            y = matmul(f"gd_out{i}", u, w_gd_out)
            saved.append((xq, zz, ab, cv, u, st))
        ys.append(y)
        if i + 1 < n_layers:
            xn, h = resid_norm(f"resid{i}", xs[i], y, row(gate, i), row(norm_g, i + 1), row(scale, i + 1), row(shift, i + 1))
            xs.append(xn)

    last = n_layers - 1
    dx, dy, loss_acc, dgate_last = loss_head("loss", xs[last], ys[last], tgt, row(gate, last))
    loss = lax.psum(loss_acc[0, 0], ("x", "y", "c"))

    dgate = [None] * n_layers
    dgate[last] = dgate_last
    dg_norm, dscale, dshift = [None] * n_layers, [None] * n_layers, [None] * n_layers
    dlb = [jnp.zeros((1, D), F32)] * n_layers
    d_hg_in, d_hg_out, d_hg_on = [None] * 2, [None] * 2, [None] * 2
    for i in range(last, -1, -1):
        kind, j = i % 3, i // 3
        h = hs[i]
        if kind == 0:
            p, u, st = saved[i]
            du = matmul_nt(f"hg_du{i}", [(dy, w_hg_out[j])])
            d_hg_out[j] = matmul_tn(f"hg_dwo{i}", u, dy)
            dp, dlb_i, dog = hg_bwd(f"hg_bwd{i}", p, row(lb_all, i), row(hg_onorm, j), st, du)
            dlb[i] = dlb_i
            d_hg_on[j] = dog
            dh = matmul_nt(f"hg_dh{i}", [(dp, w_hg_in[j])])
            d_hg_in[j] = matmul_tn(f"hg_dwi{i}", h, dp)
        elif kind == 1:
            p, u, kpost = saved[i]
            du = matmul_nt(f"sw_du{i}", [(dy, w_sw_out)])
            d_sw_out = matmul_tn(f"sw_dwo{i}", u, dy)
            dp, d_qn, d_kn, d_sk = swa_bwd(f"sw_bwd{i}", p, kpost, cos64, sin64, row(sw_qnorm, j), row(sw_knorm, j),
                                           row(sw_sinks, j), du)
            dh = matmul_nt(f"sw_dh{i}", [(dp, w_sw_in)])
            d_sw_in = matmul_tn(f"sw_dwi{i}", h, dp)
        else:
            xq, zz, ab, cv, u, st = saved[i]
            du = matmul_nt(f"gd_du{i}", [(dy, w_gd_out)])
            d_gd_out = matmul_tn(f"gd_dwo{i}", u, dy)
            dcv, dz, dab, d_al, d_dt, d_gd_on = gd_bwd(f"gd_bwd{i}", cv, zz, ab, gd_al, gd_dt, row(gd_onorm, j), st, du)
            dxq, d_conv8 = conv_bwd(f"gd_dconv{i}", xq, dcv, conv_w8)
            dh = matmul_nt(f"gd_dh{i}", [(dxq, w_gd_qkv), (dz, w_gd_z), (dab, w_gd_ab)])
            d_gd_in = jnp.concatenate([matmul_tn(f"gd_dwq{i}", h, dxq), matmul_tn(f"gd_dwz{i}", h, dz),
                                       matmul_tn(f"gd_dwab{i}", h, dab)[:, :2 * GD_VH]], axis=1)
        if i > 0:
            dx, dy, dg_norm[i], dscale[i], dshift[i], dgate[i - 1] = bwd_rowwise(
                f"bwd_row{i}", xs[i], dh, dx, row(norm_g, i), row(scale, i), row(shift, i), ys[i - 1], row(gate, i - 1))
        else:
            dx, dg_norm[i], dscale[i], dshift[i] = bwd_rowwise(
                f"bwd_row{i}", xs[i], dh, dx, row(norm_g, i), row(scale, i), row(shift, i))
    grad_x = dx[None]

    dmod = jnp.concatenate([_stack_rows(dshift), _stack_rows(dscale), _stack_rows(dgate)], axis=1)
    misc = _stack_rows(d_hg_on + [d_gd_on, _pad_lanes(d_qn), _pad_lanes(d_kn), _pad_lanes(d_sk), d_al, d_dt])
    small = jnp.concatenate([_stack_rows(dlb).reshape(-1, 128), _stack_rows(dg_norm).reshape(-1, 128),
                             dmod.reshape(-1, 128), misc], axis=0)
    small_all = all_gather("ag_small", small)
    n_lb = n_layers * D // 128
    n_mod = n_layers * 3 * D // 128
    o = 0
    dlb_parts = small_all[:, o:o + n_lb].reshape(N_DEV, n_layers, D); o += n_lb
    dgn_parts = small_all[:, o:o + n_lb].reshape(N_DEV, n_layers, D); o += n_lb
    dmod_parts = small_all[:, o:o + n_mod].reshape(N_DEV, n_layers, 3 * D); o += n_mod
    dhgon_parts = small_all[:, o:o + 2]; o += 2
    dgdon_parts = small_all[:, o:o + 1]; o += 1
    dqn_parts = small_all[:, o:o + 1, :SW_DH]; o += 1
    dkn_parts = small_all[:, o:o + 1, :SW_DH]; o += 1
    dsk_parts = small_all[:, o:o + 1, :SW_KV * SW_G]; o += 1
    dal_parts = small_all[:, o:o + 1, :GD_VH]; o += 1
    ddt_parts = small_all[:, o:o + 1, :GD_VH]; o += 1

    g_lb = lb_grad("lb_grad", hgrn_lb, dlb_parts)
    dm_mine = lax.dynamic_slice_in_dim(dmod_parts, me * aw, aw, axis=2)
    g_ada_w = ada_grad("ada_grad", jnp.transpose(c_all), jnp.transpose(dm_mine, (1, 0, 2)))

    res = {}
    res["hgrn_lb"] = adamw("aw_hgrn_lb", hgrn_lb, m_hgrn_lb, v_hgrn_lb, g_lb[None])
    res["ada_w"] = adamw("aw_ada_w", ada_w, m_ada_w, v_ada_w, g_ada_w[None])
    res["ada_b"] = adamw("aw_ada_b", ada_b, m_ada_b, v_ada_b, dmod_parts)
    res["norm_g"] = adamw("aw_norm_g", norm_g, m_norm_g, v_norm_g, dgn_parts)
    res["hg_in_w"] = adamw("aw_hg_in", hg_in_w, m_hg_in_w, v_hg_in_w,
                           all_to_all("rs_hg_in", _scatter_cols(jnp.stack(d_hg_in))))
    res["hg_out_w"] = adamw("aw_hg_out", hg_out_w, m_hg_out_w, v_hg_out_w,
                            all_to_all("rs_hg_out", _scatter_rows(jnp.stack(d_hg_out))))
    res["hg_onorm"] = adamw("aw_hg_onorm", hg_onorm, m_hg_onorm, v_hg_onorm, dhgon_parts)
    res["sw_in_w"] = adamw("aw_sw_in", sw_in_w, m_sw_in_w, v_sw_in_w, all_to_all("rs_sw_in", _scatter_cols(d_sw_in[None])))
    res["sw_out_w"] = adamw("aw_sw_out", sw_out_w, m_sw_out_w, v_sw_out_w, all_to_all("rs_sw_out", _scatter_rows(d_sw_out[None])))
    res["sw_qnorm"] = adamw("aw_sw_qn", sw_qnorm, m_sw_qnorm, v_sw_qnorm, dqn_parts)
    res["sw_knorm"] = adamw("aw_sw_kn", sw_knorm, m_sw_knorm, v_sw_knorm, dkn_parts)
    res["sw_sinks"] = adamw("aw_sw_sinks", sw_sinks, m_sw_sinks, v_sw_sinks, dsk_parts)
    res["gd_in_w"] = adamw("aw_gd_in", gd_in_w, m_gd_in_w, v_gd_in_w, all_to_all("rs_gd_in", _scatter_cols(d_gd_in[None])))
    res["gd_out_w"] = adamw("aw_gd_out", gd_out_w, m_gd_out_w, v_gd_out_w, all_to_all("rs_gd_out", _scatter_rows(d_gd_out[None])))
    res["gd_conv_w"] = adamw("aw_gd_conv", gd_conv_w, m_gd_conv_w, v_gd_conv_w,
                             all_to_all("rs_gd_conv", _scatter_cols(d_conv8[None, :CONV_K])))
    res["gd_a_log"] = adamw("aw_gd_alog", gd_a_log, m_gd_a_log, v_gd_a_log, dal_parts)
    res["gd_dt_bias"] = adamw("aw_gd_dt", gd_dt_bias, m_gd_dt_bias, v_gd_dt_bias, ddt_parts)
    res["gd_onorm"] = adamw("aw_gd_onorm", gd_onorm, m_gd_onorm, v_gd_onorm, dgdon_parts)

    order = ["hgrn_lb", "ada_w", "ada_b", "norm_g", "hg_in_w", "hg_out_w", "hg_onorm", "sw_in_w", "sw_out_w", "sw_qnorm",
             "sw_knorm", "sw_sinks", "gd_in_w", "gd_out_w", "gd_conv_w", "gd_a_log", "gd_dt_bias", "gd_onorm"]
    outs = [loss, grad_x]
    for part in range(4):
        outs += [res[n][part] for n in order]
    return tuple(outs)
```

```python
import functools

import jax
import jax.numpy as jnp
from jax import lax
from jax.experimental import pallas as pl
from jax.experimental.pallas import tpu as pltpu

F32, BF16 = jnp.float32, jnp.bfloat16
HI = lax.Precision.HIGHEST
MESH = pl.DeviceIdType.MESH
N_DEV = 8
D = 1024
EPS = 1e-6
NEG_INF = float("-inf")

HG_HEADS, HG_DH, HG_BLK, HG_CH, HG_SUB, HG_HB = 8, 128, 128, 32, 8, 8
SW_KV, SW_G, SW_DH, SW_BLK = 4, 4, 64, 128
SW_QW, SW_KVW = 1024, 256
GD_QK_HEADS, GD_DK, GD_CHUNK = 8, 128, 64
GD_QKW, GD_VW, GD_QKV = 1024, 2048, 4096
CONV_K = 4
ROPE_THETA = 10000.0

ADAM_LR, ADAM_B1, ADAM_B2, ADAM_EPS, ADAM_WD, ADAM_STEP = 0.001, 0.9, 0.999, 1e-08, 0.01, 10

VMEM_LIMIT = 56 * 1024 * 1024


def _params(**kw):
    return pltpu.CompilerParams(vmem_limit_bytes=VMEM_LIMIT, **kw)


def _dot(a, b, ca, cb):
    return lax.dot_general(a.astype(BF16), b.astype(BF16), (((ca,), (cb,)), ((), ())), preferred_element_type=F32)


def mm(a, b):
    return _dot(a, b, 1, 0)


def mm_nt(a, b):
    return _dot(a, b, 1, 1)


def mm_tn(a, b):
    return _dot(a, b, 0, 0)


def mmf(a, b):
    return lax.dot_general(a, b, (((1,), (0,)), ((), ())), precision=HI, preferred_element_type=F32)


def _split3(x):
    h1 = x.astype(BF16)
    r1 = x - h1.astype(F32)
    h2 = r1.astype(BF16)
    h3 = (r1 - h2.astype(F32)).astype(BF16)
    return h1, h2, h3


def _lin01_l(m, x):
    mb = m.astype(BF16)
    return sum(lax.dot_general(mb, p, (((1,), (0,)), ((), ())), preferred_element_type=F32) for p in _split3(x))


def _lin01_r(x, m):
    mb = m.astype(BF16)
    return sum(lax.dot_general(p, mb, (((1,), (0,)), ((), ())), preferred_element_type=F32) for p in _split3(x))


def _bdot(a, b, ca, cb, prec=None):
    return lax.dot_general(a, b, (((ca,), (cb,)), ((0,), (0,))), precision=prec, preferred_element_type=F32)


def bmm(a, b):
    return _bdot(a.astype(BF16), b.astype(BF16), 2, 1)


def bmm_nt(a, b):
    return _bdot(a.astype(BF16), b.astype(BF16), 2, 2)


def bmm_tn(a, b):
    return _bdot(a.astype(BF16), b.astype(BF16), 1, 1)


def bmmf(a, b):
    return _bdot(a, b, 2, 1, lax.Precision.HIGH)


def _blin01(m, x):
    mb = m.astype(BF16)
    return sum(_bdot(mb, p, 2, 1) for p in _split3(x))


@jax.custom_vjp
def blin01(m, mt, x):
    return _blin01(m, x)


def _blin01_fwd(m, mt, x):
    return _blin01(m, x), (m, mt)


def _blin01_bwd(res, g):
    m, mt = res
    return jnp.zeros_like(m), jnp.zeros_like(mt), _blin01(mt, g)


blin01.defvjp(_blin01_fwd, _blin01_bwd)


def _blin01_r(x, m):
    mb = m.astype(BF16)
    return sum(_bdot(p, mb, 2, 1) for p in _split3(x))


@jax.custom_vjp
def blin01_r(x, m, mt):
    return _blin01_r(x, m)


def _blin01_r_fwd(x, m, mt):
    return _blin01_r(x, m), (m, mt)


def _blin01_r_bwd(res, g):
    m, mt = res
    return _blin01_r(g, mt), jnp.zeros_like(m), jnp.zeros_like(mt)


blin01_r.defvjp(_blin01_r_fwd, _blin01_r_bwd)


@jax.custom_vjp
def lin01_l(m, mt, x):
    return _lin01_l(m, x)


def _lin01_l_fwd(m, mt, x):
    return _lin01_l(m, x), (m, mt)


def _lin01_l_bwd(res, g):
    m, mt = res
    return jnp.zeros_like(m), jnp.zeros_like(mt), _lin01_l(mt, g)


lin01_l.defvjp(_lin01_l_fwd, _lin01_l_bwd)


@jax.custom_vjp
def lin01_r(x, m, mt):
    return _lin01_r(x, m)


def _lin01_r_fwd(x, m, mt):
    return _lin01_r(x, m), (m, mt)


def _lin01_r_bwd(res, g):
    m, mt = res
    return _lin01_r(g, mt), jnp.zeros_like(m), jnp.zeros_like(mt)


lin01_r.defvjp(_lin01_r_fwd, _lin01_r_bwd)


@functools.partial(jax.custom_vjp, nondiff_argnums=(1, 2))
def rows(x, start, size):
    return lax.slice_in_dim(x, start, start + size, axis=x.ndim - 2)


def _rows_fwd(x, start, size):
    return lax.slice_in_dim(x, start, start + size, axis=x.ndim - 2), jnp.zeros(x.shape[:-1] + (1,), F32)


def _rows_bwd(start, size, res, g):
    ax = g.ndim - 2
    total = res.shape[ax]
    zeros = lambda n: jnp.zeros(g.shape[:ax] + (n,) + g.shape[ax + 1:], g.dtype)
    parts = []
    if start > 0:
        parts.append(zeros(start))
    parts.append(g)
    if total - start - size > 0:
        parts.append(zeros(total - start - size))
    return (jnp.concatenate(parts, axis=ax) if len(parts) > 1 else g,)


rows.defvjp(_rows_fwd, _rows_bwd)


@functools.partial(jax.custom_vjp, nondiff_argnums=(1,))
def rowsel(x, s):
    return lax.slice_in_dim(x, s, s + 1, axis=x.ndim - 2)


def _rowsel_fwd(x, s):
    return lax.slice_in_dim(x, s, s + 1, axis=x.ndim - 2), jnp.zeros(x.shape[:-1] + (1,), F32)


def _rowsel_bwd(s, res, g):
    r = lax.broadcasted_iota(jnp.int32, res.shape, res.ndim - 2)
    return (jnp.where(r == s, g, 0.0),)


rowsel.defvjp(_rowsel_fwd, _rowsel_bwd)


def sigmoid(x):
    return jax.nn.sigmoid(x)


def silu(x):
    return x * jax.nn.sigmoid(x)


def softplus(x):
    return jnp.maximum(x, 0.0) + jnp.log1p(jnp.exp(-jnp.abs(x)))


def _iota(shape, dim):
    return lax.broadcasted_iota(jnp.int32, shape, dim)


def norm_mod_fn(g, sc, sh, x):
    r = lax.rsqrt(jnp.mean(x * x, axis=-1, keepdims=True) + EPS)
    return (x * r * g) * (1.0 + sc) + sh


def _row_spec(tm, f):
    return pl.BlockSpec((tm, f), lambda i: (i, 0))


def _const_spec(shape):
    nd = len(shape)
    return pl.BlockSpec(shape, lambda i: (0,) * nd)


def _tiled(name, fn, consts, rows_in, row_outs, acc_outs=(), tm=512):
    T = rows_in[0].shape[0]
    tm = min(tm, T)
    n_c, n_r, n_ro = len(consts), len(rows_in), len(row_outs)

    def body(*refs):
        c_refs, r_refs = refs[:n_c], refs[n_c:n_c + n_r]
        ro_refs, ao_refs = refs[n_c + n_r:n_c + n_r + n_ro], refs[n_c + n_r + n_ro:]
        outs = fn(*[r[...] for r in c_refs], *[r[...] for r in r_refs])
        for r, v in zip(ro_refs, outs[:n_ro]):
            r[...] = v.astype(r.dtype)
        if ao_refs:
            @pl.when(pl.program_id(0) == 0)
            def _():
                for r in ao_refs:
                    r[...] = jnp.zeros(r.shape, r.dtype)
            for r, v in zip(ao_refs, outs[n_ro:]):
                r[...] += v

    out_shape = [jax.ShapeDtypeStruct((T, f), dt) for f, dt in row_outs] + [jax.ShapeDtypeStruct(s, F32) for s in acc_outs]
    out_specs = [_row_spec(tm, f) for f, _ in row_outs] + [_const_spec(s) for s in acc_outs]
    in_specs = [_const_spec(c.shape) for c in consts] + [_row_spec(tm, r.shape[1]) for r in rows_in]
    return pl.pallas_call(
        body, name=name, grid=(T // tm,), in_specs=in_specs, out_specs=out_specs, out_shape=out_shape,
        compiler_params=_params(dimension_semantics=("arbitrary",)),
    )(*consts, *rows_in)


def norm_mod(name, x, g, sc, sh):
    (h,) = _tiled(name, lambda g, sc, sh, x: (norm_mod_fn(g, sc, sh, x),), [g, sc, sh], [x], [(D, BF16)])
    return h


def resid_norm(name, x, y, gate, g, sc, sh):
    def fn(gate, g, sc, sh, x, y):
        xn = x + gate * y
        return xn, norm_mod_fn(g, sc, sh, xn)
    return _tiled(name, fn, [gate, g, sc, sh], [x, y], [(D, F32), (D, BF16)])


def loss_head(name, x, y, tgt, gate):
    def fn(gate, x, y, tgt):
        err = x + gate * y - tgt
        dx = err * (1.0 / D)
        per_tok = jnp.sum(err * err, axis=-1, keepdims=True) * (0.5 / D)
        loss = jnp.sum(per_tok, axis=0, keepdims=True)
        return dx, gate * dx, jnp.broadcast_to(loss, (1, 128)), jnp.sum(dx * y, axis=0, keepdims=True)
    return _tiled(name, fn, [gate], [x, y, tgt], [(D, F32), (D, BF16)], [(1, 128), (1, D)])


def bwd_rowwise(name, x, dh, dxn, g, sc, sh, y_prev=None, gate_prev=None):
    with_prev = y_prev is not None

    def fn(*a):
        if with_prev:
            gate_p, g, sc, sh, x, dh, dxn, yp = a
        else:
            g, sc, sh, x, dh, dxn = a
        _, f = jax.vjp(norm_mod_fn, g, sc, sh, x)
        dg, dsc, dsh, dx = f(dh)
        dx = dx + dxn
        if with_prev:
            return dx, gate_p * dx, dg, dsc, dsh, jnp.sum(dx * yp, axis=0, keepdims=True)
        return dx, dg, dsc, dsh

    if with_prev:
        return _tiled(name, fn, [gate_prev, g, sc, sh], [x, dh, dxn, y_prev], [(D, F32), (D, BF16)], [(1, D)] * 4)
    return _tiled(name, fn, [g, sc, sh], [x, dh, dxn], [(D, F32)], [(1, D)] * 3)


def matmul(name, a, w, out_dtype=F32, tm=256):
    T, K = a.shape
    N = w.shape[1]
    tm = min(tm, T)

    def body(a_ref, w_ref, o_ref):
        o_ref[...] = mm(a_ref[...], w_ref[...]).astype(o_ref.dtype)

    return pl.pallas_call(
        body, name=name, grid=(T // tm,),
        in_specs=[_row_spec(tm, K), _const_spec((K, N))], out_specs=_row_spec(tm, N),
        out_shape=jax.ShapeDtypeStruct((T, N), out_dtype),
        compiler_params=_params(dimension_semantics=("arbitrary",)),
    )(a, w)


def matmul_nt(name, pairs, out_dtype=F32, tm=256):
    T = pairs[0][0].shape[0]
    K = pairs[0][1].shape[0]
    tm = min(tm, T)
    n = len(pairs)

    def body(*refs):
        o_ref = refs[-1]
        acc = None
        for i in range(n):
            p = mm_nt(refs[2 * i][...], refs[2 * i + 1][...])
            acc = p if acc is None else acc + p
        o_ref[...] = acc.astype(o_ref.dtype)

    in_specs, args = [], []
    for a, w in pairs:
        in_specs += [_row_spec(tm, a.shape[1]), _const_spec(w.shape)]
        args += [a, w]
    return pl.pallas_call(
        body, name=name, grid=(T // tm,), in_specs=in_specs, out_specs=_row_spec(tm, K),
        out_shape=jax.ShapeDtypeStruct((T, K), out_dtype),
        compiler_params=_params(dimension_semantics=("arbitrary",)),
    )(*args)


def matmul_tn(name, a, b, tm=512, tn=1024):
    T, K = a.shape
    N = b.shape[1]
    tm = min(tm, T)
    tn = max(t for t in range(128, min(tn, N) + 1, 128) if N % t == 0)

    def body(a_ref, b_ref, o_ref):
        @pl.when(pl.program_id(1) == 0)
        def _():
            o_ref[...] = jnp.zeros(o_ref.shape, F32)
        o_ref[...] += mm_tn(a_ref[...], b_ref[...])

    return pl.pallas_call(
        body, name=name, grid=(N // tn, T // tm),
        in_specs=[pl.BlockSpec((tm, K), lambda j, i: (i, 0)), pl.BlockSpec((tm, tn), lambda j, i: (i, j))],
        out_specs=pl.BlockSpec((K, tn), lambda j, i: (0, j)),
        out_shape=jax.ShapeDtypeStruct((K, N), F32),
        compiler_params=_params(dimension_semantics=("arbitrary", "arbitrary")),
    )(a, b)


def hg_heads(tri, trit, lb, og, st, qp, fp, iv, z):
    blk = qp.shape[1]
    q = silu(qp)
    lf = jnp.log(lb + (1.0 - lb) * sigmoid(fp))
    k = (1.0 - lb) * sigmoid(-fp)
    r = _iota((1, HG_SUB, 1), 1)
    rc = _iota((1, HG_CH, 1), 1)
    outs = []
    for c in range(blk // HG_CH):
        qc, kc, vc, lfc = (rows(a, c * HG_CH, HG_CH) for a in (q, k, iv, lf))
        b = blin01(tri, trit, lfc)
        o_state = bmm_nt(qc * jnp.exp(b), st)
        parts = []
        for i in range(HG_CH // HG_SUB):
            qi, ki, vi, bi = (rows(a, i * HG_SUB, HG_SUB) for a in (qc, kc, vc, b))
            o = rows(o_state, i * HG_SUB, HG_SUB)
            for s in range(HG_SUB):
                ks, bs, vs = rowsel(ki, s), rowsel(bi, s), rowsel(vi, s)
                e = jnp.exp(jnp.where(r >= s, bi - bs, NEG_INF))
                o = o + jnp.sum(qi * ks * e, axis=-1, keepdims=True) * vs
            if i > 0:
                rb = rowsel(b, i * HG_SUB - 1)
                kt = kc * jnp.exp(jnp.where(rc < i * HG_SUB, rb - b, NEG_INF))
                o = o + bmm(bmm_nt(qi * jnp.exp(bi - rb), kt), vc)
            parts.append(o)
        bl = rowsel(b, HG_CH - 1)
        st = st * jnp.exp(bl) + bmm_tn(vc, kc * jnp.exp(bl - b))
        outs.append(jnp.concatenate(parts, axis=1))
    o = jnp.concatenate(outs, axis=1)
    on = o * lax.rsqrt(jnp.mean(o * o, axis=-1, keepdims=True) + EPS) * og
    return st, on * silu(z)


def _tri(n, nh):
    t = (_iota((nh, n, n), 1) >= _iota((nh, n, n), 2)).astype(F32)
    tt = (_iota((nh, n, n), 1) <= _iota((nh, n, n), 2)).astype(F32)
    return t, tt


def _hg_load(p_ref, lb_ref, heads):
    col = lambda ref, base: jnp.stack([ref[:, pl.ds(base + h * HG_DH, HG_DH)] for h in heads])
    return col(lb_ref, 0), col(p_ref, 0), col(p_ref, D), col(p_ref, 2 * D), col(p_ref, 3 * D)


def hg_fwd(name, p, lb, og):
    T = p.shape[0]
    nb = T // HG_BLK

    def body(p_ref, lb_ref, og_ref, u_ref, st_ref, s_scr):
        @pl.when(pl.program_id(0) == 0)
        def _():
            s_scr[...] = jnp.zeros(s_scr.shape, F32)
        st_ref[0] = s_scr[...]
        tri, trit = _tri(HG_CH, HG_HB)
        og_v = og_ref[...]
        for b0 in range(0, HG_HEADS, HG_HB):
            heads = list(range(b0, b0 + HG_HB))
            lbh, qp, fp, iv, z = _hg_load(p_ref, lb_ref, heads)
            st, u = hg_heads(tri, trit, lbh, og_v, s_scr[pl.ds(b0, HG_HB)], qp, fp, iv, z)
            s_scr[pl.ds(b0, HG_HB)] = st
            for i, h in enumerate(heads):
                u_ref[:, pl.ds(h * HG_DH, HG_DH)] = u[i].astype(BF16)

    return pl.pallas_call(
        body, name=name, grid=(nb,),
        in_specs=[_row_spec(HG_BLK, 4 * D), _const_spec((1, D)), _const_spec((1, HG_DH))],
        out_specs=[_row_spec(HG_BLK, D), pl.BlockSpec((1, HG_HEADS, HG_DH, HG_DH), lambda i: (i, 0, 0, 0))],
        out_shape=[jax.ShapeDtypeStruct((T, D), BF16), jax.ShapeDtypeStruct((nb, HG_HEADS, HG_DH, HG_DH), F32)],
        scratch_shapes=[pltpu.VMEM((HG_HEADS, HG_DH, HG_DH), F32)],
        compiler_params=_params(dimension_semantics=("arbitrary",)),
    )(p, lb, og)


def hg_bwd(name, p, lb, og, states, du):
    T = p.shape[0]
    nb = T // HG_BLK

    def body(p_ref, lb_ref, og_ref, st_ref, du_ref, dp_ref, dlb_ref, dog_ref, ds_scr):
        @pl.when(pl.program_id(0) == 0)
        def _():
            ds_scr[...] = jnp.zeros(ds_scr.shape, F32)
            dlb_ref[...] = jnp.zeros(dlb_ref.shape, F32)
            dog_ref[...] = jnp.zeros(dog_ref.shape, F32)
        tri, trit = _tri(HG_CH, HG_HB)
        og_v = og_ref[...]
        fn = functools.partial(hg_heads, tri, trit)
        for b0 in range(0, HG_HEADS, HG_HB):
            heads = list(range(b0, b0 + HG_HB))
            lbh, qp, fp, iv, z = _hg_load(p_ref, lb_ref, heads)
            _, f = jax.vjp(fn, lbh, og_v, st_ref[0, pl.ds(b0, HG_HB)], qp, fp, iv, z)
            du_h = jnp.stack([du_ref[:, pl.ds(h * HG_DH, HG_DH)].astype(F32) for h in heads])
            dlb, dog, dst, dq, df, di, dz = f((ds_scr[pl.ds(b0, HG_HB)], du_h))
            ds_scr[pl.ds(b0, HG_HB)] = dst
            dog_ref[...] += dog
            for i, h in enumerate(heads):
                for base, g in ((0, dq), (D, df), (2 * D, di), (3 * D, dz)):
                    dp_ref[:, pl.ds(base + h * HG_DH, HG_DH)] = g[i].astype(BF16)
                dlb_ref[:, pl.ds(h * HG_DH, HG_DH)] += dlb[i]

    rev = lambda i: (nb - 1 - i, 0)
    return pl.pallas_call(
        body, name=name, grid=(nb,),
        in_specs=[pl.BlockSpec((HG_BLK, 4 * D), rev), _const_spec((1, D)), _const_spec((1, HG_DH)),
                  pl.BlockSpec((1, HG_HEADS, HG_DH, HG_DH), lambda i: (nb - 1 - i, 0, 0, 0)),
                  pl.BlockSpec((HG_BLK, D), rev)],
        out_specs=[pl.BlockSpec((HG_BLK, 4 * D), rev), _const_spec((1, D)), _const_spec((1, HG_DH))],
        out_shape=[jax.ShapeDtypeStruct((T, 4 * D), BF16), jax.ShapeDtypeStruct((1, D), F32),
                   jax.ShapeDtypeStruct((1, HG_DH), F32)],
        scratch_shapes=[pltpu.VMEM((HG_HEADS, HG_DH, HG_DH), F32)],
        compiler_params=_params(dimension_semantics=("arbitrary",)),
    )(p, lb, og, states, du)


def swa_heads(qn, kn, sinks, kprev, vprev, qp4, kp, v, z4, cos4, sin4, cosk, sink_, hots, mask, p64, p64t):
    def norm_rope(xp, g, cs, sn):
        y = xp * lax.rsqrt(jnp.mean(xp * xp, axis=-1, keepdims=True) + EPS) * g
        return y * cs + blin01_r(y, p64, p64t) * sn

    q = norm_rope(qp4, qn, cos4, sin4)
    k = norm_rope(kp, kn, cosk, sink_)
    k2 = jnp.concatenate([kprev, k], axis=1)
    v2 = jnp.concatenate([vprev, v], axis=1)
    s = jnp.where(mask, bmm_nt(q, k2) * (SW_DH ** -0.5), NEG_INF)
    sink_col = jnp.concatenate([jnp.concatenate(
        [jnp.broadcast_to(jnp.sum(sinks * hot, axis=-1, keepdims=True), (SW_BLK, 1)) for hot in hot4], axis=0)[None]
        for hot4 in hots], axis=0)
    m = lax.stop_gradient(jnp.maximum(jnp.max(s, axis=-1, keepdims=True), sink_col))
    p = jnp.exp(s - m)
    p = p / (jnp.sum(p, axis=-1, keepdims=True) + jnp.exp(sink_col - m))
    o = bmm(p, v2)
    return k, v, o * silu(z4)


def _swa_consts(first):
    qi = _iota((SW_G * SW_BLK, 2 * SW_BLK), 0) % SW_BLK
    kj = _iota((SW_G * SW_BLK, 2 * SW_BLK), 1)
    rel = qi + SW_BLK - kj
    mask = (rel >= 0) & (rel < SW_BLK) & ((kj >= SW_BLK) | jnp.logical_not(first))
    i, j = _iota((SW_KV, SW_DH, SW_DH), 1), _iota((SW_KV, SW_DH, SW_DH), 2)
    p64 = (i == (j + SW_DH // 2) % SW_DH).astype(F32)
    p64t = (j == (i + SW_DH // 2) % SW_DH).astype(F32)
    lane = _iota((1, SW_KV * SW_G), 1)
    hots = [[(lane == h * SW_G + g).astype(F32) for g in range(SW_G)] for h in range(SW_KV)]
    return mask, p64, p64t, hots


def _stack_q(ref, h, base):
    return jnp.concatenate([ref[:, pl.ds(base + h * SW_G * SW_DH + g * SW_DH, SW_DH)] for g in range(SW_G)], axis=0)


def _swa_load(p_ref):
    heads = range(SW_KV)
    return (jnp.stack([_stack_q(p_ref, h, Q0) for h in heads]),
            jnp.stack([p_ref[:, pl.ds(K0 + h * SW_DH, SW_DH)] for h in heads]),
            jnp.stack([p_ref[:, pl.ds(V0 + h * SW_DH, SW_DH)] for h in heads]),
            jnp.stack([_stack_q(p_ref, h, Z0) for h in heads]))


Q0, K0, V0, Z0 = 0, SW_QW, SW_QW + SW_KVW, SW_QW + 2 * SW_KVW
SW_IN = 2 * SW_QW + 2 * SW_KVW


def swa_fwd(name, p, cos, sin, qn, kn, sinks):
    T = p.shape[0]
    nb = T // SW_BLK

    def body(p_ref, cos_ref, sin_ref, qn_ref, kn_ref, sk_ref, u_ref, ko_ref, kprev, vprev):
        first = pl.program_id(0) == 0

        @pl.when(first)
        def _():
            kprev[...] = jnp.zeros(kprev.shape, F32)
            vprev[...] = jnp.zeros(vprev.shape, F32)
        mask, p64, p64t, hots = _swa_consts(first)
        cs, sn = cos_ref[...], sin_ref[...]
        cs4, sn4 = jnp.concatenate([cs] * SW_G, axis=0), jnp.concatenate([sn] * SW_G, axis=0)
        qp4, kp, vv, z4 = _swa_load(p_ref)
        k, v, u4 = swa_heads(qn_ref[...], kn_ref[...], sk_ref[...], kprev[...], vprev[...], qp4, kp, vv, z4,
                             cs4, sn4, cs, sn, hots, mask, p64, p64t)
        kprev[...] = k
        vprev[...] = v
        for h in range(SW_KV):
            ko_ref[:, pl.ds(h * SW_DH, SW_DH)] = k[h]
            uh = u4[h]
            for g in range(SW_G):
                u_ref[:, pl.ds(h * SW_G * SW_DH + g * SW_DH, SW_DH)] = uh[g * SW_BLK:(g + 1) * SW_BLK].astype(BF16)

    return pl.pallas_call(
        body, name=name, grid=(nb,),
        in_specs=[_row_spec(SW_BLK, SW_IN), _row_spec(SW_BLK, SW_DH), _row_spec(SW_BLK, SW_DH),
                  _const_spec((1, SW_DH)), _const_spec((1, SW_DH)), _const_spec((1, SW_KV * SW_G))],
        out_specs=[_row_spec(SW_BLK, SW_QW), _row_spec(SW_BLK, SW_KVW)],
        out_shape=[jax.ShapeDtypeStruct((T, SW_QW), BF16), jax.ShapeDtypeStruct((T, SW_KVW), F32)],
        scratch_shapes=[pltpu.VMEM((SW_KV, SW_BLK, SW_DH), F32), pltpu.VMEM((SW_KV, SW_BLK, SW_DH), F32)],
        compiler_params=_params(dimension_semantics=("arbitrary",)),
    )(p, cos, sin, qn, kn, sinks)


def swa_bwd(name, p, kpost, cos, sin, qn, kn, sinks, du):
    T = p.shape[0]
    nb = T // SW_BLK

    def body(p_ref, pprev_ref, kprev_ref, cos_ref, sin_ref, qn_ref, kn_ref, sk_ref, du_ref,
             dp_ref, dqn_ref, dkn_ref, dsk_ref, dk_scr, dv_scr):
        i = pl.program_id(0)
        first = i == nb - 1

        @pl.when(i == 0)
        def _():
            dk_scr[...] = jnp.zeros(dk_scr.shape, F32)
            dv_scr[...] = jnp.zeros(dv_scr.shape, F32)
            dqn_ref[...] = jnp.zeros(dqn_ref.shape, F32)
            dkn_ref[...] = jnp.zeros(dkn_ref.shape, F32)
            dsk_ref[...] = jnp.zeros(dsk_ref.shape, F32)
        mask, p64, p64t, hots = _swa_consts(first)
        cs, sn = cos_ref[...], sin_ref[...]
        cs4, sn4 = jnp.concatenate([cs] * SW_G, axis=0), jnp.concatenate([sn] * SW_G, axis=0)
        heads = range(SW_KV)
        fn = lambda qn, kn, sk, kpv, vpv, qp4, kp, v, z4: swa_heads(
            qn, kn, sk, kpv, vpv, qp4, kp, v, z4, cs4, sn4, cs, sn, hots, mask, p64, p64t)
        kpv = jnp.stack([kprev_ref[:, pl.ds(h * SW_DH, SW_DH)] for h in heads])
        vpv = jnp.stack([pprev_ref[:, pl.ds(V0 + h * SW_DH, SW_DH)] for h in heads])
        _, f = jax.vjp(fn, qn_ref[...], kn_ref[...], sk_ref[...], kpv, vpv, *_swa_load(p_ref))
        du4 = jnp.stack([_stack_q(du_ref, h, 0).astype(F32) for h in heads])
        dqn, dkn, dsk, dkpv, dvpv, dq4, dkp, dv, dz4 = f((dk_scr[...], dv_scr[...], du4))
        dk_scr[...] = dkpv
        dv_scr[...] = dvpv
        for h in heads:
            dp_ref[:, pl.ds(K0 + h * SW_DH, SW_DH)] = dkp[h].astype(BF16)
            dp_ref[:, pl.ds(V0 + h * SW_DH, SW_DH)] = dv[h].astype(BF16)
            dqh, dzh = dq4[h], dz4[h]
            for g in range(SW_G):
                c0 = h * SW_G * SW_DH + g * SW_DH
                dp_ref[:, pl.ds(Q0 + c0, SW_DH)] = dqh[g * SW_BLK:(g + 1) * SW_BLK].astype(BF16)
                dp_ref[:, pl.ds(Z0 + c0, SW_DH)] = dzh[g * SW_BLK:(g + 1) * SW_BLK].astype(BF16)
        dqn_ref[...] += dqn
        dkn_ref[...] += dkn
        dsk_ref[...] += dsk

    rev = lambda i: (nb - 1 - i, 0)
    prev = lambda i: (jnp.maximum(nb - 2 - i, 0), 0)
    return pl.pallas_call(
        body, name=name, grid=(nb,),
        in_specs=[pl.BlockSpec((SW_BLK, SW_IN), rev), pl.BlockSpec((SW_BLK, SW_IN), prev),
                  pl.BlockSpec((SW_BLK, SW_KVW), prev), pl.BlockSpec((SW_BLK, SW_DH), rev), pl.BlockSpec((SW_BLK, SW_DH), rev),
                  _const_spec((1, SW_DH)), _const_spec((1, SW_DH)), _const_spec((1, SW_KV * SW_G)),
                  pl.BlockSpec((SW_BLK, SW_QW), rev)],
        out_specs=[pl.BlockSpec((SW_BLK, SW_IN), rev), _const_spec((1, SW_DH)), _const_spec((1, SW_DH)),
                   _const_spec((1, SW_KV * SW_G))],
        out_shape=[jax.ShapeDtypeStruct((T, SW_IN), BF16), jax.ShapeDtypeStruct((1, SW_DH), F32),
                   jax.ShapeDtypeStruct((1, SW_DH), F32), jax.ShapeDtypeStruct((1, SW_KV * SW_G), F32)],
        scratch_shapes=[pltpu.VMEM((SW_KV, SW_BLK, SW_DH), F32), pltpu.VMEM((SW_KV, SW_BLK, SW_DH), F32)],
        compiler_params=_params(dimension_semantics=("arbitrary",)),
    )(p, p, kpost, cos, sin, qn, kn, sinks, du)


CONV_TC, CONV_TL, HALO = 256, 1024, 8


def _conv_taps(xe, w_ref):
    acc = w_ref[pl.ds(CONV_K - 1, 1), :] * xe
    for k in range(CONV_K - 1):
        acc = acc + w_ref[pl.ds(k, 1), :] * pltpu.roll(xe, CONV_K - 1 - k, 0)
    return acc


def conv_fwd(name, x, w8):
    T, C = x.shape
    tc = min(CONV_TC, T)
    nt = T // tc

    def body(xp_ref, x_ref, w_ref, o_ref):
        prev = jnp.where(pl.program_id(0) == 0, 0.0, xp_ref[...])
        xe = jnp.concatenate([prev, x_ref[...]], axis=0)
        o_ref[...] = silu(_conv_taps(xe, w_ref)[HALO:])

    return pl.pallas_call(
        body, name=name, grid=(nt, C // CONV_TL),
        in_specs=[pl.BlockSpec((HALO, CONV_TL), lambda i, j: (jnp.maximum(i * (tc // HALO) - 1, 0), j)),
                  pl.BlockSpec((tc, CONV_TL), lambda i, j: (i, j)), pl.BlockSpec((HALO, CONV_TL), lambda i, j: (0, j))],
        out_specs=pl.BlockSpec((tc, CONV_TL), lambda i, j: (i, j)),
        out_shape=jax.ShapeDtypeStruct((T, C), F32),
        compiler_params=_params(dimension_semantics=("arbitrary", "arbitrary")),
    )(x, x, w8)


def conv_bwd(name, x, dy, w8):
    T, C = x.shape
    tc = min(CONV_TC, T)
    nt = T // tc
    n = tc + 2 * HALO

    def body(xp_ref, x_ref, xn_ref, dy_ref, dyn_ref, w_ref, dx_ref, dw_ref):
        i = pl.program_id(1)

        @pl.when(i == 0)
        def _():
            dw_ref[...] = jnp.zeros(dw_ref.shape, F32)
        prev = jnp.where(i == 0, 0.0, xp_ref[...])
        xe = jnp.concatenate([prev, x_ref[...], xn_ref[...]], axis=0)
        dye = jnp.concatenate([jnp.zeros((HALO, CONV_TL), F32), dy_ref[...], jnp.where(i == nt - 1, 0.0, dyn_ref[...])], axis=0)
        ce = _conv_taps(xe, w_ref)
        sg = sigmoid(ce)
        dce = dye * (sg * (1.0 + ce * (1.0 - sg)))
        dx = w_ref[pl.ds(CONV_K - 1, 1), :] * dce
        for k in range(CONV_K - 1):
            dx = dx + w_ref[pl.ds(k, 1), :] * pltpu.roll(dce, n - (CONV_K - 1 - k), 0)
        dx_ref[...] = dx[HALO:HALO + tc].astype(BF16)
        dcur = dce[HALO:HALO + tc]
        for k in range(CONV_K):
            xs = xe if k == CONV_K - 1 else pltpu.roll(xe, CONV_K - 1 - k, 0)
            dw_ref[pl.ds(k, 1), :] += jnp.sum(dcur * xs[HALO:HALO + tc], axis=0, keepdims=True)

    nh = T // HALO
    prev_map = lambda j, i: (jnp.maximum(i * (tc // HALO) - 1, 0), j)
    next_map = lambda j, i: (jnp.minimum((i + 1) * (tc // HALO), nh - 1), j)
    cur_map = lambda j, i: (i, j)
    return pl.pallas_call(
        body, name=name, grid=(C // CONV_TL, nt),
        in_specs=[pl.BlockSpec((HALO, CONV_TL), prev_map), pl.BlockSpec((tc, CONV_TL), cur_map),
                  pl.BlockSpec((HALO, CONV_TL), next_map), pl.BlockSpec((tc, CONV_TL), cur_map),
                  pl.BlockSpec((HALO, CONV_TL), next_map), pl.BlockSpec((HALO, CONV_TL), lambda j, i: (0, j))],
        out_specs=[pl.BlockSpec((tc, CONV_TL), cur_map), pl.BlockSpec((HALO, CONV_TL), lambda j, i: (0, j))],
        out_shape=[jax.ShapeDtypeStruct((T, C), BF16), jax.ShapeDtypeStruct((HALO, C), F32)],
        compiler_params=_params(dimension_semantics=("arbitrary", "arbitrary")),
    )(x, x, x, dy, dy, w8)


GD_VH = 16
GD_HB = 8


def _unit_lower_inverse(a, eye_f):
    n = -a
    t = eye_f + n
    p = n
    for _ in range(5):
        p = bmmf(p, p)
        t = t + bmmf(t, p)
    return t


@jax.custom_vjp
def unit_lower_solve2(a, r1, r2, eye_f):
    t = _unit_lower_inverse(a, eye_f)
    return bmmf(t, r1), bmmf(t, r2)


def _uls2_fwd(a, r1, r2, eye_f):
    t = _unit_lower_inverse(a, eye_f)
    x1, x2 = bmmf(t, r1), bmmf(t, r2)
    return (x1, x2), (t, x1, x2, eye_f)


def _uls2_bwd(res, g):
    t, x1, x2, eye_f = res
    high = lax.Precision.HIGH
    d1 = _bdot(t, g[0], 1, 1, high)
    d2 = _bdot(t, g[1], 1, 1, high)
    da = -(_bdot(d1, x1, 2, 2, high) + _bdot(d2, x2, 2, 2, high))
    return da, d1, d2, jnp.zeros_like(eye_f)


unit_lower_solve2.defvjp(_uls2_fwd, _uls2_bwd)


def gd_heads(al, dt, og, s, qc, kc, v, z, a_col, b_col, cst):
    lt_t, eye, incl, strict = cst
    eye_f = eye.astype(F32)
    nh = s.shape[0]
    q = qc * lax.rsqrt(jnp.sum(qc * qc, axis=-1, keepdims=True) + EPS) * (GD_DK ** -0.5)
    k = kc * lax.rsqrt(jnp.sum(kc * kc, axis=-1, keepdims=True) + EPS)
    beta = sigmoid(b_col)
    g = -jnp.exp(al) * softplus(a_col + dt)
    d_mat = jnp.broadcast_to(jnp.sum(g * lt_t, axis=1, keepdims=True), (nh, GD_CHUNK, GD_CHUNK))
    d = jnp.sum(jnp.where(eye, d_mat, 0.0), axis=-1, keepdims=True)
    dec = jnp.exp(jnp.where(incl, d - d_mat, NEG_INF))
    kb = k * beta
    u, w = unit_lower_solve2(bmm_nt(kb, k) * dec * strict, v * beta, kb * jnp.exp(d), eye_f)
    v_new = u - bmm(w, s)
    o = bmm(q * jnp.exp(d), s) + bmm(bmm_nt(q, k) * dec, v_new)
    dl = jnp.sum(g, axis=1, keepdims=True)
    s_new = s * jnp.exp(dl) + bmm_tn(k * jnp.exp(dl - d), v_new)
    on = o * lax.rsqrt(jnp.mean(o * o, axis=-1, keepdims=True) + EPS) * og
    return s_new, on * silu(z)


def _gd_consts():
    r, c = _iota((GD_CHUNK, GD_CHUNK), 0), _iota((GD_CHUNK, GD_CHUNK), 1)
    return (r <= c).astype(F32), r == c, r >= c, (r > c).astype(F32)


def _gd_load(qkv_ref, z_ref, ab_v, al_v, dt_v, heads):
    lane = _iota((1, 128), 1)
    hot_a = [(lane == h).astype(F32) for h in heads]
    hot_b = [(lane == GD_VH + h).astype(F32) for h in heads]
    col = lambda src, hot: jnp.stack([jnp.sum(src * m, axis=-1, keepdims=True) for m in hot])
    ops = (col(al_v, hot_a), col(dt_v, hot_a),
           jnp.stack([qkv_ref[:, pl.ds((h // 2) * GD_DK, GD_DK)] for h in heads]),
           jnp.stack([qkv_ref[:, pl.ds(GD_QKW + (h // 2) * GD_DK, GD_DK)] for h in heads]),
           jnp.stack([qkv_ref[:, pl.ds(2 * GD_QKW + h * GD_DK, GD_DK)] for h in heads]),
           jnp.stack([z_ref[:, pl.ds(h * GD_DK, GD_DK)] for h in heads]),
           col(ab_v, hot_a), col(ab_v, hot_b))
    return ops, hot_a, hot_b


def gd_fwd(name, qkv, z, ab, al, dt, og):
    T = qkv.shape[0]
    nc = T // GD_CHUNK

    def body(qkv_ref, z_ref, ab_ref, al_ref, dt_ref, og_ref, u_ref, st_ref, s_scr):
        @pl.when(pl.program_id(0) == 0)
        def _():
            s_scr[...] = jnp.zeros(s_scr.shape, F32)
        st_ref[0] = s_scr[...]
        cst = _gd_consts()
        ab_v, al_v, dt_v, og_v = ab_ref[...], al_ref[...], dt_ref[...], og_ref[...]
        for b0 in range(0, GD_VH, GD_HB):
            heads = list(range(b0, b0 + GD_HB))
            (alh, dth, q, k, v, zz, a_col, b_col), _, _ = _gd_load(qkv_ref, z_ref, ab_v, al_v, dt_v, heads)
            s_new, u = gd_heads(alh, dth, og_v, s_scr[pl.ds(b0, GD_HB)], q, k, v, zz, a_col, b_col, cst)
            s_scr[pl.ds(b0, GD_HB)] = s_new
            for i, h in enumerate(heads):
                u_ref[:, pl.ds(h * GD_DK, GD_DK)] = u[i].astype(BF16)

    return pl.pallas_call(
        body, name=name, grid=(nc,),
        in_specs=[_row_spec(GD_CHUNK, GD_QKV), _row_spec(GD_CHUNK, GD_VW), _row_spec(GD_CHUNK, 128),
                  _const_spec((1, 128)), _const_spec((1, 128)), _const_spec((1, 128))],
        out_specs=[_row_spec(GD_CHUNK, GD_VW), pl.BlockSpec((1, GD_VH, GD_DK, GD_DK), lambda i: (i, 0, 0, 0))],
        out_shape=[jax.ShapeDtypeStruct((T, GD_VW), BF16), jax.ShapeDtypeStruct((nc, GD_VH, GD_DK, GD_DK), F32)],
        scratch_shapes=[pltpu.VMEM((GD_VH, GD_DK, GD_DK), F32)],
        compiler_params=_params(dimension_semantics=("arbitrary",)),
    )(qkv, z, ab, al, dt, og)


def gd_bwd(name, qkv, z, ab, al, dt, og, states, du):
    T = qkv.shape[0]
    nc = T // GD_CHUNK

    def body(qkv_ref, z_ref, ab_ref, al_ref, dt_ref, og_ref, st_ref, du_ref,
             dqkv_ref, dz_ref, dab_ref, dal_ref, ddt_ref, dog_ref, ds_scr):
        @pl.when(pl.program_id(0) == 0)
        def _():
            ds_scr[...] = jnp.zeros(ds_scr.shape, F32)
            dal_ref[...] = jnp.zeros(dal_ref.shape, F32)
            ddt_ref[...] = jnp.zeros(ddt_ref.shape, F32)
            dog_ref[...] = jnp.zeros(dog_ref.shape, F32)
        cst = _gd_consts()
        ab_v, al_v, dt_v, og_v = ab_ref[...], al_ref[...], dt_ref[...], og_ref[...]
        dab = jnp.zeros((GD_CHUNK, 128), F32)
        dal_row = jnp.zeros((1, 128), F32)
        ddt_row = jnp.zeros((1, 128), F32)
        dog_row = jnp.zeros((1, 128), F32)
        for b0 in range(0, GD_VH, GD_HB):
            heads = list(range(b0, b0 + GD_HB))
            (alh, dth, q, k, v, zz, a_col, b_col), hot_a, hot_b = _gd_load(qkv_ref, z_ref, ab_v, al_v, dt_v, heads)
            fn = lambda *a: gd_heads(*a, cst)
            _, f = jax.vjp(fn, alh, dth, og_v, st_ref[0, pl.ds(b0, GD_HB)], q, k, v, zz, a_col, b_col)
            du_h = jnp.stack([du_ref[:, pl.ds(h * GD_DK, GD_DK)].astype(F32) for h in heads])
            dalh, ddth, dog, ds, dq, dk, dv, dz, da_col, db_col = f((ds_scr[pl.ds(b0, GD_HB)], du_h))
            ds_scr[pl.ds(b0, GD_HB)] = ds
            dog_row = dog_row + dog
            for i, h in enumerate(heads):
                dqkv_ref[:, pl.ds(2 * GD_QKW + h * GD_DK, GD_DK)] = dv[i]
                dz_ref[:, pl.ds(h * GD_DK, GD_DK)] = dz[i].astype(BF16)
                dab = dab + da_col[i] * hot_a[i] + db_col[i] * hot_b[i]
                dal_row = dal_row + dalh[i] * hot_a[i]
                ddt_row = ddt_row + ddth[i] * hot_a[i]
                if h % 2 == 0:
                    dqkv_ref[:, pl.ds((h // 2) * GD_DK, GD_DK)] = dq[i] + dq[i + 1]
                    dqkv_ref[:, pl.ds(GD_QKW + (h // 2) * GD_DK, GD_DK)] = dk[i] + dk[i + 1]
        dab_ref[...] = dab
        dal_ref[...] += dal_row
        ddt_ref[...] += ddt_row
        dog_ref[...] += dog_row

    rev = lambda i: (nc - 1 - i, 0)
    return pl.pallas_call(
        body, name=name, grid=(nc,),
        in_specs=[pl.BlockSpec((GD_CHUNK, GD_QKV), rev), pl.BlockSpec((GD_CHUNK, GD_VW), rev), pl.BlockSpec((GD_CHUNK, 128), rev),
                  _const_spec((1, 128)), _const_spec((1, 128)), _const_spec((1, 128)),
                  pl.BlockSpec((1, GD_VH, GD_DK, GD_DK), lambda i: (nc - 1 - i, 0, 0, 0)),
                  pl.BlockSpec((GD_CHUNK, GD_VW), rev)],
        out_specs=[pl.BlockSpec((GD_CHUNK, GD_QKV), rev), pl.BlockSpec((GD_CHUNK, GD_VW), rev), pl.BlockSpec((GD_CHUNK, 128), rev),
                   _const_spec((1, 128)), _const_spec((1, 128)), _const_spec((1, 128))],
        out_shape=[jax.ShapeDtypeStruct((T, GD_QKV), F32), jax.ShapeDtypeStruct((T, GD_VW), BF16),
                   jax.ShapeDtypeStruct((T, 128), F32)] + [jax.ShapeDtypeStruct((1, 128), F32)] * 3,
        scratch_shapes=[pltpu.VMEM((GD_VH, GD_DK, GD_DK), F32)],
        compiler_params=_params(dimension_semantics=("arbitrary",)),
    )(qkv, z, ab, al, dt, og, states, du)


def _my_pos():
    return lax.axis_index("x"), lax.axis_index("y"), lax.axis_index("c")


def _peers(pos):
    out = []
    for k in range(1, N_DEV):
        dev = tuple(1 - p if (k >> s) & 1 else p for p, s in zip(pos, (2, 1, 0)))
        out.append((dev, 4 * dev[0] + 2 * dev[1] + dev[2]))
    return out


_ANY = pl.BlockSpec(memory_space=pl.ANY)
_COMM_SCRATCH = [pltpu.SemaphoreType.DMA((N_DEV - 1,)), pltpu.SemaphoreType.DMA((N_DEV - 1,)), pltpu.SemaphoreType.DMA(())]


def all_gather(name, x):
    def body(x_ref, o_ref, send_sems, recv_sems, local_sem):
        pos = _my_pos()
        me = 4 * pos[0] + 2 * pos[1] + pos[2]
        peers = _peers(pos)
        mine = pltpu.make_async_copy(x_ref, o_ref.at[me], local_sem)
        mine.start()
        sends = [pltpu.make_async_remote_copy(x_ref, o_ref.at[me], send_sems.at[k], recv_sems.at[k],
                                              device_id=dev, device_id_type=MESH) for k, (dev, _) in enumerate(peers)]
        for cp in sends:
            cp.start()
        for k, (dev, pid) in enumerate(peers):
            pltpu.make_async_remote_copy(x_ref, o_ref.at[pid], send_sems.at[k], recv_sems.at[k],
                                         device_id=dev, device_id_type=MESH).wait_recv()
        for cp in sends:
            cp.wait_send()
        mine.wait()

    return pl.pallas_call(
        body, name=name, out_shape=jax.ShapeDtypeStruct((N_DEV,) + x.shape, x.dtype),
        in_specs=[_ANY], out_specs=_ANY, scratch_shapes=_COMM_SCRATCH,
    )(x)


def all_to_all(name, parts):
    def body(x_ref, o_ref, send_sems, recv_sems, local_sem):
        pos = _my_pos()
        me = 4 * pos[0] + 2 * pos[1] + pos[2]
        peers = _peers(pos)
        mine = pltpu.make_async_copy(x_ref.at[me], o_ref.at[me], local_sem)
        mine.start()
        sends = [pltpu.make_async_remote_copy(x_ref.at[pid], o_ref.at[me], send_sems.at[k], recv_sems.at[k],
                                              device_id=dev, device_id_type=MESH) for k, (dev, pid) in enumerate(peers)]
        for cp in sends:
            cp.start()
        for k, (dev, pid) in enumerate(peers):
            pltpu.make_async_remote_copy(x_ref.at[pid], o_ref.at[pid], send_sems.at[k], recv_sems.at[k],
                                         device_id=dev, device_id_type=MESH).wait_recv()
        for cp in sends:
            cp.wait_send()
        mine.wait()

    return pl.pallas_call(
        body, name=name, out_shape=jax.ShapeDtypeStruct(parts.shape, parts.dtype),
        in_specs=[_ANY], out_specs=_ANY, scratch_shapes=_COMM_SCRATCH,
    )(parts)


def lb_rows(r0, r1, r2, r3):
    mx = jnp.maximum(jnp.maximum(r0, r1), jnp.maximum(r2, r3))
    e = [jnp.exp(r - mx) for r in (r0, r1, r2, r3)]
    inv = 1.0 / (e[0] + e[1] + e[2] + e[3])
    c0 = e[0] * inv
    c1 = c0 + e[1] * inv
    c2 = c1 + e[2] * inv
    c3 = c2 + e[3] * inv
    return c0 - c0, c1 - c0, c2 - c0, c3 - c0


def mod_partial(name, c_all, ada_w):
    n, _, w = ada_w.shape

    def body(c_ref, w_ref, o_ref):
        o_ref[0] = mm(c_ref[...], w_ref[0])

    return pl.pallas_call(
        body, name=name, grid=(n,),
        in_specs=[_const_spec((N_DEV, D)), pl.BlockSpec((1, D, w), lambda i: (i, 0, 0))],
        out_specs=pl.BlockSpec((1, N_DEV, w), lambda i: (i, 0, 0)),
        out_shape=jax.ShapeDtypeStruct((n, N_DEV, w), F32),
    )(c_all, ada_w)


def prep(name, modp, ada_b, hgrn_lb):
    def body(mp_ref, b_ref, lb_ref, mod_ref, lbo_ref):
        mod_ref[...] = mp_ref[...] + b_ref[...]
        out = lb_rows(*[lb_ref[pl.ds(i, 1), :] for i in range(4)])
        for i in range(4):
            lbo_ref[pl.ds(i, 1), :] = out[i]

    return pl.pallas_call(
        body, name=name,
        out_shape=[jax.ShapeDtypeStruct(modp.shape, F32), jax.ShapeDtypeStruct(hgrn_lb.shape, F32)],
    )(modp, ada_b, hgrn_lb)


def lb_grad(name, hgrn_lb, dlb_parts):
    def body(lb_ref, d_ref, o_ref):
        cts = []
        for i in range(4):
            acc = d_ref[0, pl.ds(i, 1), :]
            for p in range(1, N_DEV):
                acc = acc + d_ref[p, pl.ds(i, 1), :]
            cts.append(acc)
        _, f = jax.vjp(lb_rows, *[lb_ref[pl.ds(i, 1), :] for i in range(4)])
        for i, g in enumerate(f(tuple(cts))):
            o_ref[pl.ds(i, 1), :] = g

    return pl.pallas_call(body, name=name, out_shape=jax.ShapeDtypeStruct(hgrn_lb.shape, F32))(hgrn_lb, dlb_parts)


def ada_grad(name, c_t, dm):
    n, _, w = dm.shape

    def body(c_ref, d_ref, o_ref):
        acc = c_ref[:, pl.ds(0, 1)] * d_ref[0, pl.ds(0, 1), :]
        for b in range(1, N_DEV):
            acc = acc + c_ref[:, pl.ds(b, 1)] * d_ref[0, pl.ds(b, 1), :]
        o_ref[0] = acc

    return pl.pallas_call(
        body, name=name, grid=(n,),
        in_specs=[_const_spec((D, N_DEV)), pl.BlockSpec((1, N_DEV, w), lambda i: (i, 0, 0))],
        out_specs=pl.BlockSpec((1, D, w), lambda i: (i, 0, 0)),
        out_shape=jax.ShapeDtypeStruct((n, D, w), F32),
    )(c_t, dm)


ADAMW_ROWS = 256


def adamw(name, w, m, v, gparts):
    shape = w.shape
    L = shape[-1]
    R = 1
    for s in shape[:-1]:
        R *= s
    P = gparts.shape[0]
    tr = min(R, ADAMW_ROWS)

    def body(w_ref, m_ref, v_ref, g_ref, go_ref, d_ref, mo_ref, vo_ref):
        g = g_ref[0]
        for p in range(1, P):
            g = g + g_ref[p]
        mn = ADAM_B1 * m_ref[...] + (1.0 - ADAM_B1) * g
        vn = ADAM_B2 * v_ref[...] + (1.0 - ADAM_B2) * (g * g)
        m_hat = mn / (1.0 - ADAM_B1 ** ADAM_STEP)
        v_hat = vn / (1.0 - ADAM_B2 ** ADAM_STEP)
        go_ref[...] = g
        d_ref[...] = -ADAM_LR * (m_hat / (jnp.sqrt(v_hat) + ADAM_EPS) + ADAM_WD * w_ref[...])
        mo_ref[...] = mn
        vo_ref[...] = vn

    spec = pl.BlockSpec((tr, L), lambda i: (i, 0))
    outs = pl.pallas_call(
        body, name=name, grid=(R // tr,),
        in_specs=[spec, spec, spec, pl.BlockSpec((P, tr, L), lambda i: (0, i, 0))],
        out_specs=[spec] * 4, out_shape=[jax.ShapeDtypeStruct((R, L), F32)] * 4,
        compiler_params=_params(dimension_semantics=("arbitrary",)),
    )(w.reshape(R, L), m.reshape(R, L), v.reshape(R, L), gparts.reshape(P, R, L))
    return [o.reshape(shape) for o in outs]


def _gather_cols(name, w):
    g = all_gather(name, w.astype(BF16))
    n, k, w8 = w.shape
    return jnp.transpose(g, (1, 2, 0, 3)).reshape(n, k, N_DEV * w8)


def _gather_rows(name, w):
    g = all_gather(name, w.astype(BF16))
    n, k8, nn = w.shape
    return jnp.transpose(g, (1, 0, 2, 3)).reshape(n, N_DEV * k8, nn)


def _scatter_cols(dw):
    n, k, nn = dw.shape
    return jnp.transpose(dw.reshape(n, k, N_DEV, nn // N_DEV), (2, 0, 1, 3))


def _scatter_rows(dw):
    n, k, nn = dw.shape
    return jnp.transpose(dw.reshape(n, N_DEV, k // N_DEV, nn), (1, 0, 2, 3))


def _pad_lanes(v, width=128):
    return jnp.pad(v, ((0, 0), (0, width - v.shape[1])))


def _stack_rows(parts):
    n, L = len(parts), parts[0].shape[1]
    r = lax.broadcasted_iota(jnp.int32, (n, L), 0)
    out = jnp.zeros((n, L), parts[0].dtype)
    for i, p in enumerate(parts):
        out = jnp.where(r == i, jnp.broadcast_to(p, (n, L)), out)
    return out


def kernel(x, c, positions, hgrn_lb, ada_w, ada_b, norm_g, hg_in_w, hg_out_w, hg_onorm, sw_in_w, sw_out_w, sw_qnorm, sw_knorm, sw_sinks, gd_in_w, gd_out_w, gd_conv_w, gd_a_log, gd_dt_bias, gd_onorm, loss_target, m_hgrn_lb, m_ada_w, m_ada_b, m_norm_g, m_hg_in_w, m_hg_out_w, m_hg_onorm, m_sw_in_w, m_sw_out_w, m_sw_qnorm, m_sw_knorm, m_sw_sinks, m_gd_in_w, m_gd_out_w, m_gd_conv_w, m_gd_a_log, m_gd_dt_bias, m_gd_onorm, v_hgrn_lb, v_ada_w, v_ada_b, v_norm_g, v_hg_in_w, v_hg_out_w, v_hg_onorm, v_sw_in_w, v_sw_out_w, v_sw_qnorm, v_sw_knorm, v_sw_sinks, v_gd_in_w, v_gd_out_w, v_gd_conv_w, v_gd_a_log, v_gd_dt_bias, v_gd_onorm):
    pos = _my_pos()
    me = 4 * pos[0] + 2 * pos[1] + pos[2]
    x0 = x[0]
    tgt = loss_target[0]
    n_layers = norm_g.shape[0]
    aw = ada_w.shape[2]

    c_all = all_gather("ag_c", c)[:, 0, :]
    modp = all_gather("ag_mod", mod_partial("mod_partial", c_all, ada_w))
    modp_mine = lax.dynamic_index_in_dim(modp, me, axis=2, keepdims=False)
    modp_mine = jnp.transpose(modp_mine, (1, 0, 2)).reshape(n_layers, N_DEV * aw)
    mod, lb_all = prep("prep", modp_mine, ada_b, hgrn_lb)
    shift, scale, gate = mod[:, :D], mod[:, D:2 * D], mod[:, 2 * D:]
    row = lambda a, i: a[i:i + 1]

    w_hg_in = _gather_cols("ag_hg_in", hg_in_w)
    w_hg_out = _gather_rows("ag_hg_out", hg_out_w)
    w_sw_in = _gather_cols("ag_sw_in", sw_in_w)[0]
    w_sw_out = _gather_rows("ag_sw_out", sw_out_w)[0]
    w_gd_in = _gather_cols("ag_gd_in", gd_in_w)[0]
    w_gd_out = _gather_rows("ag_gd_out", gd_out_w)[0]
    w_gd_qkv, w_gd_z = w_gd_in[:, :GD_QKV], w_gd_in[:, GD_QKV:GD_QKV + GD_VW]
    w_gd_ab = _pad_lanes(w_gd_in[:, GD_QKV + GD_VW:])
    conv_all = all_gather("ag_conv", gd_conv_w)
    conv_w = jnp.transpose(conv_all[:, 0], (1, 0, 2)).reshape(CONV_K, GD_QKV)
    conv_w8 = jnp.pad(conv_w, ((0, HALO - CONV_K), (0, 0)))
    gd_al, gd_dt = _pad_lanes(gd_a_log), _pad_lanes(gd_dt_bias)

    inv_freq = ROPE_THETA ** (-jnp.arange(0, SW_DH, 2, dtype=F32) / SW_DH)
    ang = positions[0].astype(F32)[:, None] * inv_freq
    cos, sin = jnp.cos(ang), jnp.sin(ang)
    cos64, sin64 = jnp.concatenate([cos, cos], axis=-1), jnp.concatenate([-sin, sin], axis=-1)

    xs, hs, ys, saved = [x0], [], [], []
    h = norm_mod("norm0", x0, row(norm_g, 0), row(scale, 0), row(shift, 0))
    for i in range(n_layers):
        kind, j = i % 3, i // 3
        hs.append(h)
        if kind == 0:
            p = matmul(f"hg_in{i}", h, w_hg_in[j])
            u, st = hg_fwd(f"hg_fwd{i}", p, row(lb_all, i), row(hg_onorm, j))
            y = matmul(f"hg_out{i}", u, w_hg_out[j])
            saved.append((p, u, st))
        elif kind == 1:
            p = matmul(f"sw_in{i}", h, w_sw_in)
            u, kpost = swa_fwd(f"sw_fwd{i}", p, cos64, sin64, row(sw_qnorm, j), row(sw_knorm, j), row(sw_sinks, j))
            y = matmul(f"sw_out{i}", u, w_sw_out)
            saved.append((p, u, kpost))
        else:
            xq = matmul(f"gd_qkv{i}", h, w_gd_qkv)
            zz = matmul(f"gd_z{i}", h, w_gd_z)
            ab = matmul(f"gd_ab{i}", h, w_gd_ab)
            cv = conv_fwd(f"gd_conv{i}", xq, conv_w8)
            u, st = gd_fwd(f"gd_fwd{i}", cv, zz, ab, gd_al, gd_dt, row(gd_onorm, j))
            y = matmul(f"gd_out{i}", u, w_gd_out)
            saved.append((xq, zz, ab, cv, u, st))
        ys.append(y)
        if i + 1 < n_layers:
            xn, h = resid_norm(f"resid{i}", xs[i], y, row(gate, i), row(norm_g, i + 1), row(scale, i + 1), row(shift, i + 1))
            xs.append(xn)

    last = n_layers - 1
    dx, dy, loss_acc, dgate_last = loss_head("loss", xs[last], ys[last], tgt, row(gate, last))
    loss = lax.psum(loss_acc[0, 0], ("x", "y", "c"))

    dgate = [None] * n_layers
    dgate[last] = dgate_last
    dg_norm, dscale, dshift = [None] * n_layers, [None] * n_layers, [None] * n_layers
    dlb = [jnp.zeros((1, D), F32)] * n_layers
    d_hg_in, d_hg_out, d_hg_on = [None] * 2, [None] * 2, [None] * 2
    for i in range(last, -1, -1):
        kind, j = i % 3, i // 3
        h = hs[i]
        if kind == 0:
            p, u, st = saved[i]
            du = matmul_nt(f"hg_du{i}", [(dy, w_hg_out[j])])
            d_hg_out[j] = matmul_tn(f"hg_dwo{i}", u, dy)
            dp, dlb_i, dog = hg_bwd(f"hg_bwd{i}", p, row(lb_all, i), row(hg_onorm, j), st, du)
            dlb[i] = dlb_i
            d_hg_on[j] = dog
            dh = matmul_nt(f"hg_dh{i}", [(dp, w_hg_in[j])])
            d_hg_in[j] = matmul_tn(f"hg_dwi{i}", h, dp)
        elif kind == 1:
            p, u, kpost = saved[i]
            du = matmul_nt(f"sw_du{i}", [(dy, w_sw_out)])
            d_sw_out = matmul_tn(f"sw_dwo{i}", u, dy)
            dp, d_qn, d_kn, d_sk = swa_bwd(f"sw_bwd{i}", p, kpost, cos64, sin64, row(sw_qnorm, j), row(sw_knorm, j),
                                           row(sw_sinks, j), du)
            dh = matmul_nt(f"sw_dh{i}", [(dp, w_sw_in)])
            d_sw_in = matmul_tn(f"sw_dwi{i}", h, dp)
        else:
            xq, zz, ab, cv, u, st = saved[i]
            du = matmul_nt(f"gd_du{i}", [(dy, w_gd_out)])
            d_gd_out = matmul_tn(f"gd_dwo{i}", u, dy)
            dcv, dz, dab, d_al, d_dt, d_gd_on = gd_bwd(f"gd_bwd{i}", cv, zz, ab, gd_al, gd_dt, row(gd_onorm, j), st, du)
            dxq, d_conv8 = conv_bwd(f"gd_dconv{i}", xq, dcv, conv_w8)
            dh = matmul_nt(f"gd_dh{i}", [(dxq, w_gd_qkv), (dz, w_gd_z), (dab, w_gd_ab)])
            d_gd_in = jnp.concatenate([matmul_tn(f"gd_dwq{i}", h, dxq), matmul_tn(f"gd_dwz{i}", h, dz),
                                       matmul_tn(f"gd_dwab{i}", h, dab)[:, :2 * GD_VH]], axis=1)
        if i > 0:
            dx, dy, dg_norm[i], dscale[i], dshift[i], dgate[i - 1] = bwd_rowwise(
                f"bwd_row{i}", xs[i], dh, dx, row(norm_g, i), row(scale, i), row(shift, i), ys[i - 1], row(gate, i - 1))
        else:
            dx, dg_norm[i], dscale[i], dshift[i] = bwd_rowwise(
                f"bwd_row{i}", xs[i], dh, dx, row(norm_g, i), row(scale, i), row(shift, i))
    grad_x = dx[None]

    dmod = jnp.concatenate([_stack_rows(dshift), _stack_rows(dscale), _stack_rows(dgate)], axis=1)
    misc = _stack_rows(d_hg_on + [d_gd_on, _pad_lanes(d_qn), _pad_lanes(d_kn), _pad_lanes(d_sk), d_al, d_dt])
    small = jnp.concatenate([_stack_rows(dlb).reshape(-1, 128), _stack_rows(dg_norm).reshape(-1, 128),
                             dmod.reshape(-1, 128), misc], axis=0)
    small_all = all_gather("ag_small", small)
    n_lb = n_layers * D // 128
    n_mod = n_layers * 3 * D // 128
    o = 0
    dlb_parts = small_all[:, o:o + n_lb].reshape(N_DEV, n_layers, D); o += n_lb
    dgn_parts = small_all[:, o:o + n_lb].reshape(N_DEV, n_layers, D); o += n_lb
    dmod_parts = small_all[:, o:o + n_mod].reshape(N_DEV, n_layers, 3 * D); o += n_mod
    dhgon_parts = small_all[:, o:o + 2]; o += 2
    dgdon_parts = small_all[:, o:o + 1]; o += 1
    dqn_parts = small_all[:, o:o + 1, :SW_DH]; o += 1
    dkn_parts = small_all[:, o:o + 1, :SW_DH]; o += 1
    dsk_parts = small_all[:, o:o + 1, :SW_KV * SW_G]; o += 1
    dal_parts = small_all[:, o:o + 1, :GD_VH]; o += 1
    ddt_parts = small_all[:, o:o + 1, :GD_VH]; o += 1

    g_lb = lb_grad("lb_grad", hgrn_lb, dlb_parts)
    dm_mine = lax.dynamic_slice_in_dim(dmod_parts, me * aw, aw, axis=2)
    g_ada_w = ada_grad("ada_grad", jnp.transpose(c_all), jnp.transpose(dm_mine, (1, 0, 2)))

    res = {}
    res["hgrn_lb"] = adamw("aw_hgrn_lb", hgrn_lb, m_hgrn_lb, v_hgrn_lb, g_lb[None])
    res["ada_w"] = adamw("aw_ada_w", ada_w, m_ada_w, v_ada_w, g_ada_w[None])
    res["ada_b"] = adamw("aw_ada_b", ada_b, m_ada_b, v_ada_b, dmod_parts)
    res["norm_g"] = adamw("aw_norm_g", norm_g, m_norm_g, v_norm_g, dgn_parts)
    res["hg_in_w"] = adamw("aw_hg_in", hg_in_w, m_hg_in_w, v_hg_in_w,
                           all_to_all("rs_hg_in", _scatter_cols(jnp.stack(d_hg_in))))
    res["hg_out_w"] = adamw("aw_hg_out", hg_out_w, m_hg_out_w, v_hg_out_w,
                            all_to_all("rs_hg_out", _scatter_rows(jnp.stack(d_hg_out))))
    res["hg_onorm"] = adamw("aw_hg_onorm", hg_onorm, m_hg_onorm, v_hg_onorm, dhgon_parts)
    res["sw_in_w"] = adamw("aw_sw_in", sw_in_w, m_sw_in_w, v_sw_in_w, all_to_all("rs_sw_in", _scatter_cols(d_sw_in[None])))
    res["sw_out_w"] = adamw("aw_sw_out", sw_out_w, m_sw_out_w, v_sw_out_w, all_to_all("rs_sw_out", _scatter_rows(d_sw_out[None])))
    res["sw_qnorm"] = adamw("aw_sw_qn", sw_qnorm, m_sw_qnorm, v_sw_qnorm, dqn_parts)
    res["sw_knorm"] = adamw("aw_sw_kn", sw_knorm, m_sw_knorm, v_sw_knorm, dkn_parts)
    res["sw_sinks"] = adamw("aw_sw_sinks", sw_sinks, m_sw_sinks, v_sw_sinks, dsk_parts)
    res["gd_in_w"] = adamw("aw_gd_in", gd_in_w, m_gd_in_w, v_gd_in_w, all_to_all("rs_gd_in", _scatter_cols(d_gd_in[None])))
    res["gd_out_w"] = adamw("aw_gd_out", gd_out_w, m_gd_out_w, v_gd_out_w, all_to_all("rs_gd_out", _scatter_rows(d_gd_out[None])))
    res["gd_conv_w"] = adamw("aw_gd_conv", gd_conv_w, m_gd_conv_w, v_gd_conv_w,
                             all_to_all("rs_gd_conv", _scatter_cols(d_conv8[None, :CONV_K])))
    res["gd_a_log"] = adamw("aw_gd_alog", gd_a_log, m_gd_a_log, v_gd_a_log, dal_parts)
    res["gd_dt_bias"] = adamw("aw_gd_dt", gd_dt_bias, m_gd_dt_bias, v_gd_dt_bias, ddt_parts)
    res["gd_onorm"] = adamw("aw_gd_onorm", gd_onorm, m_gd_onorm, v_gd_onorm, dgdon_parts)

    order = ["hgrn_lb", "ada_w", "ada_b", "norm_g", "hg_in_w", "hg_out_w", "hg_onorm", "sw_in_w", "sw_out_w", "sw_qnorm",
             "sw_knorm", "sw_sinks", "gd_in_w", "gd_out_w", "gd_conv_w", "gd_a_log", "gd_dt_bias", "gd_onorm"]
    outs = [loss, grad_x]
    for part in range(4):
        outs += [res[n][part] for n in order]
    return tuple(outs)
```

```python
import functools

import jax
import jax.numpy as jnp
from jax import lax
from jax.experimental import pallas as pl
from jax.experimental.pallas import tpu as pltpu

F32, BF16 = jnp.float32, jnp.bfloat16
HI = lax.Precision.HIGHEST
MESH = pl.DeviceIdType.MESH
N_DEV = 8
D = 1024
EPS = 1e-6
NEG_INF = float("-inf")

HG_HEADS, HG_DH, HG_BLK, HG_CH, HG_SUB, HG_HB = 8, 128, 128, 32, 8, 8
SW_KV, SW_G, SW_DH, SW_BLK = 4, 4, 64, 128
SW_QW, SW_KVW = 1024, 256
GD_QK_HEADS, GD_DK, GD_CHUNK = 8, 128, 64
GD_QKW, GD_VW, GD_QKV = 1024, 2048, 4096
CONV_K = 4
ROPE_THETA = 10000.0

ADAM_LR, ADAM_B1, ADAM_B2, ADAM_EPS, ADAM_WD, ADAM_STEP = 0.001, 0.9, 0.999, 1e-08, 0.01, 10

VMEM_LIMIT = 56 * 1024 * 1024


def _params(**kw):
    return pltpu.CompilerParams(vmem_limit_bytes=VMEM_LIMIT, **kw)


def _dot(a, b, ca, cb):
    return lax.dot_general(a.astype(BF16), b.astype(BF16), (((ca,), (cb,)), ((), ())), preferred_element_type=F32)


def mm(a, b):
    return _dot(a, b, 1, 0)


def mm_nt(a, b):
    return _dot(a, b, 1, 1)


def mm_tn(a, b):
    return _dot(a, b, 0, 0)


def mmf(a, b):
    return lax.dot_general(a, b, (((1,), (0,)), ((), ())), precision=HI, preferred_element_type=F32)


def _split3(x):
    h1 = x.astype(BF16)
    r1 = x - h1.astype(F32)
    h2 = r1.astype(BF16)
    h3 = (r1 - h2.astype(F32)).astype(BF16)
    return h1, h2, h3


def _lin01_l(m, x):
    mb = m.astype(BF16)
    return sum(lax.dot_general(mb, p, (((1,), (0,)), ((), ())), preferred_element_type=F32) for p in _split3(x))


def _lin01_r(x, m):
    mb = m.astype(BF16)
    return sum(lax.dot_general(p, mb, (((1,), (0,)), ((), ())), preferred_element_type=F32) for p in _split3(x))


def _bdot(a, b, ca, cb, prec=None):
    return lax.dot_general(a, b, (((ca,), (cb,)), ((0,), (0,))), precision=prec, preferred_element_type=F32)


def bmm(a, b):
    return _bdot(a.astype(BF16), b.astype(BF16), 2, 1)


def bmm_nt(a, b):
    return _bdot(a.astype(BF16), b.astype(BF16), 2, 2)


def bmm_tn(a, b):
    return _bdot(a.astype(BF16), b.astype(BF16), 1, 1)


def bmmf(a, b):
    return _bdot(a, b, 2, 1, lax.Precision.HIGH)


def _blin01(m, x):
    mb = m.astype(BF16)
    return sum(_bdot(mb, p, 2, 1) for p in _split3(x))


@jax.custom_vjp
def blin01(m, mt, x):
    return _blin01(m, x)


def _blin01_fwd(m, mt, x):
    return _blin01(m, x), (m, mt)


def _blin01_bwd(res, g):
    m, mt = res
    return jnp.zeros_like(m), jnp.zeros_like(mt), _blin01(mt, g)


blin01.defvjp(_blin01_fwd, _blin01_bwd)


def _blin01_r(x, m):
    mb = m.astype(BF16)
    return sum(_bdot(p, mb, 2, 1) for p in _split3(x))


@jax.custom_vjp
def blin01_r(x, m, mt):
    return _blin01_r(x, m)


def _blin01_r_fwd(x, m, mt):
    return _blin01_r(x, m), (m, mt)


def _blin01_r_bwd(res, g):
    m, mt = res
    return _blin01_r(g, mt), jnp.zeros_like(m), jnp.zeros_like(mt)


blin01_r.defvjp(_blin01_r_fwd, _blin01_r_bwd)


@jax.custom_vjp
def lin01_l(m, mt, x):
    return _lin01_l(m, x)


def _lin01_l_fwd(m, mt, x):
    return _lin01_l(m, x), (m, mt)


def _lin01_l_bwd(res, g):
    m, mt = res
    return jnp.zeros_like(m), jnp.zeros_like(mt), _lin01_l(mt, g)


lin01_l.defvjp(_lin01_l_fwd, _lin01_l_bwd)


@jax.custom_vjp
def lin01_r(x, m, mt):
    return _lin01_r(x, m)


def _lin01_r_fwd(x, m, mt):
    return _lin01_r(x, m), (m, mt)


def _lin01_r_bwd(res, g):
    m, mt = res
    return _lin01_r(g, mt), jnp.zeros_like(m), jnp.zeros_like(mt)


lin01_r.defvjp(_lin01_r_fwd, _lin01_r_bwd)


@functools.partial(jax.custom_vjp, nondiff_argnums=(1, 2))
def rows(x, start, size):
    return lax.slice_in_dim(x, start, start + size, axis=x.ndim - 2)


def _rows_fwd(x, start, size):
    return lax.slice_in_dim(x, start, start + size, axis=x.ndim - 2), jnp.zeros(x.shape[:-1] + (1,), F32)


def _rows_bwd(start, size, res, g):
    ax = g.ndim - 2
    total = res.shape[ax]
    zeros = lambda n: jnp.zeros(g.shape[:ax] + (n,) + g.shape[ax + 1:], g.dtype)
    parts = []
    if start > 0:
        parts.append(zeros(start))
    parts.append(g)
    if total - start - size > 0:
        parts.append(zeros(total - start - size))
    return (jnp.concatenate(parts, axis=ax) if len(parts) > 1 else g,)


rows.defvjp(_rows_fwd, _rows_bwd)


@functools.partial(jax.custom_vjp, nondiff_argnums=(1,))
def rowsel(x, s):
    return lax.slice_in_dim(x, s, s + 1, axis=x.ndim - 2)


def _rowsel_fwd(x, s):
    return lax.slice_in_dim(x, s, s + 1, axis=x.ndim - 2), jnp.zeros(x.shape[:-1] + (1,), F32)


def _rowsel_bwd(s, res, g):
    r = lax.broadcasted_iota(jnp.int32, res.shape, res.ndim - 2)
    return (jnp.where(r == s, g, 0.0),)


rowsel.defvjp(_rowsel_fwd, _rowsel_bwd)


def sigmoid(x):
    return jax.nn.sigmoid(x)


def silu(x):
    return x * jax.nn.sigmoid(x)


def softplus(x):
    return jnp.maximum(x, 0.0) + jnp.log1p(jnp.exp(-jnp.abs(x)))


def _iota(shape, dim):
    return lax.broadcasted_iota(jnp.int32, shape, dim)


def norm_mod_fn(g, sc, sh, x):
    r = lax.rsqrt(jnp.mean(x * x, axis=-1, keepdims=True) + EPS)
    return (x * r * g) * (1.0 + sc) + sh


def _row_spec(tm, f):
    return pl.BlockSpec((tm, f), lambda i: (i, 0))


def _const_spec(shape):
    nd = len(shape)
    return pl.BlockSpec(shape, lambda i: (0,) * nd)


def _tiled(name, fn, consts, rows_in, row_outs, acc_outs=(), tm=512):
    T = rows_in[0].shape[0]
    tm = min(tm, T)
    n_c, n_r, n_ro = len(consts), len(rows_in), len(row_outs)

    def body(*refs):
        c_refs, r_refs = refs[:n_c], refs[n_c:n_c + n_r]
        ro_refs, ao_refs = refs[n_c + n_r:n_c + n_r + n_ro], refs[n_c + n_r + n_ro:]
        outs = fn(*[r[...] for r in c_refs], *[r[...] for r in r_refs])
        for r, v in zip(ro_refs, outs[:n_ro]):
            r[...] = v.astype(r.dtype)
        if ao_refs:
            @pl.when(pl.program_id(0) == 0)
            def _():
                for r in ao_refs:
                    r[...] = jnp.zeros(r.shape, r.dtype)
            for r, v in zip(ao_refs, outs[n_ro:]):
                r[...] += v

    out_shape = [jax.ShapeDtypeStruct((T, f), dt) for f, dt in row_outs] + [jax.ShapeDtypeStruct(s, F32) for s in acc_outs]
    out_specs = [_row_spec(tm, f) for f, _ in row_outs] + [_const_spec(s) for s in acc_outs]
    in_specs = [_const_spec(c.shape) for c in consts] + [_row_spec(tm, r.shape[1]) for r in rows_in]
    return pl.pallas_call(
        body, name=name, grid=(T // tm,), in_specs=in_specs, out_specs=out_specs, out_shape=out_shape,
        compiler_params=_params(dimension_semantics=("arbitrary",)),
    )(*consts, *rows_in)


def norm_mod(name, x, g, sc, sh):
    (h,) = _tiled(name, lambda g, sc, sh, x: (norm_mod_fn(g, sc, sh, x),), [g, sc, sh], [x], [(D, BF16)])
    return h


def resid_norm(name, x, y, gate, g, sc, sh):
    def fn(gate, g, sc, sh, x, y):
        xn = x + gate * y
        return xn, norm_mod_fn(g, sc, sh, xn)
    return _tiled(name, fn, [gate, g, sc, sh], [x, y], [(D, F32), (D, BF16)])


def loss_head(name, x, y, tgt, gate):
    def fn(gate, x, y, tgt):
        err = x + gate * y - tgt
        dx = err * (1.0 / D)
        per_tok = jnp.sum(err * err, axis=-1, keepdims=True) * (0.5 / D)
        loss = jnp.sum(per_tok, axis=0, keepdims=True)
        return dx, gate * dx, jnp.broadcast_to(loss, (1, 128)), jnp.sum(dx * y, axis=0, keepdims=True)
    return _tiled(name, fn, [gate], [x, y, tgt], [(D, F32), (D, BF16)], [(1, 128), (1, D)])


def bwd_rowwise(name, x, dh, dxn, g, sc, sh, y_prev=None, gate_prev=None):
    with_prev = y_prev is not None

    def fn(*a):
        if with_prev:
            gate_p, g, sc, sh, x, dh, dxn, yp = a
        else:
            g, sc, sh, x, dh, dxn = a
        _, f = jax.vjp(norm_mod_fn, g, sc, sh, x)
        dg, dsc, dsh, dx = f(dh)
        dx = dx + dxn
        if with_prev:
            return dx, gate_p * dx, dg, dsc, dsh, jnp.sum(dx * yp, axis=0, keepdims=True)
        return dx, dg, dsc, dsh

    if with_prev:
        return _tiled(name, fn, [gate_prev, g, sc, sh], [x, dh, dxn, y_prev], [(D, F32), (D, BF16)], [(1, D)] * 4)
    return _tiled(name, fn, [g, sc, sh], [x, dh, dxn], [(D, F32)], [(1, D)] * 3)


MM_ROWS_WIDE, MM_ROWS_NARROW, MM_WIDE = 256, 512, 2048


def matmul(name, a, w, out_dtype=F32):
    T, K = a.shape
    N = w.shape[1]
    tm = min(MM_ROWS_NARROW if N <= MM_WIDE else MM_ROWS_WIDE, T)

    def body(a_ref, w_ref, o_ref):
        o_ref[...] = mm(a_ref[...], w_ref[...]).astype(o_ref.dtype)

    return pl.pallas_call(
        body, name=name, grid=(T // tm,),
        in_specs=[_row_spec(tm, K), _const_spec((K, N))], out_specs=_row_spec(tm, N),
        out_shape=jax.ShapeDtypeStruct((T, N), out_dtype),
        compiler_params=_params(dimension_semantics=("arbitrary",)),
    )(a, w)


def matmul_nt(name, pairs, out_dtype=F32):
    T = pairs[0][0].shape[0]
    K = pairs[0][1].shape[0]
    wide = sum(a.shape[1] for a, _ in pairs) > MM_WIDE
    tm = min(MM_ROWS_WIDE if wide else MM_ROWS_NARROW, T)
    n = len(pairs)

    def body(*refs):
        o_ref = refs[-1]
        acc = None
        for i in range(n):
            p = mm_nt(refs[2 * i][...], refs[2 * i + 1][...])
            acc = p if acc is None else acc + p
        o_ref[...] = acc.astype(o_ref.dtype)

    in_specs, args = [], []
    for a, w in pairs:
        in_specs += [_row_spec(tm, a.shape[1]), _const_spec(w.shape)]
        args += [a, w]
    return pl.pallas_call(
        body, name=name, grid=(T // tm,), in_specs=in_specs, out_specs=_row_spec(tm, K),
        out_shape=jax.ShapeDtypeStruct((T, K), out_dtype),
        compiler_params=_params(dimension_semantics=("arbitrary",)),
    )(*args)


def matmul_tn(name, a, b, tm=512, tn=1024):
    T, K = a.shape
    N = b.shape[1]
    tm = min(tm, T)
    tn = max(t for t in range(128, min(tn, N) + 1, 128) if N % t == 0)

    def body(a_ref, b_ref, o_ref):
        @pl.when(pl.program_id(1) == 0)
        def _():
            o_ref[...] = jnp.zeros(o_ref.shape, F32)
        o_ref[...] += mm_tn(a_ref[...], b_ref[...])

    return pl.pallas_call(
        body, name=name, grid=(N // tn, T // tm),
        in_specs=[pl.BlockSpec((tm, K), lambda j, i: (i, 0)), pl.BlockSpec((tm, tn), lambda j, i: (i, j))],
        out_specs=pl.BlockSpec((K, tn), lambda j, i: (0, j)),
        out_shape=jax.ShapeDtypeStruct((K, N), F32),
        compiler_params=_params(dimension_semantics=("arbitrary", "arbitrary")),
    )(a, b)


def hg_heads(tri, trit, lb, og, st, qp, fp, iv, z):
    blk = qp.shape[1]
    q = silu(qp)
    lf = jnp.log(lb + (1.0 - lb) * sigmoid(fp))
    k = (1.0 - lb) * sigmoid(-fp)
    r = _iota((1, HG_SUB, 1), 1)
    rc = _iota((1, HG_CH, 1), 1)
    outs = []
    for c in range(blk // HG_CH):
        qc, kc, vc, lfc = (rows(a, c * HG_CH, HG_CH) for a in (q, k, iv, lf))
        b = blin01(tri, trit, lfc)
        o_state = bmm_nt(qc * jnp.exp(b), st)
        parts = []
        for i in range(HG_CH // HG_SUB):
            qi, ki, vi, bi = (rows(a, i * HG_SUB, HG_SUB) for a in (qc, kc, vc, b))
            o = rows(o_state, i * HG_SUB, HG_SUB)
            for s in range(HG_SUB):
                ks, bs, vs = rowsel(ki, s), rowsel(bi, s), rowsel(vi, s)
                e = jnp.exp(jnp.where(r >= s, bi - bs, NEG_INF))
                o = o + jnp.sum(qi * ks * e, axis=-1, keepdims=True) * vs
            if i > 0:
                rb = rowsel(b, i * HG_SUB - 1)
                kt = kc * jnp.exp(jnp.where(rc < i * HG_SUB, rb - b, NEG_INF))
                o = o + bmm(bmm_nt(qi * jnp.exp(bi - rb), kt), vc)
            parts.append(o)
        bl = rowsel(b, HG_CH - 1)
        st = st * jnp.exp(bl) + bmm_tn(vc, kc * jnp.exp(bl - b))
        outs.append(jnp.concatenate(parts, axis=1))
    o = jnp.concatenate(outs, axis=1)
    on = o * lax.rsqrt(jnp.mean(o * o, axis=-1, keepdims=True) + EPS) * og
    return st, on * silu(z)


def _tri(n, nh):
    t = (_iota((nh, n, n), 1) >= _iota((nh, n, n), 2)).astype(F32)
    tt = (_iota((nh, n, n), 1) <= _iota((nh, n, n), 2)).astype(F32)
    return t, tt


def _hg_load(p_ref, lb_ref, heads):
    col = lambda ref, base: jnp.stack([ref[:, pl.ds(base + h * HG_DH, HG_DH)] for h in heads])
    return col(lb_ref, 0), col(p_ref, 0), col(p_ref, D), col(p_ref, 2 * D), col(p_ref, 3 * D)


def hg_fwd(name, p, lb, og):
    T = p.shape[0]
    nb = T // HG_BLK

    def body(p_ref, lb_ref, og_ref, u_ref, st_ref, s_scr):
        @pl.when(pl.program_id(0) == 0)
        def _():
            s_scr[...] = jnp.zeros(s_scr.shape, F32)
        st_ref[0] = s_scr[...]
        tri, trit = _tri(HG_CH, HG_HB)
        og_v = og_ref[...]
        for b0 in range(0, HG_HEADS, HG_HB):
            heads = list(range(b0, b0 + HG_HB))
            lbh, qp, fp, iv, z = _hg_load(p_ref, lb_ref, heads)
            st, u = hg_heads(tri, trit, lbh, og_v, s_scr[pl.ds(b0, HG_HB)], qp, fp, iv, z)
            s_scr[pl.ds(b0, HG_HB)] = st
            for i, h in enumerate(heads):
                u_ref[:, pl.ds(h * HG_DH, HG_DH)] = u[i].astype(BF16)

    return pl.pallas_call(
        body, name=name, grid=(nb,),
        in_specs=[_row_spec(HG_BLK, 4 * D), _const_spec((1, D)), _const_spec((1, HG_DH))],
        out_specs=[_row_spec(HG_BLK, D), pl.BlockSpec((1, HG_HEADS, HG_DH, HG_DH), lambda i: (i, 0, 0, 0))],
        out_shape=[jax.ShapeDtypeStruct((T, D), BF16), jax.ShapeDtypeStruct((nb, HG_HEADS, HG_DH, HG_DH), F32)],
        scratch_shapes=[pltpu.VMEM((HG_HEADS, HG_DH, HG_DH), F32)],
        compiler_params=_params(dimension_semantics=("arbitrary",)),
    )(p, lb, og)


def hg_bwd(name, p, lb, og, states, du):
    T = p.shape[0]
    nb = T // HG_BLK

    def body(p_ref, lb_ref, og_ref, st_ref, du_ref, dp_ref, dlb_ref, dog_ref, ds_scr):
        @pl.when(pl.program_id(0) == 0)
        def _():
            ds_scr[...] = jnp.zeros(ds_scr.shape, F32)
            dlb_ref[...] = jnp.zeros(dlb_ref.shape, F32)
            dog_ref[...] = jnp.zeros(dog_ref.shape, F32)
        tri, trit = _tri(HG_CH, HG_HB)
        og_v = og_ref[...]
        fn = functools.partial(hg_heads, tri, trit)
        for b0 in range(0, HG_HEADS, HG_HB):
            heads = list(range(b0, b0 + HG_HB))
            lbh, qp, fp, iv, z = _hg_load(p_ref, lb_ref, heads)
            _, f = jax.vjp(fn, lbh, og_v, st_ref[0, pl.ds(b0, HG_HB)], qp, fp, iv, z)
            du_h = jnp.stack([du_ref[:, pl.ds(h * HG_DH, HG_DH)].astype(F32) for h in heads])
            dlb, dog, dst, dq, df, di, dz = f((ds_scr[pl.ds(b0, HG_HB)], du_h))
            ds_scr[pl.ds(b0, HG_HB)] = dst
            dog_ref[...] += dog
            for i, h in enumerate(heads):
                for base, g in ((0, dq), (D, df), (2 * D, di), (3 * D, dz)):
                    dp_ref[:, pl.ds(base + h * HG_DH, HG_DH)] = g[i].astype(BF16)
                dlb_ref[:, pl.ds(h * HG_DH, HG_DH)] += dlb[i]

    rev = lambda i: (nb - 1 - i, 0)
    return pl.pallas_call(
        body, name=name, grid=(nb,),
        in_specs=[pl.BlockSpec((HG_BLK, 4 * D), rev), _const_spec((1, D)), _const_spec((1, HG_DH)),
                  pl.BlockSpec((1, HG_HEADS, HG_DH, HG_DH), lambda i: (nb - 1 - i, 0, 0, 0)),
                  pl.BlockSpec((HG_BLK, D), rev)],
        out_specs=[pl.BlockSpec((HG_BLK, 4 * D), rev), _const_spec((1, D)), _const_spec((1, HG_DH))],
        out_shape=[jax.ShapeDtypeStruct((T, 4 * D), BF16), jax.ShapeDtypeStruct((1, D), F32),
                   jax.ShapeDtypeStruct((1, HG_DH), F32)],
        scratch_shapes=[pltpu.VMEM((HG_HEADS, HG_DH, HG_DH), F32)],
        compiler_params=_params(dimension_semantics=("arbitrary",)),
    )(p, lb, og, states, du)


def swa_heads(qn, kn, sinks, kprev, vprev, qp4, kp, v, z4, cos4, sin4, cosk, sink_, hots, mask, p64, p64t):
    def norm_rope(xp, g, cs, sn):
        y = xp * lax.rsqrt(jnp.mean(xp * xp, axis=-1, keepdims=True) + EPS) * g
        return y * cs + blin01_r(y, p64, p64t) * sn

    q = norm_rope(qp4, qn, cos4, sin4)
    k = norm_rope(kp, kn, cosk, sink_)
    k2 = jnp.concatenate([kprev, k], axis=1)
    v2 = jnp.concatenate([vprev, v], axis=1)
    s = jnp.where(mask, bmm_nt(q, k2) * (SW_DH ** -0.5), NEG_INF)
    sink_col = jnp.concatenate([jnp.concatenate(
        [jnp.broadcast_to(jnp.sum(sinks * hot, axis=-1, keepdims=True), (SW_BLK, 1)) for hot in hot4], axis=0)[None]
        for hot4 in hots], axis=0)
    m = lax.stop_gradient(jnp.maximum(jnp.max(s, axis=-1, keepdims=True), sink_col))
    p = jnp.exp(s - m)
    p = p / (jnp.sum(p, axis=-1, keepdims=True) + jnp.exp(sink_col - m))
    o = bmm(p, v2)
    return k, v, o * silu(z4)


def _swa_consts(first):
    qi = _iota((SW_G * SW_BLK, 2 * SW_BLK), 0) % SW_BLK
    kj = _iota((SW_G * SW_BLK, 2 * SW_BLK), 1)
    rel = qi + SW_BLK - kj
    mask = (rel >= 0) & (rel < SW_BLK) & ((kj >= SW_BLK) | jnp.logical_not(first))
    i, j = _iota((SW_KV, SW_DH, SW_DH), 1), _iota((SW_KV, SW_DH, SW_DH), 2)
    p64 = (i == (j + SW_DH // 2) % SW_DH).astype(F32)
    p64t = (j == (i + SW_DH // 2) % SW_DH).astype(F32)
    lane = _iota((1, SW_KV * SW_G), 1)
    hots = [[(lane == h * SW_G + g).astype(F32) for g in range(SW_G)] for h in range(SW_KV)]
    return mask, p64, p64t, hots


def _stack_q(ref, h, base):
    return jnp.concatenate([ref[:, pl.ds(base + h * SW_G * SW_DH + g * SW_DH, SW_DH)] for g in range(SW_G)], axis=0)


def _swa_load(p_ref):
    heads = range(SW_KV)
    return (jnp.stack([_stack_q(p_ref, h, Q0) for h in heads]),
            jnp.stack([p_ref[:, pl.ds(K0 + h * SW_DH, SW_DH)] for h in heads]),
            jnp.stack([p_ref[:, pl.ds(V0 + h * SW_DH, SW_DH)] for h in heads]),
            jnp.stack([_stack_q(p_ref, h, Z0) for h in heads]))


Q0, K0, V0, Z0 = 0, SW_QW, SW_QW + SW_KVW, SW_QW + 2 * SW_KVW
SW_IN = 2 * SW_QW + 2 * SW_KVW


def swa_fwd(name, p, cos, sin, qn, kn, sinks):
    T = p.shape[0]
    nb = T // SW_BLK

    def body(p_ref, cos_ref, sin_ref, qn_ref, kn_ref, sk_ref, u_ref, ko_ref, kprev, vprev):
        first = pl.program_id(0) == 0

        @pl.when(first)
        def _():
            kprev[...] = jnp.zeros(kprev.shape, F32)
            vprev[...] = jnp.zeros(vprev.shape, F32)
        mask, p64, p64t, hots = _swa_consts(first)
        cs, sn = cos_ref[...], sin_ref[...]
        cs4, sn4 = jnp.concatenate([cs] * SW_G, axis=0), jnp.concatenate([sn] * SW_G, axis=0)
        qp4, kp, vv, z4 = _swa_load(p_ref)
        k, v, u4 = swa_heads(qn_ref[...], kn_ref[...], sk_ref[...], kprev[...], vprev[...], qp4, kp, vv, z4,
                             cs4, sn4, cs, sn, hots, mask, p64, p64t)
        kprev[...] = k
        vprev[...] = v
        for h in range(SW_KV):
            ko_ref[:, pl.ds(h * SW_DH, SW_DH)] = k[h]
            uh = u4[h]
            for g in range(SW_G):
                u_ref[:, pl.ds(h * SW_G * SW_DH + g * SW_DH, SW_DH)] = uh[g * SW_BLK:(g + 1) * SW_BLK].astype(BF16)

    return pl.pallas_call(
        body, name=name, grid=(nb,),
        in_specs=[_row_spec(SW_BLK, SW_IN), _row_spec(SW_BLK, SW_DH), _row_spec(SW_BLK, SW_DH),
                  _const_spec((1, SW_DH)), _const_spec((1, SW_DH)), _const_spec((1, SW_KV * SW_G))],
        out_specs=[_row_spec(SW_BLK, SW_QW), _row_spec(SW_BLK, SW_KVW)],
        out_shape=[jax.ShapeDtypeStruct((T, SW_QW), BF16), jax.ShapeDtypeStruct((T, SW_KVW), F32)],
        scratch_shapes=[pltpu.VMEM((SW_KV, SW_BLK, SW_DH), F32), pltpu.VMEM((SW_KV, SW_BLK, SW_DH), F32)],
        compiler_params=_params(dimension_semantics=("arbitrary",)),
    )(p, cos, sin, qn, kn, sinks)


def swa_bwd(name, p, kpost, cos, sin, qn, kn, sinks, du):
    T = p.shape[0]
    nb = T // SW_BLK

    def body(p_ref, pprev_ref, kprev_ref, cos_ref, sin_ref, qn_ref, kn_ref, sk_ref, du_ref,
             dp_ref, dqn_ref, dkn_ref, dsk_ref, dk_scr, dv_scr):
        i = pl.program_id(0)
        first = i == nb - 1

        @pl.when(i == 0)
        def _():
            dk_scr[...] = jnp.zeros(dk_scr.shape, F32)
            dv_scr[...] = jnp.zeros(dv_scr.shape, F32)
            dqn_ref[...] = jnp.zeros(dqn_ref.shape, F32)
            dkn_ref[...] = jnp.zeros(dkn_ref.shape, F32)
            dsk_ref[...] = jnp.zeros(dsk_ref.shape, F32)
        mask, p64, p64t, hots = _swa_consts(first)
        cs, sn = cos_ref[...], sin_ref[...]
        cs4, sn4 = jnp.concatenate([cs] * SW_G, axis=0), jnp.concatenate([sn] * SW_G, axis=0)
        heads = range(SW_KV)
        fn = lambda qn, kn, sk, kpv, vpv, qp4, kp, v, z4: swa_heads(
            qn, kn, sk, kpv, vpv, qp4, kp, v, z4, cs4, sn4, cs, sn, hots, mask, p64, p64t)
        kpv = jnp.stack([kprev_ref[:, pl.ds(h * SW_DH, SW_DH)] for h in heads])
        vpv = jnp.stack([pprev_ref[:, pl.ds(V0 + h * SW_DH, SW_DH)] for h in heads])
        _, f = jax.vjp(fn, qn_ref[...], kn_ref[...], sk_ref[...], kpv, vpv, *_swa_load(p_ref))
        du4 = jnp.stack([_stack_q(du_ref, h, 0).astype(F32) for h in heads])
        dqn, dkn, dsk, dkpv, dvpv, dq4, dkp, dv, dz4 = f((dk_scr[...], dv_scr[...], du4))
        dk_scr[...] = dkpv
        dv_scr[...] = dvpv
        for h in heads:
            dp_ref[:, pl.ds(K0 + h * SW_DH, SW_DH)] = dkp[h].astype(BF16)
            dp_ref[:, pl.ds(V0 + h * SW_DH, SW_DH)] = dv[h].astype(BF16)
            dqh, dzh = dq4[h], dz4[h]
            for g in range(SW_G):
                c0 = h * SW_G * SW_DH + g * SW_DH
                dp_ref[:, pl.ds(Q0 + c0, SW_DH)] = dqh[g * SW_BLK:(g + 1) * SW_BLK].astype(BF16)
                dp_ref[:, pl.ds(Z0 + c0, SW_DH)] = dzh[g * SW_BLK:(g + 1) * SW_BLK].astype(BF16)
        dqn_ref[...] += dqn
        dkn_ref[...] += dkn
        dsk_ref[...] += dsk

    rev = lambda i: (nb - 1 - i, 0)
    prev = lambda i: (jnp.maximum(nb - 2 - i, 0), 0)
    return pl.pallas_call(
        body, name=name, grid=(nb,),
        in_specs=[pl.BlockSpec((SW_BLK, SW_IN), rev), pl.BlockSpec((SW_BLK, SW_IN), prev),
                  pl.BlockSpec((SW_BLK, SW_KVW), prev), pl.BlockSpec((SW_BLK, SW_DH), rev), pl.BlockSpec((SW_BLK, SW_DH), rev),
                  _const_spec((1, SW_DH)), _const_spec((1, SW_DH)), _const_spec((1, SW_KV * SW_G)),
                  pl.BlockSpec((SW_BLK, SW_QW), rev)],
        out_specs=[pl.BlockSpec((SW_BLK, SW_IN), rev), _const_spec((1, SW_DH)), _const_spec((1, SW_DH)),
                   _const_spec((1, SW_KV * SW_G))],
        out_shape=[jax.ShapeDtypeStruct((T, SW_IN), BF16), jax.ShapeDtypeStruct((1, SW_DH), F32),
                   jax.ShapeDtypeStruct((1, SW_DH), F32), jax.ShapeDtypeStruct((1, SW_KV * SW_G), F32)],
        scratch_shapes=[pltpu.VMEM((SW_KV, SW_BLK, SW_DH), F32), pltpu.VMEM((SW_KV, SW_BLK, SW_DH), F32)],
        compiler_params=_params(dimension_semantics=("arbitrary",)),
    )(p, p, kpost, cos, sin, qn, kn, sinks, du)


CONV_TC, CONV_TL, HALO = 512, 1024, 8


def _conv_taps(xe, w_ref):
    acc = w_ref[pl.ds(CONV_K - 1, 1), :] * xe
    for k in range(CONV_K - 1):
        acc = acc + w_ref[pl.ds(k, 1), :] * pltpu.roll(xe, CONV_K - 1 - k, 0)
    return acc


def conv_fwd(name, x, w8):
    T, C = x.shape
    tc = min(CONV_TC, T)
    nt = T // tc

    def body(xp_ref, x_ref, w_ref, o_ref):
        prev = jnp.where(pl.program_id(0) == 0, 0.0, xp_ref[...])
        xe = jnp.concatenate([prev, x_ref[...]], axis=0)
        o_ref[...] = silu(_conv_taps(xe, w_ref)[HALO:])

    return pl.pallas_call(
        body, name=name, grid=(nt, C // CONV_TL),
        in_specs=[pl.BlockSpec((HALO, CONV_TL), lambda i, j: (jnp.maximum(i * (tc // HALO) - 1, 0), j)),
                  pl.BlockSpec((tc, CONV_TL), lambda i, j: (i, j)), pl.BlockSpec((HALO, CONV_TL), lambda i, j: (0, j))],
        out_specs=pl.BlockSpec((tc, CONV_TL), lambda i, j: (i, j)),
        out_shape=jax.ShapeDtypeStruct((T, C), F32),
        compiler_params=_params(dimension_semantics=("arbitrary", "arbitrary")),
    )(x, x, w8)


def conv_bwd(name, x, dy, w8):
    T, C = x.shape
    tc = min(CONV_TC, T)
    nt = T // tc
    n = tc + 2 * HALO

    def body(xp_ref, x_ref, xn_ref, dy_ref, dyn_ref, w_ref, dx_ref, dw_ref):
        i = pl.program_id(1)

        @pl.when(i == 0)
        def _():
            dw_ref[...] = jnp.zeros(dw_ref.shape, F32)
        prev = jnp.where(i == 0, 0.0, xp_ref[...])
        xe = jnp.concatenate([prev, x_ref[...], xn_ref[...]], axis=0)
        dye = jnp.concatenate([jnp.zeros((HALO, CONV_TL), F32), dy_ref[...], jnp.where(i == nt - 1, 0.0, dyn_ref[...])], axis=0)
        ce = _conv_taps(xe, w_ref)
        sg = sigmoid(ce)
        dce = dye * (sg * (1.0 + ce * (1.0 - sg)))
        dx = w_ref[pl.ds(CONV_K - 1, 1), :] * dce
        for k in range(CONV_K - 1):
            dx = dx + w_ref[pl.ds(k, 1), :] * pltpu.roll(dce, n - (CONV_K - 1 - k), 0)
        dx_ref[...] = dx[HALO:HALO + tc].astype(BF16)
        dcur = dce[HALO:HALO + tc]
        for k in range(CONV_K):
            xs = xe if k == CONV_K - 1 else pltpu.roll(xe, CONV_K - 1 - k, 0)
            dw_ref[pl.ds(k, 1), :] += jnp.sum(dcur * xs[HALO:HALO + tc], axis=0, keepdims=True)

    nh = T // HALO
    prev_map = lambda j, i: (jnp.maximum(i * (tc // HALO) - 1, 0), j)
    next_map = lambda j, i: (jnp.minimum((i + 1) * (tc // HALO), nh - 1), j)
    cur_map = lambda j, i: (i, j)
    return pl.pallas_call(
        body, name=name, grid=(C // CONV_TL, nt),
        in_specs=[pl.BlockSpec((HALO, CONV_TL), prev_map), pl.BlockSpec((tc, CONV_TL), cur_map),
                  pl.BlockSpec((HALO, CONV_TL), next_map), pl.BlockSpec((tc, CONV_TL), cur_map),
                  pl.BlockSpec((HALO, CONV_TL), next_map), pl.BlockSpec((HALO, CONV_TL), lambda j, i: (0, j))],
        out_specs=[pl.BlockSpec((tc, CONV_TL), cur_map), pl.BlockSpec((HALO, CONV_TL), lambda j, i: (0, j))],
        out_shape=[jax.ShapeDtypeStruct((T, C), BF16), jax.ShapeDtypeStruct((HALO, C), F32)],
        compiler_params=_params(dimension_semantics=("arbitrary", "arbitrary")),
    )(x, x, x, dy, dy, w8)


GD_VH = 16
GD_HB = 8


def _unit_lower_inverse(a, eye_f):
    n = -a
    t = eye_f + n
    p = n
    for _ in range(5):
        p = bmmf(p, p)
        t = t + bmmf(t, p)
    return t


@jax.custom_vjp
def unit_lower_solve2(a, r1, r2, eye_f):
    t = _unit_lower_inverse(a, eye_f)
    return bmmf(t, r1), bmmf(t, r2)


def _uls2_fwd(a, r1, r2, eye_f):
    t = _unit_lower_inverse(a, eye_f)
    x1, x2 = bmmf(t, r1), bmmf(t, r2)
    return (x1, x2), (t, x1, x2, eye_f)


def _uls2_bwd(res, g):
    t, x1, x2, eye_f = res
    high = lax.Precision.HIGH
    d1 = _bdot(t, g[0], 1, 1, high)
    d2 = _bdot(t, g[1], 1, 1, high)
    da = -(_bdot(d1, x1, 2, 2, high) + _bdot(d2, x2, 2, 2, high))
    return da, d1, d2, jnp.zeros_like(eye_f)


unit_lower_solve2.defvjp(_uls2_fwd, _uls2_bwd)


def gd_heads(al, dt, og, s, qc, kc, v, z, a_col, b_col, cst):
    lt_t, eye, incl, strict = cst
    eye_f = eye.astype(F32)
    nh = s.shape[0]
    q = qc * lax.rsqrt(jnp.sum(qc * qc, axis=-1, keepdims=True) + EPS) * (GD_DK ** -0.5)
    k = kc * lax.rsqrt(jnp.sum(kc * kc, axis=-1, keepdims=True) + EPS)
    beta = sigmoid(b_col)
    g = -jnp.exp(al) * softplus(a_col + dt)
    d_mat = jnp.broadcast_to(jnp.sum(g * lt_t, axis=1, keepdims=True), (nh, GD_CHUNK, GD_CHUNK))
    d = jnp.sum(jnp.where(eye, d_mat, 0.0), axis=-1, keepdims=True)
    dec = jnp.exp(jnp.where(incl, d - d_mat, NEG_INF))
    kb = k * beta
    u, w = unit_lower_solve2(bmm_nt(kb, k) * dec * strict, v * beta, kb * jnp.exp(d), eye_f)
    v_new = u - bmm(w, s)
    o = bmm(q * jnp.exp(d), s) + bmm(bmm_nt(q, k) * dec, v_new)
    dl = jnp.sum(g, axis=1, keepdims=True)
    s_new = s * jnp.exp(dl) + bmm_tn(k * jnp.exp(dl - d), v_new)
    on = o * lax.rsqrt(jnp.mean(o * o, axis=-1, keepdims=True) + EPS) * og
    return s_new, on * silu(z)


def _gd_consts():
    r, c = _iota((GD_CHUNK, GD_CHUNK), 0), _iota((GD_CHUNK, GD_CHUNK), 1)
    return (r <= c).astype(F32), r == c, r >= c, (r > c).astype(F32)


def _gd_load(qkv_ref, z_ref, ab_v, al_v, dt_v, heads):
    lane = _iota((1, 128), 1)
    hot_a = [(lane == h).astype(F32) for h in heads]
    hot_b = [(lane == GD_VH + h).astype(F32) for h in heads]
    col = lambda src, hot: jnp.stack([jnp.sum(src * m, axis=-1, keepdims=True) for m in hot])
    ops = (col(al_v, hot_a), col(dt_v, hot_a),
           jnp.stack([qkv_ref[:, pl.ds((h // 2) * GD_DK, GD_DK)] for h in heads]),
           jnp.stack([qkv_ref[:, pl.ds(GD_QKW + (h // 2) * GD_DK, GD_DK)] for h in heads]),
           jnp.stack([qkv_ref[:, pl.ds(2 * GD_QKW + h * GD_DK, GD_DK)] for h in heads]),
           jnp.stack([z_ref[:, pl.ds(h * GD_DK, GD_DK)] for h in heads]),
           col(ab_v, hot_a), col(ab_v, hot_b))
    return ops, hot_a, hot_b


def gd_fwd(name, qkv, z, ab, al, dt, og):
    T = qkv.shape[0]
    nc = T // GD_CHUNK

    def body(qkv_ref, z_ref, ab_ref, al_ref, dt_ref, og_ref, u_ref, st_ref, s_scr):
        @pl.when(pl.program_id(0) == 0)
        def _():
            s_scr[...] = jnp.zeros(s_scr.shape, F32)
        st_ref[0] = s_scr[...]
        cst = _gd_consts()
        ab_v, al_v, dt_v, og_v = ab_ref[...], al_ref[...], dt_ref[...], og_ref[...]
        for b0 in range(0, GD_VH, GD_HB):
            heads = list(range(b0, b0 + GD_HB))
            (alh, dth, q, k, v, zz, a_col, b_col), _, _ = _gd_load(qkv_ref, z_ref, ab_v, al_v, dt_v, heads)
            s_new, u = gd_heads(alh, dth, og_v, s_scr[pl.ds(b0, GD_HB)], q, k, v, zz, a_col, b_col, cst)
            s_scr[pl.ds(b0, GD_HB)] = s_new
            for i, h in enumerate(heads):
                u_ref[:, pl.ds(h * GD_DK, GD_DK)] = u[i].astype(BF16)

    return pl.pallas_call(
        body, name=name, grid=(nc,),
        in_specs=[_row_spec(GD_CHUNK, GD_QKV), _row_spec(GD_CHUNK, GD_VW), _row_spec(GD_CHUNK, 128),
                  _const_spec((1, 128)), _const_spec((1, 128)), _const_spec((1, 128))],
        out_specs=[_row_spec(GD_CHUNK, GD_VW), pl.BlockSpec((1, GD_VH, GD_DK, GD_DK), lambda i: (i, 0, 0, 0))],
        out_shape=[jax.ShapeDtypeStruct((T, GD_VW), BF16), jax.ShapeDtypeStruct((nc, GD_VH, GD_DK, GD_DK), F32)],
        scratch_shapes=[pltpu.VMEM((GD_VH, GD_DK, GD_DK), F32)],
        compiler_params=_params(dimension_semantics=("arbitrary",)),
    )(qkv, z, ab, al, dt, og)


def gd_bwd(name, qkv, z, ab, al, dt, og, states, du):
    T = qkv.shape[0]
    nc = T // GD_CHUNK

    def body(qkv_ref, z_ref, ab_ref, al_ref, dt_ref, og_ref, st_ref, du_ref,
             dqkv_ref, dz_ref, dab_ref, dal_ref, ddt_ref, dog_ref, ds_scr):
        @pl.when(pl.program_id(0) == 0)
        def _():
            ds_scr[...] = jnp.zeros(ds_scr.shape, F32)
            dal_ref[...] = jnp.zeros(dal_ref.shape, F32)
            ddt_ref[...] = jnp.zeros(ddt_ref.shape, F32)
            dog_ref[...] = jnp.zeros(dog_ref.shape, F32)
        cst = _gd_consts()
        ab_v, al_v, dt_v, og_v = ab_ref[...], al_ref[...], dt_ref[...], og_ref[...]
        dab = jnp.zeros((GD_CHUNK, 128), F32)
        dal_row = jnp.zeros((1, 128), F32)
        ddt_row = jnp.zeros((1, 128), F32)
        dog_row = jnp.zeros((1, 128), F32)
        for b0 in range(0, GD_VH, GD_HB):
            heads = list(range(b0, b0 + GD_HB))
            (alh, dth, q, k, v, zz, a_col, b_col), hot_a, hot_b = _gd_load(qkv_ref, z_ref, ab_v, al_v, dt_v, heads)
            fn = lambda *a: gd_heads(*a, cst)
            _, f = jax.vjp(fn, alh, dth, og_v, st_ref[0, pl.ds(b0, GD_HB)], q, k, v, zz, a_col, b_col)
            du_h = jnp.stack([du_ref[:, pl.ds(h * GD_DK, GD_DK)].astype(F32) for h in heads])
            dalh, ddth, dog, ds, dq, dk, dv, dz, da_col, db_col = f((ds_scr[pl.ds(b0, GD_HB)], du_h))
            ds_scr[pl.ds(b0, GD_HB)] = ds
            dog_row = dog_row + dog
            for i, h in enumerate(heads):
                dqkv_ref[:, pl.ds(2 * GD_QKW + h * GD_DK, GD_DK)] = dv[i]
                dz_ref[:, pl.ds(h * GD_DK, GD_DK)] = dz[i].astype(BF16)
                dab = dab + da_col[i] * hot_a[i] + db_col[i] * hot_b[i]
                dal_row = dal_row + dalh[i] * hot_a[i]
                ddt_row = ddt_row + ddth[i] * hot_a[i]
                if h % 2 == 0:
                    dqkv_ref[:, pl.ds((h // 2) * GD_DK, GD_DK)] = dq[i] + dq[i + 1]
                    dqkv_ref[:, pl.ds(GD_QKW + (h // 2) * GD_DK, GD_DK)] = dk[i] + dk[i + 1]
        dab_ref[...] = dab
        dal_ref[...] += dal_row
        ddt_ref[...] += ddt_row
        dog_ref[...] += dog_row

    rev = lambda i: (nc - 1 - i, 0)
    return pl.pallas_call(
        body, name=name, grid=(nc,),
        in_specs=[pl.BlockSpec((GD_CHUNK, GD_QKV), rev), pl.BlockSpec((GD_CHUNK, GD_VW), rev), pl.BlockSpec((GD_CHUNK, 128), rev),
                  _const_spec((1, 128)), _const_spec((1, 128)), _const_spec((1, 128)),
                  pl.BlockSpec((1, GD_VH, GD_DK, GD_DK), lambda i: (nc - 1 - i, 0, 0, 0)),
                  pl.BlockSpec((GD_CHUNK, GD_VW), rev)],
        out_specs=[pl.BlockSpec((GD_CHUNK, GD_QKV), rev), pl.BlockSpec((GD_CHUNK, GD_VW), rev), pl.BlockSpec((GD_CHUNK, 128), rev),
                   _const_spec((1, 128)), _const_spec((1, 128)), _const_spec((1, 128))],
        out_shape=[jax.ShapeDtypeStruct((T, GD_QKV), F32), jax.ShapeDtypeStruct((T, GD_VW), BF16),
                   jax.ShapeDtypeStruct((T, 128), F32)] + [jax.ShapeDtypeStruct((1, 128), F32)] * 3,
        scratch_shapes=[pltpu.VMEM((GD_VH, GD_DK, GD_DK), F32)],
        compiler_params=_params(dimension_semantics=("arbitrary",)),
    )(qkv, z, ab, al, dt, og, states, du)


def _my_pos():
    return lax.axis_index("x"), lax.axis_index("y"), lax.axis_index("c")


def _peers(pos):
    out = []
    for k in range(1, N_DEV):
        dev = tuple(1 - p if (k >> s) & 1 else p for p, s in zip(pos, (2, 1, 0)))
        out.append((dev, 4 * dev[0] + 2 * dev[1] + dev[2]))
    return out


_ANY = pl.BlockSpec(memory_space=pl.ANY)
_COMM_SCRATCH = [pltpu.SemaphoreType.DMA((N_DEV - 1,)), pltpu.SemaphoreType.DMA((N_DEV - 1,)), pltpu.SemaphoreType.DMA(())]


def all_gather(name, x):
    def body(x_ref, o_ref, send_sems, recv_sems, local_sem):
        pos = _my_pos()
        me = 4 * pos[0] + 2 * pos[1] + pos[2]
        peers = _peers(pos)
        mine = pltpu.make_async_copy(x_ref, o_ref.at[me], local_sem)
        mine.start()
        sends = [pltpu.make_async_remote_copy(x_ref, o_ref.at[me], send_sems.at[k], recv_sems.at[k],
                                              device_id=dev, device_id_type=MESH) for k, (dev, _) in enumerate(peers)]
        for cp in sends:
            cp.start()
        for k, (dev, pid) in enumerate(peers):
            pltpu.make_async_remote_copy(x_ref, o_ref.at[pid], send_sems.at[k], recv_sems.at[k],
                                         device_id=dev, device_id_type=MESH).wait_recv()
        for cp in sends:
            cp.wait_send()
        mine.wait()

    return pl.pallas_call(
        body, name=name, out_shape=jax.ShapeDtypeStruct((N_DEV,) + x.shape, x.dtype),
        in_specs=[_ANY], out_specs=_ANY, scratch_shapes=_COMM_SCRATCH,
    )(x)


def all_to_all(name, parts):
    def body(x_ref, o_ref, send_sems, recv_sems, local_sem):
        pos = _my_pos()
        me = 4 * pos[0] + 2 * pos[1] + pos[2]
        peers = _peers(pos)
        mine = pltpu.make_async_copy(x_ref.at[me], o_ref.at[me], local_sem)
        mine.start()
        sends = [pltpu.make_async_remote_copy(x_ref.at[pid], o_ref.at[me], send_sems.at[k], recv_sems.at[k],
                                              device_id=dev, device_id_type=MESH) for k, (dev, pid) in enumerate(peers)]
        for cp in sends:
            cp.start()
        for k, (dev, pid) in enumerate(peers):
            pltpu.make_async_remote_copy(x_ref.at[pid], o_ref.at[pid], send_sems.at[k], recv_sems.at[k],
                                         device_id=dev, device_id_type=MESH).wait_recv()
        for cp in sends:
            cp.wait_send()
        mine.wait()

    return pl.pallas_call(
        body, name=name, out_shape=jax.ShapeDtypeStruct(parts.shape, parts.dtype),
        in_specs=[_ANY], out_specs=_ANY, scratch_shapes=_COMM_SCRATCH,
    )(parts)


def lb_rows(r0, r1, r2, r3):
    mx = jnp.maximum(jnp.maximum(r0, r1), jnp.maximum(r2, r3))
    e = [jnp.exp(r - mx) for r in (r0, r1, r2, r3)]
    inv = 1.0 / (e[0] + e[1] + e[2] + e[3])
    c0 = e[0] * inv
    c1 = c0 + e[1] * inv
    c2 = c1 + e[2] * inv
    c3 = c2 + e[3] * inv
    return c0 - c0, c1 - c0, c2 - c0, c3 - c0


def mod_partial(name, c_all, ada_w):
    n, _, w = ada_w.shape

    def body(c_ref, w_ref, o_ref):
        o_ref[0] = mm(c_ref[...], w_ref[0])

    return pl.pallas_call(
        body, name=name, grid=(n,),
        in_specs=[_const_spec((N_DEV, D)), pl.BlockSpec((1, D, w), lambda i: (i, 0, 0))],
        out_specs=pl.BlockSpec((1, N_DEV, w), lambda i: (i, 0, 0)),
        out_shape=jax.ShapeDtypeStruct((n, N_DEV, w), F32),
    )(c_all, ada_w)


def prep(name, modp, ada_b, hgrn_lb):
    def body(mp_ref, b_ref, lb_ref, mod_ref, lbo_ref):
        mod_ref[...] = mp_ref[...] + b_ref[...]
        out = lb_rows(*[lb_ref[pl.ds(i, 1), :] for i in range(4)])
        for i in range(4):
            lbo_ref[pl.ds(i, 1), :] = out[i]

    return pl.pallas_call(
        body, name=name,
        out_shape=[jax.ShapeDtypeStruct(modp.shape, F32), jax.ShapeDtypeStruct(hgrn_lb.shape, F32)],
    )(modp, ada_b, hgrn_lb)


def lb_grad(name, hgrn_lb, dlb_parts):
    def body(lb_ref, d_ref, o_ref):
        cts = []
        for i in range(4):
            acc = d_ref[0, pl.ds(i, 1), :]
            for p in range(1, N_DEV):
                acc = acc + d_ref[p, pl.ds(i, 1), :]
            cts.append(acc)
        _, f = jax.vjp(lb_rows, *[lb_ref[pl.ds(i, 1), :] for i in range(4)])
        for i, g in enumerate(f(tuple(cts))):
            o_ref[pl.ds(i, 1), :] = g

    return pl.pallas_call(body, name=name, out_shape=jax.ShapeDtypeStruct(hgrn_lb.shape, F32))(hgrn_lb, dlb_parts)


def ada_grad(name, c_t, dm):
    n, _, w = dm.shape

    def body(c_ref, d_ref, o_ref):
        acc = c_ref[:, pl.ds(0, 1)] * d_ref[0, pl.ds(0, 1), :]
        for b in range(1, N_DEV):
            acc = acc + c_ref[:, pl.ds(b, 1)] * d_ref[0, pl.ds(b, 1), :]
        o_ref[0] = acc

    return pl.pallas_call(
        body, name=name, grid=(n,),
        in_specs=[_const_spec((D, N_DEV)), pl.BlockSpec((1, N_DEV, w), lambda i: (i, 0, 0))],
        out_specs=pl.BlockSpec((1, D, w), lambda i: (i, 0, 0)),
        out_shape=jax.ShapeDtypeStruct((n, D, w), F32),
    )(c_t, dm)


ADAMW_ROWS = 256


def adamw(name, w, m, v, gparts):
    shape = w.shape
    L = shape[-1]
    R = 1
    for s in shape[:-1]:
        R *= s
    P = gparts.shape[0]
    tr = min(R, ADAMW_ROWS)

    def body(w_ref, m_ref, v_ref, g_ref, go_ref, d_ref, mo_ref, vo_ref):
        g = g_ref[0].astype(F32)
        for p in range(1, P):
            g = g + g_ref[p].astype(F32)
        mn = ADAM_B1 * m_ref[...] + (1.0 - ADAM_B1) * g
        vn = ADAM_B2 * v_ref[...] + (1.0 - ADAM_B2) * (g * g)
        m_hat = mn / (1.0 - ADAM_B1 ** ADAM_STEP)
        v_hat = vn / (1.0 - ADAM_B2 ** ADAM_STEP)
        go_ref[...] = g
        d_ref[...] = -ADAM_LR * (m_hat / (jnp.sqrt(v_hat) + ADAM_EPS) + ADAM_WD * w_ref[...])
        mo_ref[...] = mn
        vo_ref[...] = vn

    spec = pl.BlockSpec((tr, L), lambda i: (i, 0))
    outs = pl.pallas_call(
        body, name=name, grid=(R // tr,),
        in_specs=[spec, spec, spec, pl.BlockSpec((P, tr, L), lambda i: (0, i, 0))],
        out_specs=[spec] * 4, out_shape=[jax.ShapeDtypeStruct((R, L), F32)] * 4,
        compiler_params=_params(dimension_semantics=("arbitrary",)),
    )(w.reshape(R, L), m.reshape(R, L), v.reshape(R, L), gparts.reshape(P, R, L))
    return [o.reshape(shape) for o in outs]


def _gather_cols(name, w):
    g = all_gather(name, w.astype(BF16))
    n, k, w8 = w.shape
    return jnp.transpose(g, (1, 2, 0, 3)).reshape(n, k, N_DEV * w8)


def _gather_rows(name, w):
    g = all_gather(name, w.astype(BF16))
    n, k8, nn = w.shape
    return jnp.transpose(g, (1, 0, 2, 3)).reshape(n, N_DEV * k8, nn)


def _scatter_cols(dw, wire=BF16):
    n, k, nn = dw.shape
    return jnp.transpose(dw.reshape(n, k, N_DEV, nn // N_DEV), (2, 0, 1, 3)).astype(wire)


def _scatter_rows(dw, wire=BF16):
    n, k, nn = dw.shape
    return jnp.transpose(dw.reshape(n, N_DEV, k // N_DEV, nn), (1, 0, 2, 3)).astype(wire)


def _pad_lanes(v, width=128):
    return jnp.pad(v, ((0, 0), (0, width - v.shape[1])))


def _stack_rows(parts):
    n, L = len(parts), parts[0].shape[1]
    r = lax.broadcasted_iota(jnp.int32, (n, L), 0)
    out = jnp.zeros((n, L), parts[0].dtype)
    for i, p in enumerate(parts):
        out = jnp.where(r == i, jnp.broadcast_to(p, (n, L)), out)
    return out


def kernel(x, c, positions, hgrn_lb, ada_w, ada_b, norm_g, hg_in_w, hg_out_w, hg_onorm, sw_in_w, sw_out_w, sw_qnorm, sw_knorm, sw_sinks, gd_in_w, gd_out_w, gd_conv_w, gd_a_log, gd_dt_bias, gd_onorm, loss_target, m_hgrn_lb, m_ada_w, m_ada_b, m_norm_g, m_hg_in_w, m_hg_out_w, m_hg_onorm, m_sw_in_w, m_sw_out_w, m_sw_qnorm, m_sw_knorm, m_sw_sinks, m_gd_in_w, m_gd_out_w, m_gd_conv_w, m_gd_a_log, m_gd_dt_bias, m_gd_onorm, v_hgrn_lb, v_ada_w, v_ada_b, v_norm_g, v_hg_in_w, v_hg_out_w, v_hg_onorm, v_sw_in_w, v_sw_out_w, v_sw_qnorm, v_sw_knorm, v_sw_sinks, v_gd_in_w, v_gd_out_w, v_gd_conv_w, v_gd_a_log, v_gd_dt_bias, v_gd_onorm):
    pos = _my_pos()
    me = 4 * pos[0] + 2 * pos[1] + pos[2]
    x0 = x[0]
    tgt = loss_target[0]
    n_layers = norm_g.shape[0]
    aw = ada_w.shape[2]

    c_all = all_gather("ag_c", c)[:, 0, :]
    modp = all_gather("ag_mod", mod_partial("mod_partial", c_all, ada_w))
    modp_mine = lax.dynamic_index_in_dim(modp, me, axis=2, keepdims=False)
    modp_mine = jnp.transpose(modp_mine, (1, 0, 2)).reshape(n_layers, N_DEV * aw)
    mod, lb_all = prep("prep", modp_mine, ada_b, hgrn_lb)
    shift, scale, gate = mod[:, :D], mod[:, D:2 * D], mod[:, 2 * D:]
    row = lambda a, i: a[i:i + 1]

    w_hg_in = _gather_cols("ag_hg_in", hg_in_w)
    w_hg_out = _gather_rows("ag_hg_out", hg_out_w)
    w_sw_in = _gather_cols("ag_sw_in", sw_in_w)[0]
    w_sw_out = _gather_rows("ag_sw_out", sw_out_w)[0]
    w_gd_in = _gather_cols("ag_gd_in", gd_in_w)[0]
    w_gd_out = _gather_rows("ag_gd_out", gd_out_w)[0]
    w_gd_qkv, w_gd_z = w_gd_in[:, :GD_QKV], w_gd_in[:, GD_QKV:GD_QKV + GD_VW]
    w_gd_ab = _pad_lanes(w_gd_in[:, GD_QKV + GD_VW:])
    conv_all = all_gather("ag_conv", gd_conv_w)
    conv_w = jnp.transpose(conv_all[:, 0], (1, 0, 2)).reshape(CONV_K, GD_QKV)
    conv_w8 = jnp.pad(conv_w, ((0, HALO - CONV_K), (0, 0)))
    gd_al, gd_dt = _pad_lanes(gd_a_log), _pad_lanes(gd_dt_bias)

    inv_freq = ROPE_THETA ** (-jnp.arange(0, SW_DH, 2, dtype=F32) / SW_DH)
    ang = positions[0].astype(F32)[:, None] * inv_freq
    cos, sin = jnp.cos(ang), jnp.sin(ang)
    cos64, sin64 = jnp.concatenate([cos, cos], axis=-1), jnp.concatenate([-sin, sin], axis=-1)

    xs, hs, ys, saved = [x0], [], [], []
    h = norm_mod("norm0", x0, row(norm_g, 0), row(scale, 0), row(shift, 0))
    for i in range(n_layers):
        kind, j = i % 3, i // 3
        hs.append(h)
        if kind == 0:
            p = matmul(f"hg_in{i}", h, w_hg_in[j])
            u, st = hg_fwd(f"hg_fwd{i}", p, row(lb_all, i), row(hg_onorm, j))
            y = matmul(f"hg_out{i}", u, w_hg_out[j])
            saved.append((p, u, st))
        elif kind == 1:
            p = matmul(f"sw_in{i}", h, w_sw_in)
            u, kpost = swa_fwd(f"sw_fwd{i}", p, cos64, sin64, row(sw_qnorm, j), row(sw_knorm, j), row(sw_sinks, j))
            y = matmul(f"sw_out{i}", u, w_sw_out)
            saved.append((p, u, kpost))
        else:
            xq = matmul(f"gd_qkv{i}", h, w_gd_qkv)
            zz = matmul(f"gd_z{i}", h, w_gd_z)
            ab = matmul(f"gd_ab{i}", h, w_gd_ab)
            cv = conv_fwd(f"gd_conv{i}", xq, conv_w8)
            u, st = gd_fwd(f"gd_fwd{i}", cv, zz, ab, gd_al, gd_dt, row(gd_onorm, j))
            y = matmul(f"gd_out{i}", u, w_gd_out)
            saved.append((xq, zz, ab, cv, u, st))
        ys.append(y)
        if i + 1 < n_layers:
            xn, h = resid_norm(f"resid{i}", xs[i], y, row(gate, i), row(norm_g, i + 1), row(scale, i + 1), row(shift, i + 1))
            xs.append(xn)

    last = n_layers - 1
    dx, dy, loss_acc, dgate_last = loss_head("loss", xs[last], ys[last], tgt, row(gate, last))
    loss = lax.psum(loss_acc[0, 0], ("x", "y", "c"))

    dgate = [None] * n_layers
    dgate[last] = dgate_last
    dg_norm, dscale, dshift = [None] * n_layers, [None] * n_layers, [None] * n_layers
    dlb = [jnp.zeros((1, D), F32)] * n_layers
    d_hg_in, d_hg_out, d_hg_on = [None] * 2, [None] * 2, [None] * 2
    for i in range(last, -1, -1):
        kind, j = i % 3, i // 3
        h = hs[i]
        if kind == 0:
            p, u, st = saved[i]
            du = matmul_nt(f"hg_du{i}", [(dy, w_hg_out[j])])
            d_hg_out[j] = matmul_tn(f"hg_dwo{i}", u, dy)
            dp, dlb_i, dog = hg_bwd(f"hg_bwd{i}", p, row(lb_all, i), row(hg_onorm, j), st, du)
            dlb[i] = dlb_i
            d_hg_on[j] = dog
            dh = matmul_nt(f"hg_dh{i}", [(dp, w_hg_in[j])])
            d_hg_in[j] = matmul_tn(f"hg_dwi{i}", h, dp)
        elif kind == 1:
            p, u, kpost = saved[i]
            du = matmul_nt(f"sw_du{i}", [(dy, w_sw_out)])
            d_sw_out = matmul_tn(f"sw_dwo{i}", u, dy)
            dp, d_qn, d_kn, d_sk = swa_bwd(f"sw_bwd{i}", p, kpost, cos64, sin64, row(sw_qnorm, j), row(sw_knorm, j),
                                           row(sw_sinks, j), du)
            dh = matmul_nt(f"sw_dh{i}", [(dp, w_sw_in)])
            d_sw_in = matmul_tn(f"sw_dwi{i}", h, dp)
        else:
            xq, zz, ab, cv, u, st = saved[i]
            du = matmul_nt(f"gd_du{i}", [(dy, w_gd_out)])
            d_gd_out = matmul_tn(f"gd_dwo{i}", u, dy)
            dcv, dz, dab, d_al, d_dt, d_gd_on = gd_bwd(f"gd_bwd{i}", cv, zz, ab, gd_al, gd_dt, row(gd_onorm, j), st, du)
            dxq, d_conv8 = conv_bwd(f"gd_dconv{i}", xq, dcv, conv_w8)
            dh = matmul_nt(f"gd_dh{i}", [(dxq, w_gd_qkv), (dz, w_gd_z), (dab, w_gd_ab)])
            d_gd_in = jnp.concatenate([matmul_tn(f"gd_dwq{i}", h, dxq), matmul_tn(f"gd_dwz{i}", h, dz),
                                       matmul_tn(f"gd_dwab{i}", h, dab)[:, :2 * GD_VH]], axis=1)
        if i > 0:
            dx, dy, dg_norm[i], dscale[i], dshift[i], dgate[i - 1] = bwd_rowwise(
                f"bwd_row{i}", xs[i], dh, dx, row(norm_g, i), row(scale, i), row(shift, i), ys[i - 1], row(gate, i - 1))
        else:
            dx, dg_norm[i], dscale[i], dshift[i] = bwd_rowwise(
                f"bwd_row{i}", xs[i], dh, dx, row(norm_g, i), row(scale, i), row(shift, i))
    grad_x = dx[None]

    dmod = jnp.concatenate([_stack_rows(dshift), _stack_rows(dscale), _stack_rows(dgate)], axis=1)
    misc = _stack_rows(d_hg_on + [d_gd_on, _pad_lanes(d_qn), _pad_lanes(d_kn), _pad_lanes(d_sk), d_al, d_dt])
    small = jnp.concatenate([_stack_rows(dlb).reshape(-1, 128), _stack_rows(dg_norm).reshape(-1, 128),
                             dmod.reshape(-1, 128), misc], axis=0)
    small_all = all_gather("ag_small", small)
    n_lb = n_layers * D // 128
    n_mod = n_layers * 3 * D // 128
    o = 0
    dlb_parts = small_all[:, o:o + n_lb].reshape(N_DEV, n_layers, D); o += n_lb
    dgn_parts = small_all[:, o:o + n_lb].reshape(N_DEV, n_layers, D); o += n_lb
    dmod_parts = small_all[:, o:o + n_mod].reshape(N_DEV, n_layers, 3 * D); o += n_mod
    dhgon_parts = small_all[:, o:o + 2]; o += 2
    dgdon_parts = small_all[:, o:o + 1]; o += 1
    dqn_parts = small_all[:, o:o + 1, :SW_DH]; o += 1
    dkn_parts = small_all[:, o:o + 1, :SW_DH]; o += 1
    dsk_parts = small_all[:, o:o + 1, :SW_KV * SW_G]; o += 1
    dal_parts = small_all[:, o:o + 1, :GD_VH]; o += 1
    ddt_parts = small_all[:, o:o + 1, :GD_VH]; o += 1

    g_lb = lb_grad("lb_grad", hgrn_lb, dlb_parts)
    dm_mine = lax.dynamic_slice_in_dim(dmod_parts, me * aw, aw, axis=2)
    g_ada_w = ada_grad("ada_grad", jnp.transpose(c_all), jnp.transpose(dm_mine, (1, 0, 2)))

    res = {}
    res["hgrn_lb"] = adamw("aw_hgrn_lb", hgrn_lb, m_hgrn_lb, v_hgrn_lb, g_lb[None])
    res["ada_w"] = adamw("aw_ada_w", ada_w, m_ada_w, v_ada_w, g_ada_w[None])
    res["ada_b"] = adamw("aw_ada_b", ada_b, m_ada_b, v_ada_b, dmod_parts)
    res["norm_g"] = adamw("aw_norm_g", norm_g, m_norm_g, v_norm_g, dgn_parts)
    res["hg_in_w"] = adamw("aw_hg_in", hg_in_w, m_hg_in_w, v_hg_in_w,
                           all_to_all("rs_hg_in", _scatter_cols(jnp.stack(d_hg_in))))
    res["hg_out_w"] = adamw("aw_hg_out", hg_out_w, m_hg_out_w, v_hg_out_w,
                            all_to_all("rs_hg_out", _scatter_rows(jnp.stack(d_hg_out))))
    res["hg_onorm"] = adamw("aw_hg_onorm", hg_onorm, m_hg_onorm, v_hg_onorm, dhgon_parts)
    res["sw_in_w"] = adamw("aw_sw_in", sw_in_w, m_sw_in_w, v_sw_in_w, all_to_all("rs_sw_in", _scatter_cols(d_sw_in[None])))
    res["sw_out_w"] = adamw("aw_sw_out", sw_out_w, m_sw_out_w, v_sw_out_w, all_to_all("rs_sw_out", _scatter_rows(d_sw_out[None])))
    res["sw_qnorm"] = adamw("aw_sw_qn", sw_qnorm, m_sw_qnorm, v_sw_qnorm, dqn_parts)
    res["sw_knorm"] = adamw("aw_sw_kn", sw_knorm, m_sw_knorm, v_sw_knorm, dkn_parts)
    res["sw_sinks"] = adamw("aw_sw_sinks", sw_sinks, m_sw_sinks, v_sw_sinks, dsk_parts)
    res["gd_in_w"] = adamw("aw_gd_in", gd_in_w, m_gd_in_w, v_gd_in_w, all_to_all("rs_gd_in", _scatter_cols(d_gd_in[None])))
    res["gd_out_w"] = adamw("aw_gd_out", gd_out_w, m_gd_out_w, v_gd_out_w, all_to_all("rs_gd_out", _scatter_rows(d_gd_out[None])))
    res["gd_conv_w"] = adamw("aw_gd_conv", gd_conv_w, m_gd_conv_w, v_gd_conv_w,
                             all_to_all("rs_gd_conv", _scatter_cols(d_conv8[None, :CONV_K], F32)))
    res["gd_a_log"] = adamw("aw_gd_alog", gd_a_log, m_gd_a_log, v_gd_a_log, dal_parts)
    res["gd_dt_bias"] = adamw("aw_gd_dt", gd_dt_bias, m_gd_dt_bias, v_gd_dt_bias, ddt_parts)
    res["gd_onorm"] = adamw("aw_gd_onorm", gd_onorm, m_gd_onorm, v_gd_onorm, dgdon_parts)

    order = ["hgrn_lb", "ada_w", "ada_b", "norm_g", "hg_in_w", "hg_out_w", "hg_onorm", "sw_in_w", "sw_out_w", "sw_qnorm",
             "sw_knorm", "sw_sinks", "gd_in_w", "gd_out_w", "gd_conv_w", "gd_a_log", "gd_dt_bias", "gd_onorm"]
    outs = [loss, grad_x]
    for part in range(4):
        outs += [res[n][part] for n in order]
    return tuple(outs)
```

```python
import functools

import jax
import jax.numpy as jnp
from jax import lax
from jax.experimental import pallas as pl
from jax.experimental.pallas import tpu as pltpu

F32, BF16 = jnp.float32, jnp.bfloat16
HI = lax.Precision.HIGHEST
MESH = pl.DeviceIdType.MESH
N_DEV = 8
D = 1024
EPS = 1e-6
NEG_INF = float("-inf")

HG_HEADS, HG_DH, HG_BLK, HG_CH, HG_SUB, HG_HB = 8, 128, 128, 32, 8, 8
SW_KV, SW_G, SW_DH, SW_BLK = 4, 4, 64, 128
SW_QW, SW_KVW = 1024, 256
GD_QK_HEADS, GD_DK, GD_CHUNK = 8, 128, 64
GD_QKW, GD_VW, GD_QKV = 1024, 2048, 4096
CONV_K = 4
ROPE_THETA = 10000.0

ADAM_LR, ADAM_B1, ADAM_B2, ADAM_EPS, ADAM_WD, ADAM_STEP = 0.001, 0.9, 0.999, 1e-08, 0.01, 10

VMEM_LIMIT = 56 * 1024 * 1024


def _params(**kw):
    return pltpu.CompilerParams(vmem_limit_bytes=VMEM_LIMIT, **kw)


def _dot(a, b, ca, cb):
    return lax.dot_general(a.astype(BF16), b.astype(BF16), (((ca,), (cb,)), ((), ())), preferred_element_type=F32)


def mm(a, b):
    return _dot(a, b, 1, 0)


def mm_nt(a, b):
    return _dot(a, b, 1, 1)


def mm_tn(a, b):
    return _dot(a, b, 0, 0)


def mmf(a, b):
    return lax.dot_general(a, b, (((1,), (0,)), ((), ())), precision=HI, preferred_element_type=F32)


def _split3(x):
    h1 = x.astype(BF16)
    r1 = x - h1.astype(F32)
    h2 = r1.astype(BF16)
    h3 = (r1 - h2.astype(F32)).astype(BF16)
    return h1, h2, h3


def _lin01_l(m, x):
    mb = m.astype(BF16)
    return sum(lax.dot_general(mb, p, (((1,), (0,)), ((), ())), preferred_element_type=F32) for p in _split3(x))


def _lin01_r(x, m):
    mb = m.astype(BF16)
    return sum(lax.dot_general(p, mb, (((1,), (0,)), ((), ())), preferred_element_type=F32) for p in _split3(x))


def _bdot(a, b, ca, cb, prec=None):
    return lax.dot_general(a, b, (((ca,), (cb,)), ((0,), (0,))), precision=prec, preferred_element_type=F32)


def bmm(a, b):
    return _bdot(a.astype(BF16), b.astype(BF16), 2, 1)


def bmm_nt(a, b):
    return _bdot(a.astype(BF16), b.astype(BF16), 2, 2)


def bmm_tn(a, b):
    return _bdot(a.astype(BF16), b.astype(BF16), 1, 1)


def bmmf(a, b):
    return _bdot(a, b, 2, 1, lax.Precision.HIGH)


def _blin01(m, x):
    mb = m.astype(BF16)
    return sum(_bdot(mb, p, 2, 1) for p in _split3(x))


@jax.custom_vjp
def blin01(m, mt, x):
    return _blin01(m, x)


def _blin01_fwd(m, mt, x):
    return _blin01(m, x), (m, mt)


def _blin01_bwd(res, g):
    m, mt = res
    return jnp.zeros_like(m), jnp.zeros_like(mt), _blin01(mt, g)


blin01.defvjp(_blin01_fwd, _blin01_bwd)


def _blin01_r(x, m):
    mb = m.astype(BF16)
    return sum(_bdot(p, mb, 2, 1) for p in _split3(x))


@jax.custom_vjp
def blin01_r(x, m, mt):
    return _blin01_r(x, m)


def _blin01_r_fwd(x, m, mt):
    return _blin01_r(x, m), (m, mt)


def _blin01_r_bwd(res, g):
    m, mt = res
    return _blin01_r(g, mt), jnp.zeros_like(m), jnp.zeros_like(mt)


blin01_r.defvjp(_blin01_r_fwd, _blin01_r_bwd)


@jax.custom_vjp
def lin01_l(m, mt, x):
    return _lin01_l(m, x)


def _lin01_l_fwd(m, mt, x):
    return _lin01_l(m, x), (m, mt)


def _lin01_l_bwd(res, g):
    m, mt = res
    return jnp.zeros_like(m), jnp.zeros_like(mt), _lin01_l(mt, g)


lin01_l.defvjp(_lin01_l_fwd, _lin01_l_bwd)


@jax.custom_vjp
def lin01_r(x, m, mt):
    return _lin01_r(x, m)


def _lin01_r_fwd(x, m, mt):
    return _lin01_r(x, m), (m, mt)


def _lin01_r_bwd(res, g):
    m, mt = res
    return _lin01_r(g, mt), jnp.zeros_like(m), jnp.zeros_like(mt)


lin01_r.defvjp(_lin01_r_fwd, _lin01_r_bwd)


@functools.partial(jax.custom_vjp, nondiff_argnums=(1, 2))
def rows(x, start, size):
    return lax.slice_in_dim(x, start, start + size, axis=x.ndim - 2)


def _rows_fwd(x, start, size):
    return lax.slice_in_dim(x, start, start + size, axis=x.ndim - 2), jnp.zeros(x.shape[:-1] + (1,), F32)


def _rows_bwd(start, size, res, g):
    ax = g.ndim - 2
    total = res.shape[ax]
    zeros = lambda n: jnp.zeros(g.shape[:ax] + (n,) + g.shape[ax + 1:], g.dtype)
    parts = []
    if start > 0:
        parts.append(zeros(start))
    parts.append(g)
    if total - start - size > 0:
        parts.append(zeros(total - start - size))
    return (jnp.concatenate(parts, axis=ax) if len(parts) > 1 else g,)


rows.defvjp(_rows_fwd, _rows_bwd)


@functools.partial(jax.custom_vjp, nondiff_argnums=(1,))
def rowsel(x, s):
    return lax.slice_in_dim(x, s, s + 1, axis=x.ndim - 2)


def _rowsel_fwd(x, s):
    return lax.slice_in_dim(x, s, s + 1, axis=x.ndim - 2), jnp.zeros(x.shape[:-1] + (1,), F32)


def _rowsel_bwd(s, res, g):
    r = lax.broadcasted_iota(jnp.int32, res.shape, res.ndim - 2)
    return (jnp.where(r == s, g, 0.0),)


rowsel.defvjp(_rowsel_fwd, _rowsel_bwd)


def sigmoid(x):
    return jax.nn.sigmoid(x)


def silu(x):
    return x * jax.nn.sigmoid(x)


def softplus(x):
    return jnp.maximum(x, 0.0) + jnp.log1p(jnp.exp(-jnp.abs(x)))


def _iota(shape, dim):
    return lax.broadcasted_iota(jnp.int32, shape, dim)


def norm_mod_fn(g, sc, sh, x):
    r = lax.rsqrt(jnp.mean(x * x, axis=-1, keepdims=True) + EPS)
    return (x * r * g) * (1.0 + sc) + sh


def _row_spec(tm, f):
    return pl.BlockSpec((tm, f), lambda i: (i, 0))


def _const_spec(shape):
    nd = len(shape)
    return pl.BlockSpec(shape, lambda i: (0,) * nd)


def _tiled(name, fn, consts, rows_in, row_outs, acc_outs=(), tm=512):
    T = rows_in[0].shape[0]
    tm = min(tm, T)
    n_c, n_r, n_ro = len(consts), len(rows_in), len(row_outs)

    def body(*refs):
        c_refs, r_refs = refs[:n_c], refs[n_c:n_c + n_r]
        ro_refs, ao_refs = refs[n_c + n_r:n_c + n_r + n_ro], refs[n_c + n_r + n_ro:]
        outs = fn(*[r[...] for r in c_refs], *[r[...] for r in r_refs])
        for r, v in zip(ro_refs, outs[:n_ro]):
            r[...] = v.astype(r.dtype)
        if ao_refs:
            @pl.when(pl.program_id(0) == 0)
            def _():
                for r in ao_refs:
                    r[...] = jnp.zeros(r.shape, r.dtype)
            for r, v in zip(ao_refs, outs[n_ro:]):
                r[...] += v

    out_shape = [jax.ShapeDtypeStruct((T, f), dt) for f, dt in row_outs] + [jax.ShapeDtypeStruct(s, F32) for s in acc_outs]
    out_specs = [_row_spec(tm, f) for f, _ in row_outs] + [_const_spec(s) for s in acc_outs]
    in_specs = [_const_spec(c.shape) for c in consts] + [_row_spec(tm, r.shape[1]) for r in rows_in]
    return pl.pallas_call(
        body, name=name, grid=(T // tm,), in_specs=in_specs, out_specs=out_specs, out_shape=out_shape,
        compiler_params=_params(dimension_semantics=("arbitrary",)),
    )(*consts, *rows_in)


def norm_mod(name, x, g, sc, sh):
    (h,) = _tiled(name, lambda g, sc, sh, x: (norm_mod_fn(g, sc, sh, x),), [g, sc, sh], [x], [(D, BF16)])
    return h


def resid_norm(name, x, y, gate, g, sc, sh):
    def fn(gate, g, sc, sh, x, y):
        xn = x + gate * y
        return xn, norm_mod_fn(g, sc, sh, xn)
    return _tiled(name, fn, [gate, g, sc, sh], [x, y], [(D, F32), (D, BF16)])


def loss_head(name, x, y, tgt, gate):
    def fn(gate, x, y, tgt):
        err = x + gate * y - tgt
        dx = err * (1.0 / D)
        per_tok = jnp.sum(err * err, axis=-1, keepdims=True) * (0.5 / D)
        loss = jnp.sum(per_tok, axis=0, keepdims=True)
        return dx, gate * dx, jnp.broadcast_to(loss, (1, 128)), jnp.sum(dx * y, axis=0, keepdims=True)
    return _tiled(name, fn, [gate], [x, y, tgt], [(D, F32), (D, BF16)], [(1, 128), (1, D)])


def bwd_rowwise(name, x, dh, dxn, g, sc, sh, y_prev=None, gate_prev=None):
    with_prev = y_prev is not None

    def fn(*a):
        if with_prev:
            gate_p, g, sc, sh, x, dh, dxn, yp = a
        else:
            g, sc, sh, x, dh, dxn = a
        _, f = jax.vjp(norm_mod_fn, g, sc, sh, x)
        dg, dsc, dsh, dx = f(dh)
        dx = dx + dxn
        if with_prev:
            return dx, gate_p * dx, dg, dsc, dsh, jnp.sum(dx * yp, axis=0, keepdims=True)
        return dx, dg, dsc, dsh

    if with_prev:
        return _tiled(name, fn, [gate_prev, g, sc, sh], [x, dh, dxn, y_prev], [(D, F32), (D, BF16)], [(1, D)] * 4)
    return _tiled(name, fn, [g, sc, sh], [x, dh, dxn], [(D, F32)], [(1, D)] * 3)


MM_ROWS_WIDE, MM_ROWS_NARROW, MM_WIDE = 256, 512, 2048


def matmul(name, a, w, out_dtype=F32):
    T, K = a.shape
    N = w.shape[1]
    tm = min(MM_ROWS_NARROW if N <= MM_WIDE else MM_ROWS_WIDE, T)

    def body(a_ref, w_ref, o_ref):
        o_ref[...] = mm(a_ref[...], w_ref[...]).astype(o_ref.dtype)

    return pl.pallas_call(
        body, name=name, grid=(T // tm,),
        in_specs=[_row_spec(tm, K), _const_spec((K, N))], out_specs=_row_spec(tm, N),
        out_shape=jax.ShapeDtypeStruct((T, N), out_dtype),
        compiler_params=_params(dimension_semantics=("arbitrary",)),
    )(a, w)


def matmul_nt(name, pairs, out_dtype=F32):
    T = pairs[0][0].shape[0]
    K = pairs[0][1].shape[0]
    wide = sum(a.shape[1] for a, _ in pairs) > MM_WIDE
    tm = min(MM_ROWS_WIDE if wide else MM_ROWS_NARROW, T)
    n = len(pairs)

    def body(*refs):
        o_ref = refs[-1]
        acc = None
        for i in range(n):
            p = mm_nt(refs[2 * i][...], refs[2 * i + 1][...])
            acc = p if acc is None else acc + p
        o_ref[...] = acc.astype(o_ref.dtype)

    in_specs, args = [], []
    for a, w in pairs:
        in_specs += [_row_spec(tm, a.shape[1]), _const_spec(w.shape)]
        args += [a, w]
    return pl.pallas_call(
        body, name=name, grid=(T // tm,), in_specs=in_specs, out_specs=_row_spec(tm, K),
        out_shape=jax.ShapeDtypeStruct((T, K), out_dtype),
        compiler_params=_params(dimension_semantics=("arbitrary",)),
    )(*args)


def matmul_tn(name, a, b, tm=512, tn=1024):
    T, K = a.shape
    N = b.shape[1]
    tm = min(tm, T)
    tn = max(t for t in range(128, min(tn, N) + 1, 128) if N % t == 0)

    def body(a_ref, b_ref, o_ref):
        @pl.when(pl.program_id(1) == 0)
        def _():
            o_ref[...] = jnp.zeros(o_ref.shape, F32)
        o_ref[...] += mm_tn(a_ref[...], b_ref[...])

    return pl.pallas_call(
        body, name=name, grid=(N // tn, T // tm),
        in_specs=[pl.BlockSpec((tm, K), lambda j, i: (i, 0)), pl.BlockSpec((tm, tn), lambda j, i: (i, j))],
        out_specs=pl.BlockSpec((K, tn), lambda j, i: (0, j)),
        out_shape=jax.ShapeDtypeStruct((K, N), F32),
        compiler_params=_params(dimension_semantics=("arbitrary", "arbitrary")),
    )(a, b)


def hg_heads(tri, trit, lb, og, st, qp, fp, iv, z):
    blk = qp.shape[1]
    q = silu(qp)
    lf = jnp.log(lb + (1.0 - lb) * sigmoid(fp))
    k = (1.0 - lb) * sigmoid(-fp)
    r = _iota((1, HG_SUB, 1), 1)
    rc = _iota((1, HG_CH, 1), 1)
    outs = []
    for c in range(blk // HG_CH):
        qc, kc, vc, lfc = (rows(a, c * HG_CH, HG_CH) for a in (q, k, iv, lf))
        b = blin01(tri, trit, lfc)
        o_state = bmm_nt(qc * jnp.exp(b), st)
        parts = []
        for i in range(HG_CH // HG_SUB):
            qi, ki, vi, bi = (rows(a, i * HG_SUB, HG_SUB) for a in (qc, kc, vc, b))
            o = rows(o_state, i * HG_SUB, HG_SUB)
            for s in range(HG_SUB):
                ks, bs, vs = rowsel(ki, s), rowsel(bi, s), rowsel(vi, s)
                e = jnp.exp(jnp.where(r >= s, bi - bs, NEG_INF))
                o = o + jnp.sum(qi * ks * e, axis=-1, keepdims=True) * vs
            if i > 0:
                rb = rowsel(b, i * HG_SUB - 1)
                kt = kc * jnp.exp(jnp.where(rc < i * HG_SUB, rb - b, NEG_INF))
                o = o + bmm(bmm_nt(qi * jnp.exp(bi - rb), kt), vc)
            parts.append(o)
        bl = rowsel(b, HG_CH - 1)
        st = st * jnp.exp(bl) + bmm_tn(vc, kc * jnp.exp(bl - b))
        outs.append(jnp.concatenate(parts, axis=1))
    o = jnp.concatenate(outs, axis=1)
    on = o * lax.rsqrt(jnp.mean(o * o, axis=-1, keepdims=True) + EPS) * og
    return st, on * silu(z)


def _tri(n, nh):
    t = (_iota((nh, n, n), 1) >= _iota((nh, n, n), 2)).astype(F32)
    tt = (_iota((nh, n, n), 1) <= _iota((nh, n, n), 2)).astype(F32)
    return t, tt


def _hg_load(p_ref, lb_ref, heads):
    col = lambda ref, base: jnp.stack([ref[:, pl.ds(base + h * HG_DH, HG_DH)] for h in heads])
    return col(lb_ref, 0), col(p_ref, 0), col(p_ref, D), col(p_ref, 2 * D), col(p_ref, 3 * D)


def hg_fwd(name, p, lb, og, carry=()):
    T = p.shape[0]
    nb = T // HG_BLK

    def body(p_ref, lb_ref, og_ref, u_ref, st_ref, s_scr):
        @pl.when(pl.program_id(0) == 0)
        def _():
            s_scr[...] = jnp.zeros(s_scr.shape, F32)
        st_ref[0] = s_scr[...]
        tri, trit = _tri(HG_CH, HG_HB)
        og_v = og_ref[...]
        for b0 in range(0, HG_HEADS, HG_HB):
            heads = list(range(b0, b0 + HG_HB))
            lbh, qp, fp, iv, z = _hg_load(p_ref, lb_ref, heads)
            st, u = hg_heads(tri, trit, lbh, og_v, s_scr[pl.ds(b0, HG_HB)], qp, fp, iv, z)
            s_scr[pl.ds(b0, HG_HB)] = st
            for i, h in enumerate(heads):
                u_ref[:, pl.ds(h * HG_DH, HG_DH)] = u[i].astype(BF16)

    (u, st), ex = _scan_call(
        body, name, nb,
        [_row_spec(HG_BLK, 4 * D), _const_spec((1, D)), _const_spec((1, HG_DH))],
        [_row_spec(HG_BLK, D), pl.BlockSpec((1, HG_HEADS, HG_DH, HG_DH), lambda i: (i, 0, 0, 0))],
        [jax.ShapeDtypeStruct((T, D), BF16), jax.ShapeDtypeStruct((nb, HG_HEADS, HG_DH, HG_DH), F32)],
        [pltpu.VMEM((HG_HEADS, HG_DH, HG_DH), F32)], (p, lb, og), carry)
    return u, st, ex


def hg_bwd(name, p, lb, og, states, du, carry=()):
    T = p.shape[0]
    nb = T // HG_BLK

    def body(p_ref, lb_ref, og_ref, st_ref, du_ref, dp_ref, dlb_ref, dog_ref, ds_scr):
        @pl.when(pl.program_id(0) == 0)
        def _():
            ds_scr[...] = jnp.zeros(ds_scr.shape, F32)
            dlb_ref[...] = jnp.zeros(dlb_ref.shape, F32)
            dog_ref[...] = jnp.zeros(dog_ref.shape, F32)
        tri, trit = _tri(HG_CH, HG_HB)
        og_v = og_ref[...]
        fn = functools.partial(hg_heads, tri, trit)
        for b0 in range(0, HG_HEADS, HG_HB):
            heads = list(range(b0, b0 + HG_HB))
            lbh, qp, fp, iv, z = _hg_load(p_ref, lb_ref, heads)
            _, f = jax.vjp(fn, lbh, og_v, st_ref[0, pl.ds(b0, HG_HB)], qp, fp, iv, z)
            du_h = jnp.stack([du_ref[:, pl.ds(h * HG_DH, HG_DH)].astype(F32) for h in heads])
            dlb, dog, dst, dq, df, di, dz = f((ds_scr[pl.ds(b0, HG_HB)], du_h))
            ds_scr[pl.ds(b0, HG_HB)] = dst
            dog_ref[...] += dog
            for i, h in enumerate(heads):
                for base, g in ((0, dq), (D, df), (2 * D, di), (3 * D, dz)):
                    dp_ref[:, pl.ds(base + h * HG_DH, HG_DH)] = g[i].astype(BF16)
                dlb_ref[:, pl.ds(h * HG_DH, HG_DH)] += dlb[i]

    rev = lambda i: (nb - 1 - i, 0)
    outs, ex = _scan_call(
        body, name, nb,
        [pl.BlockSpec((HG_BLK, 4 * D), rev), _const_spec((1, D)), _const_spec((1, HG_DH)),
         pl.BlockSpec((1, HG_HEADS, HG_DH, HG_DH), lambda i: (nb - 1 - i, 0, 0, 0)), pl.BlockSpec((HG_BLK, D), rev)],
        [pl.BlockSpec((HG_BLK, 4 * D), rev), _const_spec((1, D)), _const_spec((1, HG_DH))],
        [jax.ShapeDtypeStruct((T, 4 * D), BF16), jax.ShapeDtypeStruct((1, D), F32), jax.ShapeDtypeStruct((1, HG_DH), F32)],
        [pltpu.VMEM((HG_HEADS, HG_DH, HG_DH), F32)], (p, lb, og, states, du), carry)
    return (*outs, ex)


def swa_heads(qn, kn, sinks, kprev, vprev, qp4, kp, v, z4, cos4, sin4, cosk, sink_, hots, mask, p64, p64t):
    def norm_rope(xp, g, cs, sn):
        y = xp * lax.rsqrt(jnp.mean(xp * xp, axis=-1, keepdims=True) + EPS) * g
        return y * cs + blin01_r(y, p64, p64t) * sn

    q = norm_rope(qp4, qn, cos4, sin4)
    k = norm_rope(kp, kn, cosk, sink_)
    k2 = jnp.concatenate([kprev, k], axis=1)
    v2 = jnp.concatenate([vprev, v], axis=1)
    s = jnp.where(mask, bmm_nt(q, k2) * (SW_DH ** -0.5), NEG_INF)
    sink_col = jnp.concatenate([jnp.concatenate(
        [jnp.broadcast_to(jnp.sum(sinks * hot, axis=-1, keepdims=True), (SW_BLK, 1)) for hot in hot4], axis=0)[None]
        for hot4 in hots], axis=0)
    m = lax.stop_gradient(jnp.maximum(jnp.max(s, axis=-1, keepdims=True), sink_col))
    p = jnp.exp(s - m)
    p = p / (jnp.sum(p, axis=-1, keepdims=True) + jnp.exp(sink_col - m))
    o = bmm(p, v2)
    return k, v, o * silu(z4)


def _swa_consts(first):
    qi = _iota((SW_G * SW_BLK, 2 * SW_BLK), 0) % SW_BLK
    kj = _iota((SW_G * SW_BLK, 2 * SW_BLK), 1)
    rel = qi + SW_BLK - kj
    mask = (rel >= 0) & (rel < SW_BLK) & ((kj >= SW_BLK) | jnp.logical_not(first))
    i, j = _iota((SW_KV, SW_DH, SW_DH), 1), _iota((SW_KV, SW_DH, SW_DH), 2)
    p64 = (i == (j + SW_DH // 2) % SW_DH).astype(F32)
    p64t = (j == (i + SW_DH // 2) % SW_DH).astype(F32)
    lane = _iota((1, SW_KV * SW_G), 1)
    hots = [[(lane == h * SW_G + g).astype(F32) for g in range(SW_G)] for h in range(SW_KV)]
    return mask, p64, p64t, hots


def _stack_q(ref, h, base):
    return jnp.concatenate([ref[:, pl.ds(base + h * SW_G * SW_DH + g * SW_DH, SW_DH)] for g in range(SW_G)], axis=0)


def _swa_load(p_ref):
    heads = range(SW_KV)
    return (jnp.stack([_stack_q(p_ref, h, Q0) for h in heads]),
            jnp.stack([p_ref[:, pl.ds(K0 + h * SW_DH, SW_DH)] for h in heads]),
            jnp.stack([p_ref[:, pl.ds(V0 + h * SW_DH, SW_DH)] for h in heads]),
            jnp.stack([_stack_q(p_ref, h, Z0) for h in heads]))


Q0, K0, V0, Z0 = 0, SW_QW, SW_QW + SW_KVW, SW_QW + 2 * SW_KVW
SW_IN = 2 * SW_QW + 2 * SW_KVW


def swa_fwd(name, p, cos, sin, qn, kn, sinks):
    T = p.shape[0]
    nb = T // SW_BLK

    def body(p_ref, cos_ref, sin_ref, qn_ref, kn_ref, sk_ref, u_ref, ko_ref, kprev, vprev):
        first = pl.program_id(0) == 0

        @pl.when(first)
        def _():
            kprev[...] = jnp.zeros(kprev.shape, F32)
            vprev[...] = jnp.zeros(vprev.shape, F32)
        mask, p64, p64t, hots = _swa_consts(first)
        cs, sn = cos_ref[...], sin_ref[...]
        cs4, sn4 = jnp.concatenate([cs] * SW_G, axis=0), jnp.concatenate([sn] * SW_G, axis=0)
        qp4, kp, vv, z4 = _swa_load(p_ref)
        k, v, u4 = swa_heads(qn_ref[...], kn_ref[...], sk_ref[...], kprev[...], vprev[...], qp4, kp, vv, z4,
                             cs4, sn4, cs, sn, hots, mask, p64, p64t)
        kprev[...] = k
        vprev[...] = v
        for h in range(SW_KV):
            ko_ref[:, pl.ds(h * SW_DH, SW_DH)] = k[h]
            uh = u4[h]
            for g in range(SW_G):
                u_ref[:, pl.ds(h * SW_G * SW_DH + g * SW_DH, SW_DH)] = uh[g * SW_BLK:(g + 1) * SW_BLK].astype(BF16)

    return pl.pallas_call(
        body, name=name, grid=(nb,),
        in_specs=[_row_spec(SW_BLK, SW_IN), _row_spec(SW_BLK, SW_DH), _row_spec(SW_BLK, SW_DH),
                  _const_spec((1, SW_DH)), _const_spec((1, SW_DH)), _const_spec((1, SW_KV * SW_G))],
        out_specs=[_row_spec(SW_BLK, SW_QW), _row_spec(SW_BLK, SW_KVW)],
        out_shape=[jax.ShapeDtypeStruct((T, SW_QW), BF16), jax.ShapeDtypeStruct((T, SW_KVW), F32)],
        scratch_shapes=[pltpu.VMEM((SW_KV, SW_BLK, SW_DH), F32), pltpu.VMEM((SW_KV, SW_BLK, SW_DH), F32)],
        compiler_params=_params(dimension_semantics=("arbitrary",)),
    )(p, cos, sin, qn, kn, sinks)


def swa_bwd(name, p, kpost, cos, sin, qn, kn, sinks, du, carry=()):
    T = p.shape[0]
    nb = T // SW_BLK

    def body(p_ref, pprev_ref, kprev_ref, cos_ref, sin_ref, qn_ref, kn_ref, sk_ref, du_ref,
             dp_ref, dqn_ref, dkn_ref, dsk_ref, dk_scr, dv_scr):
        i = pl.program_id(0)
        first = i == nb - 1

        @pl.when(i == 0)
        def _():
            dk_scr[...] = jnp.zeros(dk_scr.shape, F32)
            dv_scr[...] = jnp.zeros(dv_scr.shape, F32)
            dqn_ref[...] = jnp.zeros(dqn_ref.shape, F32)
            dkn_ref[...] = jnp.zeros(dkn_ref.shape, F32)
            dsk_ref[...] = jnp.zeros(dsk_ref.shape, F32)
        mask, p64, p64t, hots = _swa_consts(first)
        cs, sn = cos_ref[...], sin_ref[...]
        cs4, sn4 = jnp.concatenate([cs] * SW_G, axis=0), jnp.concatenate([sn] * SW_G, axis=0)
        heads = range(SW_KV)
        fn = lambda qn, kn, sk, kpv, vpv, qp4, kp, v, z4: swa_heads(
            qn, kn, sk, kpv, vpv, qp4, kp, v, z4, cs4, sn4, cs, sn, hots, mask, p64, p64t)
        kpv = jnp.stack([kprev_ref[:, pl.ds(h * SW_DH, SW_DH)] for h in heads])
        vpv = jnp.stack([pprev_ref[:, pl.ds(V0 + h * SW_DH, SW_DH)] for h in heads])
        _, f = jax.vjp(fn, qn_ref[...], kn_ref[...], sk_ref[...], kpv, vpv, *_swa_load(p_ref))
        du4 = jnp.stack([_stack_q(du_ref, h, 0).astype(F32) for h in heads])
        dqn, dkn, dsk, dkpv, dvpv, dq4, dkp, dv, dz4 = f((dk_scr[...], dv_scr[...], du4))
        dk_scr[...] = dkpv
        dv_scr[...] = dvpv
        for h in heads:
            dp_ref[:, pl.ds(K0 + h * SW_DH, SW_DH)] = dkp[h].astype(BF16)
            dp_ref[:, pl.ds(V0 + h * SW_DH, SW_DH)] = dv[h].astype(BF16)
            dqh, dzh = dq4[h], dz4[h]
            for g in range(SW_G):
                c0 = h * SW_G * SW_DH + g * SW_DH
                dp_ref[:, pl.ds(Q0 + c0, SW_DH)] = dqh[g * SW_BLK:(g + 1) * SW_BLK].astype(BF16)
                dp_ref[:, pl.ds(Z0 + c0, SW_DH)] = dzh[g * SW_BLK:(g + 1) * SW_BLK].astype(BF16)
        dqn_ref[...] += dqn
        dkn_ref[...] += dkn
        dsk_ref[...] += dsk

    rev = lambda i: (nb - 1 - i, 0)
    prev = lambda i: (jnp.maximum(nb - 2 - i, 0), 0)
    outs, ex = _scan_call(
        body, name, nb,
        [pl.BlockSpec((SW_BLK, SW_IN), rev), pl.BlockSpec((SW_BLK, SW_IN), prev),
         pl.BlockSpec((SW_BLK, SW_KVW), prev), pl.BlockSpec((SW_BLK, SW_DH), rev), pl.BlockSpec((SW_BLK, SW_DH), rev),
         _const_spec((1, SW_DH)), _const_spec((1, SW_DH)), _const_spec((1, SW_KV * SW_G)), pl.BlockSpec((SW_BLK, SW_QW), rev)],
        [pl.BlockSpec((SW_BLK, SW_IN), rev), _const_spec((1, SW_DH)), _const_spec((1, SW_DH)), _const_spec((1, SW_KV * SW_G))],
        [jax.ShapeDtypeStruct((T, SW_IN), BF16), jax.ShapeDtypeStruct((1, SW_DH), F32),
         jax.ShapeDtypeStruct((1, SW_DH), F32), jax.ShapeDtypeStruct((1, SW_KV * SW_G), F32)],
        [pltpu.VMEM((SW_KV, SW_BLK, SW_DH), F32), pltpu.VMEM((SW_KV, SW_BLK, SW_DH), F32)],
        (p, p, kpost, cos, sin, qn, kn, sinks, du), carry)
    return (*outs, ex)


CONV_TC, CONV_TL, HALO = 512, 1024, 8


def _conv_taps(xe, w_ref):
    acc = w_ref[pl.ds(CONV_K - 1, 1), :] * xe
    for k in range(CONV_K - 1):
        acc = acc + w_ref[pl.ds(k, 1), :] * pltpu.roll(xe, CONV_K - 1 - k, 0)
    return acc


def conv_fwd(name, x, w8):
    T, C = x.shape
    tc = min(CONV_TC, T)
    nt = T // tc

    def body(xp_ref, x_ref, w_ref, o_ref):
        prev = jnp.where(pl.program_id(0) == 0, 0.0, xp_ref[...])
        xe = jnp.concatenate([prev, x_ref[...]], axis=0)
        o_ref[...] = silu(_conv_taps(xe, w_ref)[HALO:])

    return pl.pallas_call(
        body, name=name, grid=(nt, C // CONV_TL),
        in_specs=[pl.BlockSpec((HALO, CONV_TL), lambda i, j: (jnp.maximum(i * (tc // HALO) - 1, 0), j)),
                  pl.BlockSpec((tc, CONV_TL), lambda i, j: (i, j)), pl.BlockSpec((HALO, CONV_TL), lambda i, j: (0, j))],
        out_specs=pl.BlockSpec((tc, CONV_TL), lambda i, j: (i, j)),
        out_shape=jax.ShapeDtypeStruct((T, C), F32),
        compiler_params=_params(dimension_semantics=("arbitrary", "arbitrary")),
    )(x, x, w8)


def conv_bwd(name, x, dy, w8):
    T, C = x.shape
    tc = min(CONV_TC, T)
    nt = T // tc
    n = tc + 2 * HALO

    def body(xp_ref, x_ref, xn_ref, dy_ref, dyn_ref, w_ref, dx_ref, dw_ref):
        i = pl.program_id(1)

        @pl.when(i == 0)
        def _():
            dw_ref[...] = jnp.zeros(dw_ref.shape, F32)
        prev = jnp.where(i == 0, 0.0, xp_ref[...])
        xe = jnp.concatenate([prev, x_ref[...], xn_ref[...]], axis=0)
        dye = jnp.concatenate([jnp.zeros((HALO, CONV_TL), F32), dy_ref[...], jnp.where(i == nt - 1, 0.0, dyn_ref[...])], axis=0)
        ce = _conv_taps(xe, w_ref)
        sg = sigmoid(ce)
        dce = dye * (sg * (1.0 + ce * (1.0 - sg)))
        dx = w_ref[pl.ds(CONV_K - 1, 1), :] * dce
        for k in range(CONV_K - 1):
            dx = dx + w_ref[pl.ds(k, 1), :] * pltpu.roll(dce, n - (CONV_K - 1 - k), 0)
        dx_ref[...] = dx[HALO:HALO + tc].astype(BF16)
        dcur = dce[HALO:HALO + tc]
        for k in range(CONV_K):
            xs = xe if k == CONV_K - 1 else pltpu.roll(xe, CONV_K - 1 - k, 0)
            dw_ref[pl.ds(k, 1), :] += jnp.sum(dcur * xs[HALO:HALO + tc], axis=0, keepdims=True)

    nh = T // HALO
    prev_map = lambda j, i: (jnp.maximum(i * (tc // HALO) - 1, 0), j)
    next_map = lambda j, i: (jnp.minimum((i + 1) * (tc // HALO), nh - 1), j)
    cur_map = lambda j, i: (i, j)
    return pl.pallas_call(
        body, name=name, grid=(C // CONV_TL, nt),
        in_specs=[pl.BlockSpec((HALO, CONV_TL), prev_map), pl.BlockSpec((tc, CONV_TL), cur_map),
                  pl.BlockSpec((HALO, CONV_TL), next_map), pl.BlockSpec((tc, CONV_TL), cur_map),
                  pl.BlockSpec((HALO, CONV_TL), next_map), pl.BlockSpec((HALO, CONV_TL), lambda j, i: (0, j))],
        out_specs=[pl.BlockSpec((tc, CONV_TL), cur_map), pl.BlockSpec((HALO, CONV_TL), lambda j, i: (0, j))],
        out_shape=[jax.ShapeDtypeStruct((T, C), BF16), jax.ShapeDtypeStruct((HALO, C), F32)],
        compiler_params=_params(dimension_semantics=("arbitrary", "arbitrary")),
    )(x, x, x, dy, dy, w8)


GD_VH = 16
GD_HB = 8


def _unit_lower_inverse(a, eye_f):
    n = -a
    t = eye_f + n
    p = n
    for _ in range(5):
        p = bmmf(p, p)
        t = t + bmmf(t, p)
    return t


@jax.custom_vjp
def unit_lower_solve2(a, r1, r2, eye_f):
    t = _unit_lower_inverse(a, eye_f)
    return bmmf(t, r1), bmmf(t, r2)


def _uls2_fwd(a, r1, r2, eye_f):
    t = _unit_lower_inverse(a, eye_f)
    x1, x2 = bmmf(t, r1), bmmf(t, r2)
    return (x1, x2), (t, x1, x2, eye_f)


def _uls2_bwd(res, g):
    t, x1, x2, eye_f = res
    high = lax.Precision.HIGH
    d1 = _bdot(t, g[0], 1, 1, high)
    d2 = _bdot(t, g[1], 1, 1, high)
    da = -(_bdot(d1, x1, 2, 2, high) + _bdot(d2, x2, 2, 2, high))
    return da, d1, d2, jnp.zeros_like(eye_f)


unit_lower_solve2.defvjp(_uls2_fwd, _uls2_bwd)


def gd_heads(al, dt, og, s, qc, kc, v, z, a_col, b_col, cst):
    lt_t, eye, incl, strict = cst
    eye_f = eye.astype(F32)
    nh = s.shape[0]
    q = qc * lax.rsqrt(jnp.sum(qc * qc, axis=-1, keepdims=True) + EPS) * (GD_DK ** -0.5)
    k = kc * lax.rsqrt(jnp.sum(kc * kc, axis=-1, keepdims=True) + EPS)
    beta = sigmoid(b_col)
    g = -jnp.exp(al) * softplus(a_col + dt)
    d_mat = jnp.broadcast_to(jnp.sum(g * lt_t, axis=1, keepdims=True), (nh, GD_CHUNK, GD_CHUNK))
    d = jnp.sum(jnp.where(eye, d_mat, 0.0), axis=-1, keepdims=True)
    dec = jnp.exp(jnp.where(incl, d - d_mat, NEG_INF))
    kb = k * beta
    u, w = unit_lower_solve2(bmm_nt(kb, k) * dec * strict, v * beta, kb * jnp.exp(d), eye_f)
    v_new = u - bmm(w, s)
    o = bmm(q * jnp.exp(d), s) + bmm(bmm_nt(q, k) * dec, v_new)
    dl = jnp.sum(g, axis=1, keepdims=True)
    s_new = s * jnp.exp(dl) + bmm_tn(k * jnp.exp(dl - d), v_new)
    on = o * lax.rsqrt(jnp.mean(o * o, axis=-1, keepdims=True) + EPS) * og
    return s_new, on * silu(z)


def _gd_consts():
    r, c = _iota((GD_CHUNK, GD_CHUNK), 0), _iota((GD_CHUNK, GD_CHUNK), 1)
    return (r <= c).astype(F32), r == c, r >= c, (r > c).astype(F32)


def _gd_load(qkv_ref, z_ref, ab_v, al_v, dt_v, heads):
    lane = _iota((1, 128), 1)
    hot_a = [(lane == h).astype(F32) for h in heads]
    hot_b = [(lane == GD_VH + h).astype(F32) for h in heads]
    col = lambda src, hot: jnp.stack([jnp.sum(src * m, axis=-1, keepdims=True) for m in hot])
    ops = (col(al_v, hot_a), col(dt_v, hot_a),
           jnp.stack([qkv_ref[:, pl.ds((h // 2) * GD_DK, GD_DK)] for h in heads]),
           jnp.stack([qkv_ref[:, pl.ds(GD_QKW + (h // 2) * GD_DK, GD_DK)] for h in heads]),
           jnp.stack([qkv_ref[:, pl.ds(2 * GD_QKW + h * GD_DK, GD_DK)] for h in heads]),
           jnp.stack([z_ref[:, pl.ds(h * GD_DK, GD_DK)] for h in heads]),
           col(ab_v, hot_a), col(ab_v, hot_b))
    return ops, hot_a, hot_b


def gd_fwd(name, qkv, z, ab, al, dt, og):
    T = qkv.shape[0]
    nc = T // GD_CHUNK

    def body(qkv_ref, z_ref, ab_ref, al_ref, dt_ref, og_ref, u_ref, st_ref, s_scr):
        @pl.when(pl.program_id(0) == 0)
        def _():
            s_scr[...] = jnp.zeros(s_scr.shape, F32)
        st_ref[0] = s_scr[...]
        cst = _gd_consts()
        ab_v, al_v, dt_v, og_v = ab_ref[...], al_ref[...], dt_ref[...], og_ref[...]
        for b0 in range(0, GD_VH, GD_HB):
            heads = list(range(b0, b0 + GD_HB))
            (alh, dth, q, k, v, zz, a_col, b_col), _, _ = _gd_load(qkv_ref, z_ref, ab_v, al_v, dt_v, heads)
            s_new, u = gd_heads(alh, dth, og_v, s_scr[pl.ds(b0, GD_HB)], q, k, v, zz, a_col, b_col, cst)
            s_scr[pl.ds(b0, GD_HB)] = s_new
            for i, h in enumerate(heads):
                u_ref[:, pl.ds(h * GD_DK, GD_DK)] = u[i].astype(BF16)

    return pl.pallas_call(
        body, name=name, grid=(nc,),
        in_specs=[_row_spec(GD_CHUNK, GD_QKV), _row_spec(GD_CHUNK, GD_VW), _row_spec(GD_CHUNK, 128),
                  _const_spec((1, 128)), _const_spec((1, 128)), _const_spec((1, 128))],
        out_specs=[_row_spec(GD_CHUNK, GD_VW), pl.BlockSpec((1, GD_VH, GD_DK, GD_DK), lambda i: (i, 0, 0, 0))],
        out_shape=[jax.ShapeDtypeStruct((T, GD_VW), BF16), jax.ShapeDtypeStruct((nc, GD_VH, GD_DK, GD_DK), F32)],
        scratch_shapes=[pltpu.VMEM((GD_VH, GD_DK, GD_DK), F32)],
        compiler_params=_params(dimension_semantics=("arbitrary",)),
    )(qkv, z, ab, al, dt, og)


def gd_bwd(name, qkv, z, ab, al, dt, og, states, du, carry=()):
    T = qkv.shape[0]
    nc = T // GD_CHUNK

    def body(qkv_ref, z_ref, ab_ref, al_ref, dt_ref, og_ref, st_ref, du_ref,
             dqkv_ref, dz_ref, dab_ref, dal_ref, ddt_ref, dog_ref, ds_scr):
        @pl.when(pl.program_id(0) == 0)
        def _():
            ds_scr[...] = jnp.zeros(ds_scr.shape, F32)
            dal_ref[...] = jnp.zeros(dal_ref.shape, F32)
            ddt_ref[...] = jnp.zeros(ddt_ref.shape, F32)
            dog_ref[...] = jnp.zeros(dog_ref.shape, F32)
        cst = _gd_consts()
        ab_v, al_v, dt_v, og_v = ab_ref[...], al_ref[...], dt_ref[...], og_ref[...]
        dab = jnp.zeros((GD_CHUNK, 128), F32)
        dal_row = jnp.zeros((1, 128), F32)
        ddt_row = jnp.zeros((1, 128), F32)
        dog_row = jnp.zeros((1, 128), F32)
        for b0 in range(0, GD_VH, GD_HB):
            heads = list(range(b0, b0 + GD_HB))
            (alh, dth, q, k, v, zz, a_col, b_col), hot_a, hot_b = _gd_load(qkv_ref, z_ref, ab_v, al_v, dt_v, heads)
            fn = lambda *a: gd_heads(*a, cst)
            _, f = jax.vjp(fn, alh, dth, og_v, st_ref[0, pl.ds(b0, GD_HB)], q, k, v, zz, a_col, b_col)
            du_h = jnp.stack([du_ref[:, pl.ds(h * GD_DK, GD_DK)].astype(F32) for h in heads])
            dalh, ddth, dog, ds, dq, dk, dv, dz, da_col, db_col = f((ds_scr[pl.ds(b0, GD_HB)], du_h))
            ds_scr[pl.ds(b0, GD_HB)] = ds
            dog_row = dog_row + dog
            for i, h in enumerate(heads):
                dqkv_ref[:, pl.ds(2 * GD_QKW + h * GD_DK, GD_DK)] = dv[i]
                dz_ref[:, pl.ds(h * GD_DK, GD_DK)] = dz[i].astype(BF16)
                dab = dab + da_col[i] * hot_a[i] + db_col[i] * hot_b[i]
                dal_row = dal_row + dalh[i] * hot_a[i]
                ddt_row = ddt_row + ddth[i] * hot_a[i]
                if h % 2 == 0:
                    dqkv_ref[:, pl.ds((h // 2) * GD_DK, GD_DK)] = dq[i] + dq[i + 1]
                    dqkv_ref[:, pl.ds(GD_QKW + (h // 2) * GD_DK, GD_DK)] = dk[i] + dk[i + 1]
        dab_ref[...] = dab
        dal_ref[...] += dal_row
        ddt_ref[...] += ddt_row
        dog_ref[...] += dog_row

    rev = lambda i: (nc - 1 - i, 0)
    outs, ex = _scan_call(
        body, name, nc,
        [pl.BlockSpec((GD_CHUNK, GD_QKV), rev), pl.BlockSpec((GD_CHUNK, GD_VW), rev), pl.BlockSpec((GD_CHUNK, 128), rev),
         _const_spec((1, 128)), _const_spec((1, 128)), _const_spec((1, 128)),
         pl.BlockSpec((1, GD_VH, GD_DK, GD_DK), lambda i: (nc - 1 - i, 0, 0, 0)), pl.BlockSpec((GD_CHUNK, GD_VW), rev)],
        [pl.BlockSpec((GD_CHUNK, GD_QKV), rev), pl.BlockSpec((GD_CHUNK, GD_VW), rev), pl.BlockSpec((GD_CHUNK, 128), rev),
         _const_spec((1, 128)), _const_spec((1, 128)), _const_spec((1, 128))],
        [jax.ShapeDtypeStruct((T, GD_QKV), F32), jax.ShapeDtypeStruct((T, GD_VW), BF16),
         jax.ShapeDtypeStruct((T, 128), F32)] + [jax.ShapeDtypeStruct((1, 128), F32)] * 3,
        [pltpu.VMEM((GD_VH, GD_DK, GD_DK), F32)], (qkv, z, ab, al, dt, og, states, du), carry)
    return (*outs, ex)


def _my_pos():
    return lax.axis_index("x"), lax.axis_index("y"), lax.axis_index("c")


def _peers(pos):
    out = []
    for k in range(1, N_DEV):
        dev = tuple(1 - p if (k >> s) & 1 else p for p, s in zip(pos, (2, 1, 0)))
        out.append((dev, 4 * dev[0] + 2 * dev[1] + dev[2]))
    return out


_ANY = pl.BlockSpec(memory_space=pl.ANY)
_COMM_SCRATCH = [pltpu.SemaphoreType.DMA((N_DEV - 1,)), pltpu.SemaphoreType.DMA((N_DEV - 1,)), pltpu.SemaphoreType.DMA(())]
GATHER, SCATTER = "gather", "scatter"


def _exchange_descs(pos, kinds, x_refs, o_refs, send_sems, recv_sems, local_sems):
    me = 4 * pos[0] + 2 * pos[1] + pos[2]
    peers = _peers(pos)
    local, sends, recvs = [], [], []
    for i, (kind, x, o) in enumerate(zip(kinds, x_refs, o_refs)):
        local.append(pltpu.make_async_copy(x if kind == GATHER else x.at[me], o.at[me], local_sems.at[i]))
        for k, (dev, pid) in enumerate(peers):
            s = (N_DEV - 1) * i + k
            src = x if kind == GATHER else x.at[pid]
            sends.append(pltpu.make_async_remote_copy(src, o.at[me], send_sems.at[s], recv_sems.at[s],
                                                      device_id=dev, device_id_type=MESH))
            recvs.append(pltpu.make_async_remote_copy(src, o.at[pid], send_sems.at[s], recv_sems.at[s],
                                                      device_id=dev, device_id_type=MESH))
    return local, sends, recvs


def _scan_call(body, name, nsteps, in_specs, out_specs, out_shape, scratch_shapes, args, carry=()):
    n_in, n_out, n_scr, n_ex = len(in_specs), len(out_specs), len(scratch_shapes), len(carry)
    params = _params(dimension_semantics=("arbitrary",))
    if not carry:
        outs = pl.pallas_call(body, name=name, grid=(nsteps,), in_specs=in_specs, out_specs=out_specs, out_shape=out_shape,
                              scratch_shapes=scratch_shapes, compiler_params=params)(*args)
        return outs, []
    kinds = [k for k, _ in carry]
    xs = [a for _, a in carry]
    ex_shape = [jax.ShapeDtypeStruct(((N_DEV,) + a.shape) if k == GATHER else a.shape, a.dtype) for k, a in carry]

    def full(*refs):
        ins, x_refs = refs[:n_in], refs[n_in:n_in + n_ex]
        outs = refs[n_in + n_ex:n_in + n_ex + n_out]
        o_refs = refs[n_in + n_ex + n_out:n_in + 2 * n_ex + n_out]
        scr = refs[n_in + 2 * n_ex + n_out:n_in + 2 * n_ex + n_out + n_scr]
        sems = refs[n_in + 2 * n_ex + n_out + n_scr:]
        pos = _my_pos()
        step = pl.program_id(0)

        @pl.when(step == 0)
        def _():
            local, sends, _ = _exchange_descs(pos, kinds, x_refs, o_refs, *sems)
            for cp in local + sends:
                cp.start()

        body(*ins, *outs, *scr)

        @pl.when(step == nsteps - 1)
        def _():
            local, sends, recvs = _exchange_descs(pos, kinds, x_refs, o_refs, *sems)
            for cp in recvs:
                cp.wait_recv()
            for cp in sends:
                cp.wait_send()
            for cp in local:
                cp.wait()

    n_sem = (N_DEV - 1) * n_ex
    outs = pl.pallas_call(
        full, name=name, grid=(nsteps,), in_specs=list(in_specs) + [_ANY] * n_ex, out_specs=list(out_specs) + [_ANY] * n_ex,
        out_shape=list(out_shape) + ex_shape,
        scratch_shapes=list(scratch_shapes) + [pltpu.SemaphoreType.DMA((n_sem,)), pltpu.SemaphoreType.DMA((n_sem,)),
                                               pltpu.SemaphoreType.DMA((n_ex,))],
        compiler_params=params)(*args, *xs)
    return outs[:n_out], list(outs[n_out:])


def all_gather(name, x):
    def body(x_ref, o_ref, send_sems, recv_sems, local_sem):
        pos = _my_pos()
        me = 4 * pos[0] + 2 * pos[1] + pos[2]
        peers = _peers(pos)
        mine = pltpu.make_async_copy(x_ref, o_ref.at[me], local_sem)
        mine.start()
        sends = [pltpu.make_async_remote_copy(x_ref, o_ref.at[me], send_sems.at[k], recv_sems.at[k],
                                              device_id=dev, device_id_type=MESH) for k, (dev, _) in enumerate(peers)]
        for cp in sends:
            cp.start()
        for k, (dev, pid) in enumerate(peers):
            pltpu.make_async_remote_copy(x_ref, o_ref.at[pid], send_sems.at[k], recv_sems.at[k],
                                         device_id=dev, device_id_type=MESH).wait_recv()
        for cp in sends:
            cp.wait_send()
        mine.wait()

    return pl.pallas_call(
        body, name=name, out_shape=jax.ShapeDtypeStruct((N_DEV,) + x.shape, x.dtype),
        in_specs=[_ANY], out_specs=_ANY, scratch_shapes=_COMM_SCRATCH,
    )(x)


def all_to_all(name, parts):
    def body(x_ref, o_ref, send_sems, recv_sems, local_sem):
        pos = _my_pos()
        me = 4 * pos[0] + 2 * pos[1] + pos[2]
        peers = _peers(pos)
        mine = pltpu.make_async_copy(x_ref.at[me], o_ref.at[me], local_sem)
        mine.start()
        sends = [pltpu.make_async_remote_copy(x_ref.at[pid], o_ref.at[me], send_sems.at[k], recv_sems.at[k],
                                              device_id=dev, device_id_type=MESH) for k, (dev, pid) in enumerate(peers)]
        for cp in sends:
            cp.start()
        for k, (dev, pid) in enumerate(peers):
            pltpu.make_async_remote_copy(x_ref.at[pid], o_ref.at[pid], send_sems.at[k], recv_sems.at[k],
                                         device_id=dev, device_id_type=MESH).wait_recv()
        for cp in sends:
            cp.wait_send()
        mine.wait()

    return pl.pallas_call(
        body, name=name, out_shape=jax.ShapeDtypeStruct(parts.shape, parts.dtype),
        in_specs=[_ANY], out_specs=_ANY, scratch_shapes=_COMM_SCRATCH,
    )(parts)


def lb_rows(r0, r1, r2, r3):
    mx = jnp.maximum(jnp.maximum(r0, r1), jnp.maximum(r2, r3))
    e = [jnp.exp(r - mx) for r in (r0, r1, r2, r3)]
    inv = 1.0 / (e[0] + e[1] + e[2] + e[3])
    c0 = e[0] * inv
    c1 = c0 + e[1] * inv
    c2 = c1 + e[2] * inv
    c3 = c2 + e[3] * inv
    return c0 - c0, c1 - c0, c2 - c0, c3 - c0


def mod_partial(name, c_all, ada_w):
    n, _, w = ada_w.shape

    def body(c_ref, w_ref, o_ref):
        o_ref[0] = mm(c_ref[...], w_ref[0])

    return pl.pallas_call(
        body, name=name, grid=(n,),
        in_specs=[_const_spec((N_DEV, D)), pl.BlockSpec((1, D, w), lambda i: (i, 0, 0))],
        out_specs=pl.BlockSpec((1, N_DEV, w), lambda i: (i, 0, 0)),
        out_shape=jax.ShapeDtypeStruct((n, N_DEV, w), F32),
    )(c_all, ada_w)


def prep(name, modp, ada_b, hgrn_lb):
    def body(mp_ref, b_ref, lb_ref, mod_ref, lbo_ref):
        mod_ref[...] = mp_ref[...] + b_ref[...]
        out = lb_rows(*[lb_ref[pl.ds(i, 1), :] for i in range(4)])
        for i in range(4):
            lbo_ref[pl.ds(i, 1), :] = out[i]

    return pl.pallas_call(
        body, name=name,
        out_shape=[jax.ShapeDtypeStruct(modp.shape, F32), jax.ShapeDtypeStruct(hgrn_lb.shape, F32)],
    )(modp, ada_b, hgrn_lb)


def lb_grad(name, hgrn_lb, dlb_parts):
    def body(lb_ref, d_ref, o_ref):
        cts = []
        for i in range(4):
            acc = d_ref[0, pl.ds(i, 1), :]
            for p in range(1, N_DEV):
                acc = acc + d_ref[p, pl.ds(i, 1), :]
            cts.append(acc)
        _, f = jax.vjp(lb_rows, *[lb_ref[pl.ds(i, 1), :] for i in range(4)])
        for i, g in enumerate(f(tuple(cts))):
            o_ref[pl.ds(i, 1), :] = g

    return pl.pallas_call(body, name=name, out_shape=jax.ShapeDtypeStruct(hgrn_lb.shape, F32))(hgrn_lb, dlb_parts)


def ada_grad(name, c_t, dm):
    n, _, w = dm.shape

    def body(c_ref, d_ref, o_ref):
        acc = c_ref[:, pl.ds(0, 1)] * d_ref[0, pl.ds(0, 1), :]
        for b in range(1, N_DEV):
            acc = acc + c_ref[:, pl.ds(b, 1)] * d_ref[0, pl.ds(b, 1), :]
        o_ref[0] = acc

    return pl.pallas_call(
        body, name=name, grid=(n,),
        in_specs=[_const_spec((D, N_DEV)), pl.BlockSpec((1, N_DEV, w), lambda i: (i, 0, 0))],
        out_specs=pl.BlockSpec((1, D, w), lambda i: (i, 0, 0)),
        out_shape=jax.ShapeDtypeStruct((n, D, w), F32),
    )(c_t, dm)


ADAMW_ROWS = 256


def adamw(name, w, m, v, gparts):
    shape = w.shape
    L = shape[-1]
    R = 1
    for s in shape[:-1]:
        R *= s
    P = gparts.shape[0]
    tr = min(R, ADAMW_ROWS)

    def body(w_ref, m_ref, v_ref, g_ref, go_ref, d_ref, mo_ref, vo_ref):
        g = g_ref[0].astype(F32)
        for p in range(1, P):
            g = g + g_ref[p].astype(F32)
        mn = ADAM_B1 * m_ref[...] + (1.0 - ADAM_B1) * g
        vn = ADAM_B2 * v_ref[...] + (1.0 - ADAM_B2) * (g * g)
        m_hat = mn / (1.0 - ADAM_B1 ** ADAM_STEP)
        v_hat = vn / (1.0 - ADAM_B2 ** ADAM_STEP)
        go_ref[...] = g
        d_ref[...] = -ADAM_LR * (m_hat / (jnp.sqrt(v_hat) + ADAM_EPS) + ADAM_WD * w_ref[...])
        mo_ref[...] = mn
        vo_ref[...] = vn

    spec = pl.BlockSpec((tr, L), lambda i: (i, 0))
    outs = pl.pallas_call(
        body, name=name, grid=(R // tr,),
        in_specs=[spec, spec, spec, pl.BlockSpec((P, tr, L), lambda i: (0, i, 0))],
        out_specs=[spec] * 4, out_shape=[jax.ShapeDtypeStruct((R, L), F32)] * 4,
        compiler_params=_params(dimension_semantics=("arbitrary",)),
    )(w.reshape(R, L), m.reshape(R, L), v.reshape(R, L), gparts.reshape(P, R, L))
    return [o.reshape(shape) for o in outs]


def _cols_from(g):
    _, n, k, w8 = g.shape
    return jnp.transpose(g, (1, 2, 0, 3)).reshape(n, k, N_DEV * w8)


def _rows_from(g):
    _, n, k8, nn = g.shape
    return jnp.transpose(g, (1, 0, 2, 3)).reshape(n, N_DEV * k8, nn)


def _scatter_cols(dw, wire=BF16):
    n, k, nn = dw.shape
    return jnp.transpose(dw.reshape(n, k, N_DEV, nn // N_DEV), (2, 0, 1, 3)).astype(wire)


def _scatter_rows(dw, wire=BF16):
    n, k, nn = dw.shape
    return jnp.transpose(dw.reshape(n, N_DEV, k // N_DEV, nn), (1, 0, 2, 3)).astype(wire)


def _pad_lanes(v, width=128):
    return jnp.pad(v, ((0, 0), (0, width - v.shape[1])))


def _stack_rows(parts):
    n, L = len(parts), parts[0].shape[1]
    r = lax.broadcasted_iota(jnp.int32, (n, L), 0)
    out = jnp.zeros((n, L), parts[0].dtype)
    for i, p in enumerate(parts):
        out = jnp.where(r == i, jnp.broadcast_to(p, (n, L)), out)
    return out


def kernel(x, c, positions, hgrn_lb, ada_w, ada_b, norm_g, hg_in_w, hg_out_w, hg_onorm, sw_in_w, sw_out_w, sw_qnorm, sw_knorm, sw_sinks, gd_in_w, gd_out_w, gd_conv_w, gd_a_log, gd_dt_bias, gd_onorm, loss_target, m_hgrn_lb, m_ada_w, m_ada_b, m_norm_g, m_hg_in_w, m_hg_out_w, m_hg_onorm, m_sw_in_w, m_sw_out_w, m_sw_qnorm, m_sw_knorm, m_sw_sinks, m_gd_in_w, m_gd_out_w, m_gd_conv_w, m_gd_a_log, m_gd_dt_bias, m_gd_onorm, v_hgrn_lb, v_ada_w, v_ada_b, v_norm_g, v_hg_in_w, v_hg_out_w, v_hg_onorm, v_sw_in_w, v_sw_out_w, v_sw_qnorm, v_sw_knorm, v_sw_sinks, v_gd_in_w, v_gd_out_w, v_gd_conv_w, v_gd_a_log, v_gd_dt_bias, v_gd_onorm):
    pos = _my_pos()
    me = 4 * pos[0] + 2 * pos[1] + pos[2]
    x0 = x[0]
    tgt = loss_target[0]
    n_layers = norm_g.shape[0]
    aw = ada_w.shape[2]

    c_all = all_gather("ag_c", c)[:, 0, :]
    modp = all_gather("ag_mod", mod_partial("mod_partial", c_all, ada_w))
    modp_mine = lax.dynamic_index_in_dim(modp, me, axis=2, keepdims=False)
    modp_mine = jnp.transpose(modp_mine, (1, 0, 2)).reshape(n_layers, N_DEV * aw)
    mod, lb_all = prep("prep", modp_mine, ada_b, hgrn_lb)
    shift, scale, gate = mod[:, :D], mod[:, D:2 * D], mod[:, 2 * D:]
    row = lambda a, i: a[i:i + 1]

    w_hg_in = [_cols_from(all_gather("ag_hg_in0", hg_in_w[0:1].astype(BF16)))[0], None]
    later = [(GATHER, hg_in_w[1:2].astype(BF16)), (GATHER, hg_out_w.astype(BF16)), (GATHER, sw_in_w.astype(BF16)),
             (GATHER, sw_out_w.astype(BF16)), (GATHER, gd_in_w.astype(BF16)), (GATHER, gd_out_w.astype(BF16)),
             (GATHER, gd_conv_w)]
    gd_al, gd_dt = _pad_lanes(gd_a_log), _pad_lanes(gd_dt_bias)

    inv_freq = ROPE_THETA ** (-jnp.arange(0, SW_DH, 2, dtype=F32) / SW_DH)
    ang = positions[0].astype(F32)[:, None] * inv_freq
    cos, sin = jnp.cos(ang), jnp.sin(ang)
    cos64, sin64 = jnp.concatenate([cos, cos], axis=-1), jnp.concatenate([-sin, sin], axis=-1)

    xs, hs, ys, saved = [x0], [], [], []
    h = norm_mod("norm0", x0, row(norm_g, 0), row(scale, 0), row(shift, 0))
    for i in range(n_layers):
        kind, j = i % 3, i // 3
        hs.append(h)
        if kind == 0:
            p = matmul(f"hg_in{i}", h, w_hg_in[j])
            u, st, got = hg_fwd(f"hg_fwd{i}", p, row(lb_all, i), row(hg_onorm, j), later if i == 0 else ())
            if i == 0:
                w_hg_in[1] = _cols_from(got[0])[0]
                w_hg_out = _rows_from(got[1])
                w_sw_in, w_sw_out = _cols_from(got[2])[0], _rows_from(got[3])[0]
                w_gd_in, w_gd_out = _cols_from(got[4])[0], _rows_from(got[5])[0]
                w_gd_qkv, w_gd_z = w_gd_in[:, :GD_QKV], w_gd_in[:, GD_QKV:GD_QKV + GD_VW]
                w_gd_ab = _pad_lanes(w_gd_in[:, GD_QKV + GD_VW:])
                conv_w8 = jnp.pad(_cols_from(got[6])[0], ((0, HALO - CONV_K), (0, 0)))
            y = matmul(f"hg_out{i}", u, w_hg_out[j])
            saved.append((p, u, st))
        elif kind == 1:
            p = matmul(f"sw_in{i}", h, w_sw_in)
            u, kpost = swa_fwd(f"sw_fwd{i}", p, cos64, sin64, row(sw_qnorm, j), row(sw_knorm, j), row(sw_sinks, j))
            y = matmul(f"sw_out{i}", u, w_sw_out)
            saved.append((p, u, kpost))
        else:
            xq = matmul(f"gd_qkv{i}", h, w_gd_qkv)
            zz = matmul(f"gd_z{i}", h, w_gd_z)
            ab = matmul(f"gd_ab{i}", h, w_gd_ab)
            cv = conv_fwd(f"gd_conv{i}", xq, conv_w8)
            u, st = gd_fwd(f"gd_fwd{i}", cv, zz, ab, gd_al, gd_dt, row(gd_onorm, j))
            y = matmul(f"gd_out{i}", u, w_gd_out)
            saved.append((xq, zz, ab, cv, u, st))
        ys.append(y)
        if i + 1 < n_layers:
            xn, h = resid_norm(f"resid{i}", xs[i], y, row(gate, i), row(norm_g, i + 1), row(scale, i + 1), row(shift, i + 1))
            xs.append(xn)

    last = n_layers - 1
    dx, dy, loss_acc, dgate_last = loss_head("loss", xs[last], ys[last], tgt, row(gate, last))
    loss = lax.psum(loss_acc[0, 0], ("x", "y", "c"))

    dgate = [None] * n_layers
    dgate[last] = dgate_last
    dg_norm, dscale, dshift = [None] * n_layers, [None] * n_layers, [None] * n_layers
    dlb = [jnp.zeros((1, D), F32)] * n_layers
    d_hg_on = [None] * 2
    rs_hg_in, rs_hg_out = [None] * 2, [None] * 2
    pending = ()
    for i in range(last, -1, -1):
        kind, j = i % 3, i // 3
        h = hs[i]
        if kind == 0:
            p, u, st = saved[i]
            du = matmul_nt(f"hg_du{i}", [(dy, w_hg_out[j])])
            d_out = matmul_tn(f"hg_dwo{i}", u, dy)
            dp, dlb_i, dog, got = hg_bwd(f"hg_bwd{i}", p, row(lb_all, i), row(hg_onorm, j), st, du, pending)
            if got:
                rs_sw_in, rs_sw_out = got
            dlb[i] = dlb_i
            d_hg_on[j] = dog
            dh = matmul_nt(f"hg_dh{i}", [(dp, w_hg_in[j])])
            d_in = matmul_tn(f"hg_dwi{i}", h, dp)
            if i > 0:
                pending = [(SCATTER, _scatter_cols(d_in[None])), (SCATTER, _scatter_rows(d_out[None]))]
            else:
                rs_hg_in[j] = all_to_all("rs_hg_in0", _scatter_cols(d_in[None]))
                rs_hg_out[j] = all_to_all("rs_hg_out0", _scatter_rows(d_out[None]))
        elif kind == 1:
            p, u, kpost = saved[i]
            du = matmul_nt(f"sw_du{i}", [(dy, w_sw_out)])
            d_out = matmul_tn(f"sw_dwo{i}", u, dy)
            dp, d_qn, d_kn, d_sk, got = swa_bwd(f"sw_bwd{i}", p, kpost, cos64, sin64, row(sw_qnorm, j), row(sw_knorm, j),
                                                row(sw_sinks, j), du, pending)
            rs_gd_in, rs_gd_out, rs_gd_conv = got
            dh = matmul_nt(f"sw_dh{i}", [(dp, w_sw_in)])
            d_in = matmul_tn(f"sw_dwi{i}", h, dp)
            pending = [(SCATTER, _scatter_cols(d_in[None])), (SCATTER, _scatter_rows(d_out[None]))]
        else:
            xq, zz, ab, cv, u, st = saved[i]
            du = matmul_nt(f"gd_du{i}", [(dy, w_gd_out)])
            d_out = matmul_tn(f"gd_dwo{i}", u, dy)
            dcv, dz, dab, d_al, d_dt, d_gd_on, got = gd_bwd(f"gd_bwd{i}", cv, zz, ab, gd_al, gd_dt, row(gd_onorm, j), st, du,
                                                            pending)
            rs_hg_in[1], rs_hg_out[1] = got
            dxq, d_conv8 = conv_bwd(f"gd_dconv{i}", xq, dcv, conv_w8)
            dh = matmul_nt(f"gd_dh{i}", [(dxq, w_gd_qkv), (dz, w_gd_z), (dab, w_gd_ab)])
            d_in = jnp.concatenate([matmul_tn(f"gd_dwq{i}", h, dxq), matmul_tn(f"gd_dwz{i}", h, dz),
                                    matmul_tn(f"gd_dwab{i}", h, dab)[:, :2 * GD_VH]], axis=1)
            pending = [(SCATTER, _scatter_cols(d_in[None])), (SCATTER, _scatter_rows(d_out[None])),
                       (SCATTER, _scatter_cols(d_conv8[None, :CONV_K], F32))]
        if i > 0:
            dx, dy, dg_norm[i], dscale[i], dshift[i], dgate[i - 1] = bwd_rowwise(
                f"bwd_row{i}", xs[i], dh, dx, row(norm_g, i), row(scale, i), row(shift, i), ys[i - 1], row(gate, i - 1))
        else:
            dx, dg_norm[i], dscale[i], dshift[i] = bwd_rowwise(
                f"bwd_row{i}", xs[i], dh, dx, row(norm_g, i), row(scale, i), row(shift, i))
    grad_x = dx[None]

    dmod = jnp.concatenate([_stack_rows(dshift), _stack_rows(dscale), _stack_rows(dgate)], axis=1)
    misc = _stack_rows(d_hg_on + [d_gd_on, _pad_lanes(d_qn), _pad_lanes(d_kn), _pad_lanes(d_sk), d_al, d_dt])
    small = jnp.concatenate([_stack_rows(dlb).reshape(-1, 128), _stack_rows(dg_norm).reshape(-1, 128),
                             dmod.reshape(-1, 128), misc], axis=0)
    small_all = all_gather("ag_small", small)
    n_lb = n_layers * D // 128
    n_mod = n_layers * 3 * D // 128
    o = 0
    dlb_parts = small_all[:, o:o + n_lb].reshape(N_DEV, n_layers, D); o += n_lb
    dgn_parts = small_all[:, o:o + n_lb].reshape(N_DEV, n_layers, D); o += n_lb
    dmod_parts = small_all[:, o:o + n_mod].reshape(N_DEV, n_layers, 3 * D); o += n_mod
    dhgon_parts = small_all[:, o:o + 2]; o += 2
    dgdon_parts = small_all[:, o:o + 1]; o += 1
    dqn_parts = small_all[:, o:o + 1, :SW_DH]; o += 1
    dkn_parts = small_all[:, o:o + 1, :SW_DH]; o += 1
    dsk_parts = small_all[:, o:o + 1, :SW_KV * SW_G]; o += 1
    dal_parts = small_all[:, o:o + 1, :GD_VH]; o += 1
    ddt_parts = small_all[:, o:o + 1, :GD_VH]; o += 1

    g_lb = lb_grad("lb_grad", hgrn_lb, dlb_parts)
    dm_mine = lax.dynamic_slice_in_dim(dmod_parts, me * aw, aw, axis=2)
    g_ada_w = ada_grad("ada_grad", jnp.transpose(c_all), jnp.transpose(dm_mine, (1, 0, 2)))

    res = {}
    res["hgrn_lb"] = adamw("aw_hgrn_lb", hgrn_lb, m_hgrn_lb, v_hgrn_lb, g_lb[None])
    res["ada_w"] = adamw("aw_ada_w", ada_w, m_ada_w, v_ada_w, g_ada_w[None])
    res["ada_b"] = adamw("aw_ada_b", ada_b, m_ada_b, v_ada_b, dmod_parts)
    res["norm_g"] = adamw("aw_norm_g", norm_g, m_norm_g, v_norm_g, dgn_parts)
    res["hg_in_w"] = adamw("aw_hg_in", hg_in_w, m_hg_in_w, v_hg_in_w, jnp.concatenate(rs_hg_in, axis=1))
    res["hg_out_w"] = adamw("aw_hg_out", hg_out_w, m_hg_out_w, v_hg_out_w, jnp.concatenate(rs_hg_out, axis=1))
    res["hg_onorm"] = adamw("aw_hg_onorm", hg_onorm, m_hg_onorm, v_hg_onorm, dhgon_parts)
    res["sw_in_w"] = adamw("aw_sw_in", sw_in_w, m_sw_in_w, v_sw_in_w, rs_sw_in)
    res["sw_out_w"] = adamw("aw_sw_out", sw_out_w, m_sw_out_w, v_sw_out_w, rs_sw_out)
    res["sw_qnorm"] = adamw("aw_sw_qn", sw_qnorm, m_sw_qnorm, v_sw_qnorm, dqn_parts)
    res["sw_knorm"] = adamw("aw_sw_kn", sw_knorm, m_sw_knorm, v_sw_knorm, dkn_parts)
    res["sw_sinks"] = adamw("aw_sw_sinks", sw_sinks, m_sw_sinks, v_sw_sinks, dsk_parts)
    res["gd_in_w"] = adamw("aw_gd_in", gd_in_w, m_gd_in_w, v_gd_in_w, rs_gd_in)
    res["gd_out_w"] = adamw("aw_gd_out", gd_out_w, m_gd_out_w, v_gd_out_w, rs_gd_out)
    res["gd_conv_w"] = adamw("aw_gd_conv", gd_conv_w, m_gd_conv_w, v_gd_conv_w, rs_gd_conv)
    res["gd_a_log"] = adamw("aw_gd_alog", gd_a_log, m_gd_a_log, v_gd_a_log, dal_parts)
    res["gd_dt_bias"] = adamw("aw_gd_dt", gd_dt_bias, m_gd_dt_bias, v_gd_dt_bias, ddt_parts)
    res["gd_onorm"] = adamw("aw_gd_onorm", gd_onorm, m_gd_onorm, v_gd_onorm, dgdon_parts)

    order = ["hgrn_lb", "ada_w", "ada_b", "norm_g", "hg_in_w", "hg_out_w", "hg_onorm", "sw_in_w", "sw_out_w", "sw_qnorm",
             "sw_knorm", "sw_sinks", "gd_in_w", "gd_out_w", "gd_conv_w", "gd_a_log", "gd_dt_bias", "gd_onorm"]
    outs = [loss, grad_x]
    for part in range(4):
        outs += [res[n][part] for n in order]
    return tuple(outs)
```

```python
import functools

import jax
import jax.numpy as jnp
from jax import lax
from jax.experimental import pallas as pl
from jax.experimental.pallas import tpu as pltpu

F32, BF16 = jnp.float32, jnp.bfloat16
HI = lax.Precision.HIGHEST
MESH = pl.DeviceIdType.MESH
N_DEV = 8
D = 1024
EPS = 1e-6
NEG_INF = float("-inf")

HG_HEADS, HG_DH, HG_BLK, HG_CH, HG_SUB, HG_HB = 8, 128, 128, 32, 8, 8
SW_KV, SW_G, SW_DH, SW_BLK = 4, 4, 64, 128
SW_QW, SW_KVW = 1024, 256
GD_QK_HEADS, GD_DK, GD_CHUNK = 8, 128, 64
GD_QKW, GD_VW, GD_QKV = 1024, 2048, 4096
CONV_K = 4
ROPE_THETA = 10000.0

ADAM_LR, ADAM_B1, ADAM_B2, ADAM_EPS, ADAM_WD, ADAM_STEP = 0.001, 0.9, 0.999, 1e-08, 0.01, 10

VMEM_LIMIT = 56 * 1024 * 1024


def _params(**kw):
    return pltpu.CompilerParams(vmem_limit_bytes=VMEM_LIMIT, **kw)


def _dot(a, b, ca, cb):
    return lax.dot_general(a.astype(BF16), b.astype(BF16), (((ca,), (cb,)), ((), ())), preferred_element_type=F32)


def mm(a, b):
    return _dot(a, b, 1, 0)


def mm_nt(a, b):
    return _dot(a, b, 1, 1)


def mm_tn(a, b):
    return _dot(a, b, 0, 0)


def mmf(a, b):
    return lax.dot_general(a, b, (((1,), (0,)), ((), ())), precision=HI, preferred_element_type=F32)


def _split3(x):
    h1 = x.astype(BF16)
    r1 = x - h1.astype(F32)
    h2 = r1.astype(BF16)
    h3 = (r1 - h2.astype(F32)).astype(BF16)
    return h1, h2, h3


def _lin01_l(m, x):
    mb = m.astype(BF16)
    return sum(lax.dot_general(mb, p, (((1,), (0,)), ((), ())), preferred_element_type=F32) for p in _split3(x))


def _lin01_r(x, m):
    mb = m.astype(BF16)
    return sum(lax.dot_general(p, mb, (((1,), (0,)), ((), ())), preferred_element_type=F32) for p in _split3(x))


def _bdot(a, b, ca, cb, prec=None):
    return lax.dot_general(a, b, (((ca,), (cb,)), ((0,), (0,))), precision=prec, preferred_element_type=F32)


def bmm(a, b):
    return _bdot(a.astype(BF16), b.astype(BF16), 2, 1)


def bmm_nt(a, b):
    return _bdot(a.astype(BF16), b.astype(BF16), 2, 2)


def bmm_tn(a, b):
    return _bdot(a.astype(BF16), b.astype(BF16), 1, 1)


def bmmf(a, b):
    return _bdot(a, b, 2, 1, lax.Precision.HIGH)


def _blin01(m, x):
    mb = m.astype(BF16)
    return sum(_bdot(mb, p, 2, 1) for p in _split3(x))


@jax.custom_vjp
def blin01(m, mt, x):
    return _blin01(m, x)


def _blin01_fwd(m, mt, x):
    return _blin01(m, x), (m, mt)


def _blin01_bwd(res, g):
    m, mt = res
    return jnp.zeros_like(m), jnp.zeros_like(mt), _blin01(mt, g)


blin01.defvjp(_blin01_fwd, _blin01_bwd)


def _blin01_r(x, m):
    mb = m.astype(BF16)
    return sum(_bdot(p, mb, 2, 1) for p in _split3(x))


@jax.custom_vjp
def blin01_r(x, m, mt):
    return _blin01_r(x, m)


def _blin01_r_fwd(x, m, mt):
    return _blin01_r(x, m), (m, mt)


def _blin01_r_bwd(res, g):
    m, mt = res
    return _blin01_r(g, mt), jnp.zeros_like(m), jnp.zeros_like(mt)


blin01_r.defvjp(_blin01_r_fwd, _blin01_r_bwd)


@jax.custom_vjp
def lin01_l(m, mt, x):
    return _lin01_l(m, x)


def _lin01_l_fwd(m, mt, x):
    return _lin01_l(m, x), (m, mt)


def _lin01_l_bwd(res, g):
    m, mt = res
    return jnp.zeros_like(m), jnp.zeros_like(mt), _lin01_l(mt, g)


lin01_l.defvjp(_lin01_l_fwd, _lin01_l_bwd)


@jax.custom_vjp
def lin01_r(x, m, mt):
    return _lin01_r(x, m)


def _lin01_r_fwd(x, m, mt):
    return _lin01_r(x, m), (m, mt)


def _lin01_r_bwd(res, g):
    m, mt = res
    return _lin01_r(g, mt), jnp.zeros_like(m), jnp.zeros_like(mt)


lin01_r.defvjp(_lin01_r_fwd, _lin01_r_bwd)


@functools.partial(jax.custom_vjp, nondiff_argnums=(1, 2))
def rows(x, start, size):
    return lax.slice_in_dim(x, start, start + size, axis=x.ndim - 2)


def _rows_fwd(x, start, size):
    return lax.slice_in_dim(x, start, start + size, axis=x.ndim - 2), jnp.zeros(x.shape[:-1] + (1,), F32)


def _rows_bwd(start, size, res, g):
    ax = g.ndim - 2
    total = res.shape[ax]
    zeros = lambda n: jnp.zeros(g.shape[:ax] + (n,) + g.shape[ax + 1:], g.dtype)
    parts = []
    if start > 0:
        parts.append(zeros(start))
    parts.append(g)
    if total - start - size > 0:
        parts.append(zeros(total - start - size))
    return (jnp.concatenate(parts, axis=ax) if len(parts) > 1 else g,)


rows.defvjp(_rows_fwd, _rows_bwd)


@functools.partial(jax.custom_vjp, nondiff_argnums=(1,))
def rowsel(x, s):
    return lax.slice_in_dim(x, s, s + 1, axis=x.ndim - 2)


def _rowsel_fwd(x, s):
    return lax.slice_in_dim(x, s, s + 1, axis=x.ndim - 2), jnp.zeros(x.shape[:-1] + (1,), F32)


def _rowsel_bwd(s, res, g):
    r = lax.broadcasted_iota(jnp.int32, res.shape, res.ndim - 2)
    return (jnp.where(r == s, g, 0.0),)


rowsel.defvjp(_rowsel_fwd, _rowsel_bwd)


def sigmoid(x):
    return jax.nn.sigmoid(x)


def silu(x):
    return x * jax.nn.sigmoid(x)


def softplus(x):
    return jnp.maximum(x, 0.0) + jnp.log1p(jnp.exp(-jnp.abs(x)))


def _iota(shape, dim):
    return lax.broadcasted_iota(jnp.int32, shape, dim)


def norm_mod_fn(g, sc, sh, x):
    r = lax.rsqrt(jnp.mean(x * x, axis=-1, keepdims=True) + EPS)
    return (x * r * g) * (1.0 + sc) + sh


def _row_spec(tm, f):
    return pl.BlockSpec((tm, f), lambda i: (i, 0))


def _const_spec(shape):
    nd = len(shape)
    return pl.BlockSpec(shape, lambda i: (0,) * nd)


def _tiled(name, fn, consts, rows_in, row_outs, acc_outs=(), tm=512, carry=()):
    T = rows_in[0].shape[0]
    tm = min(tm, T)
    n_c, n_r, n_ro = len(consts), len(rows_in), len(row_outs)

    def body(*refs):
        c_refs, r_refs = refs[:n_c], refs[n_c:n_c + n_r]
        ro_refs, ao_refs = refs[n_c + n_r:n_c + n_r + n_ro], refs[n_c + n_r + n_ro:]
        outs = fn(*[r[...] for r in c_refs], *[r[...] for r in r_refs])
        for r, v in zip(ro_refs, outs[:n_ro]):
            r[...] = v.astype(r.dtype)
        if ao_refs:
            @pl.when(pl.program_id(0) == 0)
            def _():
                for r in ao_refs:
                    r[...] = jnp.zeros(r.shape, r.dtype)
            for r, v in zip(ao_refs, outs[n_ro:]):
                r[...] += v

    out_shape = [jax.ShapeDtypeStruct((T, f), dt) for f, dt in row_outs] + [jax.ShapeDtypeStruct(s, F32) for s in acc_outs]
    out_specs = [_row_spec(tm, f) for f, _ in row_outs] + [_const_spec(s) for s in acc_outs]
    in_specs = [_const_spec(c.shape) for c in consts] + [_row_spec(tm, r.shape[1]) for r in rows_in]
    return _scan_call(body, name, T // tm, in_specs, out_specs, out_shape, [], (*consts, *rows_in), carry)


def norm_mod(name, x, g, sc, sh):
    (h,), _ = _tiled(name, lambda g, sc, sh, x: (norm_mod_fn(g, sc, sh, x),), [g, sc, sh], [x], [(D, BF16)])
    return h


def out_resid_norm(name, u, w, x, gate, g, sc, sh):
    def fn(gate, g, sc, sh, w, x, u):
        y = mm(u, w)
        xn = x + gate * y
        return y, xn, norm_mod_fn(g, sc, sh, xn)
    return _tiled(name, fn, [gate, g, sc, sh, w], [x, u], [(D, F32), (D, F32), (D, BF16)])[0]


def out_loss(name, u, w, x, tgt, gate):
    def fn(gate, w, x, tgt, u):
        y = mm(u, w)
        err = x + gate * y - tgt
        dx = err * (1.0 / D)
        per_tok = jnp.sum(err * err, axis=-1, keepdims=True) * (0.5 / D)
        loss = jnp.sum(per_tok, axis=0, keepdims=True)
        return y, dx, gate * dx, jnp.broadcast_to(loss, (1, 128)), jnp.sum(dx * y, axis=0, keepdims=True)
    return _tiled(name, fn, [gate, w], [x, tgt, u], [(D, F32), (D, F32), (D, BF16)], [(1, 128), (1, D)])[0]


def dh_bwd_row(name, pairs, x, dxn, g, sc, sh, y_prev=None, gate_prev=None, carry=()):
    with_prev = y_prev is not None
    n = len(pairs)

    def fn(*a):
        a = list(a)
        gate_p = a.pop(0) if with_prev else None
        g, sc, sh = a[:3]
        ws, a = a[3:3 + n], a[3 + n:]
        x, dxn = a[:2]
        yp = a[2] if with_prev else None
        dps = a[3:] if with_prev else a[2:]
        dh = mm_nt(dps[0], ws[0])
        for dp, w in zip(dps[1:], ws[1:]):
            dh = dh + mm_nt(dp, w)
        _, f = jax.vjp(norm_mod_fn, g, sc, sh, x)
        dg, dsc, dsh, dx = f(dh)
        dx = dx + dxn
        if with_prev:
            return dx, gate_p * dx, dg, dsc, dsh, jnp.sum(dx * yp, axis=0, keepdims=True)
        return dx, dg, dsc, dsh

    ws, dps = [w for _, w in pairs], [dp for dp, _ in pairs]
    tm = MM_ROWS_WIDE if sum(dp.shape[1] for dp in dps) > MM_WIDE else MM_ROWS_NARROW
    if with_prev:
        return _tiled(name, fn, [gate_prev, g, sc, sh] + ws, [x, dxn, y_prev] + dps, [(D, F32), (D, BF16)], [(1, D)] * 4,
                      tm=tm, carry=carry)
    return _tiled(name, fn, [g, sc, sh] + ws, [x, dxn] + dps, [(D, F32)], [(1, D)] * 3, tm=tm, carry=carry)


MM_ROWS_WIDE, MM_ROWS_NARROW, MM_WIDE = 256, 512, 2048


def matmul(name, a, w, out_dtype=F32):
    T, K = a.shape
    N = w.shape[1]
    tm = min(MM_ROWS_NARROW if N <= MM_WIDE else MM_ROWS_WIDE, T)

    def body(a_ref, w_ref, o_ref):
        o_ref[...] = mm(a_ref[...], w_ref[...]).astype(o_ref.dtype)

    return pl.pallas_call(
        body, name=name, grid=(T // tm,),
        in_specs=[_row_spec(tm, K), _const_spec((K, N))], out_specs=_row_spec(tm, N),
        out_shape=jax.ShapeDtypeStruct((T, N), out_dtype),
        compiler_params=_params(dimension_semantics=("arbitrary",)),
    )(a, w)


def matmul_nt(name, pairs, out_dtype=F32):
    T = pairs[0][0].shape[0]
    K = pairs[0][1].shape[0]
    wide = sum(a.shape[1] for a, _ in pairs) > MM_WIDE
    tm = min(MM_ROWS_WIDE if wide else MM_ROWS_NARROW, T)
    n = len(pairs)

    def body(*refs):
        o_ref = refs[-1]
        acc = None
        for i in range(n):
            p = mm_nt(refs[2 * i][...], refs[2 * i + 1][...])
            acc = p if acc is None else acc + p
        o_ref[...] = acc.astype(o_ref.dtype)

    in_specs, args = [], []
    for a, w in pairs:
        in_specs += [_row_spec(tm, a.shape[1]), _const_spec(w.shape)]
        args += [a, w]
    return pl.pallas_call(
        body, name=name, grid=(T // tm,), in_specs=in_specs, out_specs=_row_spec(tm, K),
        out_shape=jax.ShapeDtypeStruct((T, K), out_dtype),
        compiler_params=_params(dimension_semantics=("arbitrary",)),
    )(*args)


def matmul_tn(name, a, b, tm=512, tn=2048):
    T, K = a.shape
    N = b.shape[1]
    tm = min(tm, T)
    tn = max(t for t in range(128, min(tn, N) + 1, 128) if N % t == 0)

    def body(a_ref, b_ref, o_ref):
        @pl.when(pl.program_id(1) == 0)
        def _():
            o_ref[...] = jnp.zeros(o_ref.shape, F32)
        o_ref[...] += mm_tn(a_ref[...], b_ref[...])

    return pl.pallas_call(
        body, name=name, grid=(N // tn, T // tm),
        in_specs=[pl.BlockSpec((tm, K), lambda j, i: (i, 0)), pl.BlockSpec((tm, tn), lambda j, i: (i, j))],
        out_specs=pl.BlockSpec((K, tn), lambda j, i: (0, j)),
        out_shape=jax.ShapeDtypeStruct((K, N), F32),
        compiler_params=_params(dimension_semantics=("arbitrary", "arbitrary")),
    )(a, b)


def hg_heads(tri, trit, lb, og, st, qp, fp, iv, z):
    blk = qp.shape[1]
    q = silu(qp)
    lf = jnp.log(lb + (1.0 - lb) * sigmoid(fp))
    k = (1.0 - lb) * sigmoid(-fp)
    r = _iota((1, HG_SUB, 1), 1)
    rc = _iota((1, HG_CH, 1), 1)
    outs = []
    for c in range(blk // HG_CH):
        qc, kc, vc, lfc = (rows(a, c * HG_CH, HG_CH) for a in (q, k, iv, lf))
        b = blin01(tri, trit, lfc)
        o_state = bmm_nt(qc * jnp.exp(b), st)
        parts = []
        for i in range(HG_CH // HG_SUB):
            qi, ki, vi, bi = (rows(a, i * HG_SUB, HG_SUB) for a in (qc, kc, vc, b))
            o = rows(o_state, i * HG_SUB, HG_SUB)
            for s in range(HG_SUB):
                ks, bs, vs = rowsel(ki, s), rowsel(bi, s), rowsel(vi, s)
                e = jnp.exp(jnp.where(r >= s, bi - bs, NEG_INF))
                o = o + jnp.sum(qi * ks * e, axis=-1, keepdims=True) * vs
            if i > 0:
                rb = rowsel(b, i * HG_SUB - 1)
                kt = kc * jnp.exp(jnp.where(rc < i * HG_SUB, rb - b, NEG_INF))
                o = o + bmm(bmm_nt(qi * jnp.exp(bi - rb), kt), vc)
            parts.append(o)
        bl = rowsel(b, HG_CH - 1)
        st = st * jnp.exp(bl) + bmm_tn(vc, kc * jnp.exp(bl - b))
        outs.append(jnp.concatenate(parts, axis=1))
    o = jnp.concatenate(outs, axis=1)
    on = o * lax.rsqrt(jnp.mean(o * o, axis=-1, keepdims=True) + EPS) * og
    return st, on * silu(z)


def _tri(n, nh):
    t = (_iota((nh, n, n), 1) >= _iota((nh, n, n), 2)).astype(F32)
    tt = (_iota((nh, n, n), 1) <= _iota((nh, n, n), 2)).astype(F32)
    return t, tt


def _hg_load(p_ref, lb_ref, heads):
    col = lambda ref, base: jnp.stack([ref[:, pl.ds(base + h * HG_DH, HG_DH)] for h in heads])
    return col(lb_ref, 0), col(p_ref, 0), col(p_ref, D), col(p_ref, 2 * D), col(p_ref, 3 * D)


def hg_fwd(name, p, lb, og, carry=()):
    T = p.shape[0]
    nb = T // HG_BLK

    def body(p_ref, lb_ref, og_ref, u_ref, st_ref, s_scr):
        @pl.when(pl.program_id(0) == 0)
        def _():
            s_scr[...] = jnp.zeros(s_scr.shape, F32)
        st_ref[0] = s_scr[...]
        tri, trit = _tri(HG_CH, HG_HB)
        og_v = og_ref[...]
        for b0 in range(0, HG_HEADS, HG_HB):
            heads = list(range(b0, b0 + HG_HB))
            lbh, qp, fp, iv, z = _hg_load(p_ref, lb_ref, heads)
            st, u = hg_heads(tri, trit, lbh, og_v, s_scr[pl.ds(b0, HG_HB)], qp, fp, iv, z)
            s_scr[pl.ds(b0, HG_HB)] = st
            for i, h in enumerate(heads):
                u_ref[:, pl.ds(h * HG_DH, HG_DH)] = u[i].astype(BF16)

    (u, st), ex = _scan_call(
        body, name, nb,
        [_row_spec(HG_BLK, 4 * D), _const_spec((1, D)), _const_spec((1, HG_DH))],
        [_row_spec(HG_BLK, D), pl.BlockSpec((1, HG_HEADS, HG_DH, HG_DH), lambda i: (i, 0, 0, 0))],
        [jax.ShapeDtypeStruct((T, D), BF16), jax.ShapeDtypeStruct((nb, HG_HEADS, HG_DH, HG_DH), F32)],
        [pltpu.VMEM((HG_HEADS, HG_DH, HG_DH), F32)], (p, lb, og), carry)
    return u, st, ex


def hg_bwd(name, p, lb, og, states, du, carry=()):
    T = p.shape[0]
    nb = T // HG_BLK

    def body(p_ref, lb_ref, og_ref, st_ref, du_ref, dp_ref, dlb_ref, dog_ref, ds_scr):
        @pl.when(pl.program_id(0) == 0)
        def _():
            ds_scr[...] = jnp.zeros(ds_scr.shape, F32)
            dlb_ref[...] = jnp.zeros(dlb_ref.shape, F32)
            dog_ref[...] = jnp.zeros(dog_ref.shape, F32)
        tri, trit = _tri(HG_CH, HG_HB)
        og_v = og_ref[...]
        fn = functools.partial(hg_heads, tri, trit)
        for b0 in range(0, HG_HEADS, HG_HB):
            heads = list(range(b0, b0 + HG_HB))
            lbh, qp, fp, iv, z = _hg_load(p_ref, lb_ref, heads)
            _, f = jax.vjp(fn, lbh, og_v, st_ref[0, pl.ds(b0, HG_HB)], qp, fp, iv, z)
            du_h = jnp.stack([du_ref[:, pl.ds(h * HG_DH, HG_DH)].astype(F32) for h in heads])
            dlb, dog, dst, dq, df, di, dz = f((ds_scr[pl.ds(b0, HG_HB)], du_h))
            ds_scr[pl.ds(b0, HG_HB)] = dst
            dog_ref[...] += dog
            for i, h in enumerate(heads):
                for base, g in ((0, dq), (D, df), (2 * D, di), (3 * D, dz)):
                    dp_ref[:, pl.ds(base + h * HG_DH, HG_DH)] = g[i].astype(BF16)
                dlb_ref[:, pl.ds(h * HG_DH, HG_DH)] += dlb[i]

    rev = lambda i: (nb - 1 - i, 0)
    outs, ex = _scan_call(
        body, name, nb,
        [pl.BlockSpec((HG_BLK, 4 * D), rev), _const_spec((1, D)), _const_spec((1, HG_DH)),
         pl.BlockSpec((1, HG_HEADS, HG_DH, HG_DH), lambda i: (nb - 1 - i, 0, 0, 0)), pl.BlockSpec((HG_BLK, D), rev)],
        [pl.BlockSpec((HG_BLK, 4 * D), rev), _const_spec((1, D)), _const_spec((1, HG_DH))],
        [jax.ShapeDtypeStruct((T, 4 * D), BF16), jax.ShapeDtypeStruct((1, D), F32), jax.ShapeDtypeStruct((1, HG_DH), F32)],
        [pltpu.VMEM((HG_HEADS, HG_DH, HG_DH), F32)], (p, lb, og, states, du), carry)
    return (*outs, ex)


def swa_heads(qn, kn, sinks, kprev, vprev, qp4, kp, v, z4, cos4, sin4, cosk, sink_, hots, mask, p64, p64t):
    def norm_rope(xp, g, cs, sn):
        y = xp * lax.rsqrt(jnp.mean(xp * xp, axis=-1, keepdims=True) + EPS) * g
        return y * cs + blin01_r(y, p64, p64t) * sn

    q = norm_rope(qp4, qn, cos4, sin4)
    k = norm_rope(kp, kn, cosk, sink_)
    k2 = jnp.concatenate([kprev, k], axis=1)
    v2 = jnp.concatenate([vprev, v], axis=1)
    s = jnp.where(mask, bmm_nt(q, k2) * (SW_DH ** -0.5), NEG_INF)
    sink_col = jnp.concatenate([jnp.concatenate(
        [jnp.broadcast_to(jnp.sum(sinks * hot, axis=-1, keepdims=True), (SW_BLK, 1)) for hot in hot4], axis=0)[None]
        for hot4 in hots], axis=0)
    m = lax.stop_gradient(jnp.maximum(jnp.max(s, axis=-1, keepdims=True), sink_col))
    p = jnp.exp(s - m)
    p = p / (jnp.sum(p, axis=-1, keepdims=True) + jnp.exp(sink_col - m))
    o = bmm(p, v2)
    return k, v, o * silu(z4)


def _swa_consts(first):
    qi = _iota((SW_G * SW_BLK, 2 * SW_BLK), 0) % SW_BLK
    kj = _iota((SW_G * SW_BLK, 2 * SW_BLK), 1)
    rel = qi + SW_BLK - kj
    mask = (rel >= 0) & (rel < SW_BLK) & ((kj >= SW_BLK) | jnp.logical_not(first))
    i, j = _iota((SW_KV, SW_DH, SW_DH), 1), _iota((SW_KV, SW_DH, SW_DH), 2)
    p64 = (i == (j + SW_DH // 2) % SW_DH).astype(F32)
    p64t = (j == (i + SW_DH // 2) % SW_DH).astype(F32)
    lane = _iota((1, SW_KV * SW_G), 1)
    hots = [[(lane == h * SW_G + g).astype(F32) for g in range(SW_G)] for h in range(SW_KV)]
    return mask, p64, p64t, hots


def _stack_q(ref, h, base):
    return jnp.concatenate([ref[:, pl.ds(base + h * SW_G * SW_DH + g * SW_DH, SW_DH)] for g in range(SW_G)], axis=0)


def _swa_load(p_ref):
    heads = range(SW_KV)
    return (jnp.stack([_stack_q(p_ref, h, Q0) for h in heads]),
            jnp.stack([p_ref[:, pl.ds(K0 + h * SW_DH, SW_DH)] for h in heads]),
            jnp.stack([p_ref[:, pl.ds(V0 + h * SW_DH, SW_DH)] for h in heads]),
            jnp.stack([_stack_q(p_ref, h, Z0) for h in heads]))


Q0, K0, V0, Z0 = 0, SW_QW, SW_QW + SW_KVW, SW_QW + 2 * SW_KVW
SW_IN = 2 * SW_QW + 2 * SW_KVW


def swa_fwd(name, p, cos, sin, qn, kn, sinks):
    T = p.shape[0]
    nb = T // SW_BLK

    def body(p_ref, cos_ref, sin_ref, qn_ref, kn_ref, sk_ref, u_ref, ko_ref, kprev, vprev):
        first = pl.program_id(0) == 0

        @pl.when(first)
        def _():
            kprev[...] = jnp.zeros(kprev.shape, F32)
            vprev[...] = jnp.zeros(vprev.shape, F32)
        mask, p64, p64t, hots = _swa_consts(first)
        cs, sn = cos_ref[...], sin_ref[...]
        cs4, sn4 = jnp.concatenate([cs] * SW_G, axis=0), jnp.concatenate([sn] * SW_G, axis=0)
        qp4, kp, vv, z4 = _swa_load(p_ref)
        k, v, u4 = swa_heads(qn_ref[...], kn_ref[...], sk_ref[...], kprev[...], vprev[...], qp4, kp, vv, z4,
                             cs4, sn4, cs, sn, hots, mask, p64, p64t)
        kprev[...] = k
        vprev[...] = v
        for h in range(SW_KV):
            ko_ref[:, pl.ds(h * SW_DH, SW_DH)] = k[h]
            uh = u4[h]
            for g in range(SW_G):
                u_ref[:, pl.ds(h * SW_G * SW_DH + g * SW_DH, SW_DH)] = uh[g * SW_BLK:(g + 1) * SW_BLK].astype(BF16)

    return pl.pallas_call(
        body, name=name, grid=(nb,),
        in_specs=[_row_spec(SW_BLK, SW_IN), _row_spec(SW_BLK, SW_DH), _row_spec(SW_BLK, SW_DH),
                  _const_spec((1, SW_DH)), _const_spec((1, SW_DH)), _const_spec((1, SW_KV * SW_G))],
        out_specs=[_row_spec(SW_BLK, SW_QW), _row_spec(SW_BLK, SW_KVW)],
        out_shape=[jax.ShapeDtypeStruct((T, SW_QW), BF16), jax.ShapeDtypeStruct((T, SW_KVW), F32)],
        scratch_shapes=[pltpu.VMEM((SW_KV, SW_BLK, SW_DH), F32), pltpu.VMEM((SW_KV, SW_BLK, SW_DH), F32)],
        compiler_params=_params(dimension_semantics=("arbitrary",)),
    )(p, cos, sin, qn, kn, sinks)


def swa_bwd(name, p, kpost, cos, sin, qn, kn, sinks, du, carry=()):
    T = p.shape[0]
    nb = T // SW_BLK

    def body(p_ref, pprev_ref, kprev_ref, cos_ref, sin_ref, qn_ref, kn_ref, sk_ref, du_ref,
             dp_ref, dqn_ref, dkn_ref, dsk_ref, dk_scr, dv_scr):
        i = pl.program_id(0)
        first = i == nb - 1

        @pl.when(i == 0)
        def _():
            dk_scr[...] = jnp.zeros(dk_scr.shape, F32)
            dv_scr[...] = jnp.zeros(dv_scr.shape, F32)
            dqn_ref[...] = jnp.zeros(dqn_ref.shape, F32)
            dkn_ref[...] = jnp.zeros(dkn_ref.shape, F32)
            dsk_ref[...] = jnp.zeros(dsk_ref.shape, F32)
        mask, p64, p64t, hots = _swa_consts(first)
        cs, sn = cos_ref[...], sin_ref[...]
        cs4, sn4 = jnp.concatenate([cs] * SW_G, axis=0), jnp.concatenate([sn] * SW_G, axis=0)
        heads = range(SW_KV)
        fn = lambda qn, kn, sk, kpv, vpv, qp4, kp, v, z4: swa_heads(
            qn, kn, sk, kpv, vpv, qp4, kp, v, z4, cs4, sn4, cs, sn, hots, mask, p64, p64t)
        kpv = jnp.stack([kprev_ref[:, pl.ds(h * SW_DH, SW_DH)] for h in heads])
        vpv = jnp.stack([pprev_ref[:, pl.ds(V0 + h * SW_DH, SW_DH)] for h in heads])
        _, f = jax.vjp(fn, qn_ref[...], kn_ref[...], sk_ref[...], kpv, vpv, *_swa_load(p_ref))
        du4 = jnp.stack([_stack_q(du_ref, h, 0).astype(F32) for h in heads])
        dqn, dkn, dsk, dkpv, dvpv, dq4, dkp, dv, dz4 = f((dk_scr[...], dv_scr[...], du4))
        dk_scr[...] = dkpv
        dv_scr[...] = dvpv
        for h in heads:
            dp_ref[:, pl.ds(K0 + h * SW_DH, SW_DH)] = dkp[h].astype(BF16)
            dp_ref[:, pl.ds(V0 + h * SW_DH, SW_DH)] = dv[h].astype(BF16)
            dqh, dzh = dq4[h], dz4[h]
            for g in range(SW_G):
                c0 = h * SW_G * SW_DH + g * SW_DH
                dp_ref[:, pl.ds(Q0 + c0, SW_DH)] = dqh[g * SW_BLK:(g + 1) * SW_BLK].astype(BF16)
                dp_ref[:, pl.ds(Z0 + c0, SW_DH)] = dzh[g * SW_BLK:(g + 1) * SW_BLK].astype(BF16)
        dqn_ref[...] += dqn
        dkn_ref[...] += dkn
        dsk_ref[...] += dsk

    rev = lambda i: (nb - 1 - i, 0)
    prev = lambda i: (jnp.maximum(nb - 2 - i, 0), 0)
    outs, ex = _scan_call(
        body, name, nb,
        [pl.BlockSpec((SW_BLK, SW_IN), rev), pl.BlockSpec((SW_BLK, SW_IN), prev),
         pl.BlockSpec((SW_BLK, SW_KVW), prev), pl.BlockSpec((SW_BLK, SW_DH), rev), pl.BlockSpec((SW_BLK, SW_DH), rev),
         _const_spec((1, SW_DH)), _const_spec((1, SW_DH)), _const_spec((1, SW_KV * SW_G)), pl.BlockSpec((SW_BLK, SW_QW), rev)],
        [pl.BlockSpec((SW_BLK, SW_IN), rev), _const_spec((1, SW_DH)), _const_spec((1, SW_DH)), _const_spec((1, SW_KV * SW_G))],
        [jax.ShapeDtypeStruct((T, SW_IN), BF16), jax.ShapeDtypeStruct((1, SW_DH), F32),
         jax.ShapeDtypeStruct((1, SW_DH), F32), jax.ShapeDtypeStruct((1, SW_KV * SW_G), F32)],
        [pltpu.VMEM((SW_KV, SW_BLK, SW_DH), F32), pltpu.VMEM((SW_KV, SW_BLK, SW_DH), F32)],
        (p, p, kpost, cos, sin, qn, kn, sinks, du), carry)
    return (*outs, ex)


CONV_TC, CONV_TL, HALO = 512, 1024, 8


def _conv_taps(xe, w_ref):
    acc = w_ref[pl.ds(CONV_K - 1, 1), :] * xe
    for k in range(CONV_K - 1):
        acc = acc + w_ref[pl.ds(k, 1), :] * pltpu.roll(xe, CONV_K - 1 - k, 0)
    return acc


def conv_fwd(name, x, w8):
    T, C = x.shape
    tc = min(CONV_TC, T)
    nt = T // tc

    def body(xp_ref, x_ref, w_ref, o_ref):
        prev = jnp.where(pl.program_id(0) == 0, 0.0, xp_ref[...])
        xe = jnp.concatenate([prev, x_ref[...]], axis=0)
        o_ref[...] = silu(_conv_taps(xe, w_ref)[HALO:])

    return pl.pallas_call(
        body, name=name, grid=(nt, C // CONV_TL),
        in_specs=[pl.BlockSpec((HALO, CONV_TL), lambda i, j: (jnp.maximum(i * (tc // HALO) - 1, 0), j)),
                  pl.BlockSpec((tc, CONV_TL), lambda i, j: (i, j)), pl.BlockSpec((HALO, CONV_TL), lambda i, j: (0, j))],
        out_specs=pl.BlockSpec((tc, CONV_TL), lambda i, j: (i, j)),
        out_shape=jax.ShapeDtypeStruct((T, C), F32),
        compiler_params=_params(dimension_semantics=("arbitrary", "arbitrary")),
    )(x, x, w8)


def conv_bwd(name, x, dy, w8):
    T, C = x.shape
    tc = min(CONV_TC, T)
    nt = T // tc
    n = tc + 2 * HALO

    def body(xp_ref, x_ref, xn_ref, dy_ref, dyn_ref, w_ref, dx_ref, dw_ref):
        i = pl.program_id(1)

        @pl.when(i == 0)
        def _():
            dw_ref[...] = jnp.zeros(dw_ref.shape, F32)
        prev = jnp.where(i == 0, 0.0, xp_ref[...])
        xe = jnp.concatenate([prev, x_ref[...], xn_ref[...]], axis=0)
        dye = jnp.concatenate([jnp.zeros((HALO, CONV_TL), F32), dy_ref[...], jnp.where(i == nt - 1, 0.0, dyn_ref[...])], axis=0)
        ce = _conv_taps(xe, w_ref)
        sg = sigmoid(ce)
        dce = dye * (sg * (1.0 + ce * (1.0 - sg)))
        dx = w_ref[pl.ds(CONV_K - 1, 1), :] * dce
        for k in range(CONV_K - 1):
            dx = dx + w_ref[pl.ds(k, 1), :] * pltpu.roll(dce, n - (CONV_K - 1 - k), 0)
        dx_ref[...] = dx[HALO:HALO + tc].astype(BF16)
        dcur = dce[HALO:HALO + tc]
        for k in range(CONV_K):
            xs = xe if k == CONV_K - 1 else pltpu.roll(xe, CONV_K - 1 - k, 0)
            dw_ref[pl.ds(k, 1), :] += jnp.sum(dcur * xs[HALO:HALO + tc], axis=0, keepdims=True)

    nh = T // HALO
    prev_map = lambda j, i: (jnp.maximum(i * (tc // HALO) - 1, 0), j)
    next_map = lambda j, i: (jnp.minimum((i + 1) * (tc // HALO), nh - 1), j)
    cur_map = lambda j, i: (i, j)
    return pl.pallas_call(
        body, name=name, grid=(C // CONV_TL, nt),
        in_specs=[pl.BlockSpec((HALO, CONV_TL), prev_map), pl.BlockSpec((tc, CONV_TL), cur_map),
                  pl.BlockSpec((HALO, CONV_TL), next_map), pl.BlockSpec((tc, CONV_TL), cur_map),
                  pl.BlockSpec((HALO, CONV_TL), next_map), pl.BlockSpec((HALO, CONV_TL), lambda j, i: (0, j))],
        out_specs=[pl.BlockSpec((tc, CONV_TL), cur_map), pl.BlockSpec((HALO, CONV_TL), lambda j, i: (0, j))],
        out_shape=[jax.ShapeDtypeStruct((T, C), BF16), jax.ShapeDtypeStruct((HALO, C), F32)],
        compiler_params=_params(dimension_semantics=("arbitrary", "arbitrary")),
    )(x, x, x, dy, dy, w8)


GD_VH = 16
GD_HB = 16


def _unit_lower_inverse(a, eye_f):
    n = -a
    t = eye_f + n
    p = n
    for _ in range(5):
        p = bmmf(p, p)
        t = t + bmmf(t, p)
    return t


@jax.custom_vjp
def unit_lower_solve2(a, r1, r2, eye_f):
    t = _unit_lower_inverse(a, eye_f)
    return bmmf(t, r1), bmmf(t, r2)


def _uls2_fwd(a, r1, r2, eye_f):
    t = _unit_lower_inverse(a, eye_f)
    x1, x2 = bmmf(t, r1), bmmf(t, r2)
    return (x1, x2), (t, x1, x2, eye_f)


def _uls2_bwd(res, g):
    t, x1, x2, eye_f = res
    high = lax.Precision.HIGH
    d1 = _bdot(t, g[0], 1, 1, high)
    d2 = _bdot(t, g[1], 1, 1, high)
    da = -(_bdot(d1, x1, 2, 2, high) + _bdot(d2, x2, 2, 2, high))
    return da, d1, d2, jnp.zeros_like(eye_f)


unit_lower_solve2.defvjp(_uls2_fwd, _uls2_bwd)


def gd_heads(al, dt, og, s, qc, kc, v, z, a_col, b_col, cst):
    lt_t, eye, incl, strict = cst
    eye_f = eye.astype(F32)
    nh = s.shape[0]
    q = qc * lax.rsqrt(jnp.sum(qc * qc, axis=-1, keepdims=True) + EPS) * (GD_DK ** -0.5)
    k = kc * lax.rsqrt(jnp.sum(kc * kc, axis=-1, keepdims=True) + EPS)
    beta = sigmoid(b_col)
    g = -jnp.exp(al) * softplus(a_col + dt)
    d_mat = jnp.broadcast_to(jnp.sum(g * lt_t, axis=1, keepdims=True), (nh, GD_CHUNK, GD_CHUNK))
    d = jnp.sum(jnp.where(eye, d_mat, 0.0), axis=-1, keepdims=True)
    dec = jnp.exp(jnp.where(incl, d - d_mat, NEG_INF))
    kb = k * beta
    u, w = unit_lower_solve2(bmm_nt(kb, k) * dec * strict, v * beta, kb * jnp.exp(d), eye_f)
    v_new = u - bmm(w, s)
    o = bmm(q * jnp.exp(d), s) + bmm(bmm_nt(q, k) * dec, v_new)
    dl = jnp.sum(g, axis=1, keepdims=True)
    s_new = s * jnp.exp(dl) + bmm_tn(k * jnp.exp(dl - d), v_new)
    on = o * lax.rsqrt(jnp.mean(o * o, axis=-1, keepdims=True) + EPS) * og
    return s_new, on * silu(z)


def _gd_consts():
    r, c = _iota((GD_CHUNK, GD_CHUNK), 0), _iota((GD_CHUNK, GD_CHUNK), 1)
    return (r <= c).astype(F32), r == c, r >= c, (r > c).astype(F32)


def _gd_load(qkv_ref, z_ref, ab_v, al_v, dt_v, heads):
    lane = _iota((1, 128), 1)
    hot_a = [(lane == h).astype(F32) for h in heads]
    hot_b = [(lane == GD_VH + h).astype(F32) for h in heads]
    col = lambda src, hot: jnp.stack([jnp.sum(src * m, axis=-1, keepdims=True) for m in hot])
    ops = (col(al_v, hot_a), col(dt_v, hot_a),
           jnp.stack([qkv_ref[:, pl.ds((h // 2) * GD_DK, GD_DK)] for h in heads]),
           jnp.stack([qkv_ref[:, pl.ds(GD_QKW + (h // 2) * GD_DK, GD_DK)] for h in heads]),
           jnp.stack([qkv_ref[:, pl.ds(2 * GD_QKW + h * GD_DK, GD_DK)] for h in heads]),
           jnp.stack([z_ref[:, pl.ds(h * GD_DK, GD_DK)] for h in heads]),
           col(ab_v, hot_a), col(ab_v, hot_b))
    return ops, hot_a, hot_b


def gd_fwd(name, qkv, z, ab, al, dt, og):
    T = qkv.shape[0]
    nc = T // GD_CHUNK

    def body(qkv_ref, z_ref, ab_ref, al_ref, dt_ref, og_ref, u_ref, st_ref, s_scr):
        @pl.when(pl.program_id(0) == 0)
        def _():
            s_scr[...] = jnp.zeros(s_scr.shape, F32)
        st_ref[0] = s_scr[...]
        cst = _gd_consts()
        ab_v, al_v, dt_v, og_v = ab_ref[...], al_ref[...], dt_ref[...], og_ref[...]
        for b0 in range(0, GD_VH, GD_HB):
            heads = list(range(b0, b0 + GD_HB))
            (alh, dth, q, k, v, zz, a_col, b_col), _, _ = _gd_load(qkv_ref, z_ref, ab_v, al_v, dt_v, heads)
            s_new, u = gd_heads(alh, dth, og_v, s_scr[pl.ds(b0, GD_HB)], q, k, v, zz, a_col, b_col, cst)
            s_scr[pl.ds(b0, GD_HB)] = s_new
            for i, h in enumerate(heads):
                u_ref[:, pl.ds(h * GD_DK, GD_DK)] = u[i].astype(BF16)

    return pl.pallas_call(
        body, name=name, grid=(nc,),
        in_specs=[_row_spec(GD_CHUNK, GD_QKV), _row_spec(GD_CHUNK, GD_VW), _row_spec(GD_CHUNK, 128),
                  _const_spec((1, 128)), _const_spec((1, 128)), _const_spec((1, 128))],
        out_specs=[_row_spec(GD_CHUNK, GD_VW), pl.BlockSpec((1, GD_VH, GD_DK, GD_DK), lambda i: (i, 0, 0, 0))],
        out_shape=[jax.ShapeDtypeStruct((T, GD_VW), BF16), jax.ShapeDtypeStruct((nc, GD_VH, GD_DK, GD_DK), F32)],
        scratch_shapes=[pltpu.VMEM((GD_VH, GD_DK, GD_DK), F32)],
        compiler_params=_params(dimension_semantics=("arbitrary",)),
    )(qkv, z, ab, al, dt, og)


def gd_bwd(name, qkv, z, ab, al, dt, og, states, du, carry=()):
    T = qkv.shape[0]
    nc = T // GD_CHUNK

    def body(qkv_ref, z_ref, ab_ref, al_ref, dt_ref, og_ref, st_ref, du_ref,
             dqkv_ref, dz_ref, dab_ref, dal_ref, ddt_ref, dog_ref, ds_scr):
        @pl.when(pl.program_id(0) == 0)
        def _():
            ds_scr[...] = jnp.zeros(ds_scr.shape, F32)
            dal_ref[...] = jnp.zeros(dal_ref.shape, F32)
            ddt_ref[...] = jnp.zeros(ddt_ref.shape, F32)
            dog_ref[...] = jnp.zeros(dog_ref.shape, F32)
        cst = _gd_consts()
        ab_v, al_v, dt_v, og_v = ab_ref[...], al_ref[...], dt_ref[...], og_ref[...]
        dab = jnp.zeros((GD_CHUNK, 128), F32)
        dal_row = jnp.zeros((1, 128), F32)
        ddt_row = jnp.zeros((1, 128), F32)
        dog_row = jnp.zeros((1, 128), F32)
        for b0 in range(0, GD_VH, GD_HB):
            heads = list(range(b0, b0 + GD_HB))
            (alh, dth, q, k, v, zz, a_col, b_col), hot_a, hot_b = _gd_load(qkv_ref, z_ref, ab_v, al_v, dt_v, heads)
            fn = lambda *a: gd_heads(*a, cst)
            _, f = jax.vjp(fn, alh, dth, og_v, st_ref[0, pl.ds(b0, GD_HB)], q, k, v, zz, a_col, b_col)
            du_h = jnp.stack([du_ref[:, pl.ds(h * GD_DK, GD_DK)].astype(F32) for h in heads])
            dalh, ddth, dog, ds, dq, dk, dv, dz, da_col, db_col = f((ds_scr[pl.ds(b0, GD_HB)], du_h))
            ds_scr[pl.ds(b0, GD_HB)] = ds
            dog_row = dog_row + dog
            for i, h in enumerate(heads):
                dqkv_ref[:, pl.ds(2 * GD_QKW + h * GD_DK, GD_DK)] = dv[i]
                dz_ref[:, pl.ds(h * GD_DK, GD_DK)] = dz[i].astype(BF16)
                dab = dab + da_col[i] * hot_a[i] + db_col[i] * hot_b[i]
                dal_row = dal_row + dalh[i] * hot_a[i]
                ddt_row = ddt_row + ddth[i] * hot_a[i]
                if h % 2 == 0:
                    dqkv_ref[:, pl.ds((h // 2) * GD_DK, GD_DK)] = dq[i] + dq[i + 1]
                    dqkv_ref[:, pl.ds(GD_QKW + (h // 2) * GD_DK, GD_DK)] = dk[i] + dk[i + 1]
        dab_ref[...] = dab
        dal_ref[...] += dal_row
        ddt_ref[...] += ddt_row
        dog_ref[...] += dog_row

    rev = lambda i: (nc - 1 - i, 0)
    outs, ex = _scan_call(
        body, name, nc,
        [pl.BlockSpec((GD_CHUNK, GD_QKV), rev), pl.BlockSpec((GD_CHUNK, GD_VW), rev), pl.BlockSpec((GD_CHUNK, 128), rev),
         _const_spec((1, 128)), _const_spec((1, 128)), _const_spec((1, 128)),
         pl.BlockSpec((1, GD_VH, GD_DK, GD_DK), lambda i: (nc - 1 - i, 0, 0, 0)), pl.BlockSpec((GD_CHUNK, GD_VW), rev)],
        [pl.BlockSpec((GD_CHUNK, GD_QKV), rev), pl.BlockSpec((GD_CHUNK, GD_VW), rev), pl.BlockSpec((GD_CHUNK, 128), rev),
         _const_spec((1, 128)), _const_spec((1, 128)), _const_spec((1, 128))],
        [jax.ShapeDtypeStruct((T, GD_QKV), F32), jax.ShapeDtypeStruct((T, GD_VW), BF16),
         jax.ShapeDtypeStruct((T, 128), F32)] + [jax.ShapeDtypeStruct((1, 128), F32)] * 3,
        [pltpu.VMEM((GD_VH, GD_DK, GD_DK), F32)], (qkv, z, ab, al, dt, og, states, du), carry)
    return (*outs, ex)


def _my_pos():
    return lax.axis_index("x"), lax.axis_index("y"), lax.axis_index("c")


def _peers(pos):
    out = []
    for k in range(1, N_DEV):
        dev = tuple(1 - p if (k >> s) & 1 else p for p, s in zip(pos, (2, 1, 0)))
        out.append((dev, 4 * dev[0] + 2 * dev[1] + dev[2]))
    return out


_ANY = pl.BlockSpec(memory_space=pl.ANY)
_COMM_SCRATCH = [pltpu.SemaphoreType.DMA((N_DEV - 1,)), pltpu.SemaphoreType.DMA((N_DEV - 1,)), pltpu.SemaphoreType.DMA(())]
GATHER, SCATTER = "gather", "scatter"


def _exchange_descs(pos, kinds, x_refs, o_refs, send_sems, recv_sems, local_sems, with_recvs=True):
    me = 4 * pos[0] + 2 * pos[1] + pos[2]
    peers = _peers(pos)
    local, sends, recvs = [], [], []
    for i, (kind, x, o) in enumerate(zip(kinds, x_refs, o_refs)):
        local.append(pltpu.make_async_copy(x if kind == GATHER else x.at[me], o.at[me], local_sems.at[i]))
        for k, (dev, pid) in enumerate(peers):
            s = (N_DEV - 1) * i + k
            src = x if kind == GATHER else x.at[pid]
            sends.append(pltpu.make_async_remote_copy(src, o.at[me], send_sems.at[s], recv_sems.at[s],
                                                      device_id=dev, device_id_type=MESH))
            if with_recvs:
                recvs.append(pltpu.make_async_remote_copy(src, o.at[pid], send_sems.at[s], recv_sems.at[s],
                                                          device_id=dev, device_id_type=MESH))
    return local, sends, recvs


def _scan_call(body, name, nsteps, in_specs, out_specs, out_shape, scratch_shapes, args, carry=()):
    n_in, n_out, n_scr, n_ex = len(in_specs), len(out_specs), len(scratch_shapes), len(carry)
    params = _params(dimension_semantics=("arbitrary",))
    if not carry:
        outs = pl.pallas_call(body, name=name, grid=(nsteps,), in_specs=in_specs, out_specs=out_specs, out_shape=out_shape,
                              scratch_shapes=scratch_shapes, compiler_params=params)(*args)
        return outs, []
    kinds = [k for k, _ in carry]
    xs = [a for _, a in carry]
    ex_shape = [jax.ShapeDtypeStruct(((N_DEV,) + a.shape) if k == GATHER else a.shape, a.dtype) for k, a in carry]

    def full(*refs):
        ins, x_refs = refs[:n_in], refs[n_in:n_in + n_ex]
        outs = refs[n_in + n_ex:n_in + n_ex + n_out]
        o_refs = refs[n_in + n_ex + n_out:n_in + 2 * n_ex + n_out]
        scr = refs[n_in + 2 * n_ex + n_out:n_in + 2 * n_ex + n_out + n_scr]
        sems = refs[n_in + 2 * n_ex + n_out + n_scr:]
        pos = _my_pos()
        step = pl.program_id(0)

        @pl.when(step == 0)
        def _():
            local, sends, _ = _exchange_descs(pos, kinds, x_refs, o_refs, *sems, with_recvs=False)
            for cp in local + sends:
                cp.start()

        body(*ins, *outs, *scr)

        @pl.when(step == nsteps - 1)
        def _():
            local, sends, recvs = _exchange_descs(pos, kinds, x_refs, o_refs, *sems)
            for cp in recvs:
                cp.wait_recv()
            for cp in sends:
                cp.wait_send()
            for cp in local:
                cp.wait()

    n_sem = (N_DEV - 1) * n_ex
    outs = pl.pallas_call(
        full, name=name, grid=(nsteps,), in_specs=list(in_specs) + [_ANY] * n_ex, out_specs=list(out_specs) + [_ANY] * n_ex,
        out_shape=list(out_shape) + ex_shape,
        scratch_shapes=list(scratch_shapes) + [pltpu.SemaphoreType.DMA((n_sem,)), pltpu.SemaphoreType.DMA((n_sem,)),
                                               pltpu.SemaphoreType.DMA((n_ex,))],
        compiler_params=params)(*args, *xs)
    return outs[:n_out], list(outs[n_out:])


def all_gather(name, x):
    def body(x_ref, o_ref, send_sems, recv_sems, local_sem):
        pos = _my_pos()
        me = 4 * pos[0] + 2 * pos[1] + pos[2]
        peers = _peers(pos)
        mine = pltpu.make_async_copy(x_ref, o_ref.at[me], local_sem)
        mine.start()
        sends = [pltpu.make_async_remote_copy(x_ref, o_ref.at[me], send_sems.at[k], recv_sems.at[k],
                                              device_id=dev, device_id_type=MESH) for k, (dev, _) in enumerate(peers)]
        for cp in sends:
            cp.start()
        for k, (dev, pid) in enumerate(peers):
            pltpu.make_async_remote_copy(x_ref, o_ref.at[pid], send_sems.at[k], recv_sems.at[k],
                                         device_id=dev, device_id_type=MESH).wait_recv()
        for cp in sends:
            cp.wait_send()
        mine.wait()

    return pl.pallas_call(
        body, name=name, out_shape=jax.ShapeDtypeStruct((N_DEV,) + x.shape, x.dtype),
        in_specs=[_ANY], out_specs=_ANY, scratch_shapes=_COMM_SCRATCH,
    )(x)


def lb_rows(r0, r1, r2, r3):
    mx = jnp.maximum(jnp.maximum(r0, r1), jnp.maximum(r2, r3))
    e = [jnp.exp(r - mx) for r in (r0, r1, r2, r3)]
    inv = 1.0 / (e[0] + e[1] + e[2] + e[3])
    c0 = e[0] * inv
    c1 = c0 + e[1] * inv
    c2 = c1 + e[2] * inv
    c3 = c2 + e[3] * inv
    return c0 - c0, c1 - c0, c2 - c0, c3 - c0


def mod_partial(name, c_all, ada_w):
    n, _, w = ada_w.shape

    def body(c_ref, w_ref, o_ref):
        o_ref[0] = mm(c_ref[...], w_ref[0])

    return pl.pallas_call(
        body, name=name, grid=(n,),
        in_specs=[_const_spec((N_DEV, D)), pl.BlockSpec((1, D, w), lambda i: (i, 0, 0))],
        out_specs=pl.BlockSpec((1, N_DEV, w), lambda i: (i, 0, 0)),
        out_shape=jax.ShapeDtypeStruct((n, N_DEV, w), F32),
    )(c_all, ada_w)


def prep(name, modp, ada_b, hgrn_lb):
    def body(mp_ref, b_ref, lb_ref, mod_ref, lbo_ref):
        mod_ref[...] = mp_ref[...] + b_ref[...]
        out = lb_rows(*[lb_ref[pl.ds(i, 1), :] for i in range(4)])
        for i in range(4):
            lbo_ref[pl.ds(i, 1), :] = out[i]

    return pl.pallas_call(
        body, name=name,
        out_shape=[jax.ShapeDtypeStruct(modp.shape, F32), jax.ShapeDtypeStruct(hgrn_lb.shape, F32)],
    )(modp, ada_b, hgrn_lb)


def lb_grad(name, hgrn_lb, dlb_parts):
    def body(lb_ref, d_ref, o_ref):
        cts = []
        for i in range(4):
            acc = d_ref[0, pl.ds(i, 1), :]
            for p in range(1, N_DEV):
                acc = acc + d_ref[p, pl.ds(i, 1), :]
            cts.append(acc)
        _, f = jax.vjp(lb_rows, *[lb_ref[pl.ds(i, 1), :] for i in range(4)])
        for i, g in enumerate(f(tuple(cts))):
            o_ref[pl.ds(i, 1), :] = g

    return pl.pallas_call(body, name=name, out_shape=jax.ShapeDtypeStruct(hgrn_lb.shape, F32))(hgrn_lb, dlb_parts)


def ada_grad(name, c_t, dm):
    n, _, w = dm.shape

    def body(c_ref, d_ref, o_ref):
        acc = c_ref[:, pl.ds(0, 1)] * d_ref[0, pl.ds(0, 1), :]
        for b in range(1, N_DEV):
            acc = acc + c_ref[:, pl.ds(b, 1)] * d_ref[0, pl.ds(b, 1), :]
        o_ref[0] = acc

    return pl.pallas_call(
        body, name=name, grid=(n,),
        in_specs=[_const_spec((D, N_DEV)), pl.BlockSpec((1, N_DEV, w), lambda i: (i, 0, 0))],
        out_specs=pl.BlockSpec((1, D, w), lambda i: (i, 0, 0)),
        out_shape=jax.ShapeDtypeStruct((n, D, w), F32),
    )(c_t, dm)


ADAMW_ROWS = 256


def adamw(name, w, m, v, gparts):
    shape = w.shape
    L = shape[-1]
    R = 1
    for s in shape[:-1]:
        R *= s
    P = gparts.shape[0]
    tr = min(R, ADAMW_ROWS)

    def body(w_ref, m_ref, v_ref, g_ref, go_ref, d_ref, mo_ref, vo_ref):
        g = g_ref[0].astype(F32)
        for p in range(1, P):
            g = g + g_ref[p].astype(F32)
        mn = ADAM_B1 * m_ref[...] + (1.0 - ADAM_B1) * g
        vn = ADAM_B2 * v_ref[...] + (1.0 - ADAM_B2) * (g * g)
        m_hat = mn / (1.0 - ADAM_B1 ** ADAM_STEP)
        v_hat = vn / (1.0 - ADAM_B2 ** ADAM_STEP)
        go_ref[...] = g
        d_ref[...] = -ADAM_LR * (m_hat / (jnp.sqrt(v_hat) + ADAM_EPS) + ADAM_WD * w_ref[...])
        mo_ref[...] = mn
        vo_ref[...] = vn

    spec = pl.BlockSpec((tr, L), lambda i: (i, 0))
    outs = pl.pallas_call(
        body, name=name, grid=(R // tr,),
        in_specs=[spec, spec, spec, pl.BlockSpec((P, tr, L), lambda i: (0, i, 0))],
        out_specs=[spec] * 4, out_shape=[jax.ShapeDtypeStruct((R, L), F32)] * 4,
        compiler_params=_params(dimension_semantics=("arbitrary",)),
    )(w.reshape(R, L), m.reshape(R, L), v.reshape(R, L), gparts.reshape(P, R, L))
    return [o.reshape(shape) for o in outs]


def _cols_from(g):
    _, n, k, w8 = g.shape
    return jnp.transpose(g, (1, 2, 0, 3)).reshape(n, k, N_DEV * w8)


def _rows_from(g):
    _, n, k8, nn = g.shape
    return jnp.transpose(g, (1, 0, 2, 3)).reshape(n, N_DEV * k8, nn)


def _scatter_cols(dw, wire=BF16):
    n, k, nn = dw.shape
    return jnp.transpose(dw.reshape(n, k, N_DEV, nn // N_DEV), (2, 0, 1, 3)).astype(wire)


def _scatter_rows(dw, wire=BF16):
    n, k, nn = dw.shape
    return jnp.transpose(dw.reshape(n, N_DEV, k // N_DEV, nn), (1, 0, 2, 3)).astype(wire)


def _pad_lanes(v, width=128):
    return jnp.pad(v, ((0, 0), (0, width - v.shape[1])))


def _stack_rows(parts):
    n, L = len(parts), parts[0].shape[1]
    r = lax.broadcasted_iota(jnp.int32, (n, L), 0)
    out = jnp.zeros((n, L), parts[0].dtype)
    for i, p in enumerate(parts):
        out = jnp.where(r == i, jnp.broadcast_to(p, (n, L)), out)
    return out


def kernel(x, c, positions, hgrn_lb, ada_w, ada_b, norm_g, hg_in_w, hg_out_w, hg_onorm, sw_in_w, sw_out_w, sw_qnorm, sw_knorm, sw_sinks, gd_in_w, gd_out_w, gd_conv_w, gd_a_log, gd_dt_bias, gd_onorm, loss_target, m_hgrn_lb, m_ada_w, m_ada_b, m_norm_g, m_hg_in_w, m_hg_out_w, m_hg_onorm, m_sw_in_w, m_sw_out_w, m_sw_qnorm, m_sw_knorm, m_sw_sinks, m_gd_in_w, m_gd_out_w, m_gd_conv_w, m_gd_a_log, m_gd_dt_bias, m_gd_onorm, v_hgrn_lb, v_ada_w, v_ada_b, v_norm_g, v_hg_in_w, v_hg_out_w, v_hg_onorm, v_sw_in_w, v_sw_out_w, v_sw_qnorm, v_sw_knorm, v_sw_sinks, v_gd_in_w, v_gd_out_w, v_gd_conv_w, v_gd_a_log, v_gd_dt_bias, v_gd_onorm):
    pos = _my_pos()
    me = 4 * pos[0] + 2 * pos[1] + pos[2]
    x0 = x[0]
    tgt = loss_target[0]
    n_layers = norm_g.shape[0]
    aw = ada_w.shape[2]

    c_all = all_gather("ag_c", c)[:, 0, :]
    modp = all_gather("ag_mod", mod_partial("mod_partial", c_all, ada_w))
    modp_mine = lax.dynamic_index_in_dim(modp, me, axis=2, keepdims=False)
    modp_mine = jnp.transpose(modp_mine, (1, 0, 2)).reshape(n_layers, N_DEV * aw)
    mod, lb_all = prep("prep", modp_mine, ada_b, hgrn_lb)
    shift, scale, gate = mod[:, :D], mod[:, D:2 * D], mod[:, 2 * D:]
    row = lambda a, i: a[i:i + 1]

    w_hg_in = [_cols_from(all_gather("ag_hg_in0", hg_in_w[0:1].astype(BF16)))[0], None]
    later = [(GATHER, hg_in_w[1:2].astype(BF16)), (GATHER, hg_out_w.astype(BF16)), (GATHER, sw_in_w.astype(BF16)),
             (GATHER, sw_out_w.astype(BF16)), (GATHER, gd_in_w.astype(BF16)), (GATHER, gd_out_w.astype(BF16)),
             (GATHER, gd_conv_w)]
    gd_al, gd_dt = _pad_lanes(gd_a_log), _pad_lanes(gd_dt_bias)

    inv_freq = ROPE_THETA ** (-jnp.arange(0, SW_DH, 2, dtype=F32) / SW_DH)
    ang = positions[0].astype(F32)[:, None] * inv_freq
    cos, sin = jnp.cos(ang), jnp.sin(ang)
    cos64, sin64 = jnp.concatenate([cos, cos], axis=-1), jnp.concatenate([-sin, sin], axis=-1)

    xs, hs, ys, saved = [x0], [], [], []
    last = n_layers - 1
    h = norm_mod("norm0", x0, row(norm_g, 0), row(scale, 0), row(shift, 0))
    for i in range(n_layers):
        kind, j = i % 3, i // 3
        hs.append(h)
        if kind == 0:
            p = matmul(f"hg_in{i}", h, w_hg_in[j])
            u, st, got = hg_fwd(f"hg_fwd{i}", p, row(lb_all, i), row(hg_onorm, j), later if i == 0 else ())
            if i == 0:
                w_hg_in[1] = _cols_from(got[0])[0]
                w_hg_out = _rows_from(got[1])
                w_sw_in, w_sw_out = _cols_from(got[2])[0], _rows_from(got[3])[0]
                w_gd_in, w_gd_out = _cols_from(got[4])[0], _rows_from(got[5])[0]
                w_gd_qkv, w_gd_z = w_gd_in[:, :GD_QKV], w_gd_in[:, GD_QKV:GD_QKV + GD_VW]
                w_gd_ab = _pad_lanes(w_gd_in[:, GD_QKV + GD_VW:])
                conv_w8 = jnp.pad(_cols_from(got[6])[0], ((0, HALO - CONV_K), (0, 0)))
            w_out = w_hg_out[j]
            saved.append((p, u, st))
        elif kind == 1:
            p = matmul(f"sw_in{i}", h, w_sw_in)
            u, kpost = swa_fwd(f"sw_fwd{i}", p, cos64, sin64, row(sw_qnorm, j), row(sw_knorm, j), row(sw_sinks, j))
            w_out = w_sw_out
            saved.append((p, u, kpost))
        else:
            xq = matmul(f"gd_qkv{i}", h, w_gd_qkv)
            zz = matmul(f"gd_z{i}", h, w_gd_z)
            ab = matmul(f"gd_ab{i}", h, w_gd_ab)
            cv = conv_fwd(f"gd_conv{i}", xq, conv_w8)
            u, st = gd_fwd(f"gd_fwd{i}", cv, zz, ab, gd_al, gd_dt, row(gd_onorm, j))
            w_out = w_gd_out
            saved.append((xq, zz, ab, cv, u, st))
        if i < last:
            y, xn, h = out_resid_norm(f"out{i}", u, w_out, xs[i], row(gate, i), row(norm_g, i + 1), row(scale, i + 1),
                                      row(shift, i + 1))
            xs.append(xn)
        else:
            y, dx, dy, loss_acc, dgate_last = out_loss(f"out{i}", u, w_out, xs[i], tgt, row(gate, i))
        ys.append(y)
    loss = lax.psum(loss_acc[0, 0], ("x", "y", "c"))

    dgate = [None] * n_layers
    dgate[last] = dgate_last
    dg_norm, dscale, dshift = [None] * n_layers, [None] * n_layers, [None] * n_layers
    dlb = [jnp.zeros((1, D), F32)] * n_layers
    d_hg_on = [None] * 2
    rs_hg_in, rs_hg_out = [None] * 2, [None] * 2
    pending = ()
    for i in range(last, -1, -1):
        kind, j = i % 3, i // 3
        h = hs[i]
        if kind == 0:
            p, u, st = saved[i]
            du = matmul_nt(f"hg_du{i}", [(dy, w_hg_out[j])])
            d_out = matmul_tn(f"hg_dwo{i}", u, dy)
            dp, dlb_i, dog, got = hg_bwd(f"hg_bwd{i}", p, row(lb_all, i), row(hg_onorm, j), st, du, pending)
            if got:
                rs_sw_in, rs_sw_out = got
            dlb[i] = dlb_i
            d_hg_on[j] = dog
            dh_pairs = [(dp, w_hg_in[j])]
            d_in = matmul_tn(f"hg_dwi{i}", h, dp)
            pending = [(SCATTER, _scatter_cols(d_in[None])), (SCATTER, _scatter_rows(d_out[None]))]
        elif kind == 1:
            p, u, kpost = saved[i]
            du = matmul_nt(f"sw_du{i}", [(dy, w_sw_out)])
            d_out = matmul_tn(f"sw_dwo{i}", u, dy)
            dp, d_qn, d_kn, d_sk, got = swa_bwd(f"sw_bwd{i}", p, kpost, cos64, sin64, row(sw_qnorm, j), row(sw_knorm, j),
                                                row(sw_sinks, j), du, pending)
            rs_gd_in, rs_gd_out, rs_gd_conv = got
            dh_pairs = [(dp, w_sw_in)]
            d_in = matmul_tn(f"sw_dwi{i}", h, dp)
            pending = [(SCATTER, _scatter_cols(d_in[None])), (SCATTER, _scatter_rows(d_out[None]))]
        else:
            xq, zz, ab, cv, u, st = saved[i]
            du = matmul_nt(f"gd_du{i}", [(dy, w_gd_out)])
            d_out = matmul_tn(f"gd_dwo{i}", u, dy)
            dcv, dz, dab, d_al, d_dt, d_gd_on, got = gd_bwd(f"gd_bwd{i}", cv, zz, ab, gd_al, gd_dt, row(gd_onorm, j), st, du,
                                                            pending)
            rs_hg_in[1], rs_hg_out[1] = got
            dxq, d_conv8 = conv_bwd(f"gd_dconv{i}", xq, dcv, conv_w8)
            dh_pairs = [(dxq, w_gd_qkv), (dz, w_gd_z), (dab, w_gd_ab)]
            d_in = jnp.concatenate([matmul_tn(f"gd_dwq{i}", h, dxq), matmul_tn(f"gd_dwz{i}", h, dz),
                                    matmul_tn(f"gd_dwab{i}", h, dab)[:, :2 * GD_VH]], axis=1)
            pending = [(SCATTER, _scatter_cols(d_in[None])), (SCATTER, _scatter_rows(d_out[None])),
                       (SCATTER, _scatter_cols(d_conv8[None, :CONV_K], F32))]
        if i > 0:
            (dx, dy, dg_norm[i], dscale[i], dshift[i], dgate[i - 1]), _ = dh_bwd_row(
                f"dh{i}", dh_pairs, xs[i], dx, row(norm_g, i), row(scale, i), row(shift, i), ys[i - 1], row(gate, i - 1))
        else:
            (dx, dg_norm[i], dscale[i], dshift[i]), (rs_hg_in[0], rs_hg_out[0]) = dh_bwd_row(
                f"dh{i}", dh_pairs, xs[i], dx, row(norm_g, i), row(scale, i), row(shift, i), carry=pending)
    grad_x = dx[None]

    dmod = jnp.concatenate([_stack_rows(dshift), _stack_rows(dscale), _stack_rows(dgate)], axis=1)
    misc = _stack_rows(d_hg_on + [d_gd_on, _pad_lanes(d_qn), _pad_lanes(d_kn), _pad_lanes(d_sk), d_al, d_dt])
    small = jnp.concatenate([_stack_rows(dlb).reshape(-1, 128), _stack_rows(dg_norm).reshape(-1, 128),
                             dmod.reshape(-1, 128), misc], axis=0)
    small_all = all_gather("ag_small", small)
    n_lb = n_layers * D // 128
    n_mod = n_layers * 3 * D // 128
    o = 0
    dlb_parts = small_all[:, o:o + n_lb].reshape(N_DEV, n_layers, D); o += n_lb
    dgn_parts = small_all[:, o:o + n_lb].reshape(N_DEV, n_layers, D); o += n_lb
    dmod_parts = small_all[:, o:o + n_mod].reshape(N_DEV, n_layers, 3 * D); o += n_mod
    dhgon_parts = small_all[:, o:o + 2]; o += 2
    dgdon_parts = small_all[:, o:o + 1]; o += 1
    dqn_parts = small_all[:, o:o + 1, :SW_DH]; o += 1
    dkn_parts = small_all[:, o:o + 1, :SW_DH]; o += 1
    dsk_parts = small_all[:, o:o + 1, :SW_KV * SW_G]; o += 1
    dal_parts = small_all[:, o:o + 1, :GD_VH]; o += 1
    ddt_parts = small_all[:, o:o + 1, :GD_VH]; o += 1

    g_lb = lb_grad("lb_grad", hgrn_lb, dlb_parts)
    dm_mine = lax.dynamic_slice_in_dim(dmod_parts, me * aw, aw, axis=2)
    g_ada_w = ada_grad("ada_grad", jnp.transpose(c_all), jnp.transpose(dm_mine, (1, 0, 2)))

    res = {}
    res["hgrn_lb"] = adamw("aw_hgrn_lb", hgrn_lb, m_hgrn_lb, v_hgrn_lb, g_lb[None])
    res["ada_w"] = adamw("aw_ada_w", ada_w, m_ada_w, v_ada_w, g_ada_w[None])
    res["ada_b"] = adamw("aw_ada_b", ada_b, m_ada_b, v_ada_b, dmod_parts)
    res["norm_g"] = adamw("aw_norm_g", norm_g, m_norm_g, v_norm_g, dgn_parts)
    res["hg_in_w"] = adamw("aw_hg_in", hg_in_w, m_hg_in_w, v_hg_in_w, jnp.concatenate(rs_hg_in, axis=1))
    res["hg_out_w"] = adamw("aw_hg_out", hg_out_w, m_hg_out_w, v_hg_out_w, jnp.concatenate(rs_hg_out, axis=1))
    res["hg_onorm"] = adamw("aw_hg_onorm", hg_onorm, m_hg_onorm, v_hg_onorm, dhgon_parts)
    res["sw_in_w"] = adamw("aw_sw_in", sw_in_w, m_sw_in_w, v_sw_in_w, rs_sw_in)
    res["sw_out_w"] = adamw("aw_sw_out", sw_out_w, m_sw_out_w, v_sw_out_w, rs_sw_out)
    res["sw_qnorm"] = adamw("aw_sw_qn", sw_qnorm, m_sw_qnorm, v_sw_qnorm, dqn_parts)
    res["sw_knorm"] = adamw("aw_sw_kn", sw_knorm, m_sw_knorm, v_sw_knorm, dkn_parts)
    res["sw_sinks"] = adamw("aw_sw_sinks", sw_sinks, m_sw_sinks, v_sw_sinks, dsk_parts)
    res["gd_in_w"] = adamw("aw_gd_in", gd_in_w, m_gd_in_w, v_gd_in_w, rs_gd_in)
    res["gd_out_w"] = adamw("aw_gd_out", gd_out_w, m_gd_out_w, v_gd_out_w, rs_gd_out)
    res["gd_conv_w"] = adamw("aw_gd_conv", gd_conv_w, m_gd_conv_w, v_gd_conv_w, rs_gd_conv)
    res["gd_a_log"] = adamw("aw_gd_alog", gd_a_log, m_gd_a_log, v_gd_a_log, dal_parts)
    res["gd_dt_bias"] = adamw("aw_gd_dt", gd_dt_bias, m_gd_dt_bias, v_gd_dt_bias, ddt_parts)
    res["gd_onorm"] = adamw("aw_gd_onorm", gd_onorm, m_gd_onorm, v_gd_onorm, dgdon_parts)

    order = ["hgrn_lb", "ada_w", "ada_b", "norm_g", "hg_in_w", "hg_out_w", "hg_onorm", "sw_in_w", "sw_out_w", "sw_qnorm",
             "sw_knorm", "sw_sinks", "gd_in_w", "gd_out_w", "gd_conv_w", "gd_a_log", "gd_dt_bias", "gd_onorm"]
    outs = [loss, grad_x]
    for part in range(4):
        outs += [res[n][part] for n in order]
    return tuple(outs)
```

```python
import functools

import jax
import jax.numpy as jnp
from jax import lax
from jax.experimental import pallas as pl
from jax.experimental.pallas import tpu as pltpu

F32, BF16 = jnp.float32, jnp.bfloat16
HI = lax.Precision.HIGHEST
MESH = pl.DeviceIdType.MESH
N_DEV = 8
D = 1024
EPS = 1e-6
NEG_INF = float("-inf")

HG_HEADS, HG_DH, HG_BLK, HG_CH, HG_SUB = 8, 128, 128, 32, 8
HG_SAFE = 60.0
SW_KV, SW_G, SW_DH, SW_BLK = 4, 4, 64, 128
SW_QW, SW_KVW = 1024, 256
GD_QK_HEADS, GD_DK, GD_CHUNK = 8, 128, 64
GD_QKW, GD_VW, GD_QKV = 1024, 2048, 4096
CONV_K = 4
ROPE_THETA = 10000.0

ADAM_LR, ADAM_B1, ADAM_B2, ADAM_EPS, ADAM_WD, ADAM_STEP = 0.001, 0.9, 0.999, 1e-08, 0.01, 10

VMEM_LIMIT = 56 * 1024 * 1024


def _params(**kw):
    return pltpu.CompilerParams(vmem_limit_bytes=VMEM_LIMIT, **kw)


def _dot(a, b, ca, cb):
    return lax.dot_general(a.astype(BF16), b.astype(BF16), (((ca,), (cb,)), ((), ())), preferred_element_type=F32)


def mm(a, b):
    return _dot(a, b, 1, 0)


def mm_nt(a, b):
    return _dot(a, b, 1, 1)


def mm_tn(a, b):
    return _dot(a, b, 0, 0)


def mmf(a, b):
    return lax.dot_general(a, b, (((1,), (0,)), ((), ())), precision=HI, preferred_element_type=F32)


def _split3(x):
    h1 = x.astype(BF16)
    r1 = x - h1.astype(F32)
    h2 = r1.astype(BF16)
    h3 = (r1 - h2.astype(F32)).astype(BF16)
    return h1, h2, h3


def _lin01_l(m, x):
    mb = m.astype(BF16)
    return sum(lax.dot_general(mb, p, (((1,), (0,)), ((), ())), preferred_element_type=F32) for p in _split3(x))


def _lin01_r(x, m):
    mb = m.astype(BF16)
    return sum(lax.dot_general(p, mb, (((1,), (0,)), ((), ())), preferred_element_type=F32) for p in _split3(x))


def _bdot(a, b, ca, cb, prec=None):
    return lax.dot_general(a, b, (((ca,), (cb,)), ((0,), (0,))), precision=prec, preferred_element_type=F32)


def bmm(a, b):
    return _bdot(a.astype(BF16), b.astype(BF16), 2, 1)


def bmm_nt(a, b):
    return _bdot(a.astype(BF16), b.astype(BF16), 2, 2)


def bmm_tn(a, b):
    return _bdot(a.astype(BF16), b.astype(BF16), 1, 1)


def bmmf(a, b):
    return _bdot(a, b, 2, 1, lax.Precision.HIGH)


def _blin01(m, x):
    mb = m.astype(BF16)
    return sum(_bdot(mb, p, 2, 1) for p in _split3(x))


@jax.custom_vjp
def blin01(m, mt, x):
    return _blin01(m, x)


def _blin01_fwd(m, mt, x):
    return _blin01(m, x), (m, mt)


def _blin01_bwd(res, g):
    m, mt = res
    return jnp.zeros_like(m), jnp.zeros_like(mt), _blin01(mt, g)


blin01.defvjp(_blin01_fwd, _blin01_bwd)


def _blin01_r(x, m):
    mb = m.astype(BF16)
    return sum(_bdot(p, mb, 2, 1) for p in _split3(x))


@jax.custom_vjp
def blin01_r(x, m, mt):
    return _blin01_r(x, m)


def _blin01_r_fwd(x, m, mt):
    return _blin01_r(x, m), (m, mt)


def _blin01_r_bwd(res, g):
    m, mt = res
    return _blin01_r(g, mt), jnp.zeros_like(m), jnp.zeros_like(mt)


blin01_r.defvjp(_blin01_r_fwd, _blin01_r_bwd)


@jax.custom_vjp
def lin01_l(m, mt, x):
    return _lin01_l(m, x)


def _lin01_l_fwd(m, mt, x):
    return _lin01_l(m, x), (m, mt)


def _lin01_l_bwd(res, g):
    m, mt = res
    return jnp.zeros_like(m), jnp.zeros_like(mt), _lin01_l(mt, g)


lin01_l.defvjp(_lin01_l_fwd, _lin01_l_bwd)


@jax.custom_vjp
def lin01_r(x, m, mt):
    return _lin01_r(x, m)


def _lin01_r_fwd(x, m, mt):
    return _lin01_r(x, m), (m, mt)


def _lin01_r_bwd(res, g):
    m, mt = res
    return _lin01_r(g, mt), jnp.zeros_like(m), jnp.zeros_like(mt)


lin01_r.defvjp(_lin01_r_fwd, _lin01_r_bwd)


@functools.partial(jax.custom_vjp, nondiff_argnums=(1, 2))
def rows(x, start, size):
    return lax.slice_in_dim(x, start, start + size, axis=x.ndim - 2)


def _rows_fwd(x, start, size):
    return lax.slice_in_dim(x, start, start + size, axis=x.ndim - 2), jnp.zeros(x.shape[:-1] + (1,), F32)


def _rows_bwd(start, size, res, g):
    ax = g.ndim - 2
    total = res.shape[ax]
    zeros = lambda n: jnp.zeros(g.shape[:ax] + (n,) + g.shape[ax + 1:], g.dtype)
    parts = []
    if start > 0:
        parts.append(zeros(start))
    parts.append(g)
    if total - start - size > 0:
        parts.append(zeros(total - start - size))
    return (jnp.concatenate(parts, axis=ax) if len(parts) > 1 else g,)


rows.defvjp(_rows_fwd, _rows_bwd)


@functools.partial(jax.custom_vjp, nondiff_argnums=(1,))
def rowsel(x, s):
    return lax.slice_in_dim(x, s, s + 1, axis=x.ndim - 2)


def _rowsel_fwd(x, s):
    return lax.slice_in_dim(x, s, s + 1, axis=x.ndim - 2), jnp.zeros(x.shape[:-1] + (1,), F32)


def _rowsel_bwd(s, res, g):
    r = lax.broadcasted_iota(jnp.int32, res.shape, res.ndim - 2)
    return (jnp.where(r == s, g, 0.0),)


rowsel.defvjp(_rowsel_fwd, _rowsel_bwd)


def sigmoid(x):
    return jax.nn.sigmoid(x)


def silu(x):
    return x * jax.nn.sigmoid(x)


def softplus(x):
    return jnp.maximum(x, 0.0) + jnp.log1p(jnp.exp(-jnp.abs(x)))


def _iota(shape, dim):
    return lax.broadcasted_iota(jnp.int32, shape, dim)


def norm_mod_fn(g, sc, sh, x):
    r = lax.rsqrt(jnp.mean(x * x, axis=-1, keepdims=True) + EPS)
    return (x * r * g) * (1.0 + sc) + sh


def _row_spec(tm, f):
    return pl.BlockSpec((tm, f), lambda i: (i, 0))


def _const_spec(shape):
    nd = len(shape)
    return pl.BlockSpec(shape, lambda i: (0,) * nd)


def _tiled(name, fn, consts, rows_in, row_outs, acc_outs=(), tm=512, carry=()):
    T = rows_in[0].shape[0]
    tm = min(tm, T)
    n_c, n_r, n_ro = len(consts), len(rows_in), len(row_outs)

    def body(*refs):
        c_refs, r_refs = refs[:n_c], refs[n_c:n_c + n_r]
        ro_refs, ao_refs = refs[n_c + n_r:n_c + n_r + n_ro], refs[n_c + n_r + n_ro:]
        outs = fn(*[r[...] for r in c_refs], *[r[...] for r in r_refs])
        for r, v in zip(ro_refs, outs[:n_ro]):
            r[...] = v.astype(r.dtype)
        if ao_refs:
            @pl.when(pl.program_id(0) == 0)
            def _():
                for r in ao_refs:
                    r[...] = jnp.zeros(r.shape, r.dtype)
            for r, v in zip(ao_refs, outs[n_ro:]):
                r[...] += v

    out_shape = [jax.ShapeDtypeStruct((T, f), dt) for f, dt in row_outs] + [jax.ShapeDtypeStruct(s, F32) for s in acc_outs]
    out_specs = [_row_spec(tm, f) for f, _ in row_outs] + [_const_spec(s) for s in acc_outs]
    in_specs = [_const_spec(c.shape) for c in consts] + [_row_spec(tm, r.shape[1]) for r in rows_in]
    return _scan_call(body, name, T // tm, in_specs, out_specs, out_shape, [], (*consts, *rows_in), carry)


def norm_mod(name, x, g, sc, sh):
    (h,), _ = _tiled(name, lambda g, sc, sh, x: (norm_mod_fn(g, sc, sh, x),), [g, sc, sh], [x], [(D, BF16)])
    return h


def out_resid_norm(name, u, w, x, gate, g, sc, sh):
    def fn(gate, g, sc, sh, w, x, u):
        y = mm(u, w)
        xn = x + gate * y
        return y, xn, norm_mod_fn(g, sc, sh, xn)
    return _tiled(name, fn, [gate, g, sc, sh, w], [x, u], [(D, F32), (D, F32), (D, BF16)])[0]


def out_loss(name, u, w, x, tgt, gate):
    def fn(gate, w, x, tgt, u):
        y = mm(u, w)
        err = x + gate * y - tgt
        dx = err * (1.0 / D)
        per_tok = jnp.sum(err * err, axis=-1, keepdims=True) * (0.5 / D)
        loss = jnp.sum(per_tok, axis=0, keepdims=True)
        return y, dx, gate * dx, jnp.broadcast_to(loss, (1, 128)), jnp.sum(dx * y, axis=0, keepdims=True)
    return _tiled(name, fn, [gate, w], [x, tgt, u], [(D, F32), (D, F32), (D, BF16)], [(1, 128), (1, D)])[0]


def dh_bwd_row(name, pairs, x, dxn, g, sc, sh, y_prev=None, gate_prev=None, carry=()):
    with_prev = y_prev is not None
    n = len(pairs)

    def fn(*a):
        a = list(a)
        gate_p = a.pop(0) if with_prev else None
        g, sc, sh = a[:3]
        ws, a = a[3:3 + n], a[3 + n:]
        x, dxn = a[:2]
        yp = a[2] if with_prev else None
        dps = a[3:] if with_prev else a[2:]
        dh = mm_nt(dps[0], ws[0])
        for dp, w in zip(dps[1:], ws[1:]):
            dh = dh + mm_nt(dp, w)
        _, f = jax.vjp(norm_mod_fn, g, sc, sh, x)
        dg, dsc, dsh, dx = f(dh)
        dx = dx + dxn
        if with_prev:
            return dx, gate_p * dx, dg, dsc, dsh, jnp.sum(dx * yp, axis=0, keepdims=True)
        return dx, dg, dsc, dsh

    ws, dps = [w for _, w in pairs], [dp for dp, _ in pairs]
    tm = MM_ROWS_WIDE if sum(dp.shape[1] for dp in dps) > MM_WIDE else MM_ROWS_NARROW
    if with_prev:
        return _tiled(name, fn, [gate_prev, g, sc, sh] + ws, [x, dxn, y_prev] + dps, [(D, F32), (D, BF16)], [(1, D)] * 4,
                      tm=tm, carry=carry)
    return _tiled(name, fn, [g, sc, sh] + ws, [x, dxn] + dps, [(D, F32)], [(1, D)] * 3, tm=tm, carry=carry)


MM_ROWS_WIDE, MM_ROWS_NARROW, MM_WIDE = 256, 512, 2048


def matmul(name, a, w, out_dtype=F32):
    T, K = a.shape
    N = w.shape[1]
    tm = min(MM_ROWS_NARROW if N <= MM_WIDE else MM_ROWS_WIDE, T)

    def body(a_ref, w_ref, o_ref):
        o_ref[...] = mm(a_ref[...], w_ref[...]).astype(o_ref.dtype)

    return pl.pallas_call(
        body, name=name, grid=(T // tm,),
        in_specs=[_row_spec(tm, K), _const_spec((K, N))], out_specs=_row_spec(tm, N),
        out_shape=jax.ShapeDtypeStruct((T, N), out_dtype),
        compiler_params=_params(dimension_semantics=("arbitrary",)),
    )(a, w)


def matmul_nt(name, pairs, out_dtype=F32):
    T = pairs[0][0].shape[0]
    K = pairs[0][1].shape[0]
    wide = sum(a.shape[1] for a, _ in pairs) > MM_WIDE
    tm = min(MM_ROWS_WIDE if wide else MM_ROWS_NARROW, T)
    n = len(pairs)

    def body(*refs):
        o_ref = refs[-1]
        acc = None
        for i in range(n):
            p = mm_nt(refs[2 * i][...], refs[2 * i + 1][...])
            acc = p if acc is None else acc + p
        o_ref[...] = acc.astype(o_ref.dtype)

    in_specs, args = [], []
    for a, w in pairs:
        in_specs += [_row_spec(tm, a.shape[1]), _const_spec(w.shape)]
        args += [a, w]
    return pl.pallas_call(
        body, name=name, grid=(T // tm,), in_specs=in_specs, out_specs=_row_spec(tm, K),
        out_shape=jax.ShapeDtypeStruct((T, K), out_dtype),
        compiler_params=_params(dimension_semantics=("arbitrary",)),
    )(*args)


def matmul_tn(name, a, b, tm=512, tn=2048):
    T, K = a.shape
    N = b.shape[1]
    tm = min(tm, T)
    tn = max(t for t in range(128, min(tn, N) + 1, 128) if N % t == 0)

    def body(a_ref, b_ref, o_ref):
        @pl.when(pl.program_id(1) == 0)
        def _():
            o_ref[...] = jnp.zeros(o_ref.shape, F32)
        o_ref[...] += mm_tn(a_ref[...], b_ref[...])

    return pl.pallas_call(
        body, name=name, grid=(N // tn, T // tm),
        in_specs=[pl.BlockSpec((tm, K), lambda j, i: (i, 0)), pl.BlockSpec((tm, tn), lambda j, i: (i, j))],
        out_specs=pl.BlockSpec((K, tn), lambda j, i: (0, j)),
        out_shape=jax.ShapeDtypeStruct((K, N), F32),
        compiler_params=_params(dimension_semantics=("arbitrary", "arbitrary")),
    )(a, b)


def _hg_log_f(lb, fp):
    return jnp.log(lb + (1.0 - lb) * sigmoid(fp))


def _hg_safe(lb, fp):
    lf = _hg_log_f(lb, fp)
    worst = [jnp.max(-jnp.sum(lf[:, c * HG_CH:(c + 1) * HG_CH], axis=1)) for c in range(fp.shape[1] // HG_CH)]
    return functools.reduce(jnp.maximum, worst) <= HG_SAFE


def hg_heads(fast, tri, trit, lb, og, st, qp, fp, iv, z):
    blk = qp.shape[1]
    q = silu(qp)
    lf = _hg_log_f(lb, fp)
    k = (1.0 - lb) * sigmoid(-fp)
    r = _iota((1, HG_SUB, 1), 1)
    rc = _iota((1, HG_CH, 1), 1)
    outs = []
    for c in range(blk // HG_CH):
        qc, kc, vc, lfc = (rows(a, c * HG_CH, HG_CH) for a in (q, k, iv, lf))
        b = blin01(tri, trit, lfc)
        qd = qc * jnp.exp(b)
        o_state = bmm_nt(qd, st)
        if fast:
            scores = jnp.where(tri > 0.0, _bdot(qd, kc * jnp.exp(-b), 2, 2, lax.Precision.HIGH), 0.0)
            outs.append(o_state + bmm(scores, vc))
            bl = rowsel(b, HG_CH - 1)
            st = st * jnp.exp(bl) + bmm_tn(vc, kc * jnp.exp(bl - b))
            continue
        parts = []
        for i in range(HG_CH // HG_SUB):
            qi, ki, vi, bi = (rows(a, i * HG_SUB, HG_SUB) for a in (qc, kc, vc, b))
            o = rows(o_state, i * HG_SUB, HG_SUB)
            for s in range(HG_SUB):
                ks, bs, vs = rowsel(ki, s), rowsel(bi, s), rowsel(vi, s)
                e = jnp.exp(jnp.where(r >= s, bi - bs, NEG_INF))
                o = o + jnp.sum(qi * ks * e, axis=-1, keepdims=True) * vs
            if i > 0:
                rb = rowsel(b, i * HG_SUB - 1)
                kt = kc * jnp.exp(jnp.where(rc < i * HG_SUB, rb - b, NEG_INF))
                o = o + bmm(bmm_nt(qi * jnp.exp(bi - rb), kt), vc)
            parts.append(o)
        bl = rowsel(b, HG_CH - 1)
        st = st * jnp.exp(bl) + bmm_tn(vc, kc * jnp.exp(bl - b))
        outs.append(jnp.concatenate(parts, axis=1))
    o = jnp.concatenate(outs, axis=1)
    on = o * lax.rsqrt(jnp.mean(o * o, axis=-1, keepdims=True) + EPS) * og
    return st, on * silu(z)


def _tri(n, nh):
    t = (_iota((nh, n, n), 1) >= _iota((nh, n, n), 2)).astype(F32)
    tt = (_iota((nh, n, n), 1) <= _iota((nh, n, n), 2)).astype(F32)
    return t, tt


def _hg_load(p_ref, lb_ref, heads):
    col = lambda ref, base: jnp.stack([ref[:, pl.ds(base + h * HG_DH, HG_DH)] for h in heads])
    return col(lb_ref, 0), col(p_ref, 0), col(p_ref, D), col(p_ref, 2 * D), col(p_ref, 3 * D)


def hg_fwd(name, p, lb, og, carry=()):
    T = p.shape[0]
    nb = T // HG_BLK

    def body(p_ref, lb_ref, og_ref, u_ref, st_ref, s_scr):
        @pl.when(pl.program_id(0) == 0)
        def _():
            s_scr[...] = jnp.zeros(s_scr.shape, F32)
        st_ref[0] = s_scr[...]
        tri, trit = _tri(HG_CH, HG_HEADS)
        og_v = og_ref[...]
        heads = list(range(HG_HEADS))
        lbh, qp, fp, iv, z = _hg_load(p_ref, lb_ref, heads)

        def run(fast):
            def go():
                st, u = hg_heads(fast, tri, trit, lbh, og_v, s_scr[...], qp, fp, iv, z)
                s_scr[...] = st
                for h in heads:
                    u_ref[:, pl.ds(h * HG_DH, HG_DH)] = u[h].astype(BF16)
            return go

        lax.cond(_hg_safe(lbh, fp), run(True), run(False))

    (u, st), ex = _scan_call(
        body, name, nb,
        [_row_spec(HG_BLK, 4 * D), _const_spec((1, D)), _const_spec((1, HG_DH))],
        [_row_spec(HG_BLK, D), pl.BlockSpec((1, HG_HEADS, HG_DH, HG_DH), lambda i: (i, 0, 0, 0))],
        [jax.ShapeDtypeStruct((T, D), BF16), jax.ShapeDtypeStruct((nb, HG_HEADS, HG_DH, HG_DH), F32)],
        [pltpu.VMEM((HG_HEADS, HG_DH, HG_DH), F32)], (p, lb, og), carry)
    return u, st, ex


def hg_bwd(name, p, lb, og, states, du, carry=()):
    T = p.shape[0]
    nb = T // HG_BLK

    def body(p_ref, lb_ref, og_ref, st_ref, du_ref, dp_ref, dlb_ref, dog_ref, ds_scr):
        @pl.when(pl.program_id(0) == 0)
        def _():
            ds_scr[...] = jnp.zeros(ds_scr.shape, F32)
            dlb_ref[...] = jnp.zeros(dlb_ref.shape, F32)
            dog_ref[...] = jnp.zeros(dog_ref.shape, F32)
        tri, trit = _tri(HG_CH, HG_HEADS)
        og_v = og_ref[...]
        heads = list(range(HG_HEADS))
        lbh, qp, fp, iv, z = _hg_load(p_ref, lb_ref, heads)
        du_h = jnp.stack([du_ref[:, pl.ds(h * HG_DH, HG_DH)].astype(F32) for h in heads])

        def run(fast):
            def go():
                _, f = jax.vjp(functools.partial(hg_heads, fast, tri, trit), lbh, og_v, st_ref[0], qp, fp, iv, z)
                dlb, dog, dst, dq, df, di, dz = f((ds_scr[...], du_h))
                ds_scr[...] = dst
                dog_ref[...] += dog
                for h in heads:
                    for base, g in ((0, dq), (D, df), (2 * D, di), (3 * D, dz)):
                        dp_ref[:, pl.ds(base + h * HG_DH, HG_DH)] = g[h].astype(BF16)
                    dlb_ref[:, pl.ds(h * HG_DH, HG_DH)] += dlb[h]
            return go

        lax.cond(_hg_safe(lbh, fp), run(True), run(False))

    rev = lambda i: (nb - 1 - i, 0)
    outs, ex = _scan_call(
        body, name, nb,
        [pl.BlockSpec((HG_BLK, 4 * D), rev), _const_spec((1, D)), _const_spec((1, HG_DH)),
         pl.BlockSpec((1, HG_HEADS, HG_DH, HG_DH), lambda i: (nb - 1 - i, 0, 0, 0)), pl.BlockSpec((HG_BLK, D), rev)],
        [pl.BlockSpec((HG_BLK, 4 * D), rev), _const_spec((1, D)), _const_spec((1, HG_DH))],
        [jax.ShapeDtypeStruct((T, 4 * D), BF16), jax.ShapeDtypeStruct((1, D), F32), jax.ShapeDtypeStruct((1, HG_DH), F32)],
        [pltpu.VMEM((HG_HEADS, HG_DH, HG_DH), F32)], (p, lb, og, states, du), carry)
    return (*outs, ex)


def swa_heads(qn, kn, sinks, kprev, vprev, qp4, kp, v, z4, cos4, sin4, cosk, sink_, hots, mask, p64, p64t):
    def norm_rope(xp, g, cs, sn):
        y = xp * lax.rsqrt(jnp.mean(xp * xp, axis=-1, keepdims=True) + EPS) * g
        return y * cs + blin01_r(y, p64, p64t) * sn

    q = norm_rope(qp4, qn, cos4, sin4)
    k = norm_rope(kp, kn, cosk, sink_)
    k2 = jnp.concatenate([kprev, k], axis=1)
    v2 = jnp.concatenate([vprev, v], axis=1)
    s = jnp.where(mask, bmm_nt(q, k2) * (SW_DH ** -0.5), NEG_INF)
    sink_col = jnp.concatenate([jnp.concatenate(
        [jnp.broadcast_to(jnp.sum(sinks * hot, axis=-1, keepdims=True), (SW_BLK, 1)) for hot in hot4], axis=0)[None]
        for hot4 in hots], axis=0)
    m = lax.stop_gradient(jnp.maximum(jnp.max(s, axis=-1, keepdims=True), sink_col))
    p = jnp.exp(s - m)
    p = p / (jnp.sum(p, axis=-1, keepdims=True) + jnp.exp(sink_col - m))
    o = bmm(p, v2)
    return k, v, o * silu(z4)


def _swa_consts(first):
    qi = _iota((SW_G * SW_BLK, 2 * SW_BLK), 0) % SW_BLK
    kj = _iota((SW_G * SW_BLK, 2 * SW_BLK), 1)
    rel = qi + SW_BLK - kj
    mask = (rel >= 0) & (rel < SW_BLK) & ((kj >= SW_BLK) | jnp.logical_not(first))
    i, j = _iota((SW_KV, SW_DH, SW_DH), 1), _iota((SW_KV, SW_DH, SW_DH), 2)
    p64 = (i == (j + SW_DH // 2) % SW_DH).astype(F32)
    p64t = (j == (i + SW_DH // 2) % SW_DH).astype(F32)
    lane = _iota((1, SW_KV * SW_G), 1)
    hots = [[(lane == h * SW_G + g).astype(F32) for g in range(SW_G)] for h in range(SW_KV)]
    return mask, p64, p64t, hots


def _stack_q(ref, h, base):
    return jnp.concatenate([ref[:, pl.ds(base + h * SW_G * SW_DH + g * SW_DH, SW_DH)] for g in range(SW_G)], axis=0)


def _swa_load(p_ref):
    heads = range(SW_KV)
    return (jnp.stack([_stack_q(p_ref, h, Q0) for h in heads]),
            jnp.stack([p_ref[:, pl.ds(K0 + h * SW_DH, SW_DH)] for h in heads]),
            jnp.stack([p_ref[:, pl.ds(V0 + h * SW_DH, SW_DH)] for h in heads]),
            jnp.stack([_stack_q(p_ref, h, Z0) for h in heads]))


Q0, K0, V0, Z0 = 0, SW_QW, SW_QW + SW_KVW, SW_QW + 2 * SW_KVW
SW_IN = 2 * SW_QW + 2 * SW_KVW


def swa_fwd(name, p, cos, sin, qn, kn, sinks):
    T = p.shape[0]
    nb = T // SW_BLK

    def body(p_ref, cos_ref, sin_ref, qn_ref, kn_ref, sk_ref, u_ref, ko_ref, kprev, vprev):
        first = pl.program_id(0) == 0

        @pl.when(first)
        def _():
            kprev[...] = jnp.zeros(kprev.shape, F32)
            vprev[...] = jnp.zeros(vprev.shape, F32)
        mask, p64, p64t, hots = _swa_consts(first)
        cs, sn = cos_ref[...], sin_ref[...]
        cs4, sn4 = jnp.concatenate([cs] * SW_G, axis=0), jnp.concatenate([sn] * SW_G, axis=0)
        qp4, kp, vv, z4 = _swa_load(p_ref)
        k, v, u4 = swa_heads(qn_ref[...], kn_ref[...], sk_ref[...], kprev[...], vprev[...], qp4, kp, vv, z4,
                             cs4, sn4, cs, sn, hots, mask, p64, p64t)
        kprev[...] = k
        vprev[...] = v
        for h in range(SW_KV):
            ko_ref[:, pl.ds(h * SW_DH, SW_DH)] = k[h]
            uh = u4[h]
            for g in range(SW_G):
                u_ref[:, pl.ds(h * SW_G * SW_DH + g * SW_DH, SW_DH)] = uh[g * SW_BLK:(g + 1) * SW_BLK].astype(BF16)

    return pl.pallas_call(
        body, name=name, grid=(nb,),
        in_specs=[_row_spec(SW_BLK, SW_IN), _row_spec(SW_BLK, SW_DH), _row_spec(SW_BLK, SW_DH),
                  _const_spec((1, SW_DH)), _const_spec((1, SW_DH)), _const_spec((1, SW_KV * SW_G))],
        out_specs=[_row_spec(SW_BLK, SW_QW), _row_spec(SW_BLK, SW_KVW)],
        out_shape=[jax.ShapeDtypeStruct((T, SW_QW), BF16), jax.ShapeDtypeStruct((T, SW_KVW), F32)],
        scratch_shapes=[pltpu.VMEM((SW_KV, SW_BLK, SW_DH), F32), pltpu.VMEM((SW_KV, SW_BLK, SW_DH), F32)],
        compiler_params=_params(dimension_semantics=("arbitrary",)),
    )(p, cos, sin, qn, kn, sinks)


def swa_bwd(name, p, kpost, cos, sin, qn, kn, sinks, du, carry=()):
    T = p.shape[0]
    nb = T // SW_BLK

    def body(p_ref, pprev_ref, kprev_ref, cos_ref, sin_ref, qn_ref, kn_ref, sk_ref, du_ref,
             dp_ref, dqn_ref, dkn_ref, dsk_ref, dk_scr, dv_scr):
        i = pl.program_id(0)
        first = i == nb - 1

        @pl.when(i == 0)
        def _():
            dk_scr[...] = jnp.zeros(dk_scr.shape, F32)
            dv_scr[...] = jnp.zeros(dv_scr.shape, F32)
            dqn_ref[...] = jnp.zeros(dqn_ref.shape, F32)
            dkn_ref[...] = jnp.zeros(dkn_ref.shape, F32)
            dsk_ref[...] = jnp.zeros(dsk_ref.shape, F32)
        mask, p64, p64t, hots = _swa_consts(first)
        cs, sn = cos_ref[...], sin_ref[...]
        cs4, sn4 = jnp.concatenate([cs] * SW_G, axis=0), jnp.concatenate([sn] * SW_G, axis=0)
        heads = range(SW_KV)
        fn = lambda qn, kn, sk, kpv, vpv, qp4, kp, v, z4: swa_heads(
            qn, kn, sk, kpv, vpv, qp4, kp, v, z4, cs4, sn4, cs, sn, hots, mask, p64, p64t)
        kpv = jnp.stack([kprev_ref[:, pl.ds(h * SW_DH, SW_DH)] for h in heads])
        vpv = jnp.stack([pprev_ref[:, pl.ds(V0 + h * SW_DH, SW_DH)] for h in heads])
        _, f = jax.vjp(fn, qn_ref[...], kn_ref[...], sk_ref[...], kpv, vpv, *_swa_load(p_ref))
        du4 = jnp.stack([_stack_q(du_ref, h, 0).astype(F32) for h in heads])
        dqn, dkn, dsk, dkpv, dvpv, dq4, dkp, dv, dz4 = f((dk_scr[...], dv_scr[...], du4))
        dk_scr[...] = dkpv
        dv_scr[...] = dvpv
        for h in heads:
            dp_ref[:, pl.ds(K0 + h * SW_DH, SW_DH)] = dkp[h].astype(BF16)
            dp_ref[:, pl.ds(V0 + h * SW_DH, SW_DH)] = dv[h].astype(BF16)
            dqh, dzh = dq4[h], dz4[h]
            for g in range(SW_G):
                c0 = h * SW_G * SW_DH + g * SW_DH
                dp_ref[:, pl.ds(Q0 + c0, SW_DH)] = dqh[g * SW_BLK:(g + 1) * SW_BLK].astype(BF16)
                dp_ref[:, pl.ds(Z0 + c0, SW_DH)] = dzh[g * SW_BLK:(g + 1) * SW_BLK].astype(BF16)
        dqn_ref[...] += dqn
        dkn_ref[...] += dkn
        dsk_ref[...] += dsk

    rev = lambda i: (nb - 1 - i, 0)
    prev = lambda i: (jnp.maximum(nb - 2 - i, 0), 0)
    outs, ex = _scan_call(
        body, name, nb,
        [pl.BlockSpec((SW_BLK, SW_IN), rev), pl.BlockSpec((SW_BLK, SW_IN), prev),
         pl.BlockSpec((SW_BLK, SW_KVW), prev), pl.BlockSpec((SW_BLK, SW_DH), rev), pl.BlockSpec((SW_BLK, SW_DH), rev),
         _const_spec((1, SW_DH)), _const_spec((1, SW_DH)), _const_spec((1, SW_KV * SW_G)), pl.BlockSpec((SW_BLK, SW_QW), rev)],
        [pl.BlockSpec((SW_BLK, SW_IN), rev), _const_spec((1, SW_DH)), _const_spec((1, SW_DH)), _const_spec((1, SW_KV * SW_G))],
        [jax.ShapeDtypeStruct((T, SW_IN), BF16), jax.ShapeDtypeStruct((1, SW_DH), F32),
         jax.ShapeDtypeStruct((1, SW_DH), F32), jax.ShapeDtypeStruct((1, SW_KV * SW_G), F32)],
        [pltpu.VMEM((SW_KV, SW_BLK, SW_DH), F32), pltpu.VMEM((SW_KV, SW_BLK, SW_DH), F32)],
        (p, p, kpost, cos, sin, qn, kn, sinks, du), carry)
    return (*outs, ex)


CONV_TC, CONV_TL, HALO = 512, 1024, 8


def _conv_taps(xe, w_ref):
    acc = w_ref[pl.ds(CONV_K - 1, 1), :] * xe
    for k in range(CONV_K - 1):
        acc = acc + w_ref[pl.ds(k, 1), :] * pltpu.roll(xe, CONV_K - 1 - k, 0)
    return acc


def conv_fwd(name, x, w8):
    T, C = x.shape
    tc = min(CONV_TC, T)
    nt = T // tc

    def body(xp_ref, x_ref, w_ref, o_ref):
        prev = jnp.where(pl.program_id(0) == 0, 0.0, xp_ref[...])
        xe = jnp.concatenate([prev, x_ref[...]], axis=0)
        o_ref[...] = silu(_conv_taps(xe, w_ref)[HALO:])

    return pl.pallas_call(
        body, name=name, grid=(nt, C // CONV_TL),
        in_specs=[pl.BlockSpec((HALO, CONV_TL), lambda i, j: (jnp.maximum(i * (tc // HALO) - 1, 0), j)),
                  pl.BlockSpec((tc, CONV_TL), lambda i, j: (i, j)), pl.BlockSpec((HALO, CONV_TL), lambda i, j: (0, j))],
        out_specs=pl.BlockSpec((tc, CONV_TL), lambda i, j: (i, j)),
        out_shape=jax.ShapeDtypeStruct((T, C), F32),
        compiler_params=_params(dimension_semantics=("arbitrary", "arbitrary")),
    )(x, x, w8)


def conv_bwd(name, x, dy, w8):
    T, C = x.shape
    tc = min(CONV_TC, T)
    nt = T // tc
    n = tc + 2 * HALO

    def body(xp_ref, x_ref, xn_ref, dy_ref, dyn_ref, w_ref, dx_ref, dw_ref):
        i = pl.program_id(1)

        @pl.when(i == 0)
        def _():
            dw_ref[...] = jnp.zeros(dw_ref.shape, F32)
        prev = jnp.where(i == 0, 0.0, xp_ref[...])
        xe = jnp.concatenate([prev, x_ref[...], xn_ref[...]], axis=0)
        dye = jnp.concatenate([jnp.zeros((HALO, CONV_TL), F32), dy_ref[...], jnp.where(i == nt - 1, 0.0, dyn_ref[...])], axis=0)
        ce = _conv_taps(xe, w_ref)
        sg = sigmoid(ce)
        dce = dye * (sg * (1.0 + ce * (1.0 - sg)))
        dx = w_ref[pl.ds(CONV_K - 1, 1), :] * dce
        for k in range(CONV_K - 1):
            dx = dx + w_ref[pl.ds(k, 1), :] * pltpu.roll(dce, n - (CONV_K - 1 - k), 0)
        dx_ref[...] = dx[HALO:HALO + tc].astype(BF16)
        dcur = dce[HALO:HALO + tc]
        for k in range(CONV_K):
            xs = xe if k == CONV_K - 1 else pltpu.roll(xe, CONV_K - 1 - k, 0)
            dw_ref[pl.ds(k, 1), :] += jnp.sum(dcur * xs[HALO:HALO + tc], axis=0, keepdims=True)

    nh = T // HALO
    prev_map = lambda j, i: (jnp.maximum(i * (tc // HALO) - 1, 0), j)
    next_map = lambda j, i: (jnp.minimum((i + 1) * (tc // HALO), nh - 1), j)
    cur_map = lambda j, i: (i, j)
    return pl.pallas_call(
        body, name=name, grid=(C // CONV_TL, nt),
        in_specs=[pl.BlockSpec((HALO, CONV_TL), prev_map), pl.BlockSpec((tc, CONV_TL), cur_map),
                  pl.BlockSpec((HALO, CONV_TL), next_map), pl.BlockSpec((tc, CONV_TL), cur_map),
                  pl.BlockSpec((HALO, CONV_TL), next_map), pl.BlockSpec((HALO, CONV_TL), lambda j, i: (0, j))],
        out_specs=[pl.BlockSpec((tc, CONV_TL), cur_map), pl.BlockSpec((HALO, CONV_TL), lambda j, i: (0, j))],
        out_shape=[jax.ShapeDtypeStruct((T, C), BF16), jax.ShapeDtypeStruct((HALO, C), F32)],
        compiler_params=_params(dimension_semantics=("arbitrary", "arbitrary")),
    )(x, x, x, dy, dy, w8)


GD_VH = 16
GD_HB = 16


def _unit_lower_inverse(a, eye_f):
    n = -a
    t = eye_f + n
    p = n
    for _ in range(5):
        p = bmmf(p, p)
        t = t + bmmf(t, p)
    return t


@jax.custom_vjp
def unit_lower_solve2(a, r1, r2, eye_f):
    t = _unit_lower_inverse(a, eye_f)
    return bmmf(t, r1), bmmf(t, r2)


def _uls2_fwd(a, r1, r2, eye_f):
    t = _unit_lower_inverse(a, eye_f)
    x1, x2 = bmmf(t, r1), bmmf(t, r2)
    return (x1, x2), (t, x1, x2, eye_f)


def _uls2_bwd(res, g):
    t, x1, x2, eye_f = res
    high = lax.Precision.HIGH
    d1 = _bdot(t, g[0], 1, 1, high)
    d2 = _bdot(t, g[1], 1, 1, high)
    da = -(_bdot(d1, x1, 2, 2, high) + _bdot(d2, x2, 2, 2, high))
    return da, d1, d2, jnp.zeros_like(eye_f)


unit_lower_solve2.defvjp(_uls2_fwd, _uls2_bwd)


def gd_heads(al, dt, og, s, qc, kc, v, z, a_col, b_col, cst):
    lt_t, eye, incl, strict = cst
    eye_f = eye.astype(F32)
    nh = s.shape[0]
    q = qc * lax.rsqrt(jnp.sum(qc * qc, axis=-1, keepdims=True) + EPS) * (GD_DK ** -0.5)
    k = kc * lax.rsqrt(jnp.sum(kc * kc, axis=-1, keepdims=True) + EPS)
    beta = sigmoid(b_col)
    g = -jnp.exp(al) * softplus(a_col + dt)
    d_mat = jnp.broadcast_to(jnp.sum(g * lt_t, axis=1, keepdims=True), (nh, GD_CHUNK, GD_CHUNK))
    d = jnp.sum(jnp.where(eye, d_mat, 0.0), axis=-1, keepdims=True)
    dec = jnp.exp(jnp.where(incl, d - d_mat, NEG_INF))
    kb = k * beta
    u, w = unit_lower_solve2(bmm_nt(kb, k) * dec * strict, v * beta, kb * jnp.exp(d), eye_f)
    v_new = u - bmm(w, s)
    o = bmm(q * jnp.exp(d), s) + bmm(bmm_nt(q, k) * dec, v_new)
    dl = jnp.sum(g, axis=1, keepdims=True)
    s_new = s * jnp.exp(dl) + bmm_tn(k * jnp.exp(dl - d), v_new)
    on = o * lax.rsqrt(jnp.mean(o * o, axis=-1, keepdims=True) + EPS) * og
    return s_new, on * silu(z)


def _gd_consts():
    r, c = _iota((GD_CHUNK, GD_CHUNK), 0), _iota((GD_CHUNK, GD_CHUNK), 1)
    return (r <= c).astype(F32), r == c, r >= c, (r > c).astype(F32)


def _gd_load(qkv_ref, z_ref, ab_v, al_v, dt_v, heads):
    lane = _iota((1, 128), 1)
    hot_a = [(lane == h).astype(F32) for h in heads]
    hot_b = [(lane == GD_VH + h).astype(F32) for h in heads]
    col = lambda src, hot: jnp.stack([jnp.sum(src * m, axis=-1, keepdims=True) for m in hot])
    ops = (col(al_v, hot_a), col(dt_v, hot_a),
           jnp.stack([qkv_ref[:, pl.ds((h // 2) * GD_DK, GD_DK)] for h in heads]),
           jnp.stack([qkv_ref[:, pl.ds(GD_QKW + (h // 2) * GD_DK, GD_DK)] for h in heads]),
           jnp.stack([qkv_ref[:, pl.ds(2 * GD_QKW + h * GD_DK, GD_DK)] for h in heads]),
           jnp.stack([z_ref[:, pl.ds(h * GD_DK, GD_DK)] for h in heads]),
           col(ab_v, hot_a), col(ab_v, hot_b))
    return ops, hot_a, hot_b


def gd_fwd(name, qkv, z, ab, al, dt, og):
    T = qkv.shape[0]
    nc = T // GD_CHUNK

    def body(qkv_ref, z_ref, ab_ref, al_ref, dt_ref, og_ref, u_ref, st_ref, s_scr):
        @pl.when(pl.program_id(0) == 0)
        def _():
            s_scr[...] = jnp.zeros(s_scr.shape, F32)
        st_ref[0] = s_scr[...]
        cst = _gd_consts()
        ab_v, al_v, dt_v, og_v = ab_ref[...], al_ref[...], dt_ref[...], og_ref[...]
        for b0 in range(0, GD_VH, GD_HB):
            heads = list(range(b0, b0 + GD_HB))
            (alh, dth, q, k, v, zz, a_col, b_col), _, _ = _gd_load(qkv_ref, z_ref, ab_v, al_v, dt_v, heads)
            s_new, u = gd_heads(alh, dth, og_v, s_scr[pl.ds(b0, GD_HB)], q, k, v, zz, a_col, b_col, cst)
            s_scr[pl.ds(b0, GD_HB)] = s_new
            for i, h in enumerate(heads):
                u_ref[:, pl.ds(h * GD_DK, GD_DK)] = u[i].astype(BF16)

    return pl.pallas_call(
        body, name=name, grid=(nc,),
        in_specs=[_row_spec(GD_CHUNK, GD_QKV), _row_spec(GD_CHUNK, GD_VW), _row_spec(GD_CHUNK, 128),
                  _const_spec((1, 128)), _const_spec((1, 128)), _const_spec((1, 128))],
        out_specs=[_row_spec(GD_CHUNK, GD_VW), pl.BlockSpec((1, GD_VH, GD_DK, GD_DK), lambda i: (i, 0, 0, 0))],
        out_shape=[jax.ShapeDtypeStruct((T, GD_VW), BF16), jax.ShapeDtypeStruct((nc, GD_VH, GD_DK, GD_DK), F32)],
        scratch_shapes=[pltpu.VMEM((GD_VH, GD_DK, GD_DK), F32)],
        compiler_params=_params(dimension_semantics=("arbitrary",)),
    )(qkv, z, ab, al, dt, og)


def gd_bwd(name, qkv, z, ab, al, dt, og, states, du, carry=()):
    T = qkv.shape[0]
    nc = T // GD_CHUNK

    def body(qkv_ref, z_ref, ab_ref, al_ref, dt_ref, og_ref, st_ref, du_ref,
             dqkv_ref, dz_ref, dab_ref, dal_ref, ddt_ref, dog_ref, ds_scr):
        @pl.when(pl.program_id(0) == 0)
        def _():
            ds_scr[...] = jnp.zeros(ds_scr.shape, F32)
            dal_ref[...] = jnp.zeros(dal_ref.shape, F32)
            ddt_ref[...] = jnp.zeros(ddt_ref.shape, F32)
            dog_ref[...] = jnp.zeros(dog_ref.shape, F32)
        cst = _gd_consts()
        ab_v, al_v, dt_v, og_v = ab_ref[...], al_ref[...], dt_ref[...], og_ref[...]
        dab = jnp.zeros((GD_CHUNK, 128), F32)
        dal_row = jnp.zeros((1, 128), F32)
        ddt_row = jnp.zeros((1, 128), F32)
        dog_row = jnp.zeros((1, 128), F32)
        for b0 in range(0, GD_VH, GD_HB):
            heads = list(range(b0, b0 + GD_HB))
            (alh, dth, q, k, v, zz, a_col, b_col), hot_a, hot_b = _gd_load(qkv_ref, z_ref, ab_v, al_v, dt_v, heads)
            fn = lambda *a: gd_heads(*a, cst)
            _, f = jax.vjp(fn, alh, dth, og_v, st_ref[0, pl.ds(b0, GD_HB)], q, k, v, zz, a_col, b_col)
            du_h = jnp.stack([du_ref[:, pl.ds(h * GD_DK, GD_DK)].astype(F32) for h in heads])
            dalh, ddth, dog, ds, dq, dk, dv, dz, da_col, db_col = f((ds_scr[pl.ds(b0, GD_HB)], du_h))
            ds_scr[pl.ds(b0, GD_HB)] = ds
            dog_row = dog_row + dog
            for i, h in enumerate(heads):
                dqkv_ref[:, pl.ds(2 * GD_QKW + h * GD_DK, GD_DK)] = dv[i]
                dz_ref[:, pl.ds(h * GD_DK, GD_DK)] = dz[i].astype(BF16)
                dab = dab + da_col[i] * hot_a[i] + db_col[i] * hot_b[i]
                dal_row = dal_row + dalh[i] * hot_a[i]
                ddt_row = ddt_row + ddth[i] * hot_a[i]
                if h % 2 == 0:
                    dqkv_ref[:, pl.ds((h // 2) * GD_DK, GD_DK)] = dq[i] + dq[i + 1]
                    dqkv_ref[:, pl.ds(GD_QKW + (h // 2) * GD_DK, GD_DK)] = dk[i] + dk[i + 1]
        dab_ref[...] = dab
        dal_ref[...] += dal_row
        ddt_ref[...] += ddt_row
        dog_ref[...] += dog_row

    rev = lambda i: (nc - 1 - i, 0)
    outs, ex = _scan_call(
        body, name, nc,
        [pl.BlockSpec((GD_CHUNK, GD_QKV), rev), pl.BlockSpec((GD_CHUNK, GD_VW), rev), pl.BlockSpec((GD_CHUNK, 128), rev),
         _const_spec((1, 128)), _const_spec((1, 128)), _const_spec((1, 128)),
         pl.BlockSpec((1, GD_VH, GD_DK, GD_DK), lambda i: (nc - 1 - i, 0, 0, 0)), pl.BlockSpec((GD_CHUNK, GD_VW), rev)],
        [pl.BlockSpec((GD_CHUNK, GD_QKV), rev), pl.BlockSpec((GD_CHUNK, GD_VW), rev), pl.BlockSpec((GD_CHUNK, 128), rev),
         _const_spec((1, 128)), _const_spec((1, 128)), _const_spec((1, 128))],
        [jax.ShapeDtypeStruct((T, GD_QKV), F32), jax.ShapeDtypeStruct((T, GD_VW), BF16),
         jax.ShapeDtypeStruct((T, 128), F32)] + [jax.ShapeDtypeStruct((1, 128), F32)] * 3,
        [pltpu.VMEM((GD_VH, GD_DK, GD_DK), F32)], (qkv, z, ab, al, dt, og, states, du), carry)
    return (*outs, ex)


def _my_pos():
    return lax.axis_index("x"), lax.axis_index("y"), lax.axis_index("c")


def _peers(pos):
    out = []
    for k in range(1, N_DEV):
        dev = tuple(1 - p if (k >> s) & 1 else p for p, s in zip(pos, (2, 1, 0)))
        out.append((dev, 4 * dev[0] + 2 * dev[1] + dev[2]))
    return out


_ANY = pl.BlockSpec(memory_space=pl.ANY)
_COMM_SCRATCH = [pltpu.SemaphoreType.DMA((N_DEV - 1,)), pltpu.SemaphoreType.DMA((N_DEV - 1,)), pltpu.SemaphoreType.DMA(())]
GATHER, SCATTER = "gather", "scatter"


def _exchange_descs(pos, kinds, x_refs, o_refs, send_sems, recv_sems, local_sems, with_recvs=True):
    me = 4 * pos[0] + 2 * pos[1] + pos[2]
    peers = _peers(pos)
    local, sends, recvs = [], [], []
    for i, (kind, x, o) in enumerate(zip(kinds, x_refs, o_refs)):
        local.append(pltpu.make_async_copy(x if kind == GATHER else x.at[me], o.at[me], local_sems.at[i]))
        for k, (dev, pid) in enumerate(peers):
            s = (N_DEV - 1) * i + k
            src = x if kind == GATHER else x.at[pid]
            sends.append(pltpu.make_async_remote_copy(src, o.at[me], send_sems.at[s], recv_sems.at[s],
                                                      device_id=dev, device_id_type=MESH))
            if with_recvs:
                recvs.append(pltpu.make_async_remote_copy(src, o.at[pid], send_sems.at[s], recv_sems.at[s],
                                                          device_id=dev, device_id_type=MESH))
    return local, sends, recvs


def _scan_call(body, name, nsteps, in_specs, out_specs, out_shape, scratch_shapes, args, carry=()):
    n_in, n_out, n_scr, n_ex = len(in_specs), len(out_specs), len(scratch_shapes), len(carry)
    params = _params(dimension_semantics=("arbitrary",))
    if not carry:
        outs = pl.pallas_call(body, name=name, grid=(nsteps,), in_specs=in_specs, out_specs=out_specs, out_shape=out_shape,
                              scratch_shapes=scratch_shapes, compiler_params=params)(*args)
        return outs, []
    kinds = [k for k, _ in carry]
    xs = [a for _, a in carry]
    ex_shape = [jax.ShapeDtypeStruct(((N_DEV,) + a.shape) if k == GATHER else a.shape, a.dtype) for k, a in carry]

    def full(*refs):
        ins, x_refs = refs[:n_in], refs[n_in:n_in + n_ex]
        outs = refs[n_in + n_ex:n_in + n_ex + n_out]
        o_refs = refs[n_in + n_ex + n_out:n_in + 2 * n_ex + n_out]
        scr = refs[n_in + 2 * n_ex + n_out:n_in + 2 * n_ex + n_out + n_scr]
        sems = refs[n_in + 2 * n_ex + n_out + n_scr:]
        pos = _my_pos()
        step = pl.program_id(0)

        @pl.when(step == 0)
        def _():
            local, sends, _ = _exchange_descs(pos, kinds, x_refs, o_refs, *sems, with_recvs=False)
            for cp in local + sends:
                cp.start()

        body(*ins, *outs, *scr)

        @pl.when(step == nsteps - 1)
        def _():
            local, sends, recvs = _exchange_descs(pos, kinds, x_refs, o_refs, *sems)
            for cp in recvs:
                cp.wait_recv()
            for cp in sends:
                cp.wait_send()
            for cp in local:
                cp.wait()

    n_sem = (N_DEV - 1) * n_ex
    outs = pl.pallas_call(
        full, name=name, grid=(nsteps,), in_specs=list(in_specs) + [_ANY] * n_ex, out_specs=list(out_specs) + [_ANY] * n_ex,
        out_shape=list(out_shape) + ex_shape,
        scratch_shapes=list(scratch_shapes) + [pltpu.SemaphoreType.DMA((n_sem,)), pltpu.SemaphoreType.DMA((n_sem,)),
                                               pltpu.SemaphoreType.DMA((n_ex,))],
        compiler_params=params)(*args, *xs)
    return outs[:n_out], list(outs[n_out:])


def all_gather(name, x):
    def body(x_ref, o_ref, send_sems, recv_sems, local_sem):
        pos = _my_pos()
        me = 4 * pos[0] + 2 * pos[1] + pos[2]
        peers = _peers(pos)
        mine = pltpu.make_async_copy(x_ref, o_ref.at[me], local_sem)
        mine.start()
        sends = [pltpu.make_async_remote_copy(x_ref, o_ref.at[me], send_sems.at[k], recv_sems.at[k],
                                              device_id=dev, device_id_type=MESH) for k, (dev, _) in enumerate(peers)]
        for cp in sends:
            cp.start()
        for k, (dev, pid) in enumerate(peers):
            pltpu.make_async_remote_copy(x_ref, o_ref.at[pid], send_sems.at[k], recv_sems.at[k],
                                         device_id=dev, device_id_type=MESH).wait_recv()
        for cp in sends:
            cp.wait_send()
        mine.wait()

    return pl.pallas_call(
        body, name=name, out_shape=jax.ShapeDtypeStruct((N_DEV,) + x.shape, x.dtype),
        in_specs=[_ANY], out_specs=_ANY, scratch_shapes=_COMM_SCRATCH,
    )(x)


def lb_rows(r0, r1, r2, r3):
    mx = jnp.maximum(jnp.maximum(r0, r1), jnp.maximum(r2, r3))
    e = [jnp.exp(r - mx) for r in (r0, r1, r2, r3)]
    inv = 1.0 / (e[0] + e[1] + e[2] + e[3])
    c0 = e[0] * inv
    c1 = c0 + e[1] * inv
    c2 = c1 + e[2] * inv
    c3 = c2 + e[3] * inv
    return c0 - c0, c1 - c0, c2 - c0, c3 - c0


def mod_partial(name, c_all, ada_w):
    n, _, w = ada_w.shape

    def body(c_ref, w_ref, o_ref):
        o_ref[0] = mm(c_ref[...], w_ref[0])

    return pl.pallas_call(
        body, name=name, grid=(n,),
        in_specs=[_const_spec((N_DEV, D)), pl.BlockSpec((1, D, w), lambda i: (i, 0, 0))],
        out_specs=pl.BlockSpec((1, N_DEV, w), lambda i: (i, 0, 0)),
        out_shape=jax.ShapeDtypeStruct((n, N_DEV, w), F32),
    )(c_all, ada_w)


def prep(name, modp, ada_b, hgrn_lb):
    def body(mp_ref, b_ref, lb_ref, mod_ref, lbo_ref):
        mod_ref[...] = mp_ref[...] + b_ref[...]
        out = lb_rows(*[lb_ref[pl.ds(i, 1), :] for i in range(4)])
        for i in range(4):
            lbo_ref[pl.ds(i, 1), :] = out[i]

    return pl.pallas_call(
        body, name=name,
        out_shape=[jax.ShapeDtypeStruct(modp.shape, F32), jax.ShapeDtypeStruct(hgrn_lb.shape, F32)],
    )(modp, ada_b, hgrn_lb)


def lb_grad(name, hgrn_lb, dlb_parts):
    def body(lb_ref, d_ref, o_ref):
        cts = []
        for i in range(4):
            acc = d_ref[0, pl.ds(i, 1), :]
            for p in range(1, N_DEV):
                acc = acc + d_ref[p, pl.ds(i, 1), :]
            cts.append(acc)
        _, f = jax.vjp(lb_rows, *[lb_ref[pl.ds(i, 1), :] for i in range(4)])
        for i, g in enumerate(f(tuple(cts))):
            o_ref[pl.ds(i, 1), :] = g

    return pl.pallas_call(body, name=name, out_shape=jax.ShapeDtypeStruct(hgrn_lb.shape, F32))(hgrn_lb, dlb_parts)


def ada_grad(name, c_t, dm):
    n, _, w = dm.shape

    def body(c_ref, d_ref, o_ref):
        acc = c_ref[:, pl.ds(0, 1)] * d_ref[0, pl.ds(0, 1), :]
        for b in range(1, N_DEV):
            acc = acc + c_ref[:, pl.ds(b, 1)] * d_ref[0, pl.ds(b, 1), :]
        o_ref[0] = acc

    return pl.pallas_call(
        body, name=name, grid=(n,),
        in_specs=[_const_spec((D, N_DEV)), pl.BlockSpec((1, N_DEV, w), lambda i: (i, 0, 0))],
        out_specs=pl.BlockSpec((1, D, w), lambda i: (i, 0, 0)),
        out_shape=jax.ShapeDtypeStruct((n, D, w), F32),
    )(c_t, dm)


ADAMW_ROWS = 256


def adamw(name, w, m, v, gparts):
    shape = w.shape
    L = shape[-1]
    R = 1
    for s in shape[:-1]:
        R *= s
    P = gparts.shape[0]
    tr = min(R, ADAMW_ROWS)

    def body(w_ref, m_ref, v_ref, g_ref, go_ref, d_ref, mo_ref, vo_ref):
        g = g_ref[0].astype(F32)
        for p in range(1, P):
            g = g + g_ref[p].astype(F32)
        mn = ADAM_B1 * m_ref[...] + (1.0 - ADAM_B1) * g
        vn = ADAM_B2 * v_ref[...] + (1.0 - ADAM_B2) * (g * g)
        m_hat = mn / (1.0 - ADAM_B1 ** ADAM_STEP)
        v_hat = vn / (1.0 - ADAM_B2 ** ADAM_STEP)
        go_ref[...] = g
        d_ref[...] = -ADAM_LR * (m_hat / (jnp.sqrt(v_hat) + ADAM_EPS) + ADAM_WD * w_ref[...])
        mo_ref[...] = mn
        vo_ref[...] = vn

    spec = pl.BlockSpec((tr, L), lambda i: (i, 0))
    outs = pl.pallas_call(
        body, name=name, grid=(R // tr,),
        in_specs=[spec, spec, spec, pl.BlockSpec((P, tr, L), lambda i: (0, i, 0))],
        out_specs=[spec] * 4, out_shape=[jax.ShapeDtypeStruct((R, L), F32)] * 4,
        compiler_params=_params(dimension_semantics=("arbitrary",)),
    )(w.reshape(R, L), m.reshape(R, L), v.reshape(R, L), gparts.reshape(P, R, L))
    return [o.reshape(shape) for o in outs]


def _cols_from(g):
    _, n, k, w8 = g.shape
    return jnp.transpose(g, (1, 2, 0, 3)).reshape(n, k, N_DEV * w8)


def _rows_from(g):
    _, n, k8, nn = g.shape
    return jnp.transpose(g, (1, 0, 2, 3)).reshape(n, N_DEV * k8, nn)


def _scatter_cols(dw, wire=BF16):
    n, k, nn = dw.shape
    return jnp.transpose(dw.reshape(n, k, N_DEV, nn // N_DEV), (2, 0, 1, 3)).astype(wire)


def _scatter_rows(dw, wire=BF16):
    n, k, nn = dw.shape
    return jnp.transpose(dw.reshape(n, N_DEV, k // N_DEV, nn), (1, 0, 2, 3)).astype(wire)


def _pad_lanes(v, width=128):
    return jnp.pad(v, ((0, 0), (0, width - v.shape[1])))


def _stack_rows(parts):
    n, L = len(parts), parts[0].shape[1]
    r = lax.broadcasted_iota(jnp.int32, (n, L), 0)
    out = jnp.zeros((n, L), parts[0].dtype)
    for i, p in enumerate(parts):
        out = jnp.where(r == i, jnp.broadcast_to(p, (n, L)), out)
    return out


def kernel(x, c, positions, hgrn_lb, ada_w, ada_b, norm_g, hg_in_w, hg_out_w, hg_onorm, sw_in_w, sw_out_w, sw_qnorm, sw_knorm, sw_sinks, gd_in_w, gd_out_w, gd_conv_w, gd_a_log, gd_dt_bias, gd_onorm, loss_target, m_hgrn_lb, m_ada_w, m_ada_b, m_norm_g, m_hg_in_w, m_hg_out_w, m_hg_onorm, m_sw_in_w, m_sw_out_w, m_sw_qnorm, m_sw_knorm, m_sw_sinks, m_gd_in_w, m_gd_out_w, m_gd_conv_w, m_gd_a_log, m_gd_dt_bias, m_gd_onorm, v_hgrn_lb, v_ada_w, v_ada_b, v_norm_g, v_hg_in_w, v_hg_out_w, v_hg_onorm, v_sw_in_w, v_sw_out_w, v_sw_qnorm, v_sw_knorm, v_sw_sinks, v_gd_in_w, v_gd_out_w, v_gd_conv_w, v_gd_a_log, v_gd_dt_bias, v_gd_onorm):
    pos = _my_pos()
    me = 4 * pos[0] + 2 * pos[1] + pos[2]
    x0 = x[0]
    tgt = loss_target[0]
    n_layers = norm_g.shape[0]
    aw = ada_w.shape[2]

    c_all = all_gather("ag_c", c)[:, 0, :]
    modp = all_gather("ag_mod", mod_partial("mod_partial", c_all, ada_w))
    modp_mine = lax.dynamic_index_in_dim(modp, me, axis=2, keepdims=False)
    modp_mine = jnp.transpose(modp_mine, (1, 0, 2)).reshape(n_layers, N_DEV * aw)
    mod, lb_all = prep("prep", modp_mine, ada_b, hgrn_lb)
    shift, scale, gate = mod[:, :D], mod[:, D:2 * D], mod[:, 2 * D:]
    row = lambda a, i: a[i:i + 1]

    w_hg_in = [_cols_from(all_gather("ag_hg_in0", hg_in_w[0:1].astype(BF16)))[0], None]
    later = [(GATHER, hg_in_w[1:2].astype(BF16)), (GATHER, hg_out_w.astype(BF16)), (GATHER, sw_in_w.astype(BF16)),
             (GATHER, sw_out_w.astype(BF16)), (GATHER, gd_in_w.astype(BF16)), (GATHER, gd_out_w.astype(BF16)),
             (GATHER, gd_conv_w)]
    gd_al, gd_dt = _pad_lanes(gd_a_log), _pad_lanes(gd_dt_bias)

    inv_freq = ROPE_THETA ** (-jnp.arange(0, SW_DH, 2, dtype=F32) / SW_DH)
    ang = positions[0].astype(F32)[:, None] * inv_freq
    cos, sin = jnp.cos(ang), jnp.sin(ang)
    cos64, sin64 = jnp.concatenate([cos, cos], axis=-1), jnp.concatenate([-sin, sin], axis=-1)

    xs, hs, ys, saved = [x0], [], [], []
    last = n_layers - 1
    h = norm_mod("norm0", x0, row(norm_g, 0), row(scale, 0), row(shift, 0))
    for i in range(n_layers):
        kind, j = i % 3, i // 3
        hs.append(h)
        if kind == 0:
            p = matmul(f"hg_in{i}", h, w_hg_in[j])
            u, st, got = hg_fwd(f"hg_fwd{i}", p, row(lb_all, i), row(hg_onorm, j), later if i == 0 else ())
            if i == 0:
                w_hg_in[1] = _cols_from(got[0])[0]
                w_hg_out = _rows_from(got[1])
                w_sw_in, w_sw_out = _cols_from(got[2])[0], _rows_from(got[3])[0]
                w_gd_in, w_gd_out = _cols_from(got[4])[0], _rows_from(got[5])[0]
                w_gd_qkv, w_gd_z = w_gd_in[:, :GD_QKV], w_gd_in[:, GD_QKV:GD_QKV + GD_VW]
                w_gd_ab = _pad_lanes(w_gd_in[:, GD_QKV + GD_VW:])
                conv_w8 = jnp.pad(_cols_from(got[6])[0], ((0, HALO - CONV_K), (0, 0)))
            w_out = w_hg_out[j]
            saved.append((p, u, st))
        elif kind == 1:
            p = matmul(f"sw_in{i}", h, w_sw_in)
            u, kpost = swa_fwd(f"sw_fwd{i}", p, cos64, sin64, row(sw_qnorm, j), row(sw_knorm, j), row(sw_sinks, j))
            w_out = w_sw_out
            saved.append((p, u, kpost))
        else:
            xq = matmul(f"gd_qkv{i}", h, w_gd_qkv)
            zz = matmul(f"gd_z{i}", h, w_gd_z)
            ab = matmul(f"gd_ab{i}", h, w_gd_ab)
            cv = conv_fwd(f"gd_conv{i}", xq, conv_w8)
            u, st = gd_fwd(f"gd_fwd{i}", cv, zz, ab, gd_al, gd_dt, row(gd_onorm, j))
            w_out = w_gd_out
            saved.append((xq, zz, ab, cv, u, st))
        if i < last:
            y, xn, h = out_resid_norm(f"out{i}", u, w_out, xs[i], row(gate, i), row(norm_g, i + 1), row(scale, i + 1),
                                      row(shift, i + 1))
            xs.append(xn)
        else:
            y, dx, dy, loss_acc, dgate_last = out_loss(f"out{i}", u, w_out, xs[i], tgt, row(gate, i))
        ys.append(y)
    loss = lax.psum(loss_acc[0, 0], ("x", "y", "c"))

    dgate = [None] * n_layers
    dgate[last] = dgate_last
    dg_norm, dscale, dshift = [None] * n_layers, [None] * n_layers, [None] * n_layers
    dlb = [jnp.zeros((1, D), F32)] * n_layers
    d_hg_on = [None] * 2
    rs_hg_in, rs_hg_out = [None] * 2, [None] * 2
    pending = ()
    for i in range(last, -1, -1):
        kind, j = i % 3, i // 3
        h = hs[i]
        if kind == 0:
            p, u, st = saved[i]
            du = matmul_nt(f"hg_du{i}", [(dy, w_hg_out[j])])
            d_out = matmul_tn(f"hg_dwo{i}", u, dy)
            dp, dlb_i, dog, got = hg_bwd(f"hg_bwd{i}", p, row(lb_all, i), row(hg_onorm, j), st, du, pending)
            if got:
                rs_sw_in, rs_sw_out = got
            dlb[i] = dlb_i
            d_hg_on[j] = dog
            dh_pairs = [(dp, w_hg_in[j])]
            d_in = matmul_tn(f"hg_dwi{i}", h, dp)
            pending = [(SCATTER, _scatter_cols(d_in[None])), (SCATTER, _scatter_rows(d_out[None]))]
        elif kind == 1:
            p, u, kpost = saved[i]
            du = matmul_nt(f"sw_du{i}", [(dy, w_sw_out)])
            d_out = matmul_tn(f"sw_dwo{i}", u, dy)
            dp, d_qn, d_kn, d_sk, got = swa_bwd(f"sw_bwd{i}", p, kpost, cos64, sin64, row(sw_qnorm, j), row(sw_knorm, j),
                                                row(sw_sinks, j), du, pending)
            rs_gd_in, rs_gd_out, rs_gd_conv = got
            dh_pairs = [(dp, w_sw_in)]
            d_in = matmul_tn(f"sw_dwi{i}", h, dp)
            pending = [(SCATTER, _scatter_cols(d_in[None])), (SCATTER, _scatter_rows(d_out[None]))]
        else:
            xq, zz, ab, cv, u, st = saved[i]
            du = matmul_nt(f"gd_du{i}", [(dy, w_gd_out)])
            d_out = matmul_tn(f"gd_dwo{i}", u, dy)
            dcv, dz, dab, d_al, d_dt, d_gd_on, got = gd_bwd(f"gd_bwd{i}", cv, zz, ab, gd_al, gd_dt, row(gd_onorm, j), st, du,
                                                            pending)
            rs_hg_in[1], rs_hg_out[1] = got
            dxq, d_conv8 = conv_bwd(f"gd_dconv{i}", xq, dcv, conv_w8)
            dh_pairs = [(dxq, w_gd_qkv), (dz, w_gd_z), (dab, w_gd_ab)]
            d_in = jnp.concatenate([matmul_tn(f"gd_dwq{i}", h, dxq), matmul_tn(f"gd_dwz{i}", h, dz),
                                    matmul_tn(f"gd_dwab{i}", h, dab)[:, :2 * GD_VH]], axis=1)
            pending = [(SCATTER, _scatter_cols(d_in[None])), (SCATTER, _scatter_rows(d_out[None])),
                       (SCATTER, _scatter_cols(d_conv8[None, :CONV_K], F32))]
        if i > 0:
            (dx, dy, dg_norm[i], dscale[i], dshift[i], dgate[i - 1]), _ = dh_bwd_row(
                f"dh{i}", dh_pairs, xs[i], dx, row(norm_g, i), row(scale, i), row(shift, i), ys[i - 1], row(gate, i - 1))
        else:
            (dx, dg_norm[i], dscale[i], dshift[i]), (rs_hg_in[0], rs_hg_out[0]) = dh_bwd_row(
                f"dh{i}", dh_pairs, xs[i], dx, row(norm_g, i), row(scale, i), row(shift, i), carry=pending)
    grad_x = dx[None]

    dmod = jnp.concatenate([_stack_rows(dshift), _stack_rows(dscale), _stack_rows(dgate)], axis=1)
    misc = _stack_rows(d_hg_on + [d_gd_on, _pad_lanes(d_qn), _pad_lanes(d_kn), _pad_lanes(d_sk), d_al, d_dt])
    small = jnp.concatenate([_stack_rows(dlb).reshape(-1, 128), _stack_rows(dg_norm).reshape(-1, 128),
                             dmod.reshape(-1, 128), misc], axis=0)
    small_all = all_gather("ag_small", small)
    n_lb = n_layers * D // 128
    n_mod = n_layers * 3 * D // 128
    o = 0
    dlb_parts = small_all[:, o:o + n_lb].reshape(N_DEV, n_layers, D); o += n_lb
    dgn_parts = small_all[:, o:o + n_lb].reshape(N_DEV, n_layers, D); o += n_lb
    dmod_parts = small_all[:, o:o + n_mod].reshape(N_DEV, n_layers, 3 * D); o += n_mod
    dhgon_parts = small_all[:, o:o + 2]; o += 2
    dgdon_parts = small_all[:, o:o + 1]; o += 1
    dqn_parts = small_all[:, o:o + 1, :SW_DH]; o += 1
    dkn_parts = small_all[:, o:o + 1, :SW_DH]; o += 1
    dsk_parts = small_all[:, o:o + 1, :SW_KV * SW_G]; o += 1
    dal_parts = small_all[:, o:o + 1, :GD_VH]; o += 1
    ddt_parts = small_all[:, o:o + 1, :GD_VH]; o += 1

    g_lb = lb_grad("lb_grad", hgrn_lb, dlb_parts)
    dm_mine = lax.dynamic_slice_in_dim(dmod_parts, me * aw, aw, axis=2)
    g_ada_w = ada_grad("ada_grad", jnp.transpose(c_all), jnp.transpose(dm_mine, (1, 0, 2)))

    res = {}
    res["hgrn_lb"] = adamw("aw_hgrn_lb", hgrn_lb, m_hgrn_lb, v_hgrn_lb, g_lb[None])
    res["ada_w"] = adamw("aw_ada_w", ada_w, m_ada_w, v_ada_w, g_ada_w[None])
    res["ada_b"] = adamw("aw_ada_b", ada_b, m_ada_b, v_ada_b, dmod_parts)
    res["norm_g"] = adamw("aw_norm_g", norm_g, m_norm_g, v_norm_g, dgn_parts)
    res["hg_in_w"] = adamw("aw_hg_in", hg_in_w, m_hg_in_w, v_hg_in_w, jnp.concatenate(rs_hg_in, axis=1))
    res["hg_out_w"] = adamw("aw_hg_out", hg_out_w, m_hg_out_w, v_hg_out_w, jnp.concatenate(rs_hg_out, axis=1))
    res["hg_onorm"] = adamw("aw_hg_onorm", hg_onorm, m_hg_onorm, v_hg_onorm, dhgon_parts)
    res["sw_in_w"] = adamw("aw_sw_in", sw_in_w, m_sw_in_w, v_sw_in_w, rs_sw_in)
    res["sw_out_w"] = adamw("aw_sw_out", sw_out_w, m_sw_out_w, v_sw_out_w, rs_sw_out)
    res["sw_qnorm"] = adamw("aw_sw_qn", sw_qnorm, m_sw_qnorm, v_sw_qnorm, dqn_parts)
    res["sw_knorm"] = adamw("aw_sw_kn", sw_knorm, m_sw_knorm, v_sw_knorm, dkn_parts)
    res["sw_sinks"] = adamw("aw_sw_sinks", sw_sinks, m_sw_sinks, v_sw_sinks, dsk_parts)
    res["gd_in_w"] = adamw("aw_gd_in", gd_in_w, m_gd_in_w, v_gd_in_w, rs_gd_in)
    res["gd_out_w"] = adamw("aw_gd_out", gd_out_w, m_gd_out_w, v_gd_out_w, rs_gd_out)
    res["gd_conv_w"] = adamw("aw_gd_conv", gd_conv_w, m_gd_conv_w, v_gd_conv_w, rs_gd_conv)
    res["gd_a_log"] = adamw("aw_gd_alog", gd_a_log, m_gd_a_log, v_gd_a_log, dal_parts)
    res["gd_dt_bias"] = adamw("aw_gd_dt", gd_dt_bias, m_gd_dt_bias, v_gd_dt_bias, ddt_parts)
    res["gd_onorm"] = adamw("aw_gd_onorm", gd_onorm, m_gd_onorm, v_gd_onorm, dgdon_parts)

    order = ["hgrn_lb", "ada_w", "ada_b", "norm_g", "hg_in_w", "hg_out_w", "hg_onorm", "sw_in_w", "sw_out_w", "sw_qnorm",
             "sw_knorm", "sw_sinks", "gd_in_w", "gd_out_w", "gd_conv_w", "gd_a_log", "gd_dt_bias", "gd_onorm"]
    outs = [loss, grad_x]
    for part in range(4):
        outs += [res[n][part] for n in order]
    return tuple(outs)
```

```python
import functools

import jax
import jax.numpy as jnp
from jax import lax
from jax.experimental import pallas as pl
from jax.experimental.pallas import tpu as pltpu

F32, BF16 = jnp.float32, jnp.bfloat16
HI = lax.Precision.HIGHEST
MESH = pl.DeviceIdType.MESH
N_DEV = 8
D = 1024
EPS = 1e-6
NEG_INF = float("-inf")

HG_HEADS, HG_DH, HG_BLK, HG_CH, HG_SUB = 8, 128, 128, 32, 8
HG_SAFE = 80.0
SW_KV, SW_G, SW_DH, SW_BLK = 4, 4, 64, 128
SW_QW, SW_KVW = 1024, 256
GD_QK_HEADS, GD_DK, GD_CHUNK = 8, 128, 64
GD_QKW, GD_VW, GD_QKV = 1024, 2048, 4096
CONV_K = 4
ROPE_THETA = 10000.0

ADAM_LR, ADAM_B1, ADAM_B2, ADAM_EPS, ADAM_WD, ADAM_STEP = 0.001, 0.9, 0.999, 1e-08, 0.01, 10

VMEM_LIMIT = 56 * 1024 * 1024


def _params(**kw):
    return pltpu.CompilerParams(vmem_limit_bytes=VMEM_LIMIT, **kw)


def _dot(a, b, ca, cb):
    return lax.dot_general(a.astype(BF16), b.astype(BF16), (((ca,), (cb,)), ((), ())), preferred_element_type=F32)


def mm(a, b):
    return _dot(a, b, 1, 0)


def mm_nt(a, b):
    return _dot(a, b, 1, 1)


def mm_tn(a, b):
    return _dot(a, b, 0, 0)


def mmf(a, b):
    return lax.dot_general(a, b, (((1,), (0,)), ((), ())), precision=HI, preferred_element_type=F32)


def _split3(x):
    h1 = x.astype(BF16)
    r1 = x - h1.astype(F32)
    h2 = r1.astype(BF16)
    h3 = (r1 - h2.astype(F32)).astype(BF16)
    return h1, h2, h3


def _lin01_l(m, x):
    mb = m.astype(BF16)
    return sum(lax.dot_general(mb, p, (((1,), (0,)), ((), ())), preferred_element_type=F32) for p in _split3(x))


def _lin01_r(x, m):
    mb = m.astype(BF16)
    return sum(lax.dot_general(p, mb, (((1,), (0,)), ((), ())), preferred_element_type=F32) for p in _split3(x))


def _bdot(a, b, ca, cb, prec=None):
    return lax.dot_general(a, b, (((ca,), (cb,)), ((0,), (0,))), precision=prec, preferred_element_type=F32)


def bmm(a, b):
    return _bdot(a.astype(BF16), b.astype(BF16), 2, 1)


def bmm_nt(a, b):
    return _bdot(a.astype(BF16), b.astype(BF16), 2, 2)


def bmm_tn(a, b):
    return _bdot(a.astype(BF16), b.astype(BF16), 1, 1)


def bmmf(a, b):
    return _bdot(a, b, 2, 1, lax.Precision.HIGH)


def _blin01(m, x):
    mb = m.astype(BF16)
    return sum(_bdot(mb, p, 2, 1) for p in _split3(x))


@jax.custom_vjp
def blin01(m, mt, x):
    return _blin01(m, x)


def _blin01_fwd(m, mt, x):
    return _blin01(m, x), (m, mt)


def _blin01_bwd(res, g):
    m, mt = res
    return jnp.zeros_like(m), jnp.zeros_like(mt), _blin01(mt, g)


blin01.defvjp(_blin01_fwd, _blin01_bwd)


def _blin01_r(x, m):
    mb = m.astype(BF16)
    return sum(_bdot(p, mb, 2, 1) for p in _split3(x))


@jax.custom_vjp
def blin01_r(x, m, mt):
    return _blin01_r(x, m)


def _blin01_r_fwd(x, m, mt):
    return _blin01_r(x, m), (m, mt)


def _blin01_r_bwd(res, g):
    m, mt = res
    return _blin01_r(g, mt), jnp.zeros_like(m), jnp.zeros_like(mt)


blin01_r.defvjp(_blin01_r_fwd, _blin01_r_bwd)


@jax.custom_vjp
def lin01_l(m, mt, x):
    return _lin01_l(m, x)


def _lin01_l_fwd(m, mt, x):
    return _lin01_l(m, x), (m, mt)


def _lin01_l_bwd(res, g):
    m, mt = res
    return jnp.zeros_like(m), jnp.zeros_like(mt), _lin01_l(mt, g)


lin01_l.defvjp(_lin01_l_fwd, _lin01_l_bwd)


@jax.custom_vjp
def lin01_r(x, m, mt):
    return _lin01_r(x, m)


def _lin01_r_fwd(x, m, mt):
    return _lin01_r(x, m), (m, mt)


def _lin01_r_bwd(res, g):
    m, mt = res
    return _lin01_r(g, mt), jnp.zeros_like(m), jnp.zeros_like(mt)


lin01_r.defvjp(_lin01_r_fwd, _lin01_r_bwd)


@functools.partial(jax.custom_vjp, nondiff_argnums=(1, 2))
def rows(x, start, size):
    return lax.slice_in_dim(x, start, start + size, axis=x.ndim - 2)


def _rows_fwd(x, start, size):
    return lax.slice_in_dim(x, start, start + size, axis=x.ndim - 2), jnp.zeros(x.shape[:-1] + (1,), F32)


def _rows_bwd(start, size, res, g):
    ax = g.ndim - 2
    total = res.shape[ax]
    zeros = lambda n: jnp.zeros(g.shape[:ax] + (n,) + g.shape[ax + 1:], g.dtype)
    parts = []
    if start > 0:
        parts.append(zeros(start))
    parts.append(g)
    if total - start - size > 0:
        parts.append(zeros(total - start - size))
    return (jnp.concatenate(parts, axis=ax) if len(parts) > 1 else g,)


rows.defvjp(_rows_fwd, _rows_bwd)


@functools.partial(jax.custom_vjp, nondiff_argnums=(1,))
def rowsel(x, s):
    return lax.slice_in_dim(x, s, s + 1, axis=x.ndim - 2)


def _rowsel_fwd(x, s):
    return lax.slice_in_dim(x, s, s + 1, axis=x.ndim - 2), jnp.zeros(x.shape[:-1] + (1,), F32)


def _rowsel_bwd(s, res, g):
    r = lax.broadcasted_iota(jnp.int32, res.shape, res.ndim - 2)
    return (jnp.where(r == s, g, 0.0),)


rowsel.defvjp(_rowsel_fwd, _rowsel_bwd)


def sigmoid(x):
    return jax.nn.sigmoid(x)


def silu(x):
    return x * jax.nn.sigmoid(x)


def softplus(x):
    return jnp.maximum(x, 0.0) + jnp.log1p(jnp.exp(-jnp.abs(x)))


def _iota(shape, dim):
    return lax.broadcasted_iota(jnp.int32, shape, dim)


def norm_mod_fn(g, sc, sh, x):
    r = lax.rsqrt(jnp.mean(x * x, axis=-1, keepdims=True) + EPS)
    return (x * r * g) * (1.0 + sc) + sh


def _row_spec(tm, f):
    return pl.BlockSpec((tm, f), lambda i: (i, 0))


def _const_spec(shape):
    nd = len(shape)
    return pl.BlockSpec(shape, lambda i: (0,) * nd)


def _tiled(name, fn, consts, rows_in, row_outs, acc_outs=(), tm=512, carry=()):
    T = rows_in[0].shape[0]
    tm = min(tm, T)
    n_c, n_r, n_ro = len(consts), len(rows_in), len(row_outs)

    def body(*refs):
        c_refs, r_refs = refs[:n_c], refs[n_c:n_c + n_r]
        ro_refs, ao_refs = refs[n_c + n_r:n_c + n_r + n_ro], refs[n_c + n_r + n_ro:]
        outs = fn(*[r[...] for r in c_refs], *[r[...] for r in r_refs])
        for r, v in zip(ro_refs, outs[:n_ro]):
            r[...] = v.astype(r.dtype)
        if ao_refs:
            @pl.when(pl.program_id(0) == 0)
            def _():
                for r in ao_refs:
                    r[...] = jnp.zeros(r.shape, r.dtype)
            for r, v in zip(ao_refs, outs[n_ro:]):
                r[...] += v

    out_shape = [jax.ShapeDtypeStruct((T, f), dt) for f, dt in row_outs] + [jax.ShapeDtypeStruct(s, F32) for s in acc_outs]
    out_specs = [_row_spec(tm, f) for f, _ in row_outs] + [_const_spec(s) for s in acc_outs]
    in_specs = [_const_spec(c.shape) for c in consts] + [_row_spec(tm, r.shape[1]) for r in rows_in]
    return _scan_call(body, name, T // tm, in_specs, out_specs, out_shape, [], (*consts, *rows_in), carry)


def norm_mod(name, x, g, sc, sh):
    (h,), _ = _tiled(name, lambda g, sc, sh, x: (norm_mod_fn(g, sc, sh, x),), [g, sc, sh], [x], [(D, BF16)])
    return h


def out_resid_norm(name, u, w, x, gate, g, sc, sh):
    def fn(gate, g, sc, sh, w, x, u):
        y = mm(u, w)
        xn = x + gate * y
        return y, xn, norm_mod_fn(g, sc, sh, xn)
    return _tiled(name, fn, [gate, g, sc, sh, w], [x, u], [(D, F32), (D, F32), (D, BF16)])[0]


def out_loss(name, u, w, x, tgt, gate):
    def fn(gate, w, x, tgt, u):
        y = mm(u, w)
        err = x + gate * y - tgt
        dx = err * (1.0 / D)
        per_tok = jnp.sum(err * err, axis=-1, keepdims=True) * (0.5 / D)
        loss = jnp.sum(per_tok, axis=0, keepdims=True)
        return y, dx, gate * dx, jnp.broadcast_to(loss, (1, 128)), jnp.sum(dx * y, axis=0, keepdims=True)
    return _tiled(name, fn, [gate, w], [x, tgt, u], [(D, F32), (D, F32), (D, BF16)], [(1, 128), (1, D)])[0]


def dh_bwd_row(name, pairs, x, dxn, g, sc, sh, y_prev=None, gate_prev=None, carry=()):
    with_prev = y_prev is not None
    n = len(pairs)

    def fn(*a):
        a = list(a)
        gate_p = a.pop(0) if with_prev else None
        g, sc, sh = a[:3]
        ws, a = a[3:3 + n], a[3 + n:]
        x, dxn = a[:2]
        yp = a[2] if with_prev else None
        dps = a[3:] if with_prev else a[2:]
        dh = mm_nt(dps[0], ws[0])
        for dp, w in zip(dps[1:], ws[1:]):
            dh = dh + mm_nt(dp, w)
        _, f = jax.vjp(norm_mod_fn, g, sc, sh, x)
        dg, dsc, dsh, dx = f(dh)
        dx = dx + dxn
        if with_prev:
            return dx, gate_p * dx, dg, dsc, dsh, jnp.sum(dx * yp, axis=0, keepdims=True)
        return dx, dg, dsc, dsh

    ws, dps = [w for _, w in pairs], [dp for dp, _ in pairs]
    tm = MM_ROWS_WIDE if sum(dp.shape[1] for dp in dps) > MM_WIDE else MM_ROWS_NARROW
    if with_prev:
        return _tiled(name, fn, [gate_prev, g, sc, sh] + ws, [x, dxn, y_prev] + dps, [(D, F32), (D, BF16)], [(1, D)] * 4,
                      tm=tm, carry=carry)
    return _tiled(name, fn, [g, sc, sh] + ws, [x, dxn] + dps, [(D, F32)], [(1, D)] * 3, tm=tm, carry=carry)


MM_ROWS_WIDE, MM_ROWS_NARROW, MM_WIDE = 256, 512, 2048


def matmul(name, a, w, out_dtype=F32):
    T, K = a.shape
    N = w.shape[1]
    tm = min(MM_ROWS_NARROW if N <= MM_WIDE else MM_ROWS_WIDE, T)

    def body(a_ref, w_ref, o_ref):
        o_ref[...] = mm(a_ref[...], w_ref[...]).astype(o_ref.dtype)

    return pl.pallas_call(
        body, name=name, grid=(T // tm,),
        in_specs=[_row_spec(tm, K), _const_spec((K, N))], out_specs=_row_spec(tm, N),
        out_shape=jax.ShapeDtypeStruct((T, N), out_dtype),
        compiler_params=_params(dimension_semantics=("arbitrary",)),
    )(a, w)


def matmul_nt(name, pairs, out_dtype=F32):
    T = pairs[0][0].shape[0]
    K = pairs[0][1].shape[0]
    wide = sum(a.shape[1] for a, _ in pairs) > MM_WIDE
    tm = min(MM_ROWS_WIDE if wide else MM_ROWS_NARROW, T)
    n = len(pairs)

    def body(*refs):
        o_ref = refs[-1]
        acc = None
        for i in range(n):
            p = mm_nt(refs[2 * i][...], refs[2 * i + 1][...])
            acc = p if acc is None else acc + p
        o_ref[...] = acc.astype(o_ref.dtype)

    in_specs, args = [], []
    for a, w in pairs:
        in_specs += [_row_spec(tm, a.shape[1]), _const_spec(w.shape)]
        args += [a, w]
    return pl.pallas_call(
        body, name=name, grid=(T // tm,), in_specs=in_specs, out_specs=_row_spec(tm, K),
        out_shape=jax.ShapeDtypeStruct((T, K), out_dtype),
        compiler_params=_params(dimension_semantics=("arbitrary",)),
    )(*args)


def matmul_tn(name, a, b, tm=512, tn=2048):
    T, K = a.shape
    N = b.shape[1]
    tm = min(tm, T)
    tn = max(t for t in range(128, min(tn, N) + 1, 128) if N % t == 0)

    def body(a_ref, b_ref, o_ref):
        @pl.when(pl.program_id(1) == 0)
        def _():
            o_ref[...] = jnp.zeros(o_ref.shape, F32)
        o_ref[...] += mm_tn(a_ref[...], b_ref[...])

    return pl.pallas_call(
        body, name=name, grid=(N // tn, T // tm),
        in_specs=[pl.BlockSpec((tm, K), lambda j, i: (i, 0)), pl.BlockSpec((tm, tn), lambda j, i: (i, j))],
        out_specs=pl.BlockSpec((K, tn), lambda j, i: (0, j)),
        out_shape=jax.ShapeDtypeStruct((K, N), F32),
        compiler_params=_params(dimension_semantics=("arbitrary", "arbitrary")),
    )(a, b)


def _hg_log_f(lb, fp):
    return jnp.log(lb + (1.0 - lb) * sigmoid(fp))


def _hg_safe(lb, fp):
    lf = _hg_log_f(lb, fp)
    worst = [jnp.max(-jnp.sum(lf[:, c * HG_CH:(c + 1) * HG_CH], axis=1)) for c in range(fp.shape[1] // HG_CH)]
    return functools.reduce(jnp.maximum, worst) <= HG_SAFE


def hg_heads(fast, tri, trit, lb, og, st, qp, fp, iv, z):
    blk = qp.shape[1]
    q = silu(qp)
    lf = _hg_log_f(lb, fp)
    k = (1.0 - lb) * sigmoid(-fp)
    r = _iota((1, HG_SUB, 1), 1)
    rc = _iota((1, HG_CH, 1), 1)
    outs = []
    for c in range(blk // HG_CH):
        qc, kc, vc, lfc = (rows(a, c * HG_CH, HG_CH) for a in (q, k, iv, lf))
        b = blin01(tri, trit, lfc)
        qd = qc * jnp.exp(b)
        o_state = bmm_nt(qd, st)
        if fast:
            scores = jnp.where(tri > 0.0, _bdot(qd, kc * jnp.exp(-b), 2, 2, lax.Precision.HIGH), 0.0)
            outs.append(o_state + bmm(scores, vc))
            bl = rowsel(b, HG_CH - 1)
            st = st * jnp.exp(bl) + bmm_tn(vc, kc * jnp.exp(bl - b))
            continue
        parts = []
        for i in range(HG_CH // HG_SUB):
            qi, ki, vi, bi = (rows(a, i * HG_SUB, HG_SUB) for a in (qc, kc, vc, b))
            o = rows(o_state, i * HG_SUB, HG_SUB)
            for s in range(HG_SUB):
                ks, bs, vs = rowsel(ki, s), rowsel(bi, s), rowsel(vi, s)
                e = jnp.exp(jnp.where(r >= s, bi - bs, NEG_INF))
                o = o + jnp.sum(qi * ks * e, axis=-1, keepdims=True) * vs
            if i > 0:
                rb = rowsel(b, i * HG_SUB - 1)
                kt = kc * jnp.exp(jnp.where(rc < i * HG_SUB, rb - b, NEG_INF))
                o = o + bmm(bmm_nt(qi * jnp.exp(bi - rb), kt), vc)
            parts.append(o)
        bl = rowsel(b, HG_CH - 1)
        st = st * jnp.exp(bl) + bmm_tn(vc, kc * jnp.exp(bl - b))
        outs.append(jnp.concatenate(parts, axis=1))
    o = jnp.concatenate(outs, axis=1)
    on = o * lax.rsqrt(jnp.mean(o * o, axis=-1, keepdims=True) + EPS) * og
    return st, on * silu(z)


def _tri(n, nh):
    t = (_iota((nh, n, n), 1) >= _iota((nh, n, n), 2)).astype(F32)
    tt = (_iota((nh, n, n), 1) <= _iota((nh, n, n), 2)).astype(F32)
    return t, tt


def _hg_load(p_ref, lb_ref, heads):
    col = lambda ref, base: jnp.stack([ref[:, pl.ds(base + h * HG_DH, HG_DH)] for h in heads])
    return col(lb_ref, 0), col(p_ref, 0), col(p_ref, D), col(p_ref, 2 * D), col(p_ref, 3 * D)


def hg_fwd(name, p, lb, og, carry=()):
    T = p.shape[0]
    nb = T // HG_BLK

    def body(p_ref, lb_ref, og_ref, u_ref, st_ref, s_scr):
        @pl.when(pl.program_id(0) == 0)
        def _():
            s_scr[...] = jnp.zeros(s_scr.shape, F32)
        st_ref[0] = s_scr[...]
        tri, trit = _tri(HG_CH, HG_HEADS)
        og_v = og_ref[...]
        heads = list(range(HG_HEADS))
        lbh, qp, fp, iv, z = _hg_load(p_ref, lb_ref, heads)

        def run(fast):
            def go():
                st, u = hg_heads(fast, tri, trit, lbh, og_v, s_scr[...], qp, fp, iv, z)
                s_scr[...] = st
                for h in heads:
                    u_ref[:, pl.ds(h * HG_DH, HG_DH)] = u[h].astype(BF16)
            return go

        lax.cond(_hg_safe(lbh, fp), run(True), run(False))

    (u, st), ex = _scan_call(
        body, name, nb,
        [_row_spec(HG_BLK, 4 * D), _const_spec((1, D)), _const_spec((1, HG_DH))],
        [_row_spec(HG_BLK, D), pl.BlockSpec((1, HG_HEADS, HG_DH, HG_DH), lambda i: (i, 0, 0, 0))],
        [jax.ShapeDtypeStruct((T, D), BF16), jax.ShapeDtypeStruct((nb, HG_HEADS, HG_DH, HG_DH), F32)],
        [pltpu.VMEM((HG_HEADS, HG_DH, HG_DH), F32)], (p, lb, og), carry)
    return u, st, ex


def hg_bwd(name, p, lb, og, states, du, carry=()):
    T = p.shape[0]
    nb = T // HG_BLK

    def body(p_ref, lb_ref, og_ref, st_ref, du_ref, dp_ref, dlb_ref, dog_ref, ds_scr):
        @pl.when(pl.program_id(0) == 0)
        def _():
            ds_scr[...] = jnp.zeros(ds_scr.shape, F32)
            dlb_ref[...] = jnp.zeros(dlb_ref.shape, F32)
            dog_ref[...] = jnp.zeros(dog_ref.shape, F32)
        tri, trit = _tri(HG_CH, HG_HEADS)
        og_v = og_ref[...]
        heads = list(range(HG_HEADS))
        lbh, qp, fp, iv, z = _hg_load(p_ref, lb_ref, heads)
        du_h = jnp.stack([du_ref[:, pl.ds(h * HG_DH, HG_DH)].astype(F32) for h in heads])

        def run(fast):
            def go():
                _, f = jax.vjp(functools.partial(hg_heads, fast, tri, trit), lbh, og_v, st_ref[0], qp, fp, iv, z)
                dlb, dog, dst, dq, df, di, dz = f((ds_scr[...], du_h))
                ds_scr[...] = dst
                dog_ref[...] += dog
                for h in heads:
                    for base, g in ((0, dq), (D, df), (2 * D, di), (3 * D, dz)):
                        dp_ref[:, pl.ds(base + h * HG_DH, HG_DH)] = g[h].astype(BF16)
                    dlb_ref[:, pl.ds(h * HG_DH, HG_DH)] += dlb[h]
            return go

        lax.cond(_hg_safe(lbh, fp), run(True), run(False))

    rev = lambda i: (nb - 1 - i, 0)
    outs, ex = _scan_call(
        body, name, nb,
        [pl.BlockSpec((HG_BLK, 4 * D), rev), _const_spec((1, D)), _const_spec((1, HG_DH)),
         pl.BlockSpec((1, HG_HEADS, HG_DH, HG_DH), lambda i: (nb - 1 - i, 0, 0, 0)), pl.BlockSpec((HG_BLK, D), rev)],
        [pl.BlockSpec((HG_BLK, 4 * D), rev), _const_spec((1, D)), _const_spec((1, HG_DH))],
        [jax.ShapeDtypeStruct((T, 4 * D), BF16), jax.ShapeDtypeStruct((1, D), F32), jax.ShapeDtypeStruct((1, HG_DH), F32)],
        [pltpu.VMEM((HG_HEADS, HG_DH, HG_DH), F32)], (p, lb, og, states, du), carry)
    return (*outs, ex)


def swa_heads(qn, kn, sinks, kprev, vprev, qp4, kp, v, z4, cos4, sin4, cosk, sink_, hots, mask, p64, p64t):
    def norm_rope(xp, g, cs, sn):
        y = xp * lax.rsqrt(jnp.mean(xp * xp, axis=-1, keepdims=True) + EPS) * g
        return y * cs + blin01_r(y, p64, p64t) * sn

    q = norm_rope(qp4, qn, cos4, sin4)
    k = norm_rope(kp, kn, cosk, sink_)
    k2 = jnp.concatenate([kprev, k], axis=1)
    v2 = jnp.concatenate([vprev, v], axis=1)
    s = jnp.where(mask, bmm_nt(q, k2) * (SW_DH ** -0.5), NEG_INF)
    sink_col = jnp.concatenate([jnp.concatenate(
        [jnp.broadcast_to(jnp.sum(sinks * hot, axis=-1, keepdims=True), (SW_BLK, 1)) for hot in hot4], axis=0)[None]
        for hot4 in hots], axis=0)
    m = lax.stop_gradient(jnp.maximum(jnp.max(s, axis=-1, keepdims=True), sink_col))
    p = jnp.exp(s - m)
    p = p / (jnp.sum(p, axis=-1, keepdims=True) + jnp.exp(sink_col - m))
    o = bmm(p, v2)
    return k, v, o * silu(z4)


def _swa_consts(first):
    qi = _iota((SW_G * SW_BLK, 2 * SW_BLK), 0) % SW_BLK
    kj = _iota((SW_G * SW_BLK, 2 * SW_BLK), 1)
    rel = qi + SW_BLK - kj
    mask = (rel >= 0) & (rel < SW_BLK) & ((kj >= SW_BLK) | jnp.logical_not(first))
    i, j = _iota((SW_KV, SW_DH, SW_DH), 1), _iota((SW_KV, SW_DH, SW_DH), 2)
    p64 = (i == (j + SW_DH // 2) % SW_DH).astype(F32)
    p64t = (j == (i + SW_DH // 2) % SW_DH).astype(F32)
    lane = _iota((1, SW_KV * SW_G), 1)
    hots = [[(lane == h * SW_G + g).astype(F32) for g in range(SW_G)] for h in range(SW_KV)]
    return mask, p64, p64t, hots


def _stack_q(ref, h, base):
    return jnp.concatenate([ref[:, pl.ds(base + h * SW_G * SW_DH + g * SW_DH, SW_DH)] for g in range(SW_G)], axis=0)


def _swa_load(p_ref):
    heads = range(SW_KV)
    return (jnp.stack([_stack_q(p_ref, h, Q0) for h in heads]),
            jnp.stack([p_ref[:, pl.ds(K0 + h * SW_DH, SW_DH)] for h in heads]),
            jnp.stack([p_ref[:, pl.ds(V0 + h * SW_DH, SW_DH)] for h in heads]),
            jnp.stack([_stack_q(p_ref, h, Z0) for h in heads]))


Q0, K0, V0, Z0 = 0, SW_QW, SW_QW + SW_KVW, SW_QW + 2 * SW_KVW
SW_IN = 2 * SW_QW + 2 * SW_KVW


def swa_fwd(name, p, cos, sin, qn, kn, sinks):
    T = p.shape[0]
    nb = T // SW_BLK

    def body(p_ref, cos_ref, sin_ref, qn_ref, kn_ref, sk_ref, u_ref, ko_ref, kprev, vprev):
        first = pl.program_id(0) == 0

        @pl.when(first)
        def _():
            kprev[...] = jnp.zeros(kprev.shape, F32)
            vprev[...] = jnp.zeros(vprev.shape, F32)
        mask, p64, p64t, hots = _swa_consts(first)
        cs, sn = cos_ref[...], sin_ref[...]
        cs4, sn4 = jnp.concatenate([cs] * SW_G, axis=0), jnp.concatenate([sn] * SW_G, axis=0)
        qp4, kp, vv, z4 = _swa_load(p_ref)
        k, v, u4 = swa_heads(qn_ref[...], kn_ref[...], sk_ref[...], kprev[...], vprev[...], qp4, kp, vv, z4,
                             cs4, sn4, cs, sn, hots, mask, p64, p64t)
        kprev[...] = k
        vprev[...] = v
        for h in range(SW_KV):
            ko_ref[:, pl.ds(h * SW_DH, SW_DH)] = k[h]
            uh = u4[h]
            for g in range(SW_G):
                u_ref[:, pl.ds(h * SW_G * SW_DH + g * SW_DH, SW_DH)] = uh[g * SW_BLK:(g + 1) * SW_BLK].astype(BF16)

    return pl.pallas_call(
        body, name=name, grid=(nb,),
        in_specs=[_row_spec(SW_BLK, SW_IN), _row_spec(SW_BLK, SW_DH), _row_spec(SW_BLK, SW_DH),
                  _const_spec((1, SW_DH)), _const_spec((1, SW_DH)), _const_spec((1, SW_KV * SW_G))],
        out_specs=[_row_spec(SW_BLK, SW_QW), _row_spec(SW_BLK, SW_KVW)],
        out_shape=[jax.ShapeDtypeStruct((T, SW_QW), BF16), jax.ShapeDtypeStruct((T, SW_KVW), F32)],
        scratch_shapes=[pltpu.VMEM((SW_KV, SW_BLK, SW_DH), F32), pltpu.VMEM((SW_KV, SW_BLK, SW_DH), F32)],
        compiler_params=_params(dimension_semantics=("arbitrary",)),
    )(p, cos, sin, qn, kn, sinks)


def swa_bwd(name, p, kpost, cos, sin, qn, kn, sinks, du, carry=()):
    T = p.shape[0]
    nb = T // SW_BLK

    def body(p_ref, pprev_ref, kprev_ref, cos_ref, sin_ref, qn_ref, kn_ref, sk_ref, du_ref,
             dp_ref, dqn_ref, dkn_ref, dsk_ref, dk_scr, dv_scr):
        i = pl.program_id(0)
        first = i == nb - 1

        @pl.when(i == 0)
        def _():
            dk_scr[...] = jnp.zeros(dk_scr.shape, F32)
            dv_scr[...] = jnp.zeros(dv_scr.shape, F32)
            dqn_ref[...] = jnp.zeros(dqn_ref.shape, F32)
            dkn_ref[...] = jnp.zeros(dkn_ref.shape, F32)
            dsk_ref[...] = jnp.zeros(dsk_ref.shape, F32)
        mask, p64, p64t, hots = _swa_consts(first)
        cs, sn = cos_ref[...], sin_ref[...]
        cs4, sn4 = jnp.concatenate([cs] * SW_G, axis=0), jnp.concatenate([sn] * SW_G, axis=0)
        heads = range(SW_KV)
        fn = lambda qn, kn, sk, kpv, vpv, qp4, kp, v, z4: swa_heads(
            qn, kn, sk, kpv, vpv, qp4, kp, v, z4, cs4, sn4, cs, sn, hots, mask, p64, p64t)
        kpv = jnp.stack([kprev_ref[:, pl.ds(h * SW_DH, SW_DH)] for h in heads])
        vpv = jnp.stack([pprev_ref[:, pl.ds(V0 + h * SW_DH, SW_DH)] for h in heads])
        _, f = jax.vjp(fn, qn_ref[...], kn_ref[...], sk_ref[...], kpv, vpv, *_swa_load(p_ref))
        du4 = jnp.stack([_stack_q(du_ref, h, 0).astype(F32) for h in heads])
        dqn, dkn, dsk, dkpv, dvpv, dq4, dkp, dv, dz4 = f((dk_scr[...], dv_scr[...], du4))
        dk_scr[...] = dkpv
        dv_scr[...] = dvpv
        for h in heads:
            dp_ref[:, pl.ds(K0 + h * SW_DH, SW_DH)] = dkp[h].astype(BF16)
            dp_ref[:, pl.ds(V0 + h * SW_DH, SW_DH)] = dv[h].astype(BF16)
            dqh, dzh = dq4[h], dz4[h]
            for g in range(SW_G):
                c0 = h * SW_G * SW_DH + g * SW_DH
                dp_ref[:, pl.ds(Q0 + c0, SW_DH)] = dqh[g * SW_BLK:(g + 1) * SW_BLK].astype(BF16)
                dp_ref[:, pl.ds(Z0 + c0, SW_DH)] = dzh[g * SW_BLK:(g + 1) * SW_BLK].astype(BF16)
        dqn_ref[...] += dqn
        dkn_ref[...] += dkn
        dsk_ref[...] += dsk

    rev = lambda i: (nb - 1 - i, 0)
    prev = lambda i: (jnp.maximum(nb - 2 - i, 0), 0)
    outs, ex = _scan_call(
        body, name, nb,
        [pl.BlockSpec((SW_BLK, SW_IN), rev), pl.BlockSpec((SW_BLK, SW_IN), prev),
         pl.BlockSpec((SW_BLK, SW_KVW), prev), pl.BlockSpec((SW_BLK, SW_DH), rev), pl.BlockSpec((SW_BLK, SW_DH), rev),
         _const_spec((1, SW_DH)), _const_spec((1, SW_DH)), _const_spec((1, SW_KV * SW_G)), pl.BlockSpec((SW_BLK, SW_QW), rev)],
        [pl.BlockSpec((SW_BLK, SW_IN), rev), _const_spec((1, SW_DH)), _const_spec((1, SW_DH)), _const_spec((1, SW_KV * SW_G))],
        [jax.ShapeDtypeStruct((T, SW_IN), BF16), jax.ShapeDtypeStruct((1, SW_DH), F32),
         jax.ShapeDtypeStruct((1, SW_DH), F32), jax.ShapeDtypeStruct((1, SW_KV * SW_G), F32)],
        [pltpu.VMEM((SW_KV, SW_BLK, SW_DH), F32), pltpu.VMEM((SW_KV, SW_BLK, SW_DH), F32)],
        (p, p, kpost, cos, sin, qn, kn, sinks, du), carry)
    return (*outs, ex)


CONV_TC, CONV_TL, HALO = 512, 1024, 8


def _conv_taps(xe, w_ref):
    acc = w_ref[pl.ds(CONV_K - 1, 1), :] * xe
    for k in range(CONV_K - 1):
        acc = acc + w_ref[pl.ds(k, 1), :] * pltpu.roll(xe, CONV_K - 1 - k, 0)
    return acc


def conv_fwd(name, x, w8):
    T, C = x.shape
    tc = min(CONV_TC, T)
    nt = T // tc

    def body(xp_ref, x_ref, w_ref, o_ref):
        prev = jnp.where(pl.program_id(0) == 0, 0.0, xp_ref[...])
        xe = jnp.concatenate([prev, x_ref[...]], axis=0)
        o_ref[...] = silu(_conv_taps(xe, w_ref)[HALO:])

    return pl.pallas_call(
        body, name=name, grid=(nt, C // CONV_TL),
        in_specs=[pl.BlockSpec((HALO, CONV_TL), lambda i, j: (jnp.maximum(i * (tc // HALO) - 1, 0), j)),
                  pl.BlockSpec((tc, CONV_TL), lambda i, j: (i, j)), pl.BlockSpec((HALO, CONV_TL), lambda i, j: (0, j))],
        out_specs=pl.BlockSpec((tc, CONV_TL), lambda i, j: (i, j)),
        out_shape=jax.ShapeDtypeStruct((T, C), F32),
        compiler_params=_params(dimension_semantics=("arbitrary", "arbitrary")),
    )(x, x, w8)


def conv_bwd(name, x, dy, w8):
    T, C = x.shape
    tc = min(CONV_TC, T)
    nt = T // tc
    n = tc + 2 * HALO

    def body(xp_ref, x_ref, xn_ref, dy_ref, dyn_ref, w_ref, dx_ref, dw_ref):
        i = pl.program_id(1)

        @pl.when(i == 0)
        def _():
            dw_ref[...] = jnp.zeros(dw_ref.shape, F32)
        prev = jnp.where(i == 0, 0.0, xp_ref[...])
        xe = jnp.concatenate([prev, x_ref[...], xn_ref[...]], axis=0)
        dye = jnp.concatenate([jnp.zeros((HALO, CONV_TL), F32), dy_ref[...], jnp.where(i == nt - 1, 0.0, dyn_ref[...])], axis=0)
        ce = _conv_taps(xe, w_ref)
        sg = sigmoid(ce)
        dce = dye * (sg * (1.0 + ce * (1.0 - sg)))
        dx = w_ref[pl.ds(CONV_K - 1, 1), :] * dce
        for k in range(CONV_K - 1):
            dx = dx + w_ref[pl.ds(k, 1), :] * pltpu.roll(dce, n - (CONV_K - 1 - k), 0)
        dx_ref[...] = dx[HALO:HALO + tc].astype(BF16)
        dcur = dce[HALO:HALO + tc]
        for k in range(CONV_K):
            xs = xe if k == CONV_K - 1 else pltpu.roll(xe, CONV_K - 1 - k, 0)
            dw_ref[pl.ds(k, 1), :] += jnp.sum(dcur * xs[HALO:HALO + tc], axis=0, keepdims=True)

    nh = T // HALO
    prev_map = lambda j, i: (jnp.maximum(i * (tc // HALO) - 1, 0), j)
    next_map = lambda j, i: (jnp.minimum((i + 1) * (tc // HALO), nh - 1), j)
    cur_map = lambda j, i: (i, j)
    return pl.pallas_call(
        body, name=name, grid=(C // CONV_TL, nt),
        in_specs=[pl.BlockSpec((HALO, CONV_TL), prev_map), pl.BlockSpec((tc, CONV_TL), cur_map),
                  pl.BlockSpec((HALO, CONV_TL), next_map), pl.BlockSpec((tc, CONV_TL), cur_map),
                  pl.BlockSpec((HALO, CONV_TL), next_map), pl.BlockSpec((HALO, CONV_TL), lambda j, i: (0, j))],
        out_specs=[pl.BlockSpec((tc, CONV_TL), cur_map), pl.BlockSpec((HALO, CONV_TL), lambda j, i: (0, j))],
        out_shape=[jax.ShapeDtypeStruct((T, C), BF16), jax.ShapeDtypeStruct((HALO, C), F32)],
        compiler_params=_params(dimension_semantics=("arbitrary", "arbitrary")),
    )(x, x, x, dy, dy, w8)


GD_VH = 16
GD_HB = 16


def _unit_lower_inverse(a, eye_f):
    n = -a
    t = eye_f + n
    p = n
    for _ in range(5):
        p = bmmf(p, p)
        t = t + bmmf(t, p)
    return t


@jax.custom_vjp
def unit_lower_solve2(a, r1, r2, eye_f):
    t = _unit_lower_inverse(a, eye_f)
    return bmmf(t, r1), bmmf(t, r2)


def _uls2_fwd(a, r1, r2, eye_f):
    t = _unit_lower_inverse(a, eye_f)
    x1, x2 = bmmf(t, r1), bmmf(t, r2)
    return (x1, x2), (t, x1, x2, eye_f)


def _uls2_bwd(res, g):
    t, x1, x2, eye_f = res
    high = lax.Precision.HIGH
    d1 = _bdot(t, g[0], 1, 1, high)
    d2 = _bdot(t, g[1], 1, 1, high)
    da = -(_bdot(d1, x1, 2, 2, high) + _bdot(d2, x2, 2, 2, high))
    return da, d1, d2, jnp.zeros_like(eye_f)


unit_lower_solve2.defvjp(_uls2_fwd, _uls2_bwd)


def gd_heads(al, dt, og, s, qc, kc, v, z, a_col, b_col, cst):
    lt_t, eye, incl, strict = cst
    eye_f = eye.astype(F32)
    nh = s.shape[0]
    q = qc * lax.rsqrt(jnp.sum(qc * qc, axis=-1, keepdims=True) + EPS) * (GD_DK ** -0.5)
    k = kc * lax.rsqrt(jnp.sum(kc * kc, axis=-1, keepdims=True) + EPS)
    beta = sigmoid(b_col)
    g = -jnp.exp(al) * softplus(a_col + dt)
    d_mat = jnp.broadcast_to(jnp.sum(g * lt_t, axis=1, keepdims=True), (nh, GD_CHUNK, GD_CHUNK))
    d = jnp.sum(jnp.where(eye, d_mat, 0.0), axis=-1, keepdims=True)
    dec = jnp.exp(jnp.where(incl, d - d_mat, NEG_INF))
    kb = k * beta
    u, w = unit_lower_solve2(bmm_nt(kb, k) * dec * strict, v * beta, kb * jnp.exp(d), eye_f)
    v_new = u - bmm(w, s)
    o = bmm(q * jnp.exp(d), s) + bmm(bmm_nt(q, k) * dec, v_new)
    dl = jnp.sum(g, axis=1, keepdims=True)
    s_new = s * jnp.exp(dl) + bmm_tn(k * jnp.exp(dl - d), v_new)
    on = o * lax.rsqrt(jnp.mean(o * o, axis=-1, keepdims=True) + EPS) * og
    return s_new, on * silu(z)


def _gd_consts():
    r, c = _iota((GD_CHUNK, GD_CHUNK), 0), _iota((GD_CHUNK, GD_CHUNK), 1)
    return (r <= c).astype(F32), r == c, r >= c, (r > c).astype(F32)


def _gd_load(qkv_ref, z_ref, ab_v, al_v, dt_v, heads):
    lane = _iota((1, 128), 1)
    hot_a = [(lane == h).astype(F32) for h in heads]
    hot_b = [(lane == GD_VH + h).astype(F32) for h in heads]
    col = lambda src, hot: jnp.stack([jnp.sum(src * m, axis=-1, keepdims=True) for m in hot])
    ops = (col(al_v, hot_a), col(dt_v, hot_a),
           jnp.stack([qkv_ref[:, pl.ds((h // 2) * GD_DK, GD_DK)] for h in heads]),
           jnp.stack([qkv_ref[:, pl.ds(GD_QKW + (h // 2) * GD_DK, GD_DK)] for h in heads]),
           jnp.stack([qkv_ref[:, pl.ds(2 * GD_QKW + h * GD_DK, GD_DK)] for h in heads]),
           jnp.stack([z_ref[:, pl.ds(h * GD_DK, GD_DK)] for h in heads]),
           col(ab_v, hot_a), col(ab_v, hot_b))
    return ops, hot_a, hot_b


def gd_fwd(name, qkv, z, ab, al, dt, og):
    T = qkv.shape[0]
    nc = T // GD_CHUNK

    def body(qkv_ref, z_ref, ab_ref, al_ref, dt_ref, og_ref, u_ref, st_ref, s_scr):
        @pl.when(pl.program_id(0) == 0)
        def _():
            s_scr[...] = jnp.zeros(s_scr.shape, F32)
        st_ref[0] = s_scr[...]
        cst = _gd_consts()
        ab_v, al_v, dt_v, og_v = ab_ref[...], al_ref[...], dt_ref[...], og_ref[...]
        for b0 in range(0, GD_VH, GD_HB):
            heads = list(range(b0, b0 + GD_HB))
            (alh, dth, q, k, v, zz, a_col, b_col), _, _ = _gd_load(qkv_ref, z_ref, ab_v, al_v, dt_v, heads)
            s_new, u = gd_heads(alh, dth, og_v, s_scr[pl.ds(b0, GD_HB)], q, k, v, zz, a_col, b_col, cst)
            s_scr[pl.ds(b0, GD_HB)] = s_new
            for i, h in enumerate(heads):
                u_ref[:, pl.ds(h * GD_DK, GD_DK)] = u[i].astype(BF16)

    return pl.pallas_call(
        body, name=name, grid=(nc,),
        in_specs=[_row_spec(GD_CHUNK, GD_QKV), _row_spec(GD_CHUNK, GD_VW), _row_spec(GD_CHUNK, 128),
                  _const_spec((1, 128)), _const_spec((1, 128)), _const_spec((1, 128))],
        out_specs=[_row_spec(GD_CHUNK, GD_VW), pl.BlockSpec((1, GD_VH, GD_DK, GD_DK), lambda i: (i, 0, 0, 0))],
        out_shape=[jax.ShapeDtypeStruct((T, GD_VW), BF16), jax.ShapeDtypeStruct((nc, GD_VH, GD_DK, GD_DK), F32)],
        scratch_shapes=[pltpu.VMEM((GD_VH, GD_DK, GD_DK), F32)],
        compiler_params=_params(dimension_semantics=("arbitrary",)),
    )(qkv, z, ab, al, dt, og)


def gd_bwd(name, qkv, z, ab, al, dt, og, states, du, carry=()):
    T = qkv.shape[0]
    nc = T // GD_CHUNK

    def body(qkv_ref, z_ref, ab_ref, al_ref, dt_ref, og_ref, st_ref, du_ref,
             dqkv_ref, dz_ref, dab_ref, dal_ref, ddt_ref, dog_ref, ds_scr):
        @pl.when(pl.program_id(0) == 0)
        def _():
            ds_scr[...] = jnp.zeros(ds_scr.shape, F32)
            dal_ref[...] = jnp.zeros(dal_ref.shape, F32)
            ddt_ref[...] = jnp.zeros(ddt_ref.shape, F32)
            dog_ref[...] = jnp.zeros(dog_ref.shape, F32)
        cst = _gd_consts()
        ab_v, al_v, dt_v, og_v = ab_ref[...], al_ref[...], dt_ref[...], og_ref[...]
        dab = jnp.zeros((GD_CHUNK, 128), F32)
        dal_row = jnp.zeros((1, 128), F32)
        ddt_row = jnp.zeros((1, 128), F32)
        dog_row = jnp.zeros((1, 128), F32)
        for b0 in range(0, GD_VH, GD_HB):
            heads = list(range(b0, b0 + GD_HB))
            (alh, dth, q, k, v, zz, a_col, b_col), hot_a, hot_b = _gd_load(qkv_ref, z_ref, ab_v, al_v, dt_v, heads)
            fn = lambda *a: gd_heads(*a, cst)
            _, f = jax.vjp(fn, alh, dth, og_v, st_ref[0, pl.ds(b0, GD_HB)], q, k, v, zz, a_col, b_col)
            du_h = jnp.stack([du_ref[:, pl.ds(h * GD_DK, GD_DK)].astype(F32) for h in heads])
            dalh, ddth, dog, ds, dq, dk, dv, dz, da_col, db_col = f((ds_scr[pl.ds(b0, GD_HB)], du_h))
            ds_scr[pl.ds(b0, GD_HB)] = ds
            dog_row = dog_row + dog
            for i, h in enumerate(heads):
                dqkv_ref[:, pl.ds(2 * GD_QKW + h * GD_DK, GD_DK)] = dv[i]
                dz_ref[:, pl.ds(h * GD_DK, GD_DK)] = dz[i].astype(BF16)
                dab = dab + da_col[i] * hot_a[i] + db_col[i] * hot_b[i]
                dal_row = dal_row + dalh[i] * hot_a[i]
                ddt_row = ddt_row + ddth[i] * hot_a[i]
                if h % 2 == 0:
                    dqkv_ref[:, pl.ds((h // 2) * GD_DK, GD_DK)] = dq[i] + dq[i + 1]
                    dqkv_ref[:, pl.ds(GD_QKW + (h // 2) * GD_DK, GD_DK)] = dk[i] + dk[i + 1]
        dab_ref[...] = dab
        dal_ref[...] += dal_row
        ddt_ref[...] += ddt_row
        dog_ref[...] += dog_row

    rev = lambda i: (nc - 1 - i, 0)
    outs, ex = _scan_call(
        body, name, nc,
        [pl.BlockSpec((GD_CHUNK, GD_QKV), rev), pl.BlockSpec((GD_CHUNK, GD_VW), rev), pl.BlockSpec((GD_CHUNK, 128), rev),
         _const_spec((1, 128)), _const_spec((1, 128)), _const_spec((1, 128)),
         pl.BlockSpec((1, GD_VH, GD_DK, GD_DK), lambda i: (nc - 1 - i, 0, 0, 0)), pl.BlockSpec((GD_CHUNK, GD_VW), rev)],
        [pl.BlockSpec((GD_CHUNK, GD_QKV), rev), pl.BlockSpec((GD_CHUNK, GD_VW), rev), pl.BlockSpec((GD_CHUNK, 128), rev),
         _const_spec((1, 128)), _const_spec((1, 128)), _const_spec((1, 128))],
        [jax.ShapeDtypeStruct((T, GD_QKV), F32), jax.ShapeDtypeStruct((T, GD_VW), BF16),
         jax.ShapeDtypeStruct((T, 128), F32)] + [jax.ShapeDtypeStruct((1, 128), F32)] * 3,
        [pltpu.VMEM((GD_VH, GD_DK, GD_DK), F32)], (qkv, z, ab, al, dt, og, states, du), carry)
    return (*outs, ex)


def _my_pos():
    return lax.axis_index("x"), lax.axis_index("y"), lax.axis_index("c")


def _peers(pos):
    out = []
    for k in range(1, N_DEV):
        dev = tuple(1 - p if (k >> s) & 1 else p for p, s in zip(pos, (2, 1, 0)))
        out.append((dev, 4 * dev[0] + 2 * dev[1] + dev[2]))
    return out


_ANY = pl.BlockSpec(memory_space=pl.ANY)
_COMM_SCRATCH = [pltpu.SemaphoreType.DMA((N_DEV - 1,)), pltpu.SemaphoreType.DMA((N_DEV - 1,)), pltpu.SemaphoreType.DMA(())]
GATHER, SCATTER = "gather", "scatter"


def _exchange_descs(pos, kinds, x_refs, o_refs, send_sems, recv_sems, local_sems, with_recvs=True):
    me = 4 * pos[0] + 2 * pos[1] + pos[2]
    peers = _peers(pos)
    local, sends, recvs = [], [], []
    for i, (kind, x, o) in enumerate(zip(kinds, x_refs, o_refs)):
        local.append(pltpu.make_async_copy(x if kind == GATHER else x.at[me], o.at[me], local_sems.at[i]))
        for k, (dev, pid) in enumerate(peers):
            s = (N_DEV - 1) * i + k
            src = x if kind == GATHER else x.at[pid]
            sends.append(pltpu.make_async_remote_copy(src, o.at[me], send_sems.at[s], recv_sems.at[s],
                                                      device_id=dev, device_id_type=MESH))
            if with_recvs:
                recvs.append(pltpu.make_async_remote_copy(src, o.at[pid], send_sems.at[s], recv_sems.at[s],
                                                          device_id=dev, device_id_type=MESH))
    return local, sends, recvs


def _scan_call(body, name, nsteps, in_specs, out_specs, out_shape, scratch_shapes, args, carry=()):
    n_in, n_out, n_scr, n_ex = len(in_specs), len(out_specs), len(scratch_shapes), len(carry)
    params = _params(dimension_semantics=("arbitrary",))
    if not carry:
        outs = pl.pallas_call(body, name=name, grid=(nsteps,), in_specs=in_specs, out_specs=out_specs, out_shape=out_shape,
                              scratch_shapes=scratch_shapes, compiler_params=params)(*args)
        return outs, []
    kinds = [k for k, _ in carry]
    xs = [a for _, a in carry]
    ex_shape = [jax.ShapeDtypeStruct(((N_DEV,) + a.shape) if k == GATHER else a.shape, a.dtype) for k, a in carry]

    def full(*refs):
        ins, x_refs = refs[:n_in], refs[n_in:n_in + n_ex]
        outs = refs[n_in + n_ex:n_in + n_ex + n_out]
        o_refs = refs[n_in + n_ex + n_out:n_in + 2 * n_ex + n_out]
        scr = refs[n_in + 2 * n_ex + n_out:n_in + 2 * n_ex + n_out + n_scr]
        sems = refs[n_in + 2 * n_ex + n_out + n_scr:]
        pos = _my_pos()
        step = pl.program_id(0)

        @pl.when(step == 0)
        def _():
            local, sends, _ = _exchange_descs(pos, kinds, x_refs, o_refs, *sems, with_recvs=False)
            for cp in local + sends:
                cp.start()

        body(*ins, *outs, *scr)

        @pl.when(step == nsteps - 1)
        def _():
            local, sends, recvs = _exchange_descs(pos, kinds, x_refs, o_refs, *sems)
            for cp in recvs:
                cp.wait_recv()
            for cp in sends:
                cp.wait_send()
            for cp in local:
                cp.wait()

    n_sem = (N_DEV - 1) * n_ex
    outs = pl.pallas_call(
        full, name=name, grid=(nsteps,), in_specs=list(in_specs) + [_ANY] * n_ex, out_specs=list(out_specs) + [_ANY] * n_ex,
        out_shape=list(out_shape) + ex_shape,
        scratch_shapes=list(scratch_shapes) + [pltpu.SemaphoreType.DMA((n_sem,)), pltpu.SemaphoreType.DMA((n_sem,)),
                                               pltpu.SemaphoreType.DMA((n_ex,))],
        compiler_params=params)(*args, *xs)
    return outs[:n_out], list(outs[n_out:])


def all_gather(name, x):
    def body(x_ref, o_ref, send_sems, recv_sems, local_sem):
        pos = _my_pos()
        me = 4 * pos[0] + 2 * pos[1] + pos[2]
        peers = _peers(pos)
        mine = pltpu.make_async_copy(x_ref, o_ref.at[me], local_sem)
        mine.start()
        sends = [pltpu.make_async_remote_copy(x_ref, o_ref.at[me], send_sems.at[k], recv_sems.at[k],
                                              device_id=dev, device_id_type=MESH) for k, (dev, _) in enumerate(peers)]
        for cp in sends:
            cp.start()
        for k, (dev, pid) in enumerate(peers):
            pltpu.make_async_remote_copy(x_ref, o_ref.at[pid], send_sems.at[k], recv_sems.at[k],
                                         device_id=dev, device_id_type=MESH).wait_recv()
        for cp in sends:
            cp.wait_send()
        mine.wait()

    return pl.pallas_call(
        body, name=name, out_shape=jax.ShapeDtypeStruct((N_DEV,) + x.shape, x.dtype),
        in_specs=[_ANY], out_specs=_ANY, scratch_shapes=_COMM_SCRATCH,
    )(x)


def lb_rows(r0, r1, r2, r3):
    mx = jnp.maximum(jnp.maximum(r0, r1), jnp.maximum(r2, r3))
    e = [jnp.exp(r - mx) for r in (r0, r1, r2, r3)]
    inv = 1.0 / (e[0] + e[1] + e[2] + e[3])
    c0 = e[0] * inv
    c1 = c0 + e[1] * inv
    c2 = c1 + e[2] * inv
    c3 = c2 + e[3] * inv
    return c0 - c0, c1 - c0, c2 - c0, c3 - c0


def mod_partial(name, c_all, ada_w):
    n, _, w = ada_w.shape

    def body(c_ref, w_ref, o_ref):
        o_ref[0] = mm(c_ref[...], w_ref[0])

    return pl.pallas_call(
        body, name=name, grid=(n,),
        in_specs=[_const_spec((N_DEV, D)), pl.BlockSpec((1, D, w), lambda i: (i, 0, 0))],
        out_specs=pl.BlockSpec((1, N_DEV, w), lambda i: (i, 0, 0)),
        out_shape=jax.ShapeDtypeStruct((n, N_DEV, w), F32),
    )(c_all, ada_w)


def prep(name, modp, ada_b, hgrn_lb):
    def body(mp_ref, b_ref, lb_ref, mod_ref, lbo_ref):
        mod_ref[...] = mp_ref[...] + b_ref[...]
        out = lb_rows(*[lb_ref[pl.ds(i, 1), :] for i in range(4)])
        for i in range(4):
            lbo_ref[pl.ds(i, 1), :] = out[i]

    return pl.pallas_call(
        body, name=name,
        out_shape=[jax.ShapeDtypeStruct(modp.shape, F32), jax.ShapeDtypeStruct(hgrn_lb.shape, F32)],
    )(modp, ada_b, hgrn_lb)


def lb_grad(name, hgrn_lb, dlb_parts):
    def body(lb_ref, d_ref, o_ref):
        cts = []
        for i in range(4):
            acc = d_ref[0, pl.ds(i, 1), :]
            for p in range(1, N_DEV):
                acc = acc + d_ref[p, pl.ds(i, 1), :]
            cts.append(acc)
        _, f = jax.vjp(lb_rows, *[lb_ref[pl.ds(i, 1), :] for i in range(4)])
        for i, g in enumerate(f(tuple(cts))):
            o_ref[pl.ds(i, 1), :] = g

    return pl.pallas_call(body, name=name, out_shape=jax.ShapeDtypeStruct(hgrn_lb.shape, F32))(hgrn_lb, dlb_parts)


def ada_grad(name, c_t, dm):
    n, _, w = dm.shape

    def body(c_ref, d_ref, o_ref):
        acc = c_ref[:, pl.ds(0, 1)] * d_ref[0, pl.ds(0, 1), :]
        for b in range(1, N_DEV):
            acc = acc + c_ref[:, pl.ds(b, 1)] * d_ref[0, pl.ds(b, 1), :]
        o_ref[0] = acc

    return pl.pallas_call(
        body, name=name, grid=(n,),
        in_specs=[_const_spec((D, N_DEV)), pl.BlockSpec((1, N_DEV, w), lambda i: (i, 0, 0))],
        out_specs=pl.BlockSpec((1, D, w), lambda i: (i, 0, 0)),
        out_shape=jax.ShapeDtypeStruct((n, D, w), F32),
    )(c_t, dm)


ADAMW_ROWS = 256


def adamw(name, w, m, v, gparts):
    shape = w.shape
    L = shape[-1]
    R = 1
    for s in shape[:-1]:
        R *= s
    P = gparts.shape[0]
    tr = min(R, ADAMW_ROWS)

    def body(w_ref, m_ref, v_ref, g_ref, go_ref, d_ref, mo_ref, vo_ref):
        g = g_ref[0].astype(F32)
        for p in range(1, P):
            g = g + g_ref[p].astype(F32)
        mn = ADAM_B1 * m_ref[...] + (1.0 - ADAM_B1) * g
        vn = ADAM_B2 * v_ref[...] + (1.0 - ADAM_B2) * (g * g)
        m_hat = mn / (1.0 - ADAM_B1 ** ADAM_STEP)
        v_hat = vn / (1.0 - ADAM_B2 ** ADAM_STEP)
        go_ref[...] = g
        d_ref[...] = -ADAM_LR * (m_hat / (jnp.sqrt(v_hat) + ADAM_EPS) + ADAM_WD * w_ref[...])
        mo_ref[...] = mn
        vo_ref[...] = vn

    spec = pl.BlockSpec((tr, L), lambda i: (i, 0))
    outs = pl.pallas_call(
        body, name=name, grid=(R // tr,),
        in_specs=[spec, spec, spec, pl.BlockSpec((P, tr, L), lambda i: (0, i, 0))],
        out_specs=[spec] * 4, out_shape=[jax.ShapeDtypeStruct((R, L), F32)] * 4,
        compiler_params=_params(dimension_semantics=("arbitrary",)),
    )(w.reshape(R, L), m.reshape(R, L), v.reshape(R, L), gparts.reshape(P, R, L))
    return [o.reshape(shape) for o in outs]


def _cols_from(g):
    _, n, k, w8 = g.shape
    return jnp.transpose(g, (1, 2, 0, 3)).reshape(n, k, N_DEV * w8)


def _rows_from(g):
    _, n, k8, nn = g.shape
    return jnp.transpose(g, (1, 0, 2, 3)).reshape(n, N_DEV * k8, nn)


def _scatter_cols(dw, wire=BF16):
    n, k, nn = dw.shape
    return jnp.transpose(dw.reshape(n, k, N_DEV, nn // N_DEV), (2, 0, 1, 3)).astype(wire)


def _scatter_rows(dw, wire=BF16):
    n, k, nn = dw.shape
    return jnp.transpose(dw.reshape(n, N_DEV, k // N_DEV, nn), (1, 0, 2, 3)).astype(wire)


def _pad_lanes(v, width=128):
    return jnp.pad(v, ((0, 0), (0, width - v.shape[1])))


def _stack_rows(parts):
    n, L = len(parts), parts[0].shape[1]
    r = lax.broadcasted_iota(jnp.int32, (n, L), 0)
    out = jnp.zeros((n, L), parts[0].dtype)
    for i, p in enumerate(parts):
        out = jnp.where(r == i, jnp.broadcast_to(p, (n, L)), out)
    return out


def kernel(x, c, positions, hgrn_lb, ada_w, ada_b, norm_g, hg_in_w, hg_out_w, hg_onorm, sw_in_w, sw_out_w, sw_qnorm, sw_knorm, sw_sinks, gd_in_w, gd_out_w, gd_conv_w, gd_a_log, gd_dt_bias, gd_onorm, loss_target, m_hgrn_lb, m_ada_w, m_ada_b, m_norm_g, m_hg_in_w, m_hg_out_w, m_hg_onorm, m_sw_in_w, m_sw_out_w, m_sw_qnorm, m_sw_knorm, m_sw_sinks, m_gd_in_w, m_gd_out_w, m_gd_conv_w, m_gd_a_log, m_gd_dt_bias, m_gd_onorm, v_hgrn_lb, v_ada_w, v_ada_b, v_norm_g, v_hg_in_w, v_hg_out_w, v_hg_onorm, v_sw_in_w, v_sw_out_w, v_sw_qnorm, v_sw_knorm, v_sw_sinks, v_gd_in_w, v_gd_out_w, v_gd_conv_w, v_gd_a_log, v_gd_dt_bias, v_gd_onorm):
    pos = _my_pos()
    me = 4 * pos[0] + 2 * pos[1] + pos[2]
    x0 = x[0]
    tgt = loss_target[0]
    n_layers = norm_g.shape[0]
    aw = ada_w.shape[2]

    c_all = all_gather("ag_c", c)[:, 0, :]
    modp = all_gather("ag_mod", mod_partial("mod_partial", c_all, ada_w))
    modp_mine = lax.dynamic_index_in_dim(modp, me, axis=2, keepdims=False)
    modp_mine = jnp.transpose(modp_mine, (1, 0, 2)).reshape(n_layers, N_DEV * aw)
    mod, lb_all = prep("prep", modp_mine, ada_b, hgrn_lb)
    shift, scale, gate = mod[:, :D], mod[:, D:2 * D], mod[:, 2 * D:]
    row = lambda a, i: a[i:i + 1]

    w_hg_in = [_cols_from(all_gather("ag_hg_in0", hg_in_w[0:1].astype(BF16)))[0], None]
    later = [(GATHER, hg_in_w[1:2].astype(BF16)), (GATHER, hg_out_w.astype(BF16)), (GATHER, sw_in_w.astype(BF16)),
             (GATHER, sw_out_w.astype(BF16)), (GATHER, gd_in_w.astype(BF16)), (GATHER, gd_out_w.astype(BF16)),
             (GATHER, gd_conv_w)]
    gd_al, gd_dt = _pad_lanes(gd_a_log), _pad_lanes(gd_dt_bias)

    inv_freq = ROPE_THETA ** (-jnp.arange(0, SW_DH, 2, dtype=F32) / SW_DH)
    ang = positions[0].astype(F32)[:, None] * inv_freq
    cos, sin = jnp.cos(ang), jnp.sin(ang)
    cos64, sin64 = jnp.concatenate([cos, cos], axis=-1), jnp.concatenate([-sin, sin], axis=-1)

    xs, hs, ys, saved = [x0], [], [], []
    last = n_layers - 1
    h = norm_mod("norm0", x0, row(norm_g, 0), row(scale, 0), row(shift, 0))
    for i in range(n_layers):
        kind, j = i % 3, i // 3
        hs.append(h)
        if kind == 0:
            p = matmul(f"hg_in{i}", h, w_hg_in[j])
            u, st, got = hg_fwd(f"hg_fwd{i}", p, row(lb_all, i), row(hg_onorm, j), later if i == 0 else ())
            if i == 0:
                w_hg_in[1] = _cols_from(got[0])[0]
                w_hg_out = _rows_from(got[1])
                w_sw_in, w_sw_out = _cols_from(got[2])[0], _rows_from(got[3])[0]
                w_gd_in, w_gd_out = _cols_from(got[4])[0], _rows_from(got[5])[0]
                w_gd_qkv, w_gd_z = w_gd_in[:, :GD_QKV], w_gd_in[:, GD_QKV:GD_QKV + GD_VW]
                w_gd_ab = _pad_lanes(w_gd_in[:, GD_QKV + GD_VW:])
                conv_w8 = jnp.pad(_cols_from(got[6])[0], ((0, HALO - CONV_K), (0, 0)))
            w_out = w_hg_out[j]
            saved.append((p, u, st))
        elif kind == 1:
            p = matmul(f"sw_in{i}", h, w_sw_in)
            u, kpost = swa_fwd(f"sw_fwd{i}", p, cos64, sin64, row(sw_qnorm, j), row(sw_knorm, j), row(sw_sinks, j))
            w_out = w_sw_out
            saved.append((p, u, kpost))
        else:
            xq = matmul(f"gd_qkv{i}", h, w_gd_qkv)
            zz = matmul(f"gd_z{i}", h, w_gd_z)
            ab = matmul(f"gd_ab{i}", h, w_gd_ab)
            cv = conv_fwd(f"gd_conv{i}", xq, conv_w8)
            u, st = gd_fwd(f"gd_fwd{i}", cv, zz, ab, gd_al, gd_dt, row(gd_onorm, j))
            w_out = w_gd_out
            saved.append((xq, zz, ab, cv, u, st))
        if i < last:
            y, xn, h = out_resid_norm(f"out{i}", u, w_out, xs[i], row(gate, i), row(norm_g, i + 1), row(scale, i + 1),
                                      row(shift, i + 1))
            xs.append(xn)
        else:
            y, dx, dy, loss_acc, dgate_last = out_loss(f"out{i}", u, w_out, xs[i], tgt, row(gate, i))
        ys.append(y)
    loss = lax.psum(loss_acc[0, 0], ("x", "y", "c"))

    dgate = [None] * n_layers
    dgate[last] = dgate_last
    dg_norm, dscale, dshift = [None] * n_layers, [None] * n_layers, [None] * n_layers
    dlb = [jnp.zeros((1, D), F32)] * n_layers
    d_hg_on = [None] * 2
    rs_hg_in, rs_hg_out = [None] * 2, [None] * 2
    pending = ()
    for i in range(last, -1, -1):
        kind, j = i % 3, i // 3
        h = hs[i]
        if kind == 0:
            p, u, st = saved[i]
            du = matmul_nt(f"hg_du{i}", [(dy, w_hg_out[j])])
            d_out = matmul_tn(f"hg_dwo{i}", u, dy)
            dp, dlb_i, dog, got = hg_bwd(f"hg_bwd{i}", p, row(lb_all, i), row(hg_onorm, j), st, du, pending)
            if got:
                rs_sw_in, rs_sw_out = got
            dlb[i] = dlb_i
            d_hg_on[j] = dog
            dh_pairs = [(dp, w_hg_in[j])]
            d_in = matmul_tn(f"hg_dwi{i}", h, dp)
            pending = [(SCATTER, _scatter_cols(d_in[None])), (SCATTER, _scatter_rows(d_out[None]))]
        elif kind == 1:
            p, u, kpost = saved[i]
            du = matmul_nt(f"sw_du{i}", [(dy, w_sw_out)])
            d_out = matmul_tn(f"sw_dwo{i}", u, dy)
            dp, d_qn, d_kn, d_sk, got = swa_bwd(f"sw_bwd{i}", p, kpost, cos64, sin64, row(sw_qnorm, j), row(sw_knorm, j),
                                                row(sw_sinks, j), du, pending)
            rs_gd_in, rs_gd_out, rs_gd_conv = got
            dh_pairs = [(dp, w_sw_in)]
            d_in = matmul_tn(f"sw_dwi{i}", h, dp)
            pending = [(SCATTER, _scatter_cols(d_in[None])), (SCATTER, _scatter_rows(d_out[None]))]
        else:
            xq, zz, ab, cv, u, st = saved[i]
            du = matmul_nt(f"gd_du{i}", [(dy, w_gd_out)])
            d_out = matmul_tn(f"gd_dwo{i}", u, dy)
            dcv, dz, dab, d_al, d_dt, d_gd_on, got = gd_bwd(f"gd_bwd{i}", cv, zz, ab, gd_al, gd_dt, row(gd_onorm, j), st, du,
                                                            pending)
            rs_hg_in[1], rs_hg_out[1] = got
            dxq, d_conv8 = conv_bwd(f"gd_dconv{i}", xq, dcv, conv_w8)
            dh_pairs = [(dxq, w_gd_qkv), (dz, w_gd_z), (dab, w_gd_ab)]
            d_in = jnp.concatenate([matmul_tn(f"gd_dwq{i}", h, dxq), matmul_tn(f"gd_dwz{i}", h, dz),
                                    matmul_tn(f"gd_dwab{i}", h, dab)[:, :2 * GD_VH]], axis=1)
            pending = [(SCATTER, _scatter_cols(d_in[None])), (SCATTER, _scatter_rows(d_out[None])),
                       (SCATTER, _scatter_cols(d_conv8[None, :CONV_K], F32))]
        if i > 0:
            (dx, dy, dg_norm[i], dscale[i], dshift[i], dgate[i - 1]), _ = dh_bwd_row(
                f"dh{i}", dh_pairs, xs[i], dx, row(norm_g, i), row(scale, i), row(shift, i), ys[i - 1], row(gate, i - 1))
        else:
            (dx, dg_norm[i], dscale[i], dshift[i]), (rs_hg_in[0], rs_hg_out[0]) = dh_bwd_row(
                f"dh{i}", dh_pairs, xs[i], dx, row(norm_g, i), row(scale, i), row(shift, i), carry=pending)
    grad_x = dx[None]

    dmod = jnp.concatenate([_stack_rows(dshift), _stack_rows(dscale), _stack_rows(dgate)], axis=1)
    misc = _stack_rows(d_hg_on + [d_gd_on, _pad_lanes(d_qn), _pad_lanes(d_kn), _pad_lanes(d_sk), d_al, d_dt])
    small = jnp.concatenate([_stack_rows(dlb).reshape(-1, 128), _stack_rows(dg_norm).reshape(-1, 128),
                             dmod.reshape(-1, 128), misc], axis=0)
    small_all = all_gather("ag_small", small)
    n_lb = n_layers * D // 128
    n_mod = n_layers * 3 * D // 128
    o = 0
    dlb_parts = small_all[:, o:o + n_lb].reshape(N_DEV, n_layers, D); o += n_lb
    dgn_parts = small_all[:, o:o + n_lb].reshape(N_DEV, n_layers, D); o += n_lb
    dmod_parts = small_all[:, o:o + n_mod].reshape(N_DEV, n_layers, 3 * D); o += n_mod
    dhgon_parts = small_all[:, o:o + 2]; o += 2
    dgdon_parts = small_all[:, o:o + 1]; o += 1
    dqn_parts = small_all[:, o:o + 1, :SW_DH]; o += 1
    dkn_parts = small_all[:, o:o + 1, :SW_DH]; o += 1
    dsk_parts = small_all[:, o:o + 1, :SW_KV * SW_G]; o += 1
    dal_parts = small_all[:, o:o + 1, :GD_VH]; o += 1
    ddt_parts = small_all[:, o:o + 1, :GD_VH]; o += 1

    g_lb = lb_grad("lb_grad", hgrn_lb, dlb_parts)
    dm_mine = lax.dynamic_slice_in_dim(dmod_parts, me * aw, aw, axis=2)
    g_ada_w = ada_grad("ada_grad", jnp.transpose(c_all), jnp.transpose(dm_mine, (1, 0, 2)))

    res = {}
    res["hgrn_lb"] = adamw("aw_hgrn_lb", hgrn_lb, m_hgrn_lb, v_hgrn_lb, g_lb[None])
    res["ada_w"] = adamw("aw_ada_w", ada_w, m_ada_w, v_ada_w, g_ada_w[None])
    res["ada_b"] = adamw("aw_ada_b", ada_b, m_ada_b, v_ada_b, dmod_parts)
    res["norm_g"] = adamw("aw_norm_g", norm_g, m_norm_g, v_norm_g, dgn_parts)
    res["hg_in_w"] = adamw("aw_hg_in", hg_in_w, m_hg_in_w, v_hg_in_w, jnp.concatenate(rs_hg_in, axis=1))
    res["hg_out_w"] = adamw("aw_hg_out", hg_out_w, m_hg_out_w, v_hg_out_w, jnp.concatenate(rs_hg_out, axis=1))
    res["hg_onorm"] = adamw("aw_hg_onorm", hg_onorm, m_hg_onorm, v_hg_onorm, dhgon_parts)
    res["sw_in_w"] = adamw("aw_sw_in", sw_in_w, m_sw_in_w, v_sw_in_w, rs_sw_in)
    res["sw_out_w"] = adamw("aw_sw_out", sw_out_w, m_sw_out_w, v_sw_out_w, rs_sw_out)
    res["sw_qnorm"] = adamw("aw_sw_qn", sw_qnorm, m_sw_qnorm, v_sw_qnorm, dqn_parts)
    res["sw_knorm"] = adamw("aw_sw_kn", sw_knorm, m_sw_knorm, v_sw_knorm, dkn_parts)
    res["sw_sinks"] = adamw("aw_sw_sinks", sw_sinks, m_sw_sinks, v_sw_sinks, dsk_parts)
    res["gd_in_w"] = adamw("aw_gd_in", gd_in_w, m_gd_in_w, v_gd_in_w, rs_gd_in)
    res["gd_out_w"] = adamw("aw_gd_out", gd_out_w, m_gd_out_w, v_gd_out_w, rs_gd_out)
    res["gd_conv_w"] = adamw("aw_gd_conv", gd_conv_w, m_gd_conv_w, v_gd_conv_w, rs_gd_conv)
    res["gd_a_log"] = adamw("aw_gd_alog", gd_a_log, m_gd_a_log, v_gd_a_log, dal_parts)
    res["gd_dt_bias"] = adamw("aw_gd_dt", gd_dt_bias, m_gd_dt_bias, v_gd_dt_bias, ddt_parts)
    res["gd_onorm"] = adamw("aw_gd_onorm", gd_onorm, m_gd_onorm, v_gd_onorm, dgdon_parts)

    order = ["hgrn_lb", "ada_w", "ada_b", "norm_g", "hg_in_w", "hg_out_w", "hg_onorm", "sw_in_w", "sw_out_w", "sw_qnorm",
             "sw_knorm", "sw_sinks", "gd_in_w", "gd_out_w", "gd_conv_w", "gd_a_log", "gd_dt_bias", "gd_onorm"]
    outs = [loss, grad_x]
    for part in range(4):
        outs += [res[n][part] for n in order]
    return tuple(outs)
```

```python
import functools

import jax
import jax.numpy as jnp
from jax import lax
from jax.experimental import pallas as pl
from jax.experimental.pallas import tpu as pltpu

F32, BF16 = jnp.float32, jnp.bfloat16
HI = lax.Precision.HIGHEST
MESH = pl.DeviceIdType.MESH
N_DEV = 8
D = 1024
EPS = 1e-6
NEG_INF = float("-inf")

HG_HEADS, HG_DH, HG_BLK, HG_CH, HG_SUB = 8, 128, 128, 32, 8
HG_SAFE = 60.0
SW_KV, SW_G, SW_DH, SW_BLK = 4, 4, 64, 128
SW_QW, SW_KVW = 1024, 256
GD_QK_HEADS, GD_DK, GD_CHUNK = 8, 128, 64
GD_QKW, GD_VW, GD_QKV = 1024, 2048, 4096
CONV_K = 4
ROPE_THETA = 10000.0

ADAM_LR, ADAM_B1, ADAM_B2, ADAM_EPS, ADAM_WD, ADAM_STEP = 0.001, 0.9, 0.999, 1e-08, 0.01, 10

VMEM_LIMIT = 56 * 1024 * 1024


def _params(**kw):
    return pltpu.CompilerParams(vmem_limit_bytes=VMEM_LIMIT, **kw)


def _dot(a, b, ca, cb):
    return lax.dot_general(a.astype(BF16), b.astype(BF16), (((ca,), (cb,)), ((), ())), preferred_element_type=F32)


def mm(a, b):
    return _dot(a, b, 1, 0)


def mm_nt(a, b):
    return _dot(a, b, 1, 1)


def mm_tn(a, b):
    return _dot(a, b, 0, 0)


def mmf(a, b):
    return lax.dot_general(a, b, (((1,), (0,)), ((), ())), precision=HI, preferred_element_type=F32)


def _split3(x):
    h1 = x.astype(BF16)
    r1 = x - h1.astype(F32)
    h2 = r1.astype(BF16)
    h3 = (r1 - h2.astype(F32)).astype(BF16)
    return h1, h2, h3


def _lin01_l(m, x):
    mb = m.astype(BF16)
    return sum(lax.dot_general(mb, p, (((1,), (0,)), ((), ())), preferred_element_type=F32) for p in _split3(x))


def _lin01_r(x, m):
    mb = m.astype(BF16)
    return sum(lax.dot_general(p, mb, (((1,), (0,)), ((), ())), preferred_element_type=F32) for p in _split3(x))


def _bdot(a, b, ca, cb, prec=None):
    return lax.dot_general(a, b, (((ca,), (cb,)), ((0,), (0,))), precision=prec, preferred_element_type=F32)


def bmm(a, b):
    return _bdot(a.astype(BF16), b.astype(BF16), 2, 1)


def bmm_nt(a, b):
    return _bdot(a.astype(BF16), b.astype(BF16), 2, 2)


def bmm_tn(a, b):
    return _bdot(a.astype(BF16), b.astype(BF16), 1, 1)


def bmmf(a, b):
    return _bdot(a, b, 2, 1, lax.Precision.HIGH)


def _blin01(m, x):
    mb = m.astype(BF16)
    return sum(_bdot(mb, p, 2, 1) for p in _split3(x))


@jax.custom_vjp
def blin01(m, mt, x):
    return _blin01(m, x)


def _blin01_fwd(m, mt, x):
    return _blin01(m, x), (m, mt)


def _blin01_bwd(res, g):
    m, mt = res
    return jnp.zeros_like(m), jnp.zeros_like(mt), _blin01(mt, g)


blin01.defvjp(_blin01_fwd, _blin01_bwd)


def _blin01_r(x, m):
    mb = m.astype(BF16)
    return sum(_bdot(p, mb, 2, 1) for p in _split3(x))


@jax.custom_vjp
def blin01_r(x, m, mt):
    return _blin01_r(x, m)


def _blin01_r_fwd(x, m, mt):
    return _blin01_r(x, m), (m, mt)


def _blin01_r_bwd(res, g):
    m, mt = res
    return _blin01_r(g, mt), jnp.zeros_like(m), jnp.zeros_like(mt)


blin01_r.defvjp(_blin01_r_fwd, _blin01_r_bwd)


@jax.custom_vjp
def lin01_l(m, mt, x):
    return _lin01_l(m, x)


def _lin01_l_fwd(m, mt, x):
    return _lin01_l(m, x), (m, mt)


def _lin01_l_bwd(res, g):
    m, mt = res
    return jnp.zeros_like(m), jnp.zeros_like(mt), _lin01_l(mt, g)


lin01_l.defvjp(_lin01_l_fwd, _lin01_l_bwd)


@jax.custom_vjp
def lin01_r(x, m, mt):
    return _lin01_r(x, m)


def _lin01_r_fwd(x, m, mt):
    return _lin01_r(x, m), (m, mt)


def _lin01_r_bwd(res, g):
    m, mt = res
    return _lin01_r(g, mt), jnp.zeros_like(m), jnp.zeros_like(mt)


lin01_r.defvjp(_lin01_r_fwd, _lin01_r_bwd)


@functools.partial(jax.custom_vjp, nondiff_argnums=(1, 2))
def rows(x, start, size):
    return lax.slice_in_dim(x, start, start + size, axis=x.ndim - 2)


def _rows_fwd(x, start, size):
    return lax.slice_in_dim(x, start, start + size, axis=x.ndim - 2), jnp.zeros(x.shape[:-1] + (1,), F32)


def _rows_bwd(start, size, res, g):
    ax = g.ndim - 2
    total = res.shape[ax]
    zeros = lambda n: jnp.zeros(g.shape[:ax] + (n,) + g.shape[ax + 1:], g.dtype)
    parts = []
    if start > 0:
        parts.append(zeros(start))
    parts.append(g)
    if total - start - size > 0:
        parts.append(zeros(total - start - size))
    return (jnp.concatenate(parts, axis=ax) if len(parts) > 1 else g,)


rows.defvjp(_rows_fwd, _rows_bwd)


@functools.partial(jax.custom_vjp, nondiff_argnums=(1,))
def rowsel(x, s):
    return lax.slice_in_dim(x, s, s + 1, axis=x.ndim - 2)


def _rowsel_fwd(x, s):
    return lax.slice_in_dim(x, s, s + 1, axis=x.ndim - 2), jnp.zeros(x.shape[:-1] + (1,), F32)


def _rowsel_bwd(s, res, g):
    r = lax.broadcasted_iota(jnp.int32, res.shape, res.ndim - 2)
    return (jnp.where(r == s, g, 0.0),)


rowsel.defvjp(_rowsel_fwd, _rowsel_bwd)


def sigmoid(x):
    return jax.nn.sigmoid(x)


def silu(x):
    return x * jax.nn.sigmoid(x)


def softplus(x):
    return jnp.maximum(x, 0.0) + jnp.log1p(jnp.exp(-jnp.abs(x)))


def _iota(shape, dim):
    return lax.broadcasted_iota(jnp.int32, shape, dim)


def norm_mod_fn(g, sc, sh, x):
    r = lax.rsqrt(jnp.mean(x * x, axis=-1, keepdims=True) + EPS)
    return (x * r * g) * (1.0 + sc) + sh


def _row_spec(tm, f):
    return pl.BlockSpec((tm, f), lambda i: (i, 0))


def _const_spec(shape):
    nd = len(shape)
    return pl.BlockSpec(shape, lambda i: (0,) * nd)


def _tiled(name, fn, consts, rows_in, row_outs, acc_outs=(), tm=512, carry=()):
    T = rows_in[0].shape[0]
    tm = min(tm, T)
    n_c, n_r, n_ro = len(consts), len(rows_in), len(row_outs)

    def body(*refs):
        c_refs, r_refs = refs[:n_c], refs[n_c:n_c + n_r]
        ro_refs, ao_refs = refs[n_c + n_r:n_c + n_r + n_ro], refs[n_c + n_r + n_ro:]
        outs = fn(*[r[...] for r in c_refs], *[r[...] for r in r_refs])
        for r, v in zip(ro_refs, outs[:n_ro]):
            r[...] = v.astype(r.dtype)
        if ao_refs:
            @pl.when(pl.program_id(0) == 0)
            def _():
                for r in ao_refs:
                    r[...] = jnp.zeros(r.shape, r.dtype)
            for r, v in zip(ao_refs, outs[n_ro:]):
                r[...] += v

    out_shape = [jax.ShapeDtypeStruct((T, f), dt) for f, dt in row_outs] + [jax.ShapeDtypeStruct(s, F32) for s in acc_outs]
    out_specs = [_row_spec(tm, f) for f, _ in row_outs] + [_const_spec(s) for s in acc_outs]
    in_specs = [_const_spec(c.shape) for c in consts] + [_row_spec(tm, r.shape[1]) for r in rows_in]
    return _scan_call(body, name, T // tm, in_specs, out_specs, out_shape, [], (*consts, *rows_in), carry)


def norm_mod(name, x, g, sc, sh, carry=()):
    (h,), got = _tiled(name, lambda g, sc, sh, x: (norm_mod_fn(g, sc, sh, x),), [g, sc, sh], [x], [(D, BF16)], carry=carry)
    return h, got


def out_resid_norm(name, u, w, x, gate, g, sc, sh):
    def fn(gate, g, sc, sh, w, x, u):
        y = mm(u, w)
        xn = x + gate * y
        return y, xn, norm_mod_fn(g, sc, sh, xn)
    return _tiled(name, fn, [gate, g, sc, sh, w], [x, u], [(D, F32), (D, F32), (D, BF16)])[0]


def out_loss(name, u, w, x, tgt, gate):
    def fn(gate, w, x, tgt, u):
        y = mm(u, w)
        err = x + gate * y - tgt
        dx = err * (1.0 / D)
        per_tok = jnp.sum(err * err, axis=-1, keepdims=True) * (0.5 / D)
        loss = jnp.sum(per_tok, axis=0, keepdims=True)
        return y, dx, gate * dx, jnp.broadcast_to(loss, (1, 128)), jnp.sum(dx * y, axis=0, keepdims=True)
    return _tiled(name, fn, [gate, w], [x, tgt, u], [(D, F32), (D, F32), (D, BF16)], [(1, 128), (1, D)])[0]


def dh_bwd_row(name, pairs, x, dxn, g, sc, sh, y_prev=None, gate_prev=None, carry=()):
    with_prev = y_prev is not None
    n = len(pairs)

    def fn(*a):
        a = list(a)
        gate_p = a.pop(0) if with_prev else None
        g, sc, sh = a[:3]
        ws, a = a[3:3 + n], a[3 + n:]
        x, dxn = a[:2]
        yp = a[2] if with_prev else None
        dps = a[3:] if with_prev else a[2:]
        dh = mm_nt(dps[0], ws[0])
        for dp, w in zip(dps[1:], ws[1:]):
            dh = dh + mm_nt(dp, w)
        _, f = jax.vjp(norm_mod_fn, g, sc, sh, x)
        dg, dsc, dsh, dx = f(dh)
        dx = dx + dxn
        if with_prev:
            return dx, gate_p * dx, dg, dsc, dsh, jnp.sum(dx * yp, axis=0, keepdims=True)
        return dx, dg, dsc, dsh

    ws, dps = [w for _, w in pairs], [dp for dp, _ in pairs]
    tm = MM_ROWS_WIDE if sum(dp.shape[1] for dp in dps) > MM_WIDE else MM_ROWS_NARROW
    if with_prev:
        return _tiled(name, fn, [gate_prev, g, sc, sh] + ws, [x, dxn, y_prev] + dps, [(D, F32), (D, BF16)], [(1, D)] * 4,
                      tm=tm, carry=carry)
    return _tiled(name, fn, [g, sc, sh] + ws, [x, dxn] + dps, [(D, F32)], [(1, D)] * 3, tm=tm, carry=carry)


MM_ROWS_WIDE, MM_ROWS_NARROW, MM_WIDE = 256, 512, 2048


def matmul(name, a, w, out_dtype=F32):
    T, K = a.shape
    N = w.shape[1]
    tm = min(MM_ROWS_NARROW if N <= MM_WIDE else MM_ROWS_WIDE, T)

    def body(a_ref, w_ref, o_ref):
        o_ref[...] = mm(a_ref[...], w_ref[...]).astype(o_ref.dtype)

    return pl.pallas_call(
        body, name=name, grid=(T // tm,),
        in_specs=[_row_spec(tm, K), _const_spec((K, N))], out_specs=_row_spec(tm, N),
        out_shape=jax.ShapeDtypeStruct((T, N), out_dtype),
        compiler_params=_params(dimension_semantics=("arbitrary",)),
    )(a, w)


def matmul_nt(name, pairs, out_dtype=F32):
    T = pairs[0][0].shape[0]
    K = pairs[0][1].shape[0]
    wide = sum(a.shape[1] for a, _ in pairs) > MM_WIDE
    tm = min(MM_ROWS_WIDE if wide else MM_ROWS_NARROW, T)
    n = len(pairs)

    def body(*refs):
        o_ref = refs[-1]
        acc = None
        for i in range(n):
            p = mm_nt(refs[2 * i][...], refs[2 * i + 1][...])
            acc = p if acc is None else acc + p
        o_ref[...] = acc.astype(o_ref.dtype)

    in_specs, args = [], []
    for a, w in pairs:
        in_specs += [_row_spec(tm, a.shape[1]), _const_spec(w.shape)]
        args += [a, w]
    return pl.pallas_call(
        body, name=name, grid=(T // tm,), in_specs=in_specs, out_specs=_row_spec(tm, K),
        out_shape=jax.ShapeDtypeStruct((T, K), out_dtype),
        compiler_params=_params(dimension_semantics=("arbitrary",)),
    )(*args)


def matmul_tn(name, a, b, tm=512, tn=2048):
    T, K = a.shape
    N = b.shape[1]
    tm = min(tm, T)
    tn = max(t for t in range(128, min(tn, N) + 1, 128) if N % t == 0)

    def body(a_ref, b_ref, o_ref):
        @pl.when(pl.program_id(1) == 0)
        def _():
            o_ref[...] = jnp.zeros(o_ref.shape, F32)
        o_ref[...] += mm_tn(a_ref[...], b_ref[...])

    return pl.pallas_call(
        body, name=name, grid=(N // tn, T // tm),
        in_specs=[pl.BlockSpec((tm, K), lambda j, i: (i, 0)), pl.BlockSpec((tm, tn), lambda j, i: (i, j))],
        out_specs=pl.BlockSpec((K, tn), lambda j, i: (0, j)),
        out_shape=jax.ShapeDtypeStruct((K, N), F32),
        compiler_params=_params(dimension_semantics=("arbitrary", "arbitrary")),
    )(a, b)


def _hg_log_f(lb, fp):
    return jnp.log(lb + (1.0 - lb) * sigmoid(fp))


def _hg_safe(lb, fp):
    lf = _hg_log_f(lb, fp)
    half = HG_CH // 2
    worst = [jnp.max(-jnp.sum(lf[:, c * half:(c + 1) * half], axis=1)) for c in range(fp.shape[1] // half)]
    return functools.reduce(jnp.maximum, worst) <= HG_SAFE


def hg_heads(fast, tri, trit, lb, og, st, qp, fp, iv, z):
    blk = qp.shape[1]
    q = silu(qp)
    lf = _hg_log_f(lb, fp)
    k = (1.0 - lb) * sigmoid(-fp)
    r = _iota((1, HG_SUB, 1), 1)
    rc = _iota((1, HG_CH, 1), 1)
    outs = []
    for c in range(blk // HG_CH):
        qc, kc, vc, lfc = (rows(a, c * HG_CH, HG_CH) for a in (q, k, iv, lf))
        b = blin01(tri, trit, lfc)
        qd = qc * jnp.exp(b)
        o_state = bmm_nt(qd, st)
        if fast:
            mid = rowsel(b, HG_CH // 2 - 1)
            scores = jnp.where(tri > 0.0, _bdot(qc * jnp.exp(b - mid), kc * jnp.exp(mid - b), 2, 2, lax.Precision.HIGH), 0.0)
            outs.append(o_state + bmm(scores, vc))
            bl = rowsel(b, HG_CH - 1)
            st = st * jnp.exp(bl) + bmm_tn(vc, kc * jnp.exp(bl - b))
            continue
        parts = []
        for i in range(HG_CH // HG_SUB):
            qi, ki, vi, bi = (rows(a, i * HG_SUB, HG_SUB) for a in (qc, kc, vc, b))
            o = rows(o_state, i * HG_SUB, HG_SUB)
            for s in range(HG_SUB):
                ks, bs, vs = rowsel(ki, s), rowsel(bi, s), rowsel(vi, s)
                e = jnp.exp(jnp.where(r >= s, bi - bs, NEG_INF))
                o = o + jnp.sum(qi * ks * e, axis=-1, keepdims=True) * vs
            if i > 0:
                rb = rowsel(b, i * HG_SUB - 1)
                kt = kc * jnp.exp(jnp.where(rc < i * HG_SUB, rb - b, NEG_INF))
                o = o + bmm(bmm_nt(qi * jnp.exp(bi - rb), kt), vc)
            parts.append(o)
        bl = rowsel(b, HG_CH - 1)
        st = st * jnp.exp(bl) + bmm_tn(vc, kc * jnp.exp(bl - b))
        outs.append(jnp.concatenate(parts, axis=1))
    o = jnp.concatenate(outs, axis=1)
    on = o * lax.rsqrt(jnp.mean(o * o, axis=-1, keepdims=True) + EPS) * og
    return st, on * silu(z)


def _tri(n, nh):
    t = (_iota((nh, n, n), 1) >= _iota((nh, n, n), 2)).astype(F32)
    tt = (_iota((nh, n, n), 1) <= _iota((nh, n, n), 2)).astype(F32)
    return t, tt


def _hg_load(p_ref, lb_ref, heads):
    col = lambda ref, base: jnp.stack([ref[:, pl.ds(base + h * HG_DH, HG_DH)] for h in heads])
    return col(lb_ref, 0), col(p_ref, 0), col(p_ref, D), col(p_ref, 2 * D), col(p_ref, 3 * D)


def hg_fwd(name, p, lb, og, carry=()):
    T = p.shape[0]
    nb = T // HG_BLK

    def body(p_ref, lb_ref, og_ref, u_ref, st_ref, s_scr):
        @pl.when(pl.program_id(0) == 0)
        def _():
            s_scr[...] = jnp.zeros(s_scr.shape, F32)
        st_ref[0] = s_scr[...]
        tri, trit = _tri(HG_CH, HG_HEADS)
        og_v = og_ref[...]
        heads = list(range(HG_HEADS))
        lbh, qp, fp, iv, z = _hg_load(p_ref, lb_ref, heads)

        def run(fast):
            def go():
                st, u = hg_heads(fast, tri, trit, lbh, og_v, s_scr[...], qp, fp, iv, z)
                s_scr[...] = st
                for h in heads:
                    u_ref[:, pl.ds(h * HG_DH, HG_DH)] = u[h].astype(BF16)
            return go

        lax.cond(_hg_safe(lbh, fp), run(True), run(False))

    (u, st), ex = _scan_call(
        body, name, nb,
        [_row_spec(HG_BLK, 4 * D), _const_spec((1, D)), _const_spec((1, HG_DH))],
        [_row_spec(HG_BLK, D), pl.BlockSpec((1, HG_HEADS, HG_DH, HG_DH), lambda i: (i, 0, 0, 0))],
        [jax.ShapeDtypeStruct((T, D), BF16), jax.ShapeDtypeStruct((nb, HG_HEADS, HG_DH, HG_DH), F32)],
        [pltpu.VMEM((HG_HEADS, HG_DH, HG_DH), F32)], (p, lb, og), carry)
    return u, st, ex


def hg_bwd(name, p, lb, og, states, du, carry=()):
    T = p.shape[0]
    nb = T // HG_BLK

    def body(p_ref, lb_ref, og_ref, st_ref, du_ref, dp_ref, dlb_ref, dog_ref, ds_scr):
        @pl.when(pl.program_id(0) == 0)
        def _():
            ds_scr[...] = jnp.zeros(ds_scr.shape, F32)
            dlb_ref[...] = jnp.zeros(dlb_ref.shape, F32)
            dog_ref[...] = jnp.zeros(dog_ref.shape, F32)
        tri, trit = _tri(HG_CH, HG_HEADS)
        og_v = og_ref[...]
        heads = list(range(HG_HEADS))
        lbh, qp, fp, iv, z = _hg_load(p_ref, lb_ref, heads)
        du_h = jnp.stack([du_ref[:, pl.ds(h * HG_DH, HG_DH)].astype(F32) for h in heads])

        def run(fast):
            def go():
                _, f = jax.vjp(functools.partial(hg_heads, fast, tri, trit), lbh, og_v, st_ref[0], qp, fp, iv, z)
                dlb, dog, dst, dq, df, di, dz = f((ds_scr[...], du_h))
                ds_scr[...] = dst
                dog_ref[...] += dog
                for h in heads:
                    for base, g in ((0, dq), (D, df), (2 * D, di), (3 * D, dz)):
                        dp_ref[:, pl.ds(base + h * HG_DH, HG_DH)] = g[h].astype(BF16)
                    dlb_ref[:, pl.ds(h * HG_DH, HG_DH)] += dlb[h]
            return go

        lax.cond(_hg_safe(lbh, fp), run(True), run(False))

    rev = lambda i: (nb - 1 - i, 0)
    outs, ex = _scan_call(
        body, name, nb,
        [pl.BlockSpec((HG_BLK, 4 * D), rev), _const_spec((1, D)), _const_spec((1, HG_DH)),
         pl.BlockSpec((1, HG_HEADS, HG_DH, HG_DH), lambda i: (nb - 1 - i, 0, 0, 0)), pl.BlockSpec((HG_BLK, D), rev)],
        [pl.BlockSpec((HG_BLK, 4 * D), rev), _const_spec((1, D)), _const_spec((1, HG_DH))],
        [jax.ShapeDtypeStruct((T, 4 * D), BF16), jax.ShapeDtypeStruct((1, D), F32), jax.ShapeDtypeStruct((1, HG_DH), F32)],
        [pltpu.VMEM((HG_HEADS, HG_DH, HG_DH), F32)], (p, lb, og, states, du), carry)
    return (*outs, ex)


def swa_heads(qn, kn, sinks, kprev, vprev, qp4, kp, v, z4, cos4, sin4, cosk, sink_, hots, mask, p64, p64t):
    def norm_rope(xp, g, cs, sn):
        y = xp * lax.rsqrt(jnp.mean(xp * xp, axis=-1, keepdims=True) + EPS) * g
        return y * cs + blin01_r(y, p64, p64t) * sn

    q = norm_rope(qp4, qn, cos4, sin4)
    k = norm_rope(kp, kn, cosk, sink_)
    k2 = jnp.concatenate([kprev, k], axis=1)
    v2 = jnp.concatenate([vprev, v], axis=1)
    s = jnp.where(mask, bmm_nt(q, k2) * (SW_DH ** -0.5), NEG_INF)
    sink_col = jnp.concatenate([jnp.concatenate(
        [jnp.broadcast_to(jnp.sum(sinks * hot, axis=-1, keepdims=True), (SW_BLK, 1)) for hot in hot4], axis=0)[None]
        for hot4 in hots], axis=0)
    m = lax.stop_gradient(jnp.maximum(jnp.max(s, axis=-1, keepdims=True), sink_col))
    p = jnp.exp(s - m)
    p = p / (jnp.sum(p, axis=-1, keepdims=True) + jnp.exp(sink_col - m))
    o = bmm(p, v2)
    return k, v, o * silu(z4)


def _swa_consts(first):
    qi = _iota((SW_G * SW_BLK, 2 * SW_BLK), 0) % SW_BLK
    kj = _iota((SW_G * SW_BLK, 2 * SW_BLK), 1)
    rel = qi + SW_BLK - kj
    mask = (rel >= 0) & (rel < SW_BLK) & ((kj >= SW_BLK) | jnp.logical_not(first))
    i, j = _iota((SW_KV, SW_DH, SW_DH), 1), _iota((SW_KV, SW_DH, SW_DH), 2)
    p64 = (i == (j + SW_DH // 2) % SW_DH).astype(F32)
    p64t = (j == (i + SW_DH // 2) % SW_DH).astype(F32)
    lane = _iota((1, SW_KV * SW_G), 1)
    hots = [[(lane == h * SW_G + g).astype(F32) for g in range(SW_G)] for h in range(SW_KV)]
    return mask, p64, p64t, hots


def _stack_q(ref, h, base):
    return jnp.concatenate([ref[:, pl.ds(base + h * SW_G * SW_DH + g * SW_DH, SW_DH)] for g in range(SW_G)], axis=0)


def _swa_load(p_ref):
    heads = range(SW_KV)
    return (jnp.stack([_stack_q(p_ref, h, Q0) for h in heads]),
            jnp.stack([p_ref[:, pl.ds(K0 + h * SW_DH, SW_DH)] for h in heads]),
            jnp.stack([p_ref[:, pl.ds(V0 + h * SW_DH, SW_DH)] for h in heads]),
            jnp.stack([_stack_q(p_ref, h, Z0) for h in heads]))


Q0, K0, V0, Z0 = 0, SW_QW, SW_QW + SW_KVW, SW_QW + 2 * SW_KVW
SW_IN = 2 * SW_QW + 2 * SW_KVW


def swa_fwd(name, p, cos, sin, qn, kn, sinks):
    T = p.shape[0]
    nb = T // SW_BLK

    def body(p_ref, cos_ref, sin_ref, qn_ref, kn_ref, sk_ref, u_ref, ko_ref, kprev, vprev):
        first = pl.program_id(0) == 0

        @pl.when(first)
        def _():
            kprev[...] = jnp.zeros(kprev.shape, F32)
            vprev[...] = jnp.zeros(vprev.shape, F32)
        mask, p64, p64t, hots = _swa_consts(first)
        cs, sn = cos_ref[...], sin_ref[...]
        cs4, sn4 = jnp.concatenate([cs] * SW_G, axis=0), jnp.concatenate([sn] * SW_G, axis=0)
        qp4, kp, vv, z4 = _swa_load(p_ref)
        k, v, u4 = swa_heads(qn_ref[...], kn_ref[...], sk_ref[...], kprev[...], vprev[...], qp4, kp, vv, z4,
                             cs4, sn4, cs, sn, hots, mask, p64, p64t)
        kprev[...] = k
        vprev[...] = v
        for h in range(SW_KV):
            ko_ref[:, pl.ds(h * SW_DH, SW_DH)] = k[h]
            uh = u4[h]
            for g in range(SW_G):
                u_ref[:, pl.ds(h * SW_G * SW_DH + g * SW_DH, SW_DH)] = uh[g * SW_BLK:(g + 1) * SW_BLK].astype(BF16)

    return pl.pallas_call(
        body, name=name, grid=(nb,),
        in_specs=[_row_spec(SW_BLK, SW_IN), _row_spec(SW_BLK, SW_DH), _row_spec(SW_BLK, SW_DH),
                  _const_spec((1, SW_DH)), _const_spec((1, SW_DH)), _const_spec((1, SW_KV * SW_G))],
        out_specs=[_row_spec(SW_BLK, SW_QW), _row_spec(SW_BLK, SW_KVW)],
        out_shape=[jax.ShapeDtypeStruct((T, SW_QW), BF16), jax.ShapeDtypeStruct((T, SW_KVW), F32)],
        scratch_shapes=[pltpu.VMEM((SW_KV, SW_BLK, SW_DH), F32), pltpu.VMEM((SW_KV, SW_BLK, SW_DH), F32)],
        compiler_params=_params(dimension_semantics=("arbitrary",)),
    )(p, cos, sin, qn, kn, sinks)


def swa_bwd(name, p, kpost, cos, sin, qn, kn, sinks, du, carry=()):
    T = p.shape[0]
    nb = T // SW_BLK

    def body(p_ref, pprev_ref, kprev_ref, cos_ref, sin_ref, qn_ref, kn_ref, sk_ref, du_ref,
             dp_ref, dqn_ref, dkn_ref, dsk_ref, dk_scr, dv_scr):
        i = pl.program_id(0)
        first = i == nb - 1

        @pl.when(i == 0)
        def _():
            dk_scr[...] = jnp.zeros(dk_scr.shape, F32)
            dv_scr[...] = jnp.zeros(dv_scr.shape, F32)
            dqn_ref[...] = jnp.zeros(dqn_ref.shape, F32)
            dkn_ref[...] = jnp.zeros(dkn_ref.shape, F32)
            dsk_ref[...] = jnp.zeros(dsk_ref.shape, F32)
        mask, p64, p64t, hots = _swa_consts(first)
        cs, sn = cos_ref[...], sin_ref[...]
        cs4, sn4 = jnp.concatenate([cs] * SW_G, axis=0), jnp.concatenate([sn] * SW_G, axis=0)
        heads = range(SW_KV)
        fn = lambda qn, kn, sk, kpv, vpv, qp4, kp, v, z4: swa_heads(
            qn, kn, sk, kpv, vpv, qp4, kp, v, z4, cs4, sn4, cs, sn, hots, mask, p64, p64t)
        kpv = jnp.stack([kprev_ref[:, pl.ds(h * SW_DH, SW_DH)] for h in heads])
        vpv = jnp.stack([pprev_ref[:, pl.ds(V0 + h * SW_DH, SW_DH)] for h in heads])
        _, f = jax.vjp(fn, qn_ref[...], kn_ref[...], sk_ref[...], kpv, vpv, *_swa_load(p_ref))
        du4 = jnp.stack([_stack_q(du_ref, h, 0).astype(F32) for h in heads])
        dqn, dkn, dsk, dkpv, dvpv, dq4, dkp, dv, dz4 = f((dk_scr[...], dv_scr[...], du4))
        dk_scr[...] = dkpv
        dv_scr[...] = dvpv
        for h in heads:
            dp_ref[:, pl.ds(K0 + h * SW_DH, SW_DH)] = dkp[h].astype(BF16)
            dp_ref[:, pl.ds(V0 + h * SW_DH, SW_DH)] = dv[h].astype(BF16)
            dqh, dzh = dq4[h], dz4[h]
            for g in range(SW_G):
                c0 = h * SW_G * SW_DH + g * SW_DH
                dp_ref[:, pl.ds(Q0 + c0, SW_DH)] = dqh[g * SW_BLK:(g + 1) * SW_BLK].astype(BF16)
                dp_ref[:, pl.ds(Z0 + c0, SW_DH)] = dzh[g * SW_BLK:(g + 1) * SW_BLK].astype(BF16)
        dqn_ref[...] += dqn
        dkn_ref[...] += dkn
        dsk_ref[...] += dsk

    rev = lambda i: (nb - 1 - i, 0)
    prev = lambda i: (jnp.maximum(nb - 2 - i, 0), 0)
    outs, ex = _scan_call(
        body, name, nb,
        [pl.BlockSpec((SW_BLK, SW_IN), rev), pl.BlockSpec((SW_BLK, SW_IN), prev),
         pl.BlockSpec((SW_BLK, SW_KVW), prev), pl.BlockSpec((SW_BLK, SW_DH), rev), pl.BlockSpec((SW_BLK, SW_DH), rev),
         _const_spec((1, SW_DH)), _const_spec((1, SW_DH)), _const_spec((1, SW_KV * SW_G)), pl.BlockSpec((SW_BLK, SW_QW), rev)],
        [pl.BlockSpec((SW_BLK, SW_IN), rev), _const_spec((1, SW_DH)), _const_spec((1, SW_DH)), _const_spec((1, SW_KV * SW_G))],
        [jax.ShapeDtypeStruct((T, SW_IN), BF16), jax.ShapeDtypeStruct((1, SW_DH), F32),
         jax.ShapeDtypeStruct((1, SW_DH), F32), jax.ShapeDtypeStruct((1, SW_KV * SW_G), F32)],
        [pltpu.VMEM((SW_KV, SW_BLK, SW_DH), F32), pltpu.VMEM((SW_KV, SW_BLK, SW_DH), F32)],
        (p, p, kpost, cos, sin, qn, kn, sinks, du), carry)
    return (*outs, ex)


CONV_TC, CONV_TL, HALO = 512, 1024, 8


def _conv_taps(xe, w_ref):
    acc = w_ref[pl.ds(CONV_K - 1, 1), :] * xe
    for k in range(CONV_K - 1):
        acc = acc + w_ref[pl.ds(k, 1), :] * pltpu.roll(xe, CONV_K - 1 - k, 0)
    return acc


MMCONV_ROWS = 128


def matmul_conv(name, a, w, w8):
    T, K = a.shape
    C = w.shape[1]
    tm = min(MMCONV_ROWS, T)

    def body(a_ref, w_ref, w8_ref, x_ref, c_ref, tail):
        @pl.when(pl.program_id(0) == 0)
        def _():
            tail[...] = jnp.zeros(tail.shape, F32)
        x = mm(a_ref[...], w_ref[...])
        x_ref[...] = x
        xe = jnp.concatenate([tail[...], x], axis=0)
        c_ref[...] = silu(_conv_taps(xe, w8_ref)[HALO:])
        tail[...] = x[tm - HALO:]

    return pl.pallas_call(
        body, name=name, grid=(T // tm,),
        in_specs=[_row_spec(tm, K), _const_spec((K, C)), _const_spec((HALO, C))],
        out_specs=[_row_spec(tm, C), _row_spec(tm, C)],
        out_shape=[jax.ShapeDtypeStruct((T, C), F32), jax.ShapeDtypeStruct((T, C), F32)],
        scratch_shapes=[pltpu.VMEM((HALO, C), F32)],
        compiler_params=_params(dimension_semantics=("arbitrary",)),
    )(a, w, w8)


def conv_bwd(name, x, dy, w8):
    T, C = x.shape
    tc = min(CONV_TC, T)
    nt = T // tc
    n = tc + 2 * HALO

    def body(xp_ref, x_ref, xn_ref, dy_ref, dyn_ref, w_ref, dx_ref, dw_ref):
        i = pl.program_id(1)

        @pl.when(i == 0)
        def _():
            dw_ref[...] = jnp.zeros(dw_ref.shape, F32)
        prev = jnp.where(i == 0, 0.0, xp_ref[...])
        xe = jnp.concatenate([prev, x_ref[...], xn_ref[...]], axis=0)
        dye = jnp.concatenate([jnp.zeros((HALO, CONV_TL), F32), dy_ref[...], jnp.where(i == nt - 1, 0.0, dyn_ref[...])], axis=0)
        ce = _conv_taps(xe, w_ref)
        sg = sigmoid(ce)
        dce = dye * (sg * (1.0 + ce * (1.0 - sg)))
        dx = w_ref[pl.ds(CONV_K - 1, 1), :] * dce
        for k in range(CONV_K - 1):
            dx = dx + w_ref[pl.ds(k, 1), :] * pltpu.roll(dce, n - (CONV_K - 1 - k), 0)
        dx_ref[...] = dx[HALO:HALO + tc].astype(BF16)
        dcur = dce[HALO:HALO + tc]
        for k in range(CONV_K):
            xs = xe if k == CONV_K - 1 else pltpu.roll(xe, CONV_K - 1 - k, 0)
            dw_ref[pl.ds(k, 1), :] += jnp.sum(dcur * xs[HALO:HALO + tc], axis=0, keepdims=True)

    nh = T // HALO
    prev_map = lambda j, i: (jnp.maximum(i * (tc // HALO) - 1, 0), j)
    next_map = lambda j, i: (jnp.minimum((i + 1) * (tc // HALO), nh - 1), j)
    cur_map = lambda j, i: (i, j)
    return pl.pallas_call(
        body, name=name, grid=(C // CONV_TL, nt),
        in_specs=[pl.BlockSpec((HALO, CONV_TL), prev_map), pl.BlockSpec((tc, CONV_TL), cur_map),
                  pl.BlockSpec((HALO, CONV_TL), next_map), pl.BlockSpec((tc, CONV_TL), cur_map),
                  pl.BlockSpec((HALO, CONV_TL), next_map), pl.BlockSpec((HALO, CONV_TL), lambda j, i: (0, j))],
        out_specs=[pl.BlockSpec((tc, CONV_TL), cur_map), pl.BlockSpec((HALO, CONV_TL), lambda j, i: (0, j))],
        out_shape=[jax.ShapeDtypeStruct((T, C), BF16), jax.ShapeDtypeStruct((HALO, C), F32)],
        compiler_params=_params(dimension_semantics=("arbitrary", "arbitrary")),
    )(x, x, x, dy, dy, w8)


GD_VH = 16
GD_HB = 16


def _unit_lower_inverse(a, eye_f):
    nb = lambda i, w: sum((i >= w * j).astype(jnp.int32) for j in range(1, GD_CHUNK // w))
    r, c = _iota((GD_CHUNK, GD_CHUNK), 0), _iota((GD_CHUNK, GD_CHUNK), 1)
    same16, same32 = nb(r, 16) == nb(c, 16), nb(r, 32) == nb(c, 32)
    n = -jnp.where(same16, a, 0.0)
    t = eye_f + n
    p = n
    for _ in range(3):
        p = bmmf(p, p)
        t = t + bmmf(t, p)
    for lower in (same32 & jnp.logical_not(same16), jnp.logical_not(same32)):
        t = t - bmmf(bmmf(t, jnp.where(lower, a, 0.0)), t)
    return t


@jax.custom_vjp
def unit_lower_solve2(a, r1, r2, eye_f):
    t = _unit_lower_inverse(a, eye_f)
    return bmmf(t, r1), bmmf(t, r2)


def _uls2_fwd(a, r1, r2, eye_f):
    t = _unit_lower_inverse(a, eye_f)
    x1, x2 = bmmf(t, r1), bmmf(t, r2)
    return (x1, x2), (t, x1, x2, eye_f)


def _uls2_bwd(res, g):
    t, x1, x2, eye_f = res
    high = lax.Precision.HIGH
    d1 = _bdot(t, g[0], 1, 1, high)
    d2 = _bdot(t, g[1], 1, 1, high)
    da = -(_bdot(d1, x1, 2, 2, high) + _bdot(d2, x2, 2, 2, high))
    return da, d1, d2, jnp.zeros_like(eye_f)


unit_lower_solve2.defvjp(_uls2_fwd, _uls2_bwd)


def gd_heads(al, dt, og, s, qc, kc, v, z, a_col, b_col, cst):
    lt_t, eye, incl, strict = cst
    eye_f = eye.astype(F32)
    nh = s.shape[0]
    q = qc * lax.rsqrt(jnp.sum(qc * qc, axis=-1, keepdims=True) + EPS) * (GD_DK ** -0.5)
    k = kc * lax.rsqrt(jnp.sum(kc * kc, axis=-1, keepdims=True) + EPS)
    beta = sigmoid(b_col)
    g = -jnp.exp(al) * softplus(a_col + dt)
    d_mat = jnp.broadcast_to(jnp.sum(g * lt_t, axis=1, keepdims=True), (nh, GD_CHUNK, GD_CHUNK))
    d = jnp.sum(jnp.where(eye, d_mat, 0.0), axis=-1, keepdims=True)
    dec = jnp.exp(jnp.where(incl, d - d_mat, NEG_INF))
    kb = k * beta
    u, w = unit_lower_solve2(bmm_nt(kb, k) * dec * strict, v * beta, kb * jnp.exp(d), eye_f)
    v_new = u - bmm(w, s)
    o = bmm(q * jnp.exp(d), s) + bmm(bmm_nt(q, k) * dec, v_new)
    dl = jnp.sum(g, axis=1, keepdims=True)
    s_new = s * jnp.exp(dl) + bmm_tn(k * jnp.exp(dl - d), v_new)
    on = o * lax.rsqrt(jnp.mean(o * o, axis=-1, keepdims=True) + EPS) * og
    return s_new, on * silu(z)


def _gd_consts():
    r, c = _iota((GD_CHUNK, GD_CHUNK), 0), _iota((GD_CHUNK, GD_CHUNK), 1)
    return (r <= c).astype(F32), r == c, r >= c, (r > c).astype(F32)


def _gd_load(qkv_ref, z_ref, ab_v, al_v, dt_v, heads):
    lane = _iota((1, 128), 1)
    hot_a = [(lane == h).astype(F32) for h in heads]
    hot_b = [(lane == GD_VH + h).astype(F32) for h in heads]
    col = lambda src, hot: jnp.stack([jnp.sum(src * m, axis=-1, keepdims=True) for m in hot])
    ops = (col(al_v, hot_a), col(dt_v, hot_a),
           jnp.stack([qkv_ref[:, pl.ds((h // 2) * GD_DK, GD_DK)] for h in heads]),
           jnp.stack([qkv_ref[:, pl.ds(GD_QKW + (h // 2) * GD_DK, GD_DK)] for h in heads]),
           jnp.stack([qkv_ref[:, pl.ds(2 * GD_QKW + h * GD_DK, GD_DK)] for h in heads]),
           jnp.stack([z_ref[:, pl.ds(h * GD_DK, GD_DK)] for h in heads]),
           col(ab_v, hot_a), col(ab_v, hot_b))
    return ops, hot_a, hot_b


def gd_fwd(name, qkv, z, ab, al, dt, og):
    T = qkv.shape[0]
    nc = T // GD_CHUNK

    def body(qkv_ref, z_ref, ab_ref, al_ref, dt_ref, og_ref, u_ref, st_ref, s_scr):
        @pl.when(pl.program_id(0) == 0)
        def _():
            s_scr[...] = jnp.zeros(s_scr.shape, F32)
        st_ref[0] = s_scr[...]
        cst = _gd_consts()
        ab_v, al_v, dt_v, og_v = ab_ref[...], al_ref[...], dt_ref[...], og_ref[...]
        for b0 in range(0, GD_VH, GD_HB):
            heads = list(range(b0, b0 + GD_HB))
            (alh, dth, q, k, v, zz, a_col, b_col), _, _ = _gd_load(qkv_ref, z_ref, ab_v, al_v, dt_v, heads)
            s_new, u = gd_heads(alh, dth, og_v, s_scr[pl.ds(b0, GD_HB)], q, k, v, zz, a_col, b_col, cst)
            s_scr[pl.ds(b0, GD_HB)] = s_new
            for i, h in enumerate(heads):
                u_ref[:, pl.ds(h * GD_DK, GD_DK)] = u[i].astype(BF16)

    return pl.pallas_call(
        body, name=name, grid=(nc,),
        in_specs=[_row_spec(GD_CHUNK, GD_QKV), _row_spec(GD_CHUNK, GD_VW), _row_spec(GD_CHUNK, 128),
                  _const_spec((1, 128)), _const_spec((1, 128)), _const_spec((1, 128))],
        out_specs=[_row_spec(GD_CHUNK, GD_VW), pl.BlockSpec((1, GD_VH, GD_DK, GD_DK), lambda i: (i, 0, 0, 0))],
        out_shape=[jax.ShapeDtypeStruct((T, GD_VW), BF16), jax.ShapeDtypeStruct((nc, GD_VH, GD_DK, GD_DK), F32)],
        scratch_shapes=[pltpu.VMEM((GD_VH, GD_DK, GD_DK), F32)],
        compiler_params=_params(dimension_semantics=("arbitrary",)),
    )(qkv, z, ab, al, dt, og)


def gd_bwd(name, qkv, z, ab, al, dt, og, states, du, carry=()):
    T = qkv.shape[0]
    nc = T // GD_CHUNK

    def body(qkv_ref, z_ref, ab_ref, al_ref, dt_ref, og_ref, st_ref, du_ref,
             dqkv_ref, dz_ref, dab_ref, dal_ref, ddt_ref, dog_ref, ds_scr):
        @pl.when(pl.program_id(0) == 0)
        def _():
            ds_scr[...] = jnp.zeros(ds_scr.shape, F32)
            dal_ref[...] = jnp.zeros(dal_ref.shape, F32)
            ddt_ref[...] = jnp.zeros(ddt_ref.shape, F32)
            dog_ref[...] = jnp.zeros(dog_ref.shape, F32)
        cst = _gd_consts()
        ab_v, al_v, dt_v, og_v = ab_ref[...], al_ref[...], dt_ref[...], og_ref[...]
        dab = jnp.zeros((GD_CHUNK, 128), F32)
        dal_row = jnp.zeros((1, 128), F32)
        ddt_row = jnp.zeros((1, 128), F32)
        dog_row = jnp.zeros((1, 128), F32)
        for b0 in range(0, GD_VH, GD_HB):
            heads = list(range(b0, b0 + GD_HB))
            (alh, dth, q, k, v, zz, a_col, b_col), hot_a, hot_b = _gd_load(qkv_ref, z_ref, ab_v, al_v, dt_v, heads)
            fn = lambda *a: gd_heads(*a, cst)
            _, f = jax.vjp(fn, alh, dth, og_v, st_ref[0, pl.ds(b0, GD_HB)], q, k, v, zz, a_col, b_col)
            du_h = jnp.stack([du_ref[:, pl.ds(h * GD_DK, GD_DK)].astype(F32) for h in heads])
            dalh, ddth, dog, ds, dq, dk, dv, dz, da_col, db_col = f((ds_scr[pl.ds(b0, GD_HB)], du_h))
            ds_scr[pl.ds(b0, GD_HB)] = ds
            dog_row = dog_row + dog
            for i, h in enumerate(heads):
                dqkv_ref[:, pl.ds(2 * GD_QKW + h * GD_DK, GD_DK)] = dv[i]
                dz_ref[:, pl.ds(h * GD_DK, GD_DK)] = dz[i].astype(BF16)
                dab = dab + da_col[i] * hot_a[i] + db_col[i] * hot_b[i]
                dal_row = dal_row + dalh[i] * hot_a[i]
                ddt_row = ddt_row + ddth[i] * hot_a[i]
                if h % 2 == 0:
                    dqkv_ref[:, pl.ds((h // 2) * GD_DK, GD_DK)] = dq[i] + dq[i + 1]
                    dqkv_ref[:, pl.ds(GD_QKW + (h // 2) * GD_DK, GD_DK)] = dk[i] + dk[i + 1]
        dab_ref[...] = dab
        dal_ref[...] += dal_row
        ddt_ref[...] += ddt_row
        dog_ref[...] += dog_row

    rev = lambda i: (nc - 1 - i, 0)
    outs, ex = _scan_call(
        body, name, nc,
        [pl.BlockSpec((GD_CHUNK, GD_QKV), rev), pl.BlockSpec((GD_CHUNK, GD_VW), rev), pl.BlockSpec((GD_CHUNK, 128), rev),
         _const_spec((1, 128)), _const_spec((1, 128)), _const_spec((1, 128)),
         pl.BlockSpec((1, GD_VH, GD_DK, GD_DK), lambda i: (nc - 1 - i, 0, 0, 0)), pl.BlockSpec((GD_CHUNK, GD_VW), rev)],
        [pl.BlockSpec((GD_CHUNK, GD_QKV), rev), pl.BlockSpec((GD_CHUNK, GD_VW), rev), pl.BlockSpec((GD_CHUNK, 128), rev),
         _const_spec((1, 128)), _const_spec((1, 128)), _const_spec((1, 128))],
        [jax.ShapeDtypeStruct((T, GD_QKV), F32), jax.ShapeDtypeStruct((T, GD_VW), BF16),
         jax.ShapeDtypeStruct((T, 128), F32)] + [jax.ShapeDtypeStruct((1, 128), F32)] * 3,
        [pltpu.VMEM((GD_VH, GD_DK, GD_DK), F32)], (qkv, z, ab, al, dt, og, states, du), carry)
    return (*outs, ex)


def _my_pos():
    return lax.axis_index("x"), lax.axis_index("y"), lax.axis_index("c")


def _peers(pos):
    out = []
    for k in range(1, N_DEV):
        dev = tuple(1 - p if (k >> s) & 1 else p for p, s in zip(pos, (2, 1, 0)))
        out.append((dev, 4 * dev[0] + 2 * dev[1] + dev[2]))
    return out


_ANY = pl.BlockSpec(memory_space=pl.ANY)
_COMM_SCRATCH = [pltpu.SemaphoreType.DMA((N_DEV - 1,)), pltpu.SemaphoreType.DMA((N_DEV - 1,)), pltpu.SemaphoreType.DMA(())]
GATHER, SCATTER = "gather", "scatter"


def _exchange_descs(pos, kinds, x_refs, o_refs, send_sems, recv_sems, local_sems, with_recvs=True):
    me = 4 * pos[0] + 2 * pos[1] + pos[2]
    peers = _peers(pos)
    local, sends, recvs = [], [], []
    for i, (kind, x, o) in enumerate(zip(kinds, x_refs, o_refs)):
        local.append(pltpu.make_async_copy(x if kind == GATHER else x.at[me], o.at[me], local_sems.at[i]))
        for k, (dev, pid) in enumerate(peers):
            s = (N_DEV - 1) * i + k
            src = x if kind == GATHER else x.at[pid]
            sends.append(pltpu.make_async_remote_copy(src, o.at[me], send_sems.at[s], recv_sems.at[s],
                                                      device_id=dev, device_id_type=MESH))
            if with_recvs:
                recvs.append(pltpu.make_async_remote_copy(src, o.at[pid], send_sems.at[s], recv_sems.at[s],
                                                          device_id=dev, device_id_type=MESH))
    return local, sends, recvs


def _scan_call(body, name, nsteps, in_specs, out_specs, out_shape, scratch_shapes, args, carry=()):
    n_in, n_out, n_scr, n_ex = len(in_specs), len(out_specs), len(scratch_shapes), len(carry)
    params = _params(dimension_semantics=("arbitrary",))
    if not carry:
        outs = pl.pallas_call(body, name=name, grid=(nsteps,), in_specs=in_specs, out_specs=out_specs, out_shape=out_shape,
                              scratch_shapes=scratch_shapes, compiler_params=params)(*args)
        return outs, []
    kinds = [k for k, _ in carry]
    xs = [a for _, a in carry]
    ex_shape = [jax.ShapeDtypeStruct(((N_DEV,) + a.shape) if k == GATHER else a.shape, a.dtype) for k, a in carry]

    def full(*refs):
        ins, x_refs = refs[:n_in], refs[n_in:n_in + n_ex]
        outs = refs[n_in + n_ex:n_in + n_ex + n_out]
        o_refs = refs[n_in + n_ex + n_out:n_in + 2 * n_ex + n_out]
        scr = refs[n_in + 2 * n_ex + n_out:n_in + 2 * n_ex + n_out + n_scr]
        sems = refs[n_in + 2 * n_ex + n_out + n_scr:]
        pos = _my_pos()
        step = pl.program_id(0)

        @pl.when(step == 0)
        def _():
            local, sends, _ = _exchange_descs(pos, kinds, x_refs, o_refs, *sems, with_recvs=False)
            for cp in local + sends:
                cp.start()

        body(*ins, *outs, *scr)

        @pl.when(step == nsteps - 1)
        def _():
            local, sends, recvs = _exchange_descs(pos, kinds, x_refs, o_refs, *sems)
            for cp in recvs:
                cp.wait_recv()
            for cp in sends:
                cp.wait_send()
            for cp in local:
                cp.wait()

    n_sem = (N_DEV - 1) * n_ex
    outs = pl.pallas_call(
        full, name=name, grid=(nsteps,), in_specs=list(in_specs) + [_ANY] * n_ex, out_specs=list(out_specs) + [_ANY] * n_ex,
        out_shape=list(out_shape) + ex_shape,
        scratch_shapes=list(scratch_shapes) + [pltpu.SemaphoreType.DMA((n_sem,)), pltpu.SemaphoreType.DMA((n_sem,)),
                                               pltpu.SemaphoreType.DMA((n_ex,))],
        compiler_params=params)(*args, *xs)
    return outs[:n_out], list(outs[n_out:])


def all_gather(name, x):
    def body(x_ref, o_ref, send_sems, recv_sems, local_sem):
        pos = _my_pos()
        me = 4 * pos[0] + 2 * pos[1] + pos[2]
        peers = _peers(pos)
        mine = pltpu.make_async_copy(x_ref, o_ref.at[me], local_sem)
        mine.start()
        sends = [pltpu.make_async_remote_copy(x_ref, o_ref.at[me], send_sems.at[k], recv_sems.at[k],
                                              device_id=dev, device_id_type=MESH) for k, (dev, _) in enumerate(peers)]
        for cp in sends:
            cp.start()
        for k, (dev, pid) in enumerate(peers):
            pltpu.make_async_remote_copy(x_ref, o_ref.at[pid], send_sems.at[k], recv_sems.at[k],
                                         device_id=dev, device_id_type=MESH).wait_recv()
        for cp in sends:
            cp.wait_send()
        mine.wait()

    return pl.pallas_call(
        body, name=name, out_shape=jax.ShapeDtypeStruct((N_DEV,) + x.shape, x.dtype),
        in_specs=[_ANY], out_specs=_ANY, scratch_shapes=_COMM_SCRATCH,
    )(x)


def lb_rows(r0, r1, r2, r3):
    mx = jnp.maximum(jnp.maximum(r0, r1), jnp.maximum(r2, r3))
    e = [jnp.exp(r - mx) for r in (r0, r1, r2, r3)]
    inv = 1.0 / (e[0] + e[1] + e[2] + e[3])
    c0 = e[0] * inv
    c1 = c0 + e[1] * inv
    c2 = c1 + e[2] * inv
    c3 = c2 + e[3] * inv
    return c0 - c0, c1 - c0, c2 - c0, c3 - c0


def mod_partial(name, c_all, ada_w):
    n, _, w = ada_w.shape

    def body(c_ref, w_ref, o_ref):
        o_ref[0] = mm(c_ref[...], w_ref[0])

    return pl.pallas_call(
        body, name=name, grid=(n,),
        in_specs=[_const_spec((N_DEV, D)), pl.BlockSpec((1, D, w), lambda i: (i, 0, 0))],
        out_specs=pl.BlockSpec((1, N_DEV, w), lambda i: (i, 0, 0)),
        out_shape=jax.ShapeDtypeStruct((n, N_DEV, w), F32),
    )(c_all, ada_w)


def prep(name, modp, ada_b, hgrn_lb):
    def body(mp_ref, b_ref, lb_ref, mod_ref, lbo_ref):
        mod_ref[...] = mp_ref[...] + b_ref[...]
        out = lb_rows(*[lb_ref[pl.ds(i, 1), :] for i in range(4)])
        for i in range(4):
            lbo_ref[pl.ds(i, 1), :] = out[i]

    return pl.pallas_call(
        body, name=name,
        out_shape=[jax.ShapeDtypeStruct(modp.shape, F32), jax.ShapeDtypeStruct(hgrn_lb.shape, F32)],
    )(modp, ada_b, hgrn_lb)


def lb_grad(name, hgrn_lb, dlb_parts):
    def body(lb_ref, d_ref, o_ref):
        cts = []
        for i in range(4):
            acc = d_ref[0, pl.ds(i, 1), :]
            for p in range(1, N_DEV):
                acc = acc + d_ref[p, pl.ds(i, 1), :]
            cts.append(acc)
        _, f = jax.vjp(lb_rows, *[lb_ref[pl.ds(i, 1), :] for i in range(4)])
        for i, g in enumerate(f(tuple(cts))):
            o_ref[pl.ds(i, 1), :] = g

    return pl.pallas_call(body, name=name, out_shape=jax.ShapeDtypeStruct(hgrn_lb.shape, F32))(hgrn_lb, dlb_parts)


def ada_grad(name, c_t, dm):
    n, _, w = dm.shape

    def body(c_ref, d_ref, o_ref):
        acc = c_ref[:, pl.ds(0, 1)] * d_ref[0, pl.ds(0, 1), :]
        for b in range(1, N_DEV):
            acc = acc + c_ref[:, pl.ds(b, 1)] * d_ref[0, pl.ds(b, 1), :]
        o_ref[0] = acc

    return pl.pallas_call(
        body, name=name, grid=(n,),
        in_specs=[_const_spec((D, N_DEV)), pl.BlockSpec((1, N_DEV, w), lambda i: (i, 0, 0))],
        out_specs=pl.BlockSpec((1, D, w), lambda i: (i, 0, 0)),
        out_shape=jax.ShapeDtypeStruct((n, D, w), F32),
    )(c_t, dm)


ADAMW_ROWS = 256


def adamw(name, w, m, v, gparts):
    shape = w.shape
    L = shape[-1]
    R = 1
    for s in shape[:-1]:
        R *= s
    P = gparts.shape[0]
    tr = min(R, ADAMW_ROWS)

    def body(w_ref, m_ref, v_ref, g_ref, go_ref, d_ref, mo_ref, vo_ref):
        g = g_ref[0].astype(F32)
        for p in range(1, P):
            g = g + g_ref[p].astype(F32)
        mn = ADAM_B1 * m_ref[...] + (1.0 - ADAM_B1) * g
        vn = ADAM_B2 * v_ref[...] + (1.0 - ADAM_B2) * (g * g)
        m_hat = mn / (1.0 - ADAM_B1 ** ADAM_STEP)
        v_hat = vn / (1.0 - ADAM_B2 ** ADAM_STEP)
        go_ref[...] = g
        d_ref[...] = -ADAM_LR * (m_hat / (jnp.sqrt(v_hat) + ADAM_EPS) + ADAM_WD * w_ref[...])
        mo_ref[...] = mn
        vo_ref[...] = vn

    spec = pl.BlockSpec((tr, L), lambda i: (i, 0))
    outs = pl.pallas_call(
        body, name=name, grid=(R // tr,),
        in_specs=[spec, spec, spec, pl.BlockSpec((P, tr, L), lambda i: (0, i, 0))],
        out_specs=[spec] * 4, out_shape=[jax.ShapeDtypeStruct((R, L), F32)] * 4,
        compiler_params=_params(dimension_semantics=("arbitrary",)),
    )(w.reshape(R, L), m.reshape(R, L), v.reshape(R, L), gparts.reshape(P, R, L))
    return [o.reshape(shape) for o in outs]


def _cols_from(g):
    _, n, k, w8 = g.shape
    return jnp.transpose(g, (1, 2, 0, 3)).reshape(n, k, N_DEV * w8)


def _rows_from(g):
    _, n, k8, nn = g.shape
    return jnp.transpose(g, (1, 0, 2, 3)).reshape(n, N_DEV * k8, nn)


def _scatter_cols(dw, wire=BF16):
    n, k, nn = dw.shape
    return jnp.transpose(dw.reshape(n, k, N_DEV, nn // N_DEV), (2, 0, 1, 3)).astype(wire)


def _scatter_rows(dw, wire=BF16):
    n, k, nn = dw.shape
    return jnp.transpose(dw.reshape(n, N_DEV, k // N_DEV, nn), (1, 0, 2, 3)).astype(wire)


def _pad_lanes(v, width=128):
    return jnp.pad(v, ((0, 0), (0, width - v.shape[1])))


def _stack_rows(parts):
    n, L = len(parts), parts[0].shape[1]
    r = lax.broadcasted_iota(jnp.int32, (n, L), 0)
    out = jnp.zeros((n, L), parts[0].dtype)
    for i, p in enumerate(parts):
        out = jnp.where(r == i, jnp.broadcast_to(p, (n, L)), out)
    return out


def kernel(x, c, positions, hgrn_lb, ada_w, ada_b, norm_g, hg_in_w, hg_out_w, hg_onorm, sw_in_w, sw_out_w, sw_qnorm, sw_knorm, sw_sinks, gd_in_w, gd_out_w, gd_conv_w, gd_a_log, gd_dt_bias, gd_onorm, loss_target, m_hgrn_lb, m_ada_w, m_ada_b, m_norm_g, m_hg_in_w, m_hg_out_w, m_hg_onorm, m_sw_in_w, m_sw_out_w, m_sw_qnorm, m_sw_knorm, m_sw_sinks, m_gd_in_w, m_gd_out_w, m_gd_conv_w, m_gd_a_log, m_gd_dt_bias, m_gd_onorm, v_hgrn_lb, v_ada_w, v_ada_b, v_norm_g, v_hg_in_w, v_hg_out_w, v_hg_onorm, v_sw_in_w, v_sw_out_w, v_sw_qnorm, v_sw_knorm, v_sw_sinks, v_gd_in_w, v_gd_out_w, v_gd_conv_w, v_gd_a_log, v_gd_dt_bias, v_gd_onorm):
    pos = _my_pos()
    me = 4 * pos[0] + 2 * pos[1] + pos[2]
    x0 = x[0]
    tgt = loss_target[0]
    n_layers = norm_g.shape[0]
    aw = ada_w.shape[2]

    c_all = all_gather("ag_c", c)[:, 0, :]
    modp = all_gather("ag_mod", mod_partial("mod_partial", c_all, ada_w))
    modp_mine = lax.dynamic_index_in_dim(modp, me, axis=2, keepdims=False)
    modp_mine = jnp.transpose(modp_mine, (1, 0, 2)).reshape(n_layers, N_DEV * aw)
    mod, lb_all = prep("prep", modp_mine, ada_b, hgrn_lb)
    shift, scale, gate = mod[:, :D], mod[:, D:2 * D], mod[:, 2 * D:]
    row = lambda a, i: a[i:i + 1]

    later = [(GATHER, hg_in_w[1:2].astype(BF16)), (GATHER, hg_out_w.astype(BF16)), (GATHER, sw_in_w.astype(BF16)),
             (GATHER, sw_out_w.astype(BF16)), (GATHER, gd_in_w.astype(BF16)), (GATHER, gd_out_w.astype(BF16)),
             (GATHER, gd_conv_w)]
    gd_al, gd_dt = _pad_lanes(gd_a_log), _pad_lanes(gd_dt_bias)

    inv_freq = ROPE_THETA ** (-jnp.arange(0, SW_DH, 2, dtype=F32) / SW_DH)
    ang = positions[0].astype(F32)[:, None] * inv_freq
    cos, sin = jnp.cos(ang), jnp.sin(ang)
    cos64, sin64 = jnp.concatenate([cos, cos], axis=-1), jnp.concatenate([-sin, sin], axis=-1)

    xs, hs, ys, saved = [x0], [], [], []
    last = n_layers - 1
    h, (g_in0,) = norm_mod("norm0", x0, row(norm_g, 0), row(scale, 0), row(shift, 0), [(GATHER, hg_in_w[0:1].astype(BF16))])
    w_hg_in = [_cols_from(g_in0)[0], None]
    for i in range(n_layers):
        kind, j = i % 3, i // 3
        hs.append(h)
        if kind == 0:
            p = matmul(f"hg_in{i}", h, w_hg_in[j])
            u, st, got = hg_fwd(f"hg_fwd{i}", p, row(lb_all, i), row(hg_onorm, j), later if i == 0 else ())
            if i == 0:
                w_hg_in[1] = _cols_from(got[0])[0]
                w_hg_out = _rows_from(got[1])
                w_sw_in, w_sw_out = _cols_from(got[2])[0], _rows_from(got[3])[0]
                w_gd_in, w_gd_out = _cols_from(got[4])[0], _rows_from(got[5])[0]
                w_gd_qkv, w_gd_z = w_gd_in[:, :GD_QKV], w_gd_in[:, GD_QKV:GD_QKV + GD_VW]
                w_gd_ab = _pad_lanes(w_gd_in[:, GD_QKV + GD_VW:])
                conv_w8 = jnp.pad(_cols_from(got[6])[0], ((0, HALO - CONV_K), (0, 0)))
            w_out = w_hg_out[j]
            saved.append((p, u, st))
        elif kind == 1:
            p = matmul(f"sw_in{i}", h, w_sw_in)
            u, kpost = swa_fwd(f"sw_fwd{i}", p, cos64, sin64, row(sw_qnorm, j), row(sw_knorm, j), row(sw_sinks, j))
            w_out = w_sw_out
            saved.append((p, u, kpost))
        else:
            xq, cv = matmul_conv(f"gd_qkv{i}", h, w_gd_qkv, conv_w8)
            zz = matmul(f"gd_z{i}", h, w_gd_z)
            ab = matmul(f"gd_ab{i}", h, w_gd_ab)
            u, st = gd_fwd(f"gd_fwd{i}", cv, zz, ab, gd_al, gd_dt, row(gd_onorm, j))
            w_out = w_gd_out
            saved.append((xq, zz, ab, cv, u, st))
        if i < last:
            y, xn, h = out_resid_norm(f"out{i}", u, w_out, xs[i], row(gate, i), row(norm_g, i + 1), row(scale, i + 1),
                                      row(shift, i + 1))
            xs.append(xn)
        else:
            y, dx, dy, loss_acc, dgate_last = out_loss(f"out{i}", u, w_out, xs[i], tgt, row(gate, i))
        ys.append(y)
    loss = lax.psum(loss_acc[0, 0], ("x", "y", "c"))

    dgate = [None] * n_layers
    dgate[last] = dgate_last
    dg_norm, dscale, dshift = [None] * n_layers, [None] * n_layers, [None] * n_layers
    dlb = [jnp.zeros((1, D), F32)] * n_layers
    d_hg_on = [None] * 2
    rs_hg_in, rs_hg_out = [None] * 2, [None] * 2
    pending = ()
    for i in range(last, -1, -1):
        kind, j = i % 3, i // 3
        h = hs[i]
        if kind == 0:
            p, u, st = saved[i]
            du = matmul_nt(f"hg_du{i}", [(dy, w_hg_out[j])])
            d_out = matmul_tn(f"hg_dwo{i}", u, dy)
            dp, dlb_i, dog, got = hg_bwd(f"hg_bwd{i}", p, row(lb_all, i), row(hg_onorm, j), st, du, pending)
            if got:
                rs_sw_in, rs_sw_out = got
            dlb[i] = dlb_i
            d_hg_on[j] = dog
            dh_pairs = [(dp, w_hg_in[j])]
            d_in = matmul_tn(f"hg_dwi{i}", h, dp)
            pending = [(SCATTER, _scatter_cols(d_in[None])), (SCATTER, _scatter_rows(d_out[None]))]
        elif kind == 1:
            p, u, kpost = saved[i]
            du = matmul_nt(f"sw_du{i}", [(dy, w_sw_out)])
            d_out = matmul_tn(f"sw_dwo{i}", u, dy)
            dp, d_qn, d_kn, d_sk, got = swa_bwd(f"sw_bwd{i}", p, kpost, cos64, sin64, row(sw_qnorm, j), row(sw_knorm, j),
                                                row(sw_sinks, j), du, pending)
            rs_gd_in, rs_gd_out, rs_gd_conv = got
            dh_pairs = [(dp, w_sw_in)]
            d_in = matmul_tn(f"sw_dwi{i}", h, dp)
            pending = [(SCATTER, _scatter_cols(d_in[None])), (SCATTER, _scatter_rows(d_out[None]))]
        else:
            xq, zz, ab, cv, u, st = saved[i]
            du = matmul_nt(f"gd_du{i}", [(dy, w_gd_out)])
            d_out = matmul_tn(f"gd_dwo{i}", u, dy)
            dcv, dz, dab, d_al, d_dt, d_gd_on, got = gd_bwd(f"gd_bwd{i}", cv, zz, ab, gd_al, gd_dt, row(gd_onorm, j), st, du,
                                                            pending)
            rs_hg_in[1], rs_hg_out[1] = got
            dxq, d_conv8 = conv_bwd(f"gd_dconv{i}", xq, dcv, conv_w8)
            dh_pairs = [(dxq, w_gd_qkv), (dz, w_gd_z), (dab, w_gd_ab)]
            d_in = jnp.concatenate([matmul_tn(f"gd_dwq{i}", h, dxq), matmul_tn(f"gd_dwz{i}", h, dz),
                                    matmul_tn(f"gd_dwab{i}", h, dab)[:, :2 * GD_VH]], axis=1)
            pending = [(SCATTER, _scatter_cols(d_in[None])), (SCATTER, _scatter_rows(d_out[None])),
                       (SCATTER, _scatter_cols(d_conv8[None, :CONV_K], F32))]
        if i > 0:
            (dx, dy, dg_norm[i], dscale[i], dshift[i], dgate[i - 1]), _ = dh_bwd_row(
                f"dh{i}", dh_pairs, xs[i], dx, row(norm_g, i), row(scale, i), row(shift, i), ys[i - 1], row(gate, i - 1))
        else:
            (dx, dg_norm[i], dscale[i], dshift[i]), (rs_hg_in[0], rs_hg_out[0]) = dh_bwd_row(
                f"dh{i}", dh_pairs, xs[i], dx, row(norm_g, i), row(scale, i), row(shift, i), carry=pending)
    grad_x = dx[None]

    dmod = jnp.concatenate([_stack_rows(dshift), _stack_rows(dscale), _stack_rows(dgate)], axis=1)
    misc = _stack_rows(d_hg_on + [d_gd_on, _pad_lanes(d_qn), _pad_lanes(d_kn), _pad_lanes(d_sk), d_al, d_dt])
    small = jnp.concatenate([_stack_rows(dlb).reshape(-1, 128), _stack_rows(dg_norm).reshape(-1, 128),
                             dmod.reshape(-1, 128), misc], axis=0)
    small_all = all_gather("ag_small", small)
    n_lb = n_layers * D // 128
    n_mod = n_layers * 3 * D // 128
    o = 0
    dlb_parts = small_all[:, o:o + n_lb].reshape(N_DEV, n_layers, D); o += n_lb
    dgn_parts = small_all[:, o:o + n_lb].reshape(N_DEV, n_layers, D); o += n_lb
    dmod_parts = small_all[:, o:o + n_mod].reshape(N_DEV, n_layers, 3 * D); o += n_mod
    dhgon_parts = small_all[:, o:o + 2]; o += 2
    dgdon_parts = small_all[:, o:o + 1]; o += 1
    dqn_parts = small_all[:, o:o + 1, :SW_DH]; o += 1
    dkn_parts = small_all[:, o:o + 1, :SW_DH]; o += 1
    dsk_parts = small_all[:, o:o + 1, :SW_KV * SW_G]; o += 1
    dal_parts = small_all[:, o:o + 1, :GD_VH]; o += 1
    ddt_parts = small_all[:, o:o + 1, :GD_VH]; o += 1

    g_lb = lb_grad("lb_grad", hgrn_lb, dlb_parts)
    dm_mine = lax.dynamic_slice_in_dim(dmod_parts, me * aw, aw, axis=2)
    g_ada_w = ada_grad("ada_grad", jnp.transpose(c_all), jnp.transpose(dm_mine, (1, 0, 2)))

    res = {}
    res["hgrn_lb"] = adamw("aw_hgrn_lb", hgrn_lb, m_hgrn_lb, v_hgrn_lb, g_lb[None])
    res["ada_w"] = adamw("aw_ada_w", ada_w, m_ada_w, v_ada_w, g_ada_w[None])
    res["ada_b"] = adamw("aw_ada_b", ada_b, m_ada_b, v_ada_b, dmod_parts)
    res["norm_g"] = adamw("aw_norm_g", norm_g, m_norm_g, v_norm_g, dgn_parts)
    res["hg_in_w"] = adamw("aw_hg_in", hg_in_w, m_hg_in_w, v_hg_in_w, jnp.concatenate(rs_hg_in, axis=1))
    res["hg_out_w"] = adamw("aw_hg_out", hg_out_w, m_hg_out_w, v_hg_out_w, jnp.concatenate(rs_hg_out, axis=1))
    res["hg_onorm"] = adamw("aw_hg_onorm", hg_onorm, m_hg_onorm, v_hg_onorm, dhgon_parts)
    res["sw_in_w"] = adamw("aw_sw_in", sw_in_w, m_sw_in_w, v_sw_in_w, rs_sw_in)
    res["sw_out_w"] = adamw("aw_sw_out", sw_out_w, m_sw_out_w, v_sw_out_w, rs_sw_out)
    res["sw_qnorm"] = adamw("aw_sw_qn", sw_qnorm, m_sw_qnorm, v_sw_qnorm, dqn_parts)
    res["sw_knorm"] = adamw("aw_sw_kn", sw_knorm, m_sw_knorm, v_sw_knorm, dkn_parts)
    res["sw_sinks"] = adamw("aw_sw_sinks", sw_sinks, m_sw_sinks, v_sw_sinks, dsk_parts)
    res["gd_in_w"] = adamw("aw_gd_in", gd_in_w, m_gd_in_w, v_gd_in_w, rs_gd_in)
    res["gd_out_w"] = adamw("aw_gd_out", gd_out_w, m_gd_out_w, v_gd_out_w, rs_gd_out)
    res["gd_conv_w"] = adamw("aw_gd_conv", gd_conv_w, m_gd_conv_w, v_gd_conv_w, rs_gd_conv)
    res["gd_a_log"] = adamw("aw_gd_alog", gd_a_log, m_gd_a_log, v_gd_a_log, dal_parts)
    res["gd_dt_bias"] = adamw("aw_gd_dt", gd_dt_bias, m_gd_dt_bias, v_gd_dt_bias, ddt_parts)
    res["gd_onorm"] = adamw("aw_gd_onorm", gd_onorm, m_gd_onorm, v_gd_onorm, dgdon_parts)

    order = ["hgrn_lb", "ada_w", "ada_b", "norm_g", "hg_in_w", "hg_out_w", "hg_onorm", "sw_in_w", "sw_out_w", "sw_qnorm",
             "sw_knorm", "sw_sinks", "gd_in_w", "gd_out_w", "gd_conv_w", "gd_a_log", "gd_dt_bias", "gd_onorm"]
    outs = [loss, grad_x]
    for part in range(4):
        outs += [res[n][part] for n in order]
    return tuple(outs)
```

```python
import functools

import jax
import jax.numpy as jnp
from jax import lax
from jax.experimental import pallas as pl
from jax.experimental.pallas import tpu as pltpu

F32, BF16 = jnp.float32, jnp.bfloat16
MESH = pl.DeviceIdType.MESH
N_DEV = 8
D = 1024
EPS = 1e-6
NEG_INF = float("-inf")

HG_HEADS, HG_DH, HG_BLK, HG_CH, HG_SUB = 8, 128, 128, 32, 8
HG_SAFE = 60.0
SW_KV, SW_G, SW_DH, SW_BLK = 4, 4, 64, 128
SW_QW, SW_KVW = 1024, 256
GD_DK, GD_CHUNK = 128, 64
GD_QKW, GD_VW, GD_QKV = 1024, 2048, 4096
CONV_K = 4
ROPE_THETA = 10000.0

ADAM_LR, ADAM_B1, ADAM_B2, ADAM_EPS, ADAM_WD, ADAM_STEP = 0.001, 0.9, 0.999, 1e-08, 0.01, 10

VMEM_LIMIT = 56 * 1024 * 1024


def _params(**kw):
    return pltpu.CompilerParams(vmem_limit_bytes=VMEM_LIMIT, **kw)


def _dot(a, b, ca, cb):
    return lax.dot_general(a.astype(BF16), b.astype(BF16), (((ca,), (cb,)), ((), ())), preferred_element_type=F32)


def mm(a, b):
    return _dot(a, b, 1, 0)


def mm_nt(a, b):
    return _dot(a, b, 1, 1)


def mm_tn(a, b):
    return _dot(a, b, 0, 0)


def _split3(x):
    h1 = x.astype(BF16)
    r1 = x - h1.astype(F32)
    h2 = r1.astype(BF16)
    h3 = (r1 - h2.astype(F32)).astype(BF16)
    return h1, h2, h3


def _bdot(a, b, ca, cb, prec=None):
    return lax.dot_general(a, b, (((ca,), (cb,)), ((0,), (0,))), precision=prec, preferred_element_type=F32)


def bmm(a, b):
    return _bdot(a.astype(BF16), b.astype(BF16), 2, 1)


def bmm_nt(a, b):
    return _bdot(a.astype(BF16), b.astype(BF16), 2, 2)


def bmm_tn(a, b):
    return _bdot(a.astype(BF16), b.astype(BF16), 1, 1)


def bmmf(a, b):
    return _bdot(a, b, 2, 1, lax.Precision.HIGH)


def _blin01(m, x):
    mb = m.astype(BF16)
    return sum(_bdot(mb, p, 2, 1) for p in _split3(x))


@jax.custom_vjp
def blin01(m, mt, x):
    return _blin01(m, x)


def _blin01_fwd(m, mt, x):
    return _blin01(m, x), (m, mt)


def _blin01_bwd(res, g):
    m, mt = res
    return jnp.zeros_like(m), jnp.zeros_like(mt), _blin01(mt, g)


blin01.defvjp(_blin01_fwd, _blin01_bwd)


def _blin01_r(x, m):
    mb = m.astype(BF16)
    return sum(_bdot(p, mb, 2, 1) for p in _split3(x))


@jax.custom_vjp
def blin01_r(x, m, mt):
    return _blin01_r(x, m)


def _blin01_r_fwd(x, m, mt):
    return _blin01_r(x, m), (m, mt)


def _blin01_r_bwd(res, g):
    m, mt = res
    return _blin01_r(g, mt), jnp.zeros_like(m), jnp.zeros_like(mt)


blin01_r.defvjp(_blin01_r_fwd, _blin01_r_bwd)


@functools.partial(jax.custom_vjp, nondiff_argnums=(1, 2))
def rows(x, start, size):
    return lax.slice_in_dim(x, start, start + size, axis=x.ndim - 2)


def _rows_fwd(x, start, size):
    return lax.slice_in_dim(x, start, start + size, axis=x.ndim - 2), jnp.zeros(x.shape[:-1] + (1,), F32)


def _rows_bwd(start, size, res, g):
    ax = g.ndim - 2
    total = res.shape[ax]
    zeros = lambda n: jnp.zeros(g.shape[:ax] + (n,) + g.shape[ax + 1:], g.dtype)
    parts = []
    if start > 0:
        parts.append(zeros(start))
    parts.append(g)
    if total - start - size > 0:
        parts.append(zeros(total - start - size))
    return (jnp.concatenate(parts, axis=ax) if len(parts) > 1 else g,)


rows.defvjp(_rows_fwd, _rows_bwd)


@functools.partial(jax.custom_vjp, nondiff_argnums=(1,))
def rowsel(x, s):
    return lax.slice_in_dim(x, s, s + 1, axis=x.ndim - 2)


def _rowsel_fwd(x, s):
    return lax.slice_in_dim(x, s, s + 1, axis=x.ndim - 2), jnp.zeros(x.shape[:-1] + (1,), F32)


def _rowsel_bwd(s, res, g):
    r = lax.broadcasted_iota(jnp.int32, res.shape, res.ndim - 2)
    return (jnp.where(r == s, g, 0.0),)


rowsel.defvjp(_rowsel_fwd, _rowsel_bwd)


def sigmoid(x):
    return jax.nn.sigmoid(x)


def silu(x):
    return x * jax.nn.sigmoid(x)


def softplus(x):
    return jnp.maximum(x, 0.0) + jnp.log1p(jnp.exp(-jnp.abs(x)))


def _iota(shape, dim):
    return lax.broadcasted_iota(jnp.int32, shape, dim)


def norm_mod_fn(g, sc, sh, x):
    r = lax.rsqrt(jnp.mean(x * x, axis=-1, keepdims=True) + EPS)
    return (x * r * g) * (1.0 + sc) + sh


def _row_spec(tm, f):
    return pl.BlockSpec((tm, f), lambda i: (i, 0))


def _const_spec(shape):
    nd = len(shape)
    return pl.BlockSpec(shape, lambda i: (0,) * nd)


def _tiled(name, fn, consts, rows_in, row_outs, acc_outs=(), tm=512, carry=()):
    T = rows_in[0].shape[0]
    tm = min(tm, T)
    n_c, n_r, n_ro = len(consts), len(rows_in), len(row_outs)

    def body(*refs):
        c_refs, r_refs = refs[:n_c], refs[n_c:n_c + n_r]
        ro_refs, ao_refs = refs[n_c + n_r:n_c + n_r + n_ro], refs[n_c + n_r + n_ro:]
        outs = fn(*[r[...] for r in c_refs], *[r[...] for r in r_refs])
        for r, v in zip(ro_refs, outs[:n_ro]):
            r[...] = v.astype(r.dtype)
        if ao_refs:
            @pl.when(pl.program_id(0) == 0)
            def _():
                for r in ao_refs:
                    r[...] = jnp.zeros(r.shape, r.dtype)
            for r, v in zip(ao_refs, outs[n_ro:]):
                r[...] += v

    out_shape = [jax.ShapeDtypeStruct((T, f), dt) for f, dt in row_outs] + [jax.ShapeDtypeStruct(s, F32) for s in acc_outs]
    out_specs = [_row_spec(tm, f) for f, _ in row_outs] + [_const_spec(s) for s in acc_outs]
    in_specs = [_const_spec(c.shape) for c in consts] + [_row_spec(tm, r.shape[1]) for r in rows_in]
    return _scan_call(body, name, T // tm, in_specs, out_specs, out_shape, [], (*consts, *rows_in), carry)


def norm_mod(name, x, g, sc, sh, carry=()):
    (h,), got = _tiled(name, lambda g, sc, sh, x: (norm_mod_fn(g, sc, sh, x),), [g, sc, sh], [x], [(D, BF16)], carry=carry)
    return h, got


def out_resid_norm(name, u, w, x, gate, g, sc, sh):
    def fn(gate, g, sc, sh, w, x, u):
        y = mm(u, w)
        xn = x + gate * y
        return y, xn, norm_mod_fn(g, sc, sh, xn)
    return _tiled(name, fn, [gate, g, sc, sh, w], [x, u], [(D, F32), (D, F32), (D, BF16)])[0]


def out_loss(name, u, w, x, tgt, gate):
    def fn(gate, w, x, tgt, u):
        y = mm(u, w)
        err = x + gate * y - tgt
        dx = err * (1.0 / D)
        per_tok = jnp.sum(err * err, axis=-1, keepdims=True) * (0.5 / D)
        loss = jnp.sum(per_tok, axis=0, keepdims=True)
        return y, dx, gate * dx, jnp.broadcast_to(loss, (1, 128)), jnp.sum(dx * y, axis=0, keepdims=True)
    return _tiled(name, fn, [gate, w], [x, tgt, u], [(D, F32), (D, F32), (D, BF16)], [(1, 128), (1, D)])[0]


def dh_bwd_row(name, pairs, x, dxn, g, sc, sh, y_prev=None, gate_prev=None, carry=()):
    with_prev = y_prev is not None
    n = len(pairs)

    def fn(*a):
        a = list(a)
        gate_p = a.pop(0) if with_prev else None
        g, sc, sh = a[:3]
        ws, a = a[3:3 + n], a[3 + n:]
        x, dxn = a[:2]
        yp = a[2] if with_prev else None
        dps = a[3:] if with_prev else a[2:]
        dh = mm_nt(dps[0], ws[0])
        for dp, w in zip(dps[1:], ws[1:]):
            dh = dh + mm_nt(dp, w)
        _, f = jax.vjp(norm_mod_fn, g, sc, sh, x)
        dg, dsc, dsh, dx = f(dh)
        dx = dx + dxn
        if with_prev:
            return dx, gate_p * dx, dg, dsc, dsh, jnp.sum(dx * yp, axis=0, keepdims=True)
        return dx, dg, dsc, dsh

    ws, dps = [w for _, w in pairs], [dp for dp, _ in pairs]
    tm = MM_ROWS_WIDE if sum(dp.shape[1] for dp in dps) > MM_WIDE else MM_ROWS_NARROW
    if with_prev:
        return _tiled(name, fn, [gate_prev, g, sc, sh] + ws, [x, dxn, y_prev] + dps, [(D, F32), (D, BF16)], [(1, D)] * 4,
                      tm=tm, carry=carry)
    return _tiled(name, fn, [g, sc, sh] + ws, [x, dxn] + dps, [(D, F32)], [(1, D)] * 3, tm=tm, carry=carry)


MM_ROWS_WIDE, MM_ROWS_NARROW, MM_WIDE = 256, 512, 2048


def matmul(name, a, w, out_dtype=F32):
    T, K = a.shape
    N = w.shape[1]
    tm = min(MM_ROWS_NARROW if N <= MM_WIDE else MM_ROWS_WIDE, T)

    def body(a_ref, w_ref, o_ref):
        o_ref[...] = mm(a_ref[...], w_ref[...]).astype(o_ref.dtype)

    return pl.pallas_call(
        body, name=name, grid=(T // tm,),
        in_specs=[_row_spec(tm, K), _const_spec((K, N))], out_specs=_row_spec(tm, N),
        out_shape=jax.ShapeDtypeStruct((T, N), out_dtype),
        compiler_params=_params(dimension_semantics=("arbitrary",)),
    )(a, w)


def matmul_nt(name, pairs, out_dtype=F32):
    T = pairs[0][0].shape[0]
    K = pairs[0][1].shape[0]
    wide = sum(a.shape[1] for a, _ in pairs) > MM_WIDE
    tm = min(MM_ROWS_WIDE if wide else MM_ROWS_NARROW, T)
    n = len(pairs)

    def body(*refs):
        o_ref = refs[-1]
        acc = None
        for i in range(n):
            p = mm_nt(refs[2 * i][...], refs[2 * i + 1][...])
            acc = p if acc is None else acc + p
        o_ref[...] = acc.astype(o_ref.dtype)

    in_specs, args = [], []
    for a, w in pairs:
        in_specs += [_row_spec(tm, a.shape[1]), _const_spec(w.shape)]
        args += [a, w]
    return pl.pallas_call(
        body, name=name, grid=(T // tm,), in_specs=in_specs, out_specs=_row_spec(tm, K),
        out_shape=jax.ShapeDtypeStruct((T, K), out_dtype),
        compiler_params=_params(dimension_semantics=("arbitrary",)),
    )(*args)


def matmul_tn(name, a, b, tm=512, tn=2048):
    T, K = a.shape
    N = b.shape[1]
    tm = min(tm, T)
    tn = max(t for t in range(128, min(tn, N) + 1, 128) if N % t == 0)

    def body(a_ref, b_ref, o_ref):
        @pl.when(pl.program_id(1) == 0)
        def _():
            o_ref[...] = jnp.zeros(o_ref.shape, F32)
        o_ref[...] += mm_tn(a_ref[...], b_ref[...])

    return pl.pallas_call(
        body, name=name, grid=(N // tn, T // tm),
        in_specs=[pl.BlockSpec((tm, K), lambda j, i: (i, 0)), pl.BlockSpec((tm, tn), lambda j, i: (i, j))],
        out_specs=pl.BlockSpec((K, tn), lambda j, i: (0, j)),
        out_shape=jax.ShapeDtypeStruct((K, N), F32),
        compiler_params=_params(dimension_semantics=("arbitrary", "arbitrary")),
    )(a, b)


def _hg_log_f(lb, fp):
    return jnp.log(lb + (1.0 - lb) * sigmoid(fp))


def _hg_safe(lb, fp):
    lf = _hg_log_f(lb, fp)
    half = HG_CH // 2
    worst = [jnp.max(-jnp.sum(lf[:, c * half:(c + 1) * half], axis=1)) for c in range(fp.shape[1] // half)]
    return functools.reduce(jnp.maximum, worst) <= HG_SAFE


def hg_heads(fast, tri, trit, lb, og, st, qp, fp, iv, z):
    blk = qp.shape[1]
    q = silu(qp)
    lf = _hg_log_f(lb, fp)
    k = (1.0 - lb) * sigmoid(-fp)
    r = _iota((1, HG_SUB, 1), 1)
    rc = _iota((1, HG_CH, 1), 1)
    outs = []
    for c in range(blk // HG_CH):
        qc, kc, vc, lfc = (rows(a, c * HG_CH, HG_CH) for a in (q, k, iv, lf))
        b = blin01(tri, trit, lfc)
        qd = qc * jnp.exp(b)
        o_state = bmm_nt(qd, st)
        if fast:
            mid = rowsel(b, HG_CH // 2 - 1)
            scores = jnp.where(tri > 0.0, _bdot(qc * jnp.exp(b - mid), kc * jnp.exp(mid - b), 2, 2, lax.Precision.HIGH), 0.0)
            outs.append(o_state + bmm(scores, vc))
            bl = rowsel(b, HG_CH - 1)
            st = st * jnp.exp(bl) + bmm_tn(vc, kc * jnp.exp(bl - b))
            continue
        parts = []
        for i in range(HG_CH // HG_SUB):
            qi, ki, vi, bi = (rows(a, i * HG_SUB, HG_SUB) for a in (qc, kc, vc, b))
            o = rows(o_state, i * HG_SUB, HG_SUB)
            for s in range(HG_SUB):
                ks, bs, vs = rowsel(ki, s), rowsel(bi, s), rowsel(vi, s)
                e = jnp.exp(jnp.where(r >= s, bi - bs, NEG_INF))
                o = o + jnp.sum(qi * ks * e, axis=-1, keepdims=True) * vs
            if i > 0:
                rb = rowsel(b, i * HG_SUB - 1)
                kt = kc * jnp.exp(jnp.where(rc < i * HG_SUB, rb - b, NEG_INF))
                o = o + bmm(bmm_nt(qi * jnp.exp(bi - rb), kt), vc)
            parts.append(o)
        bl = rowsel(b, HG_CH - 1)
        st = st * jnp.exp(bl) + bmm_tn(vc, kc * jnp.exp(bl - b))
        outs.append(jnp.concatenate(parts, axis=1))
    o = jnp.concatenate(outs, axis=1)
    on = o * lax.rsqrt(jnp.mean(o * o, axis=-1, keepdims=True) + EPS) * og
    return st, on * silu(z)


def _tri(n, nh):
    t = (_iota((nh, n, n), 1) >= _iota((nh, n, n), 2)).astype(F32)
    tt = (_iota((nh, n, n), 1) <= _iota((nh, n, n), 2)).astype(F32)
    return t, tt


def _hg_load(p_ref, lb_ref, heads):
    col = lambda ref, base: jnp.stack([ref[:, pl.ds(base + h * HG_DH, HG_DH)] for h in heads])
    return col(lb_ref, 0), col(p_ref, 0), col(p_ref, D), col(p_ref, 2 * D), col(p_ref, 3 * D)


def hg_fwd(name, p, lb, og, carry=()):
    T = p.shape[0]
    nb = T // HG_BLK

    def body(p_ref, lb_ref, og_ref, u_ref, st_ref, s_scr):
        @pl.when(pl.program_id(0) == 0)
        def _():
            s_scr[...] = jnp.zeros(s_scr.shape, F32)
        st_ref[0] = s_scr[...]
        tri, trit = _tri(HG_CH, HG_HEADS)
        og_v = og_ref[...]
        heads = list(range(HG_HEADS))
        lbh, qp, fp, iv, z = _hg_load(p_ref, lb_ref, heads)

        def run(fast):
            def go():
                st, u = hg_heads(fast, tri, trit, lbh, og_v, s_scr[...], qp, fp, iv, z)
                s_scr[...] = st
                for h in heads:
                    u_ref[:, pl.ds(h * HG_DH, HG_DH)] = u[h].astype(BF16)
            return go

        run(False)()

    (u, st), ex = _scan_call(
        body, name, nb,
        [_row_spec(HG_BLK, 4 * D), _const_spec((1, D)), _const_spec((1, HG_DH))],
        [_row_spec(HG_BLK, D), pl.BlockSpec((1, HG_HEADS, HG_DH, HG_DH), lambda i: (i, 0, 0, 0))],
        [jax.ShapeDtypeStruct((T, D), BF16), jax.ShapeDtypeStruct((nb, HG_HEADS, HG_DH, HG_DH), F32)],
        [pltpu.VMEM((HG_HEADS, HG_DH, HG_DH), F32)], (p, lb, og), carry)
    return u, st, ex


def hg_bwd(name, p, lb, og, states, du, carry=()):
    T = p.shape[0]
    nb = T // HG_BLK

    def body(p_ref, lb_ref, og_ref, st_ref, du_ref, dp_ref, dlb_ref, dog_ref, ds_scr):
        @pl.when(pl.program_id(0) == 0)
        def _():
            ds_scr[...] = jnp.zeros(ds_scr.shape, F32)
            dlb_ref[...] = jnp.zeros(dlb_ref.shape, F32)
            dog_ref[...] = jnp.zeros(dog_ref.shape, F32)
        tri, trit = _tri(HG_CH, HG_HEADS)
        og_v = og_ref[...]
        heads = list(range(HG_HEADS))
        lbh, qp, fp, iv, z = _hg_load(p_ref, lb_ref, heads)
        du_h = jnp.stack([du_ref[:, pl.ds(h * HG_DH, HG_DH)].astype(F32) for h in heads])

        def run(fast):
            def go():
                _, f = jax.vjp(functools.partial(hg_heads, fast, tri, trit), lbh, og_v, st_ref[0], qp, fp, iv, z)
                dlb, dog, dst, dq, df, di, dz = f((ds_scr[...], du_h))
                ds_scr[...] = dst
                dog_ref[...] += dog
                for h in heads:
                    for base, g in ((0, dq), (D, df), (2 * D, di), (3 * D, dz)):
                        dp_ref[:, pl.ds(base + h * HG_DH, HG_DH)] = g[h].astype(BF16)
                    dlb_ref[:, pl.ds(h * HG_DH, HG_DH)] += dlb[h]
            return go

        run(False)()

    rev = lambda i: (nb - 1 - i, 0)
    outs, ex = _scan_call(
        body, name, nb,
        [pl.BlockSpec((HG_BLK, 4 * D), rev), _const_spec((1, D)), _const_spec((1, HG_DH)),
         pl.BlockSpec((1, HG_HEADS, HG_DH, HG_DH), lambda i: (nb - 1 - i, 0, 0, 0)), pl.BlockSpec((HG_BLK, D), rev)],
        [pl.BlockSpec((HG_BLK, 4 * D), rev), _const_spec((1, D)), _const_spec((1, HG_DH))],
        [jax.ShapeDtypeStruct((T, 4 * D), BF16), jax.ShapeDtypeStruct((1, D), F32), jax.ShapeDtypeStruct((1, HG_DH), F32)],
        [pltpu.VMEM((HG_HEADS, HG_DH, HG_DH), F32)], (p, lb, og, states, du), carry)
    return (*outs, ex)


def swa_heads(qn, kn, sinks, kprev, vprev, qp4, kp, v, z4, cos4, sin4, cosk, sink_, hots, mask, p64, p64t):
    def norm_rope(xp, g, cs, sn):
        y = xp * lax.rsqrt(jnp.mean(xp * xp, axis=-1, keepdims=True) + EPS) * g
        return y * cs + blin01_r(y, p64, p64t) * sn

    q = norm_rope(qp4, qn, cos4, sin4)
    k = norm_rope(kp, kn, cosk, sink_)
    k2 = jnp.concatenate([kprev, k], axis=1)
    v2 = jnp.concatenate([vprev, v], axis=1)
    s = jnp.where(mask, bmm_nt(q, k2) * (SW_DH ** -0.5), NEG_INF)
    sink_col = jnp.concatenate([jnp.concatenate(
        [jnp.broadcast_to(jnp.sum(sinks * hot, axis=-1, keepdims=True), (SW_BLK, 1)) for hot in hot4], axis=0)[None]
        for hot4 in hots], axis=0)
    m = lax.stop_gradient(jnp.maximum(jnp.max(s, axis=-1, keepdims=True), sink_col))
    p = jnp.exp(s - m)
    p = p / (jnp.sum(p, axis=-1, keepdims=True) + jnp.exp(sink_col - m))
    o = bmm(p, v2)
    return k, v, o * silu(z4)


def _swa_consts(first):
    qi = _iota((SW_G * SW_BLK, 2 * SW_BLK), 0) % SW_BLK
    kj = _iota((SW_G * SW_BLK, 2 * SW_BLK), 1)
    rel = qi + SW_BLK - kj
    mask = (rel >= 0) & (rel < SW_BLK) & ((kj >= SW_BLK) | jnp.logical_not(first))
    i, j = _iota((SW_KV, SW_DH, SW_DH), 1), _iota((SW_KV, SW_DH, SW_DH), 2)
    p64 = (i == (j + SW_DH // 2) % SW_DH).astype(F32)
    p64t = (j == (i + SW_DH // 2) % SW_DH).astype(F32)
    lane = _iota((1, SW_KV * SW_G), 1)
    hots = [[(lane == h * SW_G + g).astype(F32) for g in range(SW_G)] for h in range(SW_KV)]
    return mask, p64, p64t, hots


def _stack_q(ref, h, base):
    return jnp.concatenate([ref[:, pl.ds(base + h * SW_G * SW_DH + g * SW_DH, SW_DH)] for g in range(SW_G)], axis=0)


def _swa_load(p_ref):
    heads = range(SW_KV)
    return (jnp.stack([_stack_q(p_ref, h, Q0) for h in heads]),
            jnp.stack([p_ref[:, pl.ds(K0 + h * SW_DH, SW_DH)] for h in heads]),
            jnp.stack([p_ref[:, pl.ds(V0 + h * SW_DH, SW_DH)] for h in heads]),
            jnp.stack([_stack_q(p_ref, h, Z0) for h in heads]))


Q0, K0, V0, Z0 = 0, SW_QW, SW_QW + SW_KVW, SW_QW + 2 * SW_KVW
SW_IN = 2 * SW_QW + 2 * SW_KVW


def swa_fwd(name, p, cos, sin, qn, kn, sinks):
    T = p.shape[0]
    nb = T // SW_BLK

    def body(p_ref, cos_ref, sin_ref, qn_ref, kn_ref, sk_ref, u_ref, ko_ref, kprev, vprev):
        first = pl.program_id(0) == 0

        @pl.when(first)
        def _():
            kprev[...] = jnp.zeros(kprev.shape, F32)
            vprev[...] = jnp.zeros(vprev.shape, F32)
        mask, p64, p64t, hots = _swa_consts(first)
        cs, sn = cos_ref[...], sin_ref[...]
        cs4, sn4 = jnp.concatenate([cs] * SW_G, axis=0), jnp.concatenate([sn] * SW_G, axis=0)
        qp4, kp, vv, z4 = _swa_load(p_ref)
        k, v, u4 = swa_heads(qn_ref[...], kn_ref[...], sk_ref[...], kprev[...], vprev[...], qp4, kp, vv, z4,
                             cs4, sn4, cs, sn, hots, mask, p64, p64t)
        kprev[...] = k
        vprev[...] = v
        for h in range(SW_KV):
            ko_ref[:, pl.ds(h * SW_DH, SW_DH)] = k[h]
            uh = u4[h]
            for g in range(SW_G):
                u_ref[:, pl.ds(h * SW_G * SW_DH + g * SW_DH, SW_DH)] = uh[g * SW_BLK:(g + 1) * SW_BLK].astype(BF16)

    return pl.pallas_call(
        body, name=name, grid=(nb,),
        in_specs=[_row_spec(SW_BLK, SW_IN), _row_spec(SW_BLK, SW_DH), _row_spec(SW_BLK, SW_DH),
                  _const_spec((1, SW_DH)), _const_spec((1, SW_DH)), _const_spec((1, SW_KV * SW_G))],
        out_specs=[_row_spec(SW_BLK, SW_QW), _row_spec(SW_BLK, SW_KVW)],
        out_shape=[jax.ShapeDtypeStruct((T, SW_QW), BF16), jax.ShapeDtypeStruct((T, SW_KVW), F32)],
        scratch_shapes=[pltpu.VMEM((SW_KV, SW_BLK, SW_DH), F32), pltpu.VMEM((SW_KV, SW_BLK, SW_DH), F32)],
        compiler_params=_params(dimension_semantics=("arbitrary",)),
    )(p, cos, sin, qn, kn, sinks)


def swa_bwd(name, p, kpost, cos, sin, qn, kn, sinks, du, carry=()):
    T = p.shape[0]
    nb = T // SW_BLK

    def body(p_ref, pprev_ref, kprev_ref, cos_ref, sin_ref, qn_ref, kn_ref, sk_ref, du_ref,
             dp_ref, dqn_ref, dkn_ref, dsk_ref, dk_scr, dv_scr):
        i = pl.program_id(0)
        first = i == nb - 1

        @pl.when(i == 0)
        def _():
            dk_scr[...] = jnp.zeros(dk_scr.shape, F32)
            dv_scr[...] = jnp.zeros(dv_scr.shape, F32)
            dqn_ref[...] = jnp.zeros(dqn_ref.shape, F32)
            dkn_ref[...] = jnp.zeros(dkn_ref.shape, F32)
            dsk_ref[...] = jnp.zeros(dsk_ref.shape, F32)
        mask, p64, p64t, hots = _swa_consts(first)
        cs, sn = cos_ref[...], sin_ref[...]
        cs4, sn4 = jnp.concatenate([cs] * SW_G, axis=0), jnp.concatenate([sn] * SW_G, axis=0)
        heads = range(SW_KV)
        fn = lambda qn, kn, sk, kpv, vpv, qp4, kp, v, z4: swa_heads(
            qn, kn, sk, kpv, vpv, qp4, kp, v, z4, cs4, sn4, cs, sn, hots, mask, p64, p64t)
        kpv = jnp.stack([kprev_ref[:, pl.ds(h * SW_DH, SW_DH)] for h in heads])
        vpv = jnp.stack([pprev_ref[:, pl.ds(V0 + h * SW_DH, SW_DH)] for h in heads])
        _, f = jax.vjp(fn, qn_ref[...], kn_ref[...], sk_ref[...], kpv, vpv, *_swa_load(p_ref))
        du4 = jnp.stack([_stack_q(du_ref, h, 0).astype(F32) for h in heads])
        dqn, dkn, dsk, dkpv, dvpv, dq4, dkp, dv, dz4 = f((dk_scr[...], dv_scr[...], du4))
        dk_scr[...] = dkpv
        dv_scr[...] = dvpv
        for h in heads:
            dp_ref[:, pl.ds(K0 + h * SW_DH, SW_DH)] = dkp[h].astype(BF16)
            dp_ref[:, pl.ds(V0 + h * SW_DH, SW_DH)] = dv[h].astype(BF16)
            dqh, dzh = dq4[h], dz4[h]
            for g in range(SW_G):
                c0 = h * SW_G * SW_DH + g * SW_DH
                dp_ref[:, pl.ds(Q0 + c0, SW_DH)] = dqh[g * SW_BLK:(g + 1) * SW_BLK].astype(BF16)
                dp_ref[:, pl.ds(Z0 + c0, SW_DH)] = dzh[g * SW_BLK:(g + 1) * SW_BLK].astype(BF16)
        dqn_ref[...] += dqn
        dkn_ref[...] += dkn
        dsk_ref[...] += dsk

    rev = lambda i: (nb - 1 - i, 0)
    prev = lambda i: (jnp.maximum(nb - 2 - i, 0), 0)
    outs, ex = _scan_call(
        body, name, nb,
        [pl.BlockSpec((SW_BLK, SW_IN), rev), pl.BlockSpec((SW_BLK, SW_IN), prev),
         pl.BlockSpec((SW_BLK, SW_KVW), prev), pl.BlockSpec((SW_BLK, SW_DH), rev), pl.BlockSpec((SW_BLK, SW_DH), rev),
         _const_spec((1, SW_DH)), _const_spec((1, SW_DH)), _const_spec((1, SW_KV * SW_G)), pl.BlockSpec((SW_BLK, SW_QW), rev)],
        [pl.BlockSpec((SW_BLK, SW_IN), rev), _const_spec((1, SW_DH)), _const_spec((1, SW_DH)), _const_spec((1, SW_KV * SW_G))],
        [jax.ShapeDtypeStruct((T, SW_IN), BF16), jax.ShapeDtypeStruct((1, SW_DH), F32),
         jax.ShapeDtypeStruct((1, SW_DH), F32), jax.ShapeDtypeStruct((1, SW_KV * SW_G), F32)],
        [pltpu.VMEM((SW_KV, SW_BLK, SW_DH), F32), pltpu.VMEM((SW_KV, SW_BLK, SW_DH), F32)],
        (p, p, kpost, cos, sin, qn, kn, sinks, du), carry)
    return (*outs, ex)


CONV_TC, CONV_TL, HALO = 512, 1024, 8


def _conv_taps(xe, w_ref):
    acc = w_ref[pl.ds(CONV_K - 1, 1), :] * xe
    for k in range(CONV_K - 1):
        acc = acc + w_ref[pl.ds(k, 1), :] * pltpu.roll(xe, CONV_K - 1 - k, 0)
    return acc


MMCONV_ROWS = 128


def matmul_conv(name, a, w, w8):
    T, K = a.shape
    C = w.shape[1]
    tm = min(MMCONV_ROWS, T)

    def body(a_ref, w_ref, w8_ref, x_ref, c_ref, tail):
        @pl.when(pl.program_id(0) == 0)
        def _():
            tail[...] = jnp.zeros(tail.shape, F32)
        x = mm(a_ref[...], w_ref[...])
        x_ref[...] = x
        xe = jnp.concatenate([tail[...], x], axis=0)
        c_ref[...] = silu(_conv_taps(xe, w8_ref)[HALO:])
        tail[...] = x[tm - HALO:]

    return pl.pallas_call(
        body, name=name, grid=(T // tm,),
        in_specs=[_row_spec(tm, K), _const_spec((K, C)), _const_spec((HALO, C))],
        out_specs=[_row_spec(tm, C), _row_spec(tm, C)],
        out_shape=[jax.ShapeDtypeStruct((T, C), F32), jax.ShapeDtypeStruct((T, C), F32)],
        scratch_shapes=[pltpu.VMEM((HALO, C), F32)],
        compiler_params=_params(dimension_semantics=("arbitrary",)),
    )(a, w, w8)


def conv_bwd(name, x, dy, w8):
    T, C = x.shape
    tc = min(CONV_TC, T)
    nt = T // tc
    n = tc + 2 * HALO

    def body(xp_ref, x_ref, xn_ref, dy_ref, dyn_ref, w_ref, dx_ref, dw_ref):
        i = pl.program_id(1)

        @pl.when(i == 0)
        def _():
            dw_ref[...] = jnp.zeros(dw_ref.shape, F32)
        prev = jnp.where(i == 0, 0.0, xp_ref[...])
        xe = jnp.concatenate([prev, x_ref[...], xn_ref[...]], axis=0)
        dye = jnp.concatenate([jnp.zeros((HALO, CONV_TL), F32), dy_ref[...], jnp.where(i == nt - 1, 0.0, dyn_ref[...])], axis=0)
        ce = _conv_taps(xe, w_ref)
        sg = sigmoid(ce)
        dce = dye * (sg * (1.0 + ce * (1.0 - sg)))
        dx = w_ref[pl.ds(CONV_K - 1, 1), :] * dce
        for k in range(CONV_K - 1):
            dx = dx + w_ref[pl.ds(k, 1), :] * pltpu.roll(dce, n - (CONV_K - 1 - k), 0)
        dx_ref[...] = dx[HALO:HALO + tc].astype(BF16)
        dcur = dce[HALO:HALO + tc]
        for k in range(CONV_K):
            xs = xe if k == CONV_K - 1 else pltpu.roll(xe, CONV_K - 1 - k, 0)
            dw_ref[pl.ds(k, 1), :] += jnp.sum(dcur * xs[HALO:HALO + tc], axis=0, keepdims=True)

    nh = T // HALO
    prev_map = lambda j, i: (jnp.maximum(i * (tc // HALO) - 1, 0), j)
    next_map = lambda j, i: (jnp.minimum((i + 1) * (tc // HALO), nh - 1), j)
    cur_map = lambda j, i: (i, j)
    return pl.pallas_call(
        body, name=name, grid=(C // CONV_TL, nt),
        in_specs=[pl.BlockSpec((HALO, CONV_TL), prev_map), pl.BlockSpec((tc, CONV_TL), cur_map),
                  pl.BlockSpec((HALO, CONV_TL), next_map), pl.BlockSpec((tc, CONV_TL), cur_map),
                  pl.BlockSpec((HALO, CONV_TL), next_map), pl.BlockSpec((HALO, CONV_TL), lambda j, i: (0, j))],
        out_specs=[pl.BlockSpec((tc, CONV_TL), cur_map), pl.BlockSpec((HALO, CONV_TL), lambda j, i: (0, j))],
        out_shape=[jax.ShapeDtypeStruct((T, C), BF16), jax.ShapeDtypeStruct((HALO, C), F32)],
        compiler_params=_params(dimension_semantics=("arbitrary", "arbitrary")),
    )(x, x, x, dy, dy, w8)


GD_VH = 16
GD_HB = 16


def _unit_lower_inverse(a, eye_f):
    nb = lambda i, w: sum((i >= w * j).astype(jnp.int32) for j in range(1, GD_CHUNK // w))
    r, c = _iota((GD_CHUNK, GD_CHUNK), 0), _iota((GD_CHUNK, GD_CHUNK), 1)
    same16, same32 = nb(r, 16) == nb(c, 16), nb(r, 32) == nb(c, 32)
    n = -jnp.where(same16, a, 0.0)
    t = eye_f + n
    p = n
    for _ in range(3):
        p = bmmf(p, p)
        t = t + bmmf(t, p)
    for lower in (same32 & jnp.logical_not(same16), jnp.logical_not(same32)):
        t = t - bmmf(bmmf(t, jnp.where(lower, a, 0.0)), t)
    return t


@jax.custom_vjp
def unit_lower_solve2(a, r1, r2, eye_f):
    t = _unit_lower_inverse(a, eye_f)
    return bmmf(t, r1), bmmf(t, r2)


def _uls2_fwd(a, r1, r2, eye_f):
    t = _unit_lower_inverse(a, eye_f)
    x1, x2 = bmmf(t, r1), bmmf(t, r2)
    return (x1, x2), (t, x1, x2, eye_f)


def _uls2_bwd(res, g):
    t, x1, x2, eye_f = res
    high = lax.Precision.HIGH
    d1 = _bdot(t, g[0], 1, 1, high)
    d2 = _bdot(t, g[1], 1, 1, high)
    da = -(_bdot(d1, x1, 2, 2, high) + _bdot(d2, x2, 2, 2, high))
    return da, d1, d2, jnp.zeros_like(eye_f)


unit_lower_solve2.defvjp(_uls2_fwd, _uls2_bwd)


def gd_heads(al, dt, og, s, qc, kc, v, z, a_col, b_col, cst):
    lt_t, eye, incl, strict = cst
    eye_f = eye.astype(F32)
    nh = s.shape[0]
    q = qc * lax.rsqrt(jnp.sum(qc * qc, axis=-1, keepdims=True) + EPS) * (GD_DK ** -0.5)
    k = kc * lax.rsqrt(jnp.sum(kc * kc, axis=-1, keepdims=True) + EPS)
    beta = sigmoid(b_col)
    g = -jnp.exp(al) * softplus(a_col + dt)
    d_mat = jnp.broadcast_to(jnp.sum(g * lt_t, axis=1, keepdims=True), (nh, GD_CHUNK, GD_CHUNK))
    d = jnp.sum(jnp.where(eye, d_mat, 0.0), axis=-1, keepdims=True)
    dec = jnp.exp(jnp.where(incl, d - d_mat, NEG_INF))
    kb = k * beta
    u, w = unit_lower_solve2(bmm_nt(kb, k) * dec * strict, v * beta, kb * jnp.exp(d), eye_f)
    v_new = u - bmm(w, s)
    o = bmm(q * jnp.exp(d), s) + bmm(bmm_nt(q, k) * dec, v_new)
    dl = jnp.sum(g, axis=1, keepdims=True)
    s_new = s * jnp.exp(dl) + bmm_tn(k * jnp.exp(dl - d), v_new)
    on = o * lax.rsqrt(jnp.mean(o * o, axis=-1, keepdims=True) + EPS) * og
    return s_new, on * silu(z)


def _gd_consts():
    r, c = _iota((GD_CHUNK, GD_CHUNK), 0), _iota((GD_CHUNK, GD_CHUNK), 1)
    return (r <= c).astype(F32), r == c, r >= c, (r > c).astype(F32)


def _gd_load(qkv_ref, z_ref, ab_v, al_v, dt_v, heads):
    lane = _iota((1, 128), 1)
    hot_a = [(lane == h).astype(F32) for h in heads]
    hot_b = [(lane == GD_VH + h).astype(F32) for h in heads]
    col = lambda src, hot: jnp.stack([jnp.sum(src * m, axis=-1, keepdims=True) for m in hot])
    ops = (col(al_v, hot_a), col(dt_v, hot_a),
           jnp.stack([qkv_ref[:, pl.ds((h // 2) * GD_DK, GD_DK)] for h in heads]),
           jnp.stack([qkv_ref[:, pl.ds(GD_QKW + (h // 2) * GD_DK, GD_DK)] for h in heads]),
           jnp.stack([qkv_ref[:, pl.ds(2 * GD_QKW + h * GD_DK, GD_DK)] for h in heads]),
           jnp.stack([z_ref[:, pl.ds(h * GD_DK, GD_DK)] for h in heads]),
           col(ab_v, hot_a), col(ab_v, hot_b))
    return ops, hot_a, hot_b


def gd_fwd(name, qkv, z, ab, al, dt, og):
    T = qkv.shape[0]
    nc = T // GD_CHUNK

    def body(qkv_ref, z_ref, ab_ref, al_ref, dt_ref, og_ref, u_ref, st_ref, s_scr):
        @pl.when(pl.program_id(0) == 0)
        def _():
            s_scr[...] = jnp.zeros(s_scr.shape, F32)
        st_ref[0] = s_scr[...]
        cst = _gd_consts()
        ab_v, al_v, dt_v, og_v = ab_ref[...], al_ref[...], dt_ref[...], og_ref[...]
        for b0 in range(0, GD_VH, GD_HB):
            heads = list(range(b0, b0 + GD_HB))
            (alh, dth, q, k, v, zz, a_col, b_col), _, _ = _gd_load(qkv_ref, z_ref, ab_v, al_v, dt_v, heads)
            s_new, u = gd_heads(alh, dth, og_v, s_scr[pl.ds(b0, GD_HB)], q, k, v, zz, a_col, b_col, cst)
            s_scr[pl.ds(b0, GD_HB)] = s_new
            for i, h in enumerate(heads):
                u_ref[:, pl.ds(h * GD_DK, GD_DK)] = u[i].astype(BF16)

    return pl.pallas_call(
        body, name=name, grid=(nc,),
        in_specs=[_row_spec(GD_CHUNK, GD_QKV), _row_spec(GD_CHUNK, GD_VW), _row_spec(GD_CHUNK, 128),
                  _const_spec((1, 128)), _const_spec((1, 128)), _const_spec((1, 128))],
        out_specs=[_row_spec(GD_CHUNK, GD_VW), pl.BlockSpec((1, GD_VH, GD_DK, GD_DK), lambda i: (i, 0, 0, 0))],
        out_shape=[jax.ShapeDtypeStruct((T, GD_VW), BF16), jax.ShapeDtypeStruct((nc, GD_VH, GD_DK, GD_DK), F32)],
        scratch_shapes=[pltpu.VMEM((GD_VH, GD_DK, GD_DK), F32)],
        compiler_params=_params(dimension_semantics=("arbitrary",)),
    )(qkv, z, ab, al, dt, og)


def gd_bwd(name, qkv, z, ab, al, dt, og, states, du, carry=()):
    T = qkv.shape[0]
    nc = T // GD_CHUNK

    def body(qkv_ref, z_ref, ab_ref, al_ref, dt_ref, og_ref, st_ref, du_ref,
             dqkv_ref, dz_ref, dab_ref, dal_ref, ddt_ref, dog_ref, ds_scr):
        @pl.when(pl.program_id(0) == 0)
        def _():
            ds_scr[...] = jnp.zeros(ds_scr.shape, F32)
            dal_ref[...] = jnp.zeros(dal_ref.shape, F32)
            ddt_ref[...] = jnp.zeros(ddt_ref.shape, F32)
            dog_ref[...] = jnp.zeros(dog_ref.shape, F32)
        cst = _gd_consts()
        ab_v, al_v, dt_v, og_v = ab_ref[...], al_ref[...], dt_ref[...], og_ref[...]
        dab = jnp.zeros((GD_CHUNK, 128), F32)
        dal_row = jnp.zeros((1, 128), F32)
        ddt_row = jnp.zeros((1, 128), F32)
        dog_row = jnp.zeros((1, 128), F32)
        for b0 in range(0, GD_VH, GD_HB):
            heads = list(range(b0, b0 + GD_HB))
            (alh, dth, q, k, v, zz, a_col, b_col), hot_a, hot_b = _gd_load(qkv_ref, z_ref, ab_v, al_v, dt_v, heads)
            fn = lambda *a: gd_heads(*a, cst)
            _, f = jax.vjp(fn, alh, dth, og_v, st_ref[0, pl.ds(b0, GD_HB)], q, k, v, zz, a_col, b_col)
            du_h = jnp.stack([du_ref[:, pl.ds(h * GD_DK, GD_DK)].astype(F32) for h in heads])
            dalh, ddth, dog, ds, dq, dk, dv, dz, da_col, db_col = f((ds_scr[pl.ds(b0, GD_HB)], du_h))
            ds_scr[pl.ds(b0, GD_HB)] = ds
            dog_row = dog_row + dog
            for i, h in enumerate(heads):
                dqkv_ref[:, pl.ds(2 * GD_QKW + h * GD_DK, GD_DK)] = dv[i]
                dz_ref[:, pl.ds(h * GD_DK, GD_DK)] = dz[i].astype(BF16)
                dab = dab + da_col[i] * hot_a[i] + db_col[i] * hot_b[i]
                dal_row = dal_row + dalh[i] * hot_a[i]
                ddt_row = ddt_row + ddth[i] * hot_a[i]
                if h % 2 == 0:
                    dqkv_ref[:, pl.ds((h // 2) * GD_DK, GD_DK)] = dq[i] + dq[i + 1]
                    dqkv_ref[:, pl.ds(GD_QKW + (h // 2) * GD_DK, GD_DK)] = dk[i] + dk[i + 1]
        dab_ref[...] = dab
        dal_ref[...] += dal_row
        ddt_ref[...] += ddt_row
        dog_ref[...] += dog_row

    rev = lambda i: (nc - 1 - i, 0)
    outs, ex = _scan_call(
        body, name, nc,
        [pl.BlockSpec((GD_CHUNK, GD_QKV), rev), pl.BlockSpec((GD_CHUNK, GD_VW), rev), pl.BlockSpec((GD_CHUNK, 128), rev),
         _const_spec((1, 128)), _const_spec((1, 128)), _const_spec((1, 128)),
         pl.BlockSpec((1, GD_VH, GD_DK, GD_DK), lambda i: (nc - 1 - i, 0, 0, 0)), pl.BlockSpec((GD_CHUNK, GD_VW), rev)],
        [pl.BlockSpec((GD_CHUNK, GD_QKV), rev), pl.BlockSpec((GD_CHUNK, GD_VW), rev), pl.BlockSpec((GD_CHUNK, 128), rev),
         _const_spec((1, 128)), _const_spec((1, 128)), _const_spec((1, 128))],
        [jax.ShapeDtypeStruct((T, GD_QKV), F32), jax.ShapeDtypeStruct((T, GD_VW), BF16),
         jax.ShapeDtypeStruct((T, 128), F32)] + [jax.ShapeDtypeStruct((1, 128), F32)] * 3,
        [pltpu.VMEM((GD_VH, GD_DK, GD_DK), F32)], (qkv, z, ab, al, dt, og, states, du), carry)
    return (*outs, ex)


def _my_pos():
    return lax.axis_index("x"), lax.axis_index("y"), lax.axis_index("c")


def _peers(pos):
    out = []
    for k in range(1, N_DEV):
        dev = tuple(1 - p if (k >> s) & 1 else p for p, s in zip(pos, (2, 1, 0)))
        out.append((dev, 4 * dev[0] + 2 * dev[1] + dev[2]))
    return out


_ANY = pl.BlockSpec(memory_space=pl.ANY)
_COMM_SCRATCH = [pltpu.SemaphoreType.DMA((N_DEV - 1,)), pltpu.SemaphoreType.DMA((N_DEV - 1,)), pltpu.SemaphoreType.DMA(())]
GATHER, SCATTER = "gather", "scatter"


def _exchange_descs(pos, kinds, x_refs, o_refs, send_sems, recv_sems, local_sems, with_recvs=True):
    me = 4 * pos[0] + 2 * pos[1] + pos[2]
    peers = _peers(pos)
    local, sends, recvs = [], [], []
    for i, (kind, x, o) in enumerate(zip(kinds, x_refs, o_refs)):
        local.append(pltpu.make_async_copy(x if kind == GATHER else x.at[me], o.at[me], local_sems.at[i]))
        for k, (dev, pid) in enumerate(peers):
            s = (N_DEV - 1) * i + k
            src = x if kind == GATHER else x.at[pid]
            sends.append(pltpu.make_async_remote_copy(src, o.at[me], send_sems.at[s], recv_sems.at[s],
                                                      device_id=dev, device_id_type=MESH))
            if with_recvs:
                recvs.append(pltpu.make_async_remote_copy(src, o.at[pid], send_sems.at[s], recv_sems.at[s],
                                                          device_id=dev, device_id_type=MESH))
    return local, sends, recvs


def _scan_call(body, name, nsteps, in_specs, out_specs, out_shape, scratch_shapes, args, carry=()):
    n_in, n_out, n_scr, n_ex = len(in_specs), len(out_specs), len(scratch_shapes), len(carry)
    params = _params(dimension_semantics=("arbitrary",))
    if not carry:
        outs = pl.pallas_call(body, name=name, grid=(nsteps,), in_specs=in_specs, out_specs=out_specs, out_shape=out_shape,
                              scratch_shapes=scratch_shapes, compiler_params=params)(*args)
        return outs, []
    kinds = [k for k, _ in carry]
    xs = [a for _, a in carry]
    ex_shape = [jax.ShapeDtypeStruct(((N_DEV,) + a.shape) if k == GATHER else a.shape, a.dtype) for k, a in carry]

    def full(*refs):
        ins, x_refs = refs[:n_in], refs[n_in:n_in + n_ex]
        outs = refs[n_in + n_ex:n_in + n_ex + n_out]
        o_refs = refs[n_in + n_ex + n_out:n_in + 2 * n_ex + n_out]
        scr = refs[n_in + 2 * n_ex + n_out:n_in + 2 * n_ex + n_out + n_scr]
        sems = refs[n_in + 2 * n_ex + n_out + n_scr:]
        pos = _my_pos()
        step = pl.program_id(0)

        @pl.when(step == 0)
        def _():
            local, sends, _ = _exchange_descs(pos, kinds, x_refs, o_refs, *sems, with_recvs=False)
            for cp in local + sends:
                cp.start()

        body(*ins, *outs, *scr)

        @pl.when(step == nsteps - 1)
        def _():
            local, sends, recvs = _exchange_descs(pos, kinds, x_refs, o_refs, *sems)
            for cp in recvs:
                cp.wait_recv()
            for cp in sends:
                cp.wait_send()
            for cp in local:
                cp.wait()

    n_sem = (N_DEV - 1) * n_ex
    outs = pl.pallas_call(
        full, name=name, grid=(nsteps,), in_specs=list(in_specs) + [_ANY] * n_ex, out_specs=list(out_specs) + [_ANY] * n_ex,
        out_shape=list(out_shape) + ex_shape,
        scratch_shapes=list(scratch_shapes) + [pltpu.SemaphoreType.DMA((n_sem,)), pltpu.SemaphoreType.DMA((n_sem,)),
                                               pltpu.SemaphoreType.DMA((n_ex,))],
        compiler_params=params)(*args, *xs)
    return outs[:n_out], list(outs[n_out:])


def all_gather(name, x):
    def body(x_ref, o_ref, send_sems, recv_sems, local_sem):
        pos = _my_pos()
        me = 4 * pos[0] + 2 * pos[1] + pos[2]
        peers = _peers(pos)
        mine = pltpu.make_async_copy(x_ref, o_ref.at[me], local_sem)
        mine.start()
        sends = [pltpu.make_async_remote_copy(x_ref, o_ref.at[me], send_sems.at[k], recv_sems.at[k],
                                              device_id=dev, device_id_type=MESH) for k, (dev, _) in enumerate(peers)]
        for cp in sends:
            cp.start()
        for k, (dev, pid) in enumerate(peers):
            pltpu.make_async_remote_copy(x_ref, o_ref.at[pid], send_sems.at[k], recv_sems.at[k],
                                         device_id=dev, device_id_type=MESH).wait_recv()
        for cp in sends:
            cp.wait_send()
        mine.wait()

    return pl.pallas_call(
        body, name=name, out_shape=jax.ShapeDtypeStruct((N_DEV,) + x.shape, x.dtype),
        in_specs=[_ANY], out_specs=_ANY, scratch_shapes=_COMM_SCRATCH,
    )(x)


def lb_rows(r0, r1, r2, r3):
    mx = jnp.maximum(jnp.maximum(r0, r1), jnp.maximum(r2, r3))
    e = [jnp.exp(r - mx) for r in (r0, r1, r2, r3)]
    inv = 1.0 / (e[0] + e[1] + e[2] + e[3])
    c0 = e[0] * inv
    c1 = c0 + e[1] * inv
    c2 = c1 + e[2] * inv
    c3 = c2 + e[3] * inv
    return c0 - c0, c1 - c0, c2 - c0, c3 - c0


def mod_partial(name, c_all, ada_w):
    n, _, w = ada_w.shape

    def body(c_ref, w_ref, o_ref):
        o_ref[0] = mm(c_ref[...], w_ref[0])

    return pl.pallas_call(
        body, name=name, grid=(n,),
        in_specs=[_const_spec((N_DEV, D)), pl.BlockSpec((1, D, w), lambda i: (i, 0, 0))],
        out_specs=pl.BlockSpec((1, N_DEV, w), lambda i: (i, 0, 0)),
        out_shape=jax.ShapeDtypeStruct((n, N_DEV, w), F32),
    )(c_all, ada_w)


def prep(name, modp, ada_b, hgrn_lb):
    def body(mp_ref, b_ref, lb_ref, mod_ref, lbo_ref):
        mod_ref[...] = mp_ref[...] + b_ref[...]
        out = lb_rows(*[lb_ref[pl.ds(i, 1), :] for i in range(4)])
        for i in range(4):
            lbo_ref[pl.ds(i, 1), :] = out[i]

    return pl.pallas_call(
        body, name=name,
        out_shape=[jax.ShapeDtypeStruct(modp.shape, F32), jax.ShapeDtypeStruct(hgrn_lb.shape, F32)],
    )(modp, ada_b, hgrn_lb)


def lb_grad(name, hgrn_lb, dlb_parts):
    def body(lb_ref, d_ref, o_ref):
        cts = []
        for i in range(4):
            acc = d_ref[0, pl.ds(i, 1), :]
            for p in range(1, N_DEV):
                acc = acc + d_ref[p, pl.ds(i, 1), :]
            cts.append(acc)
        _, f = jax.vjp(lb_rows, *[lb_ref[pl.ds(i, 1), :] for i in range(4)])
        for i, g in enumerate(f(tuple(cts))):
            o_ref[pl.ds(i, 1), :] = g

    return pl.pallas_call(body, name=name, out_shape=jax.ShapeDtypeStruct(hgrn_lb.shape, F32))(hgrn_lb, dlb_parts)


def ada_grad(name, c_t, dm):
    n, _, w = dm.shape

    def body(c_ref, d_ref, o_ref):
        acc = c_ref[:, pl.ds(0, 1)] * d_ref[0, pl.ds(0, 1), :]
        for b in range(1, N_DEV):
            acc = acc + c_ref[:, pl.ds(b, 1)] * d_ref[0, pl.ds(b, 1), :]
        o_ref[0] = acc

    return pl.pallas_call(
        body, name=name, grid=(n,),
        in_specs=[_const_spec((D, N_DEV)), pl.BlockSpec((1, N_DEV, w), lambda i: (i, 0, 0))],
        out_specs=pl.BlockSpec((1, D, w), lambda i: (i, 0, 0)),
        out_shape=jax.ShapeDtypeStruct((n, D, w), F32),
    )(c_t, dm)


ADAMW_ROWS = 256


def adamw(name, w, m, v, gparts):
    shape = w.shape
    L = shape[-1]
    R = 1
    for s in shape[:-1]:
        R *= s
    P = gparts.shape[0]
    tr = min(R, ADAMW_ROWS)

    def body(w_ref, m_ref, v_ref, g_ref, go_ref, d_ref, mo_ref, vo_ref):
        g = g_ref[0].astype(F32)
        for p in range(1, P):
            g = g + g_ref[p].astype(F32)
        mn = ADAM_B1 * m_ref[...] + (1.0 - ADAM_B1) * g
        vn = ADAM_B2 * v_ref[...] + (1.0 - ADAM_B2) * (g * g)
        m_hat = mn / (1.0 - ADAM_B1 ** ADAM_STEP)
        v_hat = vn / (1.0 - ADAM_B2 ** ADAM_STEP)
        go_ref[...] = g
        d_ref[...] = -ADAM_LR * (m_hat / (jnp.sqrt(v_hat) + ADAM_EPS) + ADAM_WD * w_ref[...])
        mo_ref[...] = mn
        vo_ref[...] = vn

    spec = pl.BlockSpec((tr, L), lambda i: (i, 0))
    outs = pl.pallas_call(
        body, name=name, grid=(R // tr,),
        in_specs=[spec, spec, spec, pl.BlockSpec((P, tr, L), lambda i: (0, i, 0))],
        out_specs=[spec] * 4, out_shape=[jax.ShapeDtypeStruct((R, L), F32)] * 4,
        compiler_params=_params(dimension_semantics=("arbitrary",)),
    )(w.reshape(R, L), m.reshape(R, L), v.reshape(R, L), gparts.reshape(P, R, L))
    return [o.reshape(shape) for o in outs]


def _cols_from(g):
    _, n, k, w8 = g.shape
    return jnp.transpose(g, (1, 2, 0, 3)).reshape(n, k, N_DEV * w8)


def _rows_from(g):
    _, n, k8, nn = g.shape
    return jnp.transpose(g, (1, 0, 2, 3)).reshape(n, N_DEV * k8, nn)


def _scatter_cols(dw, wire=BF16):
    n, k, nn = dw.shape
    return jnp.transpose(dw.reshape(n, k, N_DEV, nn // N_DEV), (2, 0, 1, 3)).astype(wire)


def _scatter_rows(dw, wire=BF16):
    n, k, nn = dw.shape
    return jnp.transpose(dw.reshape(n, N_DEV, k // N_DEV, nn), (1, 0, 2, 3)).astype(wire)


def _pad_lanes(v, width=128):
    return jnp.pad(v, ((0, 0), (0, width - v.shape[1])))


def _stack_rows(parts):
    n, L = len(parts), parts[0].shape[1]
    r = lax.broadcasted_iota(jnp.int32, (n, L), 0)
    out = jnp.zeros((n, L), parts[0].dtype)
    for i, p in enumerate(parts):
        out = jnp.where(r == i, jnp.broadcast_to(p, (n, L)), out)
    return out


def kernel(x, c, positions, hgrn_lb, ada_w, ada_b, norm_g, hg_in_w, hg_out_w, hg_onorm, sw_in_w, sw_out_w, sw_qnorm, sw_knorm, sw_sinks, gd_in_w, gd_out_w, gd_conv_w, gd_a_log, gd_dt_bias, gd_onorm, loss_target, m_hgrn_lb, m_ada_w, m_ada_b, m_norm_g, m_hg_in_w, m_hg_out_w, m_hg_onorm, m_sw_in_w, m_sw_out_w, m_sw_qnorm, m_sw_knorm, m_sw_sinks, m_gd_in_w, m_gd_out_w, m_gd_conv_w, m_gd_a_log, m_gd_dt_bias, m_gd_onorm, v_hgrn_lb, v_ada_w, v_ada_b, v_norm_g, v_hg_in_w, v_hg_out_w, v_hg_onorm, v_sw_in_w, v_sw_out_w, v_sw_qnorm, v_sw_knorm, v_sw_sinks, v_gd_in_w, v_gd_out_w, v_gd_conv_w, v_gd_a_log, v_gd_dt_bias, v_gd_onorm):
    pos = _my_pos()
    me = 4 * pos[0] + 2 * pos[1] + pos[2]
    x0 = x[0]
    tgt = loss_target[0]
    n_layers = norm_g.shape[0]
    aw = ada_w.shape[2]

    c_all = all_gather("ag_c", c)[:, 0, :]
    modp = all_gather("ag_mod", mod_partial("mod_partial", c_all, ada_w))
    modp_mine = lax.dynamic_index_in_dim(modp, me, axis=2, keepdims=False)
    modp_mine = jnp.transpose(modp_mine, (1, 0, 2)).reshape(n_layers, N_DEV * aw)
    mod, lb_all = prep("prep", modp_mine, ada_b, hgrn_lb)
    shift, scale, gate = mod[:, :D], mod[:, D:2 * D], mod[:, 2 * D:]
    row = lambda a, i: a[i:i + 1]

    later = [(GATHER, hg_in_w[1:2].astype(BF16)), (GATHER, hg_out_w.astype(BF16)), (GATHER, sw_in_w.astype(BF16)),
             (GATHER, sw_out_w.astype(BF16)), (GATHER, gd_in_w.astype(BF16)), (GATHER, gd_out_w.astype(BF16)),
             (GATHER, gd_conv_w)]
    gd_al, gd_dt = _pad_lanes(gd_a_log), _pad_lanes(gd_dt_bias)

    inv_freq = ROPE_THETA ** (-jnp.arange(0, SW_DH, 2, dtype=F32) / SW_DH)
    ang = positions[0].astype(F32)[:, None] * inv_freq
    cos, sin = jnp.cos(ang), jnp.sin(ang)
    cos64, sin64 = jnp.concatenate([cos, cos], axis=-1), jnp.concatenate([-sin, sin], axis=-1)

    xs, hs, ys, saved = [x0], [], [], []
    last = n_layers - 1
    h, (g_in0,) = norm_mod("norm0", x0, row(norm_g, 0), row(scale, 0), row(shift, 0), [(GATHER, hg_in_w[0:1].astype(BF16))])
    w_hg_in = [_cols_from(g_in0)[0], None]
    for i in range(n_layers):
        kind, j = i % 3, i // 3
        hs.append(h)
        if kind == 0:
            p = matmul(f"hg_in{i}", h, w_hg_in[j])
            u, st, got = hg_fwd(f"hg_fwd{i}", p, row(lb_all, i), row(hg_onorm, j), later if i == 0 else ())
            if i == 0:
                w_hg_in[1] = _cols_from(got[0])[0]
                w_hg_out = _rows_from(got[1])
                w_sw_in, w_sw_out = _cols_from(got[2])[0], _rows_from(got[3])[0]
                w_gd_in, w_gd_out = _cols_from(got[4])[0], _rows_from(got[5])[0]
                w_gd_qkv, w_gd_z = w_gd_in[:, :GD_QKV], w_gd_in[:, GD_QKV:GD_QKV + GD_VW]
                w_gd_ab = _pad_lanes(w_gd_in[:, GD_QKV + GD_VW:])
                conv_w8 = jnp.pad(_cols_from(got[6])[0], ((0, HALO - CONV_K), (0, 0)))
            w_out = w_hg_out[j]
            saved.append((p, u, st))
        elif kind == 1:
            p = matmul(f"sw_in{i}", h, w_sw_in)
            u, kpost = swa_fwd(f"sw_fwd{i}", p, cos64, sin64, row(sw_qnorm, j), row(sw_knorm, j), row(sw_sinks, j))
            w_out = w_sw_out
            saved.append((p, u, kpost))
        else:
            xq, cv = matmul_conv(f"gd_qkv{i}", h, w_gd_qkv, conv_w8)
            zz = matmul(f"gd_z{i}", h, w_gd_z)
            ab = matmul(f"gd_ab{i}", h, w_gd_ab)
            u, st = gd_fwd(f"gd_fwd{i}", cv, zz, ab, gd_al, gd_dt, row(gd_onorm, j))
            w_out = w_gd_out
            saved.append((xq, zz, ab, cv, u, st))
        if i < last:
            y, xn, h = out_resid_norm(f"out{i}", u, w_out, xs[i], row(gate, i), row(norm_g, i + 1), row(scale, i + 1),
                                      row(shift, i + 1))
            xs.append(xn)
        else:
            y, dx, dy, loss_acc, dgate_last = out_loss(f"out{i}", u, w_out, xs[i], tgt, row(gate, i))
        ys.append(y)
    loss = lax.psum(loss_acc[0, 0], ("x", "y", "c"))

    dgate = [None] * n_layers
    dgate[last] = dgate_last
    dg_norm, dscale, dshift = [None] * n_layers, [None] * n_layers, [None] * n_layers
    dlb = [jnp.zeros((1, D), F32)] * n_layers
    d_hg_on = [None] * 2
    rs_hg_in, rs_hg_out = [None] * 2, [None] * 2
    pending = ()
    for i in range(last, -1, -1):
        kind, j = i % 3, i // 3
        h = hs[i]
        if kind == 0:
            p, u, st = saved[i]
            du = matmul_nt(f"hg_du{i}", [(dy, w_hg_out[j])])
            d_out = matmul_tn(f"hg_dwo{i}", u, dy)
            dp, dlb_i, dog, got = hg_bwd(f"hg_bwd{i}", p, row(lb_all, i), row(hg_onorm, j), st, du, pending)
            if got:
                rs_sw_in, rs_sw_out = got
            dlb[i] = dlb_i
            d_hg_on[j] = dog
            dh_pairs = [(dp, w_hg_in[j])]
            d_in = matmul_tn(f"hg_dwi{i}", h, dp)
            pending = [(SCATTER, _scatter_cols(d_in[None])), (SCATTER, _scatter_rows(d_out[None]))]
        elif kind == 1:
            p, u, kpost = saved[i]
            du = matmul_nt(f"sw_du{i}", [(dy, w_sw_out)])
            d_out = matmul_tn(f"sw_dwo{i}", u, dy)
            dp, d_qn, d_kn, d_sk, got = swa_bwd(f"sw_bwd{i}", p, kpost, cos64, sin64, row(sw_qnorm, j), row(sw_knorm, j),
                                                row(sw_sinks, j), du, pending)
            rs_gd_in, rs_gd_out, rs_gd_conv = got
            dh_pairs = [(dp, w_sw_in)]
            d_in = matmul_tn(f"sw_dwi{i}", h, dp)
            pending = [(SCATTER, _scatter_cols(d_in[None])), (SCATTER, _scatter_rows(d_out[None]))]
        else:
            xq, zz, ab, cv, u, st = saved[i]
            du = matmul_nt(f"gd_du{i}", [(dy, w_gd_out)])
            d_out = matmul_tn(f"gd_dwo{i}", u, dy)
            dcv, dz, dab, d_al, d_dt, d_gd_on, got = gd_bwd(f"gd_bwd{i}", cv, zz, ab, gd_al, gd_dt, row(gd_onorm, j), st, du,
                                                            pending)
            rs_hg_in[1], rs_hg_out[1] = got
            dxq, d_conv8 = conv_bwd(f"gd_dconv{i}", xq, dcv, conv_w8)
            dh_pairs = [(dxq, w_gd_qkv), (dz, w_gd_z), (dab, w_gd_ab)]
            d_in = jnp.concatenate([matmul_tn(f"gd_dwq{i}", h, dxq), matmul_tn(f"gd_dwz{i}", h, dz),
                                    matmul_tn(f"gd_dwab{i}", h, dab)[:, :2 * GD_VH]], axis=1)
            pending = [(SCATTER, _scatter_cols(d_in[None])), (SCATTER, _scatter_rows(d_out[None])),
                       (SCATTER, _scatter_cols(d_conv8[None, :CONV_K], F32))]
        if i > 0:
            (dx, dy, dg_norm[i], dscale[i], dshift[i], dgate[i - 1]), _ = dh_bwd_row(
                f"dh{i}", dh_pairs, xs[i], dx, row(norm_g, i), row(scale, i), row(shift, i), ys[i - 1], row(gate, i - 1))
        else:
            (dx, dg_norm[i], dscale[i], dshift[i]), (rs_hg_in[0], rs_hg_out[0]) = dh_bwd_row(
                f"dh{i}", dh_pairs, xs[i], dx, row(norm_g, i), row(scale, i), row(shift, i), carry=pending)
    grad_x = dx[None]

    dmod = jnp.concatenate([_stack_rows(dshift), _stack_rows(dscale), _stack_rows(dgate)], axis=1)
    misc = _stack_rows(d_hg_on + [d_gd_on, _pad_lanes(d_qn), _pad_lanes(d_kn), _pad_lanes(d_sk), d_al, d_dt])
    small = jnp.concatenate([_stack_rows(dlb).reshape(-1, 128), _stack_rows(dg_norm).reshape(-1, 128),
                             dmod.reshape(-1, 128), misc], axis=0)
    small_all = all_gather("ag_small", small)
    n_lb = n_layers * D // 128
    n_mod = n_layers * 3 * D // 128
    o = 0
    dlb_parts = small_all[:, o:o + n_lb].reshape(N_DEV, n_layers, D); o += n_lb
    dgn_parts = small_all[:, o:o + n_lb].reshape(N_DEV, n_layers, D); o += n_lb
    dmod_parts = small_all[:, o:o + n_mod].reshape(N_DEV, n_layers, 3 * D); o += n_mod
    dhgon_parts = small_all[:, o:o + 2]; o += 2
    dgdon_parts = small_all[:, o:o + 1]; o += 1
    dqn_parts = small_all[:, o:o + 1, :SW_DH]; o += 1
    dkn_parts = small_all[:, o:o + 1, :SW_DH]; o += 1
    dsk_parts = small_all[:, o:o + 1, :SW_KV * SW_G]; o += 1
    dal_parts = small_all[:, o:o + 1, :GD_VH]; o += 1
    ddt_parts = small_all[:, o:o + 1, :GD_VH]; o += 1

    g_lb = lb_grad("lb_grad", hgrn_lb, dlb_parts)
    dm_mine = lax.dynamic_slice_in_dim(dmod_parts, me * aw, aw, axis=2)
    g_ada_w = ada_grad("ada_grad", jnp.transpose(c_all), jnp.transpose(dm_mine, (1, 0, 2)))

    res = {}
    res["hgrn_lb"] = adamw("aw_hgrn_lb", hgrn_lb, m_hgrn_lb, v_hgrn_lb, g_lb[None])
    res["ada_w"] = adamw("aw_ada_w", ada_w, m_ada_w, v_ada_w, g_ada_w[None])
    res["ada_b"] = adamw("aw_ada_b", ada_b, m_ada_b, v_ada_b, dmod_parts)
    res["norm_g"] = adamw("aw_norm_g", norm_g, m_norm_g, v_norm_g, dgn_parts)
    res["hg_in_w"] = adamw("aw_hg_in", hg_in_w, m_hg_in_w, v_hg_in_w, jnp.concatenate(rs_hg_in, axis=1))
    res["hg_out_w"] = adamw("aw_hg_out", hg_out_w, m_hg_out_w, v_hg_out_w, jnp.concatenate(rs_hg_out, axis=1))
    res["hg_onorm"] = adamw("aw_hg_onorm", hg_onorm, m_hg_onorm, v_hg_onorm, dhgon_parts)
    res["sw_in_w"] = adamw("aw_sw_in", sw_in_w, m_sw_in_w, v_sw_in_w, rs_sw_in)
    res["sw_out_w"] = adamw("aw_sw_out", sw_out_w, m_sw_out_w, v_sw_out_w, rs_sw_out)
    res["sw_qnorm"] = adamw("aw_sw_qn", sw_qnorm, m_sw_qnorm, v_sw_qnorm, dqn_parts)
    res["sw_knorm"] = adamw("aw_sw_kn", sw_knorm, m_sw_knorm, v_sw_knorm, dkn_parts)
    res["sw_sinks"] = adamw("aw_sw_sinks", sw_sinks, m_sw_sinks, v_sw_sinks, dsk_parts)
    res["gd_in_w"] = adamw("aw_gd_in", gd_in_w, m_gd_in_w, v_gd_in_w, rs_gd_in)
    res["gd_out_w"] = adamw("aw_gd_out", gd_out_w, m_gd_out_w, v_gd_out_w, rs_gd_out)
    res["gd_conv_w"] = adamw("aw_gd_conv", gd_conv_w, m_gd_conv_w, v_gd_conv_w, rs_gd_conv)
    res["gd_a_log"] = adamw("aw_gd_alog", gd_a_log, m_gd_a_log, v_gd_a_log, dal_parts)
    res["gd_dt_bias"] = adamw("aw_gd_dt", gd_dt_bias, m_gd_dt_bias, v_gd_dt_bias, ddt_parts)
    res["gd_onorm"] = adamw("aw_gd_onorm", gd_onorm, m_gd_onorm, v_gd_onorm, dgdon_parts)

    order = ["hgrn_lb", "ada_w", "ada_b", "norm_g", "hg_in_w", "hg_out_w", "hg_onorm", "sw_in_w", "sw_out_w", "sw_qnorm",
             "sw_knorm", "sw_sinks", "gd_in_w", "gd_out_w", "gd_conv_w", "gd_a_log", "gd_dt_bias", "gd_onorm"]
    outs = [loss, grad_x]
    for part in range(4):
        outs += [res[n][part] for n in order]
    return tuple(outs)
```
